```python
import jax, jax.numpy as jnp
from jax import lax
import numpy as np

D_MODEL = 1024
BATCH = 4
SEQ = 4096
DEPTH = 4
DEC_BATCH = 128
DEC_SEQ = 8
PAST_LEN = 2048
PAGE_SIZE = 128

N_META = 16
N_EVEN = (DEPTH + 1) // 2
N_ODD = DEPTH // 2
CONV_A_CH = D_MODEL // 2
CONV_A_WIDTH = 31
N_HEADS_B = 8
HEAD_DIM_B = (D_MODEL // 2) // N_HEADS_B
ATT_B_CH = N_HEADS_B * HEAD_DIM_B
Q_BLOCK = 128
FORGET_BIAS_INIT = 3.0
SCONV_CH = D_MODEL // 2
SCONV_WIDTH = 3
POOL_CH = D_MODEL // 2
POOL_WINDOWS = (2, 4, 8, 16)
N_POOL_GROUPS = len(POOL_WINDOWS)
POOL_GROUP_CH = POOL_CH // N_POOL_GROUPS
POOL_HIST = max(POOL_WINDOWS) - 1
D_FF = ((8 * D_MODEL // 3 + 255) // 256) * 256
EVEN_IN = 2 * CONV_A_CH + 3 * ATT_B_CH + N_HEADS_B
EVEN_MIX = CONV_A_CH + ATT_B_CH
ODD_IN = 3 * SCONV_CH + POOL_CH
ODD_MIX = SCONV_CH + POOL_CH
DEEPNORM_ALPHA = float((2 * DEPTH) ** 0.25)
DEEPNORM_BETA = float((8 * DEPTH) ** -0.25)
LN_EPS = 1e-5

kernel_name = "hybrid_conformer_fox_shortconv_pool_step"


def layer_norm(x, g, b):
    xf = x.astype(jnp.float32)
    mu = xf.mean(-1, keepdims=True)
    var = jnp.square(xf - mu).mean(-1, keepdims=True)
    return ((xf - mu) * lax.rsqrt(var + LN_EPS) * g.astype(jnp.float32) + b.astype(jnp.float32)).astype(x.dtype)


def depthwise_causal(x_ext, w):
    c = x_ext.shape[-1]
    return lax.conv_general_dilated(x_ext, w[:, None, :].astype(x_ext.dtype), window_strides=(1,),
                                    padding='VALID', dimension_numbers=('NWC', 'WIO', 'NWC'),
                                    feature_group_count=c)


def fox_attend(q, c_q, pos_q, k, v, c_k, pos_k):
    s = jnp.einsum('bqhd,bkhd->bhqk', q, k).astype(jnp.float32) * (HEAD_DIM_B ** -0.5)
    bias = jnp.transpose(c_q, (0, 2, 1))[:, :, :, None] - jnp.transpose(c_k, (0, 2, 1))[:, :, None, :]
    mask = pos_k[None, :] <= pos_q[:, None]
    s = jnp.where(mask[None, None], s + bias, -jnp.inf)
    p = jax.nn.softmax(s, axis=-1)
    return jnp.einsum('bhqk,bkhd->bqhd', p.astype(v.dtype), v)


def fox_prompt(q, k, v, logf):
    b, s_tot = q.shape[:2]
    n_blocks = (s_tot - N_META) // Q_BLOCK
    c = jnp.cumsum(logf.astype(jnp.float32), axis=1)
    pos = jnp.arange(s_tot)
    meta_out = fox_attend(q[:, :N_META], c[:, :N_META], pos[:N_META],
                          k[:, :N_META], v[:, :N_META], c[:, :N_META], pos[:N_META])
    qb = q[:, N_META:].reshape(b, n_blocks, Q_BLOCK, N_HEADS_B, HEAD_DIM_B).swapaxes(0, 1)
    cb = c[:, N_META:].reshape(b, n_blocks, Q_BLOCK, N_HEADS_B).swapaxes(0, 1)
    pb = pos[N_META:].reshape(n_blocks, Q_BLOCK)
    out = lax.map(lambda a: fox_attend(a[0], a[1], a[2], k, v, c, pos), (qb, cb, pb))
    out = out.swapaxes(0, 1).reshape(b, s_tot - N_META, N_HEADS_B, HEAD_DIM_B)
    return jnp.concatenate([meta_out, out], axis=1)


def even_projections(x, w_in, b_f):
    b, t = x.shape[:2]
    h = x @ w_in
    cuts = [CONV_A_CH, 2 * CONV_A_CH, 2 * CONV_A_CH + ATT_B_CH,
            2 * CONV_A_CH + 2 * ATT_B_CH, 2 * CONV_A_CH + 3 * ATT_B_CH]
    u, g, q, k, v, f = jnp.split(h, cuts, axis=-1)
    a = u * jax.nn.sigmoid(g)
    q = q.reshape(b, t, N_HEADS_B, HEAD_DIM_B)
    k = k.reshape(b, t, N_HEADS_B, HEAD_DIM_B)
    v = v.reshape(b, t, N_HEADS_B, HEAD_DIM_B)
    logf = jax.nn.log_sigmoid((f + b_f).astype(jnp.float32)).astype(x.dtype)
    return a, q, k, v, logf


def conformer_tail(a_ext, w_dw, b_dw, g, b):
    y = depthwise_causal(a_ext, w_dw) + b_dw
    return jax.nn.silu(layer_norm(y, g, b))


def multiscale_pool(p_ext, hist_valid, w_mix, scale):
    b = p_ext.shape[0]
    t_len = p_ext.shape[1] - POOL_HIST
    cs = jnp.cumsum(p_ext.astype(jnp.float32), axis=1)
    cs = jnp.concatenate([jnp.zeros_like(cs[:, :1]), cs], axis=1)
    cur = p_ext[:, POOL_HIST:].astype(jnp.float32)
    t = jnp.arange(t_len)
    outs = []
    for gi, w in enumerate(POOL_WINDOWS):
        sl = slice(gi * POOL_GROUP_CH, (gi + 1) * POOL_GROUP_CH)
        win_sum = cs[:, POOL_HIST + 1:, sl] - cs[:, POOL_HIST + 1 - w:POOL_HIST + 1 - w + t_len, sl]
        count = jnp.minimum(w, hist_valid + t + 1).astype(jnp.float32)
        outs.append(win_sum / count[None, :, None] - cur[:, :, sl])
    d = jnp.stack(outs, axis=2).astype(p_ext.dtype)
    d = jnp.einsum('btgc,gcd->btgd', d, w_mix).reshape(b, t_len, POOL_CH)
    return d * scale


def odd_mixer(x, hist_s, hist_p, hist_valid, w_in, w_sc, w_mix, scale, w_out):
    h = x @ w_in
    hc, bg, cg, pd = jnp.split(h, [SCONV_CH, 2 * SCONV_CH, 3 * SCONV_CH], axis=-1)
    y_ext = jnp.concatenate([hist_s, cg * hc], axis=1)
    out_c = bg * depthwise_causal(y_ext, w_sc)
    p_ext = jnp.concatenate([hist_p, pd], axis=1)
    out_d = multiscale_pool(p_ext, hist_valid, w_mix, scale)
    mix = jnp.concatenate([out_c, out_d], axis=-1) @ w_out
    return mix, y_ext[:, -(SCONV_WIDTH - 1):], p_ext[:, -POOL_HIST:]


def swiglu(x, wg, wu, wd):
    return (jax.nn.silu(x @ wg) * (x @ wu)) @ wd


def setup_inputs(seed: int = 0) -> dict:
    key = jax.random.key(seed)
    ks = iter(jax.random.split(key, 40))

    def nrm(shape, scale=1.0):
        return jax.random.normal(next(ks), shape, jnp.float32) * scale

    n_pages = PAST_LEN // PAGE_SIZE
    n_used = DEC_BATCH * n_pages
    n_phys = n_used + max(1, n_used // 4)
    page_table = jax.random.permutation(next(ks), n_phys)[:n_used].reshape(DEC_BATCH, n_pages).astype(jnp.int32)
    d_in = D_MODEL ** -0.5
    return {
        "x_prompt": nrm((BATCH, SEQ, D_MODEL)),
        "x_sample": nrm((DEC_BATCH, DEC_SEQ, D_MODEL)),
        "cache_k": nrm((n_phys, N_EVEN, PAGE_SIZE, N_HEADS_B, HEAD_DIM_B)),
        "cache_v": nrm((n_phys, N_EVEN, PAGE_SIZE, N_HEADS_B, HEAD_DIM_B)),
        "cache_logf": jax.nn.log_sigmoid(FORGET_BIAS_INIT + nrm((n_phys, N_EVEN, PAGE_SIZE, N_HEADS_B))),
        "page_table": page_table,
        "state_conv_a": nrm((DEC_BATCH, N_EVEN, CONV_A_WIDTH - 1, CONV_A_CH)),
        "state_sconv": nrm((DEC_BATCH, N_ODD, SCONV_WIDTH - 1, SCONV_CH)),
        "state_pool": nrm((DEC_BATCH, N_ODD, POOL_HIST, POOL_CH)),
        "meta_tokens": nrm((N_META, D_MODEL)),
        "w_in_even": nrm((N_EVEN, D_MODEL, EVEN_IN), d_in),
        "b_forget": FORGET_BIAS_INIT + nrm((N_EVEN, N_HEADS_B), 0.5),
        "w_dw_a": nrm((N_EVEN, CONV_A_WIDTH, CONV_A_CH), CONV_A_WIDTH ** -0.5),
        "b_dw_a": nrm((N_EVEN, CONV_A_CH), 0.02),
        "ln_a_g": 1.0 + nrm((N_EVEN, CONV_A_CH), 0.02),
        "ln_a_b": nrm((N_EVEN, CONV_A_CH), 0.02),
        "w_out_even": nrm((N_EVEN, EVEN_MIX, D_MODEL), EVEN_MIX ** -0.5 * DEEPNORM_BETA),
        "w_in_odd": nrm((N_ODD, D_MODEL, ODD_IN), d_in),
        "w_sconv": nrm((N_ODD, SCONV_WIDTH, SCONV_CH), SCONV_WIDTH ** -0.5),
        "w_pool_mix": nrm((N_ODD, N_POOL_GROUPS, POOL_GROUP_CH, POOL_GROUP_CH), POOL_GROUP_CH ** -0.5),
        "pool_scale": 1.0 + nrm((N_ODD, POOL_CH), 0.1),
        "w_out_odd": nrm((N_ODD, ODD_MIX, D_MODEL), ODD_MIX ** -0.5 * DEEPNORM_BETA),
        "ln_mix_g": 1.0 + nrm((DEPTH, D_MODEL), 0.02),
        "ln_mix_b": nrm((DEPTH, D_MODEL), 0.02),
        "w_ffn_gate": nrm((DEPTH, D_MODEL, D_FF), d_in),
        "w_ffn_up": nrm((DEPTH, D_MODEL, D_FF), d_in),
        "w_ffn_down": nrm((DEPTH, D_FF, D_MODEL), D_FF ** -0.5 * DEEPNORM_BETA),
        "ln_ffn_g": 1.0 + nrm((DEPTH, D_MODEL), 0.02),
        "ln_ffn_b": nrm((DEPTH, D_MODEL), 0.02),
    }


def reference(x_prompt, x_sample, cache_k, cache_v, cache_logf, page_table, state_conv_a, state_sconv,
              state_pool, meta_tokens, w_in_even, b_forget, w_dw_a, b_dw_a, ln_a_g, ln_a_b, w_out_even,
              w_in_odd, w_sconv, w_pool_mix, pool_scale, w_out_odd, ln_mix_g, ln_mix_b, w_ffn_gate,
              w_ffn_up, w_ffn_down, ln_ffn_g, ln_ffn_b):
    b = x_prompt.shape[0]
    db, ds = x_sample.shape[:2]
    n_pages = page_table.shape[1]
    past = n_pages * cache_k.shape[2]
    xp = jnp.concatenate([jnp.broadcast_to(meta_tokens[None].astype(x_prompt.dtype), (b, N_META, D_MODEL)),
                          x_prompt], axis=1)
    xs = x_sample
    pos_s_q = past + jnp.arange(ds)
    pos_s_k = jnp.arange(past + ds)

    kp, vp, lfp, cap, scp, plp = [], [], [], [], [], []
    ks_, vs_, lfs, cas, scs, pls = [], [], [], [], [], []
    for layer in range(DEPTH):
        if layer % 2 == 0:
            i = layer // 2
            a, q, k, v, lf = even_projections(xp, w_in_even[i], b_forget[i])
            a_ext = jnp.concatenate([jnp.zeros((b, CONV_A_WIDTH - 1, CONV_A_CH), a.dtype), a], axis=1)
            conv_out = conformer_tail(a_ext, w_dw_a[i], b_dw_a[i], ln_a_g[i], ln_a_b[i])
            att = fox_prompt(q, k, v, lf)
            mix_p = jnp.concatenate([conv_out, att.reshape(b, -1, ATT_B_CH)], axis=-1) @ w_out_even[i]
            kp.append(k); vp.append(v); lfp.append(lf); cap.append(a_ext[:, -(CONV_A_WIDTH - 1):])
            a, q, k, v, lf = even_projections(xs, w_in_even[i], b_forget[i])
            a_ext = jnp.concatenate([state_conv_a[:, i].astype(a.dtype), a], axis=1)
            conv_out = conformer_tail(a_ext, w_dw_a[i], b_dw_a[i], ln_a_g[i], ln_a_b[i])
            k_past = cache_k[page_table, i].reshape(db, past, N_HEADS_B, HEAD_DIM_B)
            v_past = cache_v[page_table, i].reshape(db, past, N_HEADS_B, HEAD_DIM_B)
            lf_past = cache_logf[page_table, i].reshape(db, past, N_HEADS_B)
            k_all = jnp.concatenate([k_past.astype(k.dtype), k], axis=1)
            v_all = jnp.concatenate([v_past.astype(v.dtype), v], axis=1)
            c_all = jnp.cumsum(jnp.concatenate([lf_past.astype(jnp.float32), lf.astype(jnp.float32)], axis=1), axis=1)
            att = fox_attend(q, c_all[:, past:], pos_s_q, k_all, v_all, c_all, pos_s_k)
            mix_s = jnp.concatenate([conv_out, att.reshape(db, ds, ATT_B_CH)], axis=-1) @ w_out_even[i]
            ks_.append(k); vs_.append(v); lfs.append(lf); cas.append(a_ext[:, -(CONV_A_WIDTH - 1):])
        else:
            i = layer // 2
            zs = jnp.zeros((b, SCONV_WIDTH - 1, SCONV_CH), xp.dtype)
            zp = jnp.zeros((b, POOL_HIST, POOL_CH), xp.dtype)
            mix_p, s_new, p_new = odd_mixer(xp, zs, zp, 0, w_in_odd[i], w_sconv[i], w_pool_mix[i],
                                            pool_scale[i], w_out_odd[i])
            scp.append(s_new); plp.append(p_new)
            mix_s, s_new, p_new = odd_mixer(xs, state_sconv[:, i].astype(xs.dtype), state_pool[:, i].astype(xs.dtype),
                                            POOL_HIST, w_in_odd[i], w_sconv[i], w_pool_mix[i],
                                            pool_scale[i], w_out_odd[i])
            scs.append(s_new); pls.append(p_new)
        xp = layer_norm(DEEPNORM_ALPHA * xp + mix_p, ln_mix_g[layer], ln_mix_b[layer])
        xs = layer_norm(DEEPNORM_ALPHA * xs + mix_s, ln_mix_g[layer], ln_mix_b[layer])
        xp = layer_norm(DEEPNORM_ALPHA * xp + swiglu(xp, w_ffn_gate[layer], w_ffn_up[layer], w_ffn_down[layer]),
                        ln_ffn_g[layer], ln_ffn_b[layer])
        xs = layer_norm(DEEPNORM_ALPHA * xs + swiglu(xs, w_ffn_gate[layer], w_ffn_up[layer], w_ffn_down[layer]),
                        ln_ffn_g[layer], ln_ffn_b[layer])

    y_prompt = xp[:, N_META:]
    y_sample = xs
    k_prompt = jnp.stack(kp, axis=1)
    v_prompt = jnp.stack(vp, axis=1)
    logf_prompt = jnp.stack(lfp, axis=1)
    conv_a_prompt = jnp.stack(cap, axis=1)
    sconv_prompt = jnp.stack(scp, axis=1)
    pool_prompt = jnp.stack(plp, axis=1)
    k_sample = jnp.stack(ks_, axis=1)
    v_sample = jnp.stack(vs_, axis=1)
    logf_sample = jnp.stack(lfs, axis=1)
    conv_a_sample = jnp.stack(cas, axis=1)
    sconv_sample = jnp.stack(scs, axis=1)
    pool_sample = jnp.stack(pls, axis=1)
    return (y_prompt, y_sample, k_prompt, v_prompt, logf_prompt, conv_a_prompt, sconv_prompt, pool_prompt,
            k_sample, v_sample, logf_sample, conv_a_sample, sconv_sample, pool_sample)
```

```python
import functools

import jax
import jax.numpy as jnp
from jax import lax
from jax.experimental import pallas as pl
from jax.experimental.pallas import tpu as pltpu

N_META = 16
N_HEADS = 8
HEAD_DIM = 64
CONV_W = 31
CONV_BACK = 32
SCONV_W = 3
POOL_WINDOWS = (2, 4, 8, 16)
POOL_HIST = max(POOL_WINDOWS) - 1
LN_EPS = 1e-5
SEQ_TILE = 256
LANES = 128
SUBLANES = 8
AUX_ONES = 24
VMEM_LIMIT = 56 * 1024 * 1024
NEG = -1e30
F32 = jnp.float32
BF16 = jnp.bfloat16
NT_DIMS = (((1,), (1,)), ((), ()))


def _params(*sem):
    return pltpu.CompilerParams(dimension_semantics=sem, vmem_limit_bytes=VMEM_LIMIT)


def _dot(a, b):
    return jnp.dot(a, b, preferred_element_type=F32)


def _ln(z, g, b):
    mu = jnp.mean(z, axis=-1, keepdims=True)
    zc = z - mu
    var = jnp.mean(zc * zc, axis=-1, keepdims=True)
    return zc * lax.rsqrt(var + LN_EPS) * g + b


def _silu(x):
    return x * jax.nn.sigmoid(x)


def _full(shape):
    return pl.BlockSpec(shape, lambda *_: (0,) * len(shape))


def _even_in_kernel(x_ref, w_ref, bf_ref, a_ref, q_ref, k_ref, v_ref, lf_ref, *, ca, att):
    xb = x_ref[...].astype(BF16)

    def mm(lo, hi):
        return _dot(xb, w_ref[:, lo:hi])

    u = mm(0, ca)
    g = mm(ca, 2 * ca)
    a_ref[...] = u * jax.nn.sigmoid(g)
    o = 2 * ca
    q_ref[...] = mm(o, o + att) * (HEAD_DIM ** -0.5)
    k_ref[...] = mm(o + att, o + 2 * att)
    v_ref[...] = mm(o + 2 * att, o + 3 * att)
    z = mm(o + 3 * att, o + 3 * att + LANES) + bf_ref[...]
    lf = jnp.minimum(z, 0.0) - jnp.log1p(jnp.exp(-jnp.abs(z)))
    lane = lax.broadcasted_iota(jnp.int32, lf.shape, 1)
    lf_ref[...] = jnp.where(lane < N_HEADS, lf, 0.0)


def _even_in(x, w, bf, *, tm, ca, att):
    n, d = x.shape
    row = lambda c: pl.BlockSpec((tm, c), lambda i: (i, 0))
    return pl.pallas_call(
        functools.partial(_even_in_kernel, ca=ca, att=att),
        grid=(n // tm,),
        in_specs=[row(d), _full(w.shape), _full(bf.shape)],
        out_specs=[row(ca), row(att), row(att), row(att), row(LANES)],
        out_shape=[jax.ShapeDtypeStruct((n, c), F32) for c in (ca, att, att, att, LANES)],
        compiler_params=_params("parallel"),
        name="even_in_proj",
    )(x, w, bf)


def _odd_in_kernel(x_ref, w_ref, y_ref, bg_ref, pd_ref, *, c):
    xb = x_ref[...].astype(BF16)
    hc = _dot(xb, w_ref[:, 0:c])
    bg_ref[...] = _dot(xb, w_ref[:, c:2 * c])
    cg = _dot(xb, w_ref[:, 2 * c:3 * c])
    y_ref[...] = cg * hc
    pd_ref[...] = _dot(xb, w_ref[:, 3 * c:4 * c])


def _odd_in(x, w, *, tm, c):
    n, d = x.shape
    row = lambda cc: pl.BlockSpec((tm, cc), lambda i: (i, 0))
    return pl.pallas_call(
        functools.partial(_odd_in_kernel, c=c),
        grid=(n // tm,),
        in_specs=[row(d), _full(w.shape)],
        out_specs=[row(c)] * 3,
        out_shape=[jax.ShapeDtypeStruct((n, c), F32)] * 3,
        compiler_params=_params("parallel"),
        name="odd_in_proj",
    )(x, w)


def _even_out_kernel(x_ref, cv_ref, at_ref, ag_ref, ab_ref, w_ref, g_ref, b_ref, o_ref, *, alpha, ca):
    cv = _silu(_ln(cv_ref[...], ag_ref[...], ab_ref[...]))
    mix = _dot(cv.astype(BF16), w_ref[0:ca, :]) + _dot(at_ref[...].astype(BF16), w_ref[ca:, :])
    o_ref[...] = _ln(alpha * x_ref[...] + mix, g_ref[...], b_ref[...])


def _even_out(x, cv, at, ag, ab, w, g, b, *, tm, alpha):
    n, d = x.shape
    ca = cv.shape[1]
    row = lambda c: pl.BlockSpec((tm, c), lambda i: (i, 0))
    return pl.pallas_call(
        functools.partial(_even_out_kernel, alpha=alpha, ca=ca),
        grid=(n // tm,),
        in_specs=[row(d), row(ca), row(at.shape[1]), _full(ag.shape), _full(ab.shape), _full(w.shape),
                  _full(g.shape), _full(b.shape)],
        out_specs=row(d),
        out_shape=jax.ShapeDtypeStruct((n, d), F32),
        compiler_params=_params("parallel"),
        name="even_out_proj",
    )(x, cv, at, ag, ab, w, g, b)


def _odd_out_kernel(x_ref, oc_ref, dd_ref, wm_ref, sc_ref, w_ref, g_ref, b_ref, o_ref, *, alpha, c):
    od = _dot(dd_ref[...].astype(BF16), wm_ref[...]) * sc_ref[...]
    mix = _dot(oc_ref[...].astype(BF16), w_ref[0:c, :]) + _dot(od.astype(BF16), w_ref[c:, :])
    o_ref[...] = _ln(alpha * x_ref[...] + mix, g_ref[...], b_ref[...])


def _odd_out(x, oc, dd, wm, sc, w, g, b, *, tm, alpha):
    n, d = x.shape
    c = oc.shape[1]
    row = lambda cc: pl.BlockSpec((tm, cc), lambda i: (i, 0))
    return pl.pallas_call(
        functools.partial(_odd_out_kernel, alpha=alpha, c=c),
        grid=(n // tm,),
        in_specs=[row(d), row(c), row(c), _full(wm.shape), _full(sc.shape), _full(w.shape),
                  _full(g.shape), _full(b.shape)],
        out_specs=row(d),
        out_shape=jax.ShapeDtypeStruct((n, d), F32),
        compiler_params=_params("parallel"),
        name="odd_out_proj",
    )(x, oc, dd, wm, sc, w, g, b)


def _ffn_kernel(x_ref, wg_ref, wu_ref, wd_ref, g_ref, b_ref, o_ref, h_ref, *, alpha, chunk, nb, sp, fp):
    x = x_ref[...]
    xb = x.astype(BF16)
    for c in range(0, wg_ref.shape[1], chunk):
        gate = _dot(xb, wg_ref[:, c:c + chunk])
        up = _dot(xb, wu_ref[:, c:c + chunk])
        h_ref[:, c:c + chunk] = (_silu(gate) * up).astype(BF16)
    z = alpha * x + _dot(h_ref[...], wd_ref[...])
    out = _ln(z, g_ref[...], b_ref[...])
    tm = x.shape[0]
    r = pl.program_id(0) * tm + lax.broadcasted_iota(jnp.int32, (tm, 1), 0)
    keep = jnp.ones((tm, 1), F32)
    for bi in range(nb):
        keep = jnp.where((r >= bi * sp) & (r < bi * sp + fp), 0.0, keep)
    o_ref[...] = out * keep


def _ffn(x, wg, wu, wd, g, b, *, tm, alpha, nb, sp, fp):
    n, d = x.shape
    dff = wg.shape[1]
    chunk = SEQ_TILE if dff % SEQ_TILE == 0 else dff
    row = pl.BlockSpec((tm, d), lambda i: (i, 0))
    return pl.pallas_call(
        functools.partial(_ffn_kernel, alpha=alpha, chunk=chunk, nb=nb, sp=sp, fp=fp),
        grid=(n // tm,),
        in_specs=[row, _full(wg.shape), _full(wu.shape), _full(wd.shape), _full(g.shape), _full(b.shape)],
        out_specs=row,
        out_shape=jax.ShapeDtypeStruct((n, d), F32),
        scratch_shapes=[pltpu.VMEM((tm, dff), BF16)],
        compiler_params=_params("parallel"),
        name="swiglu_ffn",
    )(x, wg, wu, wd, g, b)


def _conv_tile(win, w_rows):
    rows = win.shape[0]
    t = rows - CONV_BACK
    lead = CONV_BACK - (CONV_W - 1)
    acc = None
    for r in range(SUBLANES):
        rolled = win if r == 0 else pltpu.roll(win, rows - r, axis=0)
        for m in range(CONV_BACK // SUBLANES + 1):
            j = SUBLANES * m + r - lead
            if 0 <= j < CONV_W:
                term = rolled[SUBLANES * m:SUBLANES * m + t] * w_rows[j]
                acc = term if acc is None else acc + term
    return acc


def _prompt_conv_kernel(a_ref, w_ref, b_ref, o_ref, *, n_tiles):
    w_rows = [w_ref[j:j + 1, :] for j in range(CONV_W)]
    bias = b_ref[...]
    c = a_ref.shape[1]
    win0 = jnp.concatenate([jnp.zeros((CONV_BACK, c), F32), a_ref[0:SEQ_TILE, :]], axis=0)
    o_ref[0:SEQ_TILE, :] = _conv_tile(win0, w_rows) + bias

    def body(i, carry):
        s0 = pl.multiple_of(i * SEQ_TILE, SEQ_TILE)
        win = a_ref[pl.ds(s0 - CONV_BACK, SEQ_TILE + CONV_BACK), :]
        o_ref[pl.ds(s0, SEQ_TILE), :] = _conv_tile(win, w_rows) + bias
        return carry

    lax.fori_loop(1, n_tiles, body, 0)


def _prompt_conv(a, w, b, *, nb, sp):
    n, c = a.shape
    blk = pl.BlockSpec((sp, LANES), lambda bi, ci: (bi, ci))
    return pl.pallas_call(
        functools.partial(_prompt_conv_kernel, n_tiles=sp // SEQ_TILE),
        grid=(nb, c // LANES),
        in_specs=[blk, pl.BlockSpec((CONV_W, LANES), lambda bi, ci: (0, ci)),
                  pl.BlockSpec((1, LANES), lambda bi, ci: (0, ci))],
        out_specs=blk,
        out_shape=jax.ShapeDtypeStruct((n, c), F32),
        compiler_params=_params("parallel", "parallel"),
        name="prompt_conv",
    )(a, w, b)


def _sample_conv_kernel(st_ref, a_ref, w_ref, b_ref, cv_in_ref, o_ref, st_out_ref, ext_ref, *, ds):
    del cv_in_ref
    dbt, hist, c = st_ref.shape
    off = CONV_BACK - hist
    ext_ref[:, off:CONV_BACK, :] = st_ref[...]
    ext_ref[:, CONV_BACK:CONV_BACK + ds, :] = a_ref[...].reshape(dbt, ds, c)
    acc = None
    for j in range(CONV_W):
        term = ext_ref[:, off + j:off + j + ds, :] * w_ref[j:j + 1, :]
        acc = term if acc is None else acc + term
    o_ref[...] = (acc + b_ref[...]).reshape(dbt * ds, c)
    st_out_ref[...] = ext_ref[:, off + ds:CONV_BACK + ds, :]


def _sample_conv(state, layer, a, w, b, cv, *, s0, db, ds, dbt):
    n, c = a.shape
    hist = state.shape[2]
    rows = dbt * ds
    row_blk = pl.BlockSpec((rows, c), lambda i: (s0 // rows + i, 0))
    return pl.pallas_call(
        functools.partial(_sample_conv_kernel, ds=ds),
        grid=(db // dbt,),
        in_specs=[pl.BlockSpec((dbt, None, hist, c), lambda i: (i, layer, 0, 0)), row_blk,
                  _full(w.shape), _full(b.shape), pl.BlockSpec(memory_space=pl.ANY)],
        out_specs=[row_blk, pl.BlockSpec((dbt, hist, c), lambda i: (i, 0, 0))],
        out_shape=[jax.ShapeDtypeStruct((n, c), F32), jax.ShapeDtypeStruct((db, hist, c), F32)],
        scratch_shapes=[pltpu.VMEM((dbt, CONV_BACK + ds, c), F32)],
        input_output_aliases={4: 0},
        compiler_params=_params("parallel"),
        name="sample_conv",
    )(state, a, w, b, cv)


def _fox_prep_kernel(q_ref, k_ref, v_ref, lf_ref, qa_ref, ka_ref, va_ref, carry_ref, *, fp):
    s = pl.program_id(1)

    @pl.when(s == 0)
    def _():
        carry_ref[...] = jnp.zeros_like(carry_ref)

    t = SEQ_TILE
    row = lax.broadcasted_iota(jnp.int32, (t, LANES), 0)
    lane = lax.broadcasted_iota(jnp.int32, (t, LANES), 1)
    c = jnp.where(s * t + row >= fp, lf_ref[...], 0.0)
    sh = 1
    while sh < t:
        c = c + jnp.where(row >= sh, pltpu.roll(c, sh, axis=0), 0.0)
        sh *= 2
    c = c + carry_ref[0:1, :]
    carry_ref[0:1, :] = c[t - 1:t, :]

    c1 = c.astype(BF16).astype(F32)
    r1 = c - c1
    c2 = r1.astype(BF16).astype(F32)
    c3 = (r1 - c2).astype(BF16).astype(F32)
    g = jnp.where(lane < N_HEADS, c1,
                  jnp.where(lane < 2 * N_HEADS, pltpu.roll(c2, N_HEADS, axis=1),
                            jnp.where(lane < 3 * N_HEADS, pltpu.roll(c3, 2 * N_HEADS, axis=1), 0.0)))
    ones_grp = (lane >= HEAD_DIM + AUX_ONES) & (lane < HEAD_DIM + 2 * AUX_ONES)
    ck = [jnp.where((lane >= HEAD_DIM) & (lane < HEAD_DIM + AUX_ONES), -pltpu.roll(g, HEAD_DIM, axis=1),
                    jnp.where(ones_grp, 1.0, 0.0))]
    cq = [jnp.where(ones_grp, pltpu.roll(g, HEAD_DIM + AUX_ONES, axis=1), 0.0)]
    ck.append(pltpu.roll(ck[0], HEAD_DIM, axis=1))
    cq.append(pltpu.roll(cq[0], HEAD_DIM, axis=1))

    lane1 = lax.broadcasted_iota(jnp.int32, (1, LANES), 1)
    for h in range(N_HEADS):
        odd = h % 2
        pr = h // 2
        base = 0 if odd else HEAD_DIM
        a = lane1 - base
        mine = (lane1 & (N_HEADS - 1)) == h
        data = ((lane1 >= HEAD_DIM) if odd else (lane1 < HEAD_DIM)).astype(F32)
        sel1 = ((a >= 0) & (a < AUX_ONES) & mine).astype(F32)
        selc = ((a >= AUX_ONES) & (a < 2 * AUX_ONES) & mine).astype(F32)
        sl = slice(pr * LANES, (pr + 1) * LANES)
        qa_ref[h] = (q_ref[:, sl] * data + (cq[odd] * selc + sel1)).astype(BF16)
        ka_ref[h] = (k_ref[:, sl] * data + ck[odd]).astype(BF16)
        va_ref[h] = (v_ref[:, sl] * data).astype(BF16)


def _fox_prep(q, k, v, lf, *, nb, sp, fp):
    ns = sp // SEQ_TILE
    att = q.shape[1]
    row = lambda c: pl.BlockSpec((SEQ_TILE, c), lambda bi, si: (bi * ns + si, 0))
    hm = pl.BlockSpec((None, N_HEADS, SEQ_TILE, LANES), lambda bi, si: (bi, 0, si, 0))
    return pl.pallas_call(
        functools.partial(_fox_prep_kernel, fp=fp),
        grid=(nb, ns),
        in_specs=[row(att), row(att), row(att), row(LANES)],
        out_specs=[hm, hm, hm],
        out_shape=[jax.ShapeDtypeStruct((nb, N_HEADS, sp, LANES), BF16)] * 3,
        scratch_shapes=[pltpu.VMEM((SUBLANES, LANES), F32)],
        compiler_params=_params("parallel", "arbitrary"),
        name="fox_prep",
    )(q, k, v, lf)


def _flash_kernel(q_ref, k_ref, v_ref, o_ref, *, fp, npad):
    i = pl.program_id(2)
    t = SEQ_TILE
    lo = lax.broadcasted_iota(jnp.int32, (t, LANES), 1) < HEAD_DIM
    qs = (q_ref[0], q_ref[1])

    def step(j, carry, masked):
        ms, ls, acc = carry
        ks = pl.multiple_of(j * t, t)
        if masked:
            rowg = i * t + lax.broadcasted_iota(jnp.int32, (t, t), 0)
            colg = j * t + lax.broadcasted_iota(jnp.int32, (t, t), 1)
            ok = (colg >= fp) & (colg <= rowg)
        new_m, new_l, als, pvs = [], [], [], []
        for h in range(2):
            kh = k_ref[h, pl.ds(ks, t), :]
            s = lax.dot_general(qs[h], kh, NT_DIMS, preferred_element_type=F32)
            if masked:
                s = jnp.where(ok, s, NEG)
            mn = jnp.maximum(ms[h], jnp.max(s, axis=1, keepdims=True))
            p = jnp.exp(s - mn)
            al = jnp.exp(ms[h] - mn)
            new_m.append(mn)
            new_l.append(al * ls[h] + jnp.sum(p, axis=1, keepdims=True))
            als.append(al)
            pvs.append(_dot(p.astype(BF16), v_ref[h, pl.ds(ks, t), :]))
        acc = jnp.where(lo, als[0], als[1]) * acc + pvs[0] + pvs[1]
        return (tuple(new_m), tuple(new_l), acc)

    init_m = jnp.full((t, 1), NEG, F32)
    init_l = jnp.zeros((t, 1), F32)
    carry = ((init_m, init_m), (init_l, init_l), jnp.zeros((t, LANES), F32))
    for j in range(npad):
        carry = lax.cond(j <= i, functools.partial(step, j, masked=True), lambda c: c, carry)
    carry = lax.fori_loop(npad, i, functools.partial(step, masked=False), carry)
    carry = lax.cond(i >= npad, functools.partial(step, i, masked=True), lambda c: c, carry)
    _, ls, acc = carry
    o_ref[...] = acc * jnp.where(lo, 1.0 / ls[0], 1.0 / ls[1])


def _flash(qa, ka, va, *, n, fp):
    nb, _, sp, _ = qa.shape
    ns = sp // SEQ_TILE
    npad = -(-fp // SEQ_TILE)
    kv = pl.BlockSpec((None, 2, sp, LANES), lambda bi, pi, qi: (bi, pi, 0, 0))
    return pl.pallas_call(
        functools.partial(_flash_kernel, fp=fp, npad=npad),
        grid=(nb, N_HEADS // 2, ns),
        in_specs=[pl.BlockSpec((None, 2, SEQ_TILE, LANES), lambda bi, pi, qi: (bi, pi, qi, 0)), kv, kv],
        out_specs=pl.BlockSpec((SEQ_TILE, LANES), lambda bi, pi, qi: (bi * ns + qi, pi)),
        out_shape=jax.ShapeDtypeStruct((n, N_HEADS * HEAD_DIM), F32),
        compiler_params=_params("parallel", "parallel", "arbitrary"),
        name="fox_flash",
    )(qa, ka, va)


def _sample_attn_kernel(pt_ref, q_ref, kn_ref, vn_ref, lfn_ref, *rest, n_pages, page):
    del pt_ref
    k_refs = rest[:n_pages]
    v_refs = rest[n_pages:2 * n_pages]
    lf_refs = rest[2 * n_pages:3 * n_pages]
    o_ref = rest[3 * n_pages + 1]
    ds, c = q_ref.shape
    rows = ds * N_HEADS
    q = q_ref[...]
    sub = lax.broadcasted_iota(jnp.int32, (N_HEADS, c), 0)
    lane = lax.broadcasted_iota(jnp.int32, (N_HEADS, c), 1)
    hm = ((lane >= sub * HEAD_DIM) & (lane < (sub + 1) * HEAD_DIM)).astype(F32)
    qbd = jnp.concatenate([jnp.broadcast_to(q[t:t + 1, :], (N_HEADS, c)) * hm for t in range(ds)],
                          axis=0).astype(BF16)
    hm_t = jnp.concatenate([hm] * ds, axis=0)

    groups = [list(range(g, min(g + 2, n_pages))) for g in range(0, n_pages, 2)]
    s_parts = []
    for grp in groups:
        kp = jnp.concatenate([k_refs[r][...] for r in grp], axis=1).astype(BF16)
        s_parts.append(_dot(qbd, kp))
    s_past = jnp.concatenate(s_parts, axis=1)
    past = n_pages * page

    lf = jnp.concatenate([r[...] for r in lf_refs], axis=1)
    lane_p = lax.broadcasted_iota(jnp.int32, (N_HEADS, past), 1)
    suf = lf
    sh = 1
    while sh < past:
        suf = suf + jnp.where(lane_p < past - sh, pltpu.roll(suf, past - sh, axis=1), 0.0)
        sh *= 2
    s_past = s_past + jnp.concatenate([suf - lf] * ds, axis=0)

    pad = jnp.zeros((LANES - ds, c), F32)
    kn = jnp.concatenate([kn_ref[...], pad], axis=0).astype(BF16)
    vn = jnp.concatenate([vn_ref[...], pad], axis=0).astype(BF16)
    s_new = lax.dot_general(qbd, kn, NT_DIMS, preferred_element_type=F32)
    cn = lfn_ref[...]
    sub_n = lax.broadcasted_iota(jnp.int32, cn.shape, 0)
    sh = 1
    while sh < ds:
        cn = cn + jnp.where(sub_n >= sh, pltpu.roll(cn, sh, axis=0), 0.0)
        sh *= 2
    cn_t = jnp.concatenate([cn, jnp.zeros((LANES - ds, LANES), F32)], axis=0).T[0:N_HEADS, :]
    rown = lax.broadcasted_iota(jnp.int32, (rows, LANES), 0)
    u = lax.broadcasted_iota(jnp.int32, (rows, LANES), 1)
    s_new = jnp.where(u * N_HEADS <= rown, s_new - jnp.concatenate([cn_t] * ds, axis=0), NEG)

    m = jnp.maximum(jnp.max(s_past, axis=1, keepdims=True), jnp.max(s_new, axis=1, keepdims=True))
    p_past = jnp.exp(s_past - m)
    p_new = jnp.exp(s_new - m)
    l = jnp.sum(p_past, axis=1, keepdims=True) + jnp.sum(p_new, axis=1, keepdims=True)
    o = _dot(p_new.astype(BF16), vn)
    for gi, grp in enumerate(groups):
        vp = jnp.concatenate([v_refs[r][...] for r in grp], axis=1).astype(BF16)
        lo = grp[0] * page
        o = o + lax.dot_general(p_past[:, lo:lo + len(grp) * page].astype(BF16), vp, NT_DIMS,
                                preferred_element_type=F32)
    o = o * (1.0 / l) * hm_t
    rr = lax.broadcasted_iota(jnp.int32, (ds, rows), 1)
    tt = lax.broadcasted_iota(jnp.int32, (ds, rows), 0)
    pick = ((rr >= tt * N_HEADS) & (rr < (tt + 1) * N_HEADS)).astype(BF16)
    o_ref[...] = _dot(pick, o.astype(BF16))


def _sample_attn(page_table, layer, q, k, v, lf, cache_k, cache_v, cache_lft, att, *, s0, db, ds):
    n, c = q.shape
    n_pages = page_table.shape[1]
    page = cache_k.shape[3]
    row = lambda cc: pl.BlockSpec((ds, cc), lambda i, pt: (s0 // ds + i, 0))

    def paged(shape, r):
        return pl.BlockSpec((None, None) + shape, lambda i, pt: (pt[i * n_pages + r], layer, 0, 0))

    in_specs = ([row(c), row(c), row(c), row(LANES)]
                + [paged((c, page), r) for r in range(n_pages)]
                + [paged((c, page), r) for r in range(n_pages)]
                + [paged((N_HEADS, page), r) for r in range(n_pages)]
                + [pl.BlockSpec(memory_space=pl.ANY)])
    n_in = len(in_specs)
    return pl.pallas_call(
        functools.partial(_sample_attn_kernel, n_pages=n_pages, page=page),
        grid_spec=pltpu.PrefetchScalarGridSpec(
            num_scalar_prefetch=1, grid=(db,), in_specs=in_specs, out_specs=row(c)),
        out_shape=jax.ShapeDtypeStruct((n, c), F32),
        input_output_aliases={n_in: 0},
        compiler_params=_params("parallel"),
        name="sample_paged_attn",
    )(page_table.reshape(-1), q, k, v, lf, *([cache_k] * n_pages), *([cache_v] * n_pages),
      *([cache_lft] * n_pages), att)


def _odd_tile(wy, wp, bg, w_rows, grp, pos0):
    t = bg.shape[0]
    ny = wy.shape[0]
    y0 = wy[SUBLANES:]
    y1 = pltpu.roll(wy, 1, axis=0)[SUBLANES:]
    y2 = pltpu.roll(wy, 2, axis=0)[SUBLANES:]
    out_c = bg * (w_rows[0] * y2 + w_rows[1] * y1 + w_rows[2] * y0)
    del ny
    back = 2 * SUBLANES
    s2 = wp + pltpu.roll(wp, 1, axis=0)
    s4 = s2 + pltpu.roll(s2, 2, axis=0)
    s8 = s4 + pltpu.roll(s4, 4, axis=0)
    s16 = s8 + pltpu.roll(s8, 8, axis=0)
    win = jnp.where(grp == 0, s2, jnp.where(grp == 1, s4, jnp.where(grp == 2, s8, s16)))[back:]
    wsz = jnp.where(grp == 0, POOL_WINDOWS[0],
                    jnp.where(grp == 1, POOL_WINDOWS[1], jnp.where(grp == 2, POOL_WINDOWS[2], POOL_WINDOWS[3])))
    pos = pos0 + lax.broadcasted_iota(jnp.int32, (t, LANES), 0)
    cnt = jnp.maximum(jnp.minimum(wsz, pos + 1), 1).astype(F32)
    d = win / cnt - wp[back:]
    return out_c, d


def _prompt_odd_kernel(y_ref, bg_ref, pd_ref, w_ref, oc_ref, d_ref, *, n_tiles, fp):
    grp = pl.program_id(1)
    w_rows = [w_ref[j:j + 1, :] for j in range(SCONV_W)]
    c = y_ref.shape[1]
    t = SEQ_TILE
    wy0 = jnp.concatenate([jnp.zeros((SUBLANES, c), F32), y_ref[0:t, :]], axis=0)
    wp0 = jnp.concatenate([jnp.zeros((2 * SUBLANES, c), F32), pd_ref[0:t, :]], axis=0)
    oc, d = _odd_tile(wy0, wp0, bg_ref[0:t, :], w_rows, grp, -fp)
    oc_ref[0:t, :] = oc
    d_ref[0:t, :] = d

    def body(i, carry):
        s0 = pl.multiple_of(i * t, t)
        wy = y_ref[pl.ds(s0 - SUBLANES, t + SUBLANES), :]
        wp = pd_ref[pl.ds(s0 - 2 * SUBLANES, t + 2 * SUBLANES), :]
        oc, d = _odd_tile(wy, wp, bg_ref[pl.ds(s0, t), :], w_rows, grp, s0 - fp)
        oc_ref[pl.ds(s0, t), :] = oc
        d_ref[pl.ds(s0, t), :] = d
        return carry

    lax.fori_loop(1, n_tiles, body, 0)


def _prompt_odd(y, bg, pd, w, *, nb, sp, fp):
    n, c = y.shape
    assert c // LANES == len(POOL_WINDOWS)
    blk = pl.BlockSpec((sp, LANES), lambda bi, ci: (bi, ci))
    return pl.pallas_call(
        functools.partial(_prompt_odd_kernel, n_tiles=sp // SEQ_TILE, fp=fp),
        grid=(nb, c // LANES),
        in_specs=[blk, blk, blk, pl.BlockSpec((SCONV_W, LANES), lambda bi, ci: (0, ci))],
        out_specs=[blk, blk],
        out_shape=[jax.ShapeDtypeStruct((n, c), F32)] * 2,
        compiler_params=_params("parallel", "parallel"),
        name="prompt_sconv_pool",
    )(y, bg, pd, w)


def _sample_odd_kernel(ss_ref, sp_ref, y_ref, bg_ref, pd_ref, w_ref, oc_in_ref, d_in_ref,
                       oc_ref, d_ref, ss_out_ref, sp_out_ref, yext_ref, pext_ref, *, ds):
    del oc_in_ref, d_in_ref
    dbt, _, c = ss_ref.shape
    hs = SCONV_W - 1
    y = y_ref[...].reshape(dbt, ds, c)
    pd = pd_ref[...].reshape(dbt, ds, c)
    bg = bg_ref[...].reshape(dbt, ds, c)
    yext_ref[:, SUBLANES - hs:SUBLANES, :] = ss_ref[...]
    yext_ref[:, SUBLANES:SUBLANES + ds, :] = y
    conv = None
    for j in range(SCONV_W):
        lo = SUBLANES - hs + j
        term = yext_ref[:, lo:lo + ds, :] * w_ref[j:j + 1, :]
        conv = term if conv is None else conv + term
    oc_ref[...] = (bg * conv).reshape(dbt * ds, c)
    ss_out_ref[...] = yext_ref[:, SUBLANES + ds - hs:SUBLANES + ds, :]

    base = 2 * SUBLANES
    pext_ref[:, base - POOL_HIST:base, :] = sp_ref[...]
    pext_ref[:, base:base + ds, :] = pd
    run = pd
    sums = {}
    for i in range(1, max(POOL_WINDOWS)):
        run = run + pext_ref[:, base - i:base - i + ds, :]
        if i + 1 in POOL_WINDOWS:
            sums[i + 1] = run
    gc = c // len(POOL_WINDOWS)
    lane = lax.broadcasted_iota(jnp.int32, (dbt, ds, c), 2)
    mean = sums[POOL_WINDOWS[-1]] / float(POOL_WINDOWS[-1])
    for gi in range(len(POOL_WINDOWS) - 2, -1, -1):
        mean = jnp.where(lane < (gi + 1) * gc, sums[POOL_WINDOWS[gi]] / float(POOL_WINDOWS[gi]), mean)
    d_ref[...] = (mean - pd).reshape(dbt * ds, c)
    sp_out_ref[...] = pext_ref[:, base + ds - POOL_HIST:base + ds, :]


def _sample_odd(state_s, state_p, layer, y, bg, pd, w, oc, dd, *, s0, db, ds, dbt):
    n, c = y.shape
    rows = dbt * ds
    row_blk = pl.BlockSpec((rows, c), lambda i: (s0 // rows + i, 0))
    hs = state_s.shape[2]
    hp = state_p.shape[2]
    anyspec = pl.BlockSpec(memory_space=pl.ANY)
    return pl.pallas_call(
        functools.partial(_sample_odd_kernel, ds=ds),
        grid=(db // dbt,),
        in_specs=[pl.BlockSpec((dbt, None, hs, c), lambda i: (i, layer, 0, 0)),
                  pl.BlockSpec((dbt, None, hp, c), lambda i: (i, layer, 0, 0)),
                  row_blk, row_blk, row_blk, _full(w.shape), anyspec, anyspec],
        out_specs=[row_blk, row_blk, pl.BlockSpec((dbt, hs, c), lambda i: (i, 0, 0)),
                   pl.BlockSpec((dbt, hp, c), lambda i: (i, 0, 0))],
        out_shape=[jax.ShapeDtypeStruct((n, c), F32), jax.ShapeDtypeStruct((n, c), F32),
                   jax.ShapeDtypeStruct((db, hs, c), F32), jax.ShapeDtypeStruct((db, hp, c), F32)],
        scratch_shapes=[pltpu.VMEM((dbt, SUBLANES + ds, c), F32), pltpu.VMEM((dbt, 2 * SUBLANES + ds, c), F32)],
        input_output_aliases={6: 0, 7: 1},
        compiler_params=_params("parallel"),
        name="sample_sconv_pool",
    )(state_s, state_p, y, bg, pd, w, oc, dd)


def kernel(x_prompt, x_sample, cache_k, cache_v, cache_logf, page_table, state_conv_a, state_sconv, state_pool, meta_tokens, w_in_even, b_forget, w_dw_a, b_dw_a, ln_a_g, ln_a_b, w_out_even, w_in_odd, w_sconv, w_pool_mix, pool_scale, w_out_odd, ln_mix_g, ln_mix_b, w_ffn_gate, w_ffn_up, w_ffn_down, ln_ffn_g, ln_ffn_b):
    nb, seq, d = x_prompt.shape
    db, ds, _ = x_sample.shape
    depth = w_ffn_gate.shape[0]
    alpha = float((2 * depth) ** 0.25)
    ca = state_conv_a.shape[-1]
    att = N_HEADS * HEAD_DIM
    assert ds == SUBLANES and cache_k.shape[3] == N_HEADS and cache_k.shape[4] == HEAD_DIM
    assert state_conv_a.shape[2] == CONV_W - 1 and state_pool.shape[2] == POOL_HIST
    assert meta_tokens.shape[0] == N_META

    s_real = N_META + seq
    fp = (-s_real) % SEQ_TILE
    if fp < CONV_BACK:
        fp += SEQ_TILE
    sp = fp + s_real
    s0 = nb * sp
    n = s0 + db * ds
    tm = next(t for t in (512, 256, 128, 64, 32, 16, 8) if n % t == 0)
    dbt = next(t for t in (8, 4, 2, 1) if db % t == 0)

    xp = jnp.concatenate([jnp.zeros((nb, fp, d), F32),
                          jnp.broadcast_to(meta_tokens[None].astype(F32), (nb, N_META, d)),
                          x_prompt], axis=1)
    x = jnp.concatenate([xp.reshape(s0, d), x_sample.reshape(db * ds, d)], axis=0)

    n_phys = cache_k.shape[0]
    n_even = cache_k.shape[1]
    page = cache_k.shape[2]
    ck = jnp.transpose(cache_k, (0, 1, 3, 4, 2)).reshape(n_phys, n_even, att, page)
    cv = jnp.transpose(cache_v, (0, 1, 3, 4, 2)).reshape(n_phys, n_even, att, page)
    clft = jnp.swapaxes(cache_logf, 2, 3)

    row2 = lambda v: v.reshape(1, -1).astype(F32)

    def prompt_rows(arr, lo, hi):
        return arr[:s0].reshape(nb, sp, -1)[:, lo:hi]

    def sample_rows(arr):
        return arr[s0:].reshape(db, ds, -1)

    kp, vp, lfp, cap, scp, plp = [], [], [], [], [], []
    ks_, vs_, lfs, cas, scs, pls = [], [], [], [], [], []
    for layer in range(depth):
        i = layer // 2
        if layer % 2 == 0:
            w_in = jnp.pad(w_in_even[i], ((0, 0), (0, LANES - N_HEADS))).astype(BF16)
            bf = jnp.pad(b_forget[i].astype(F32), (0, LANES - N_HEADS)).reshape(1, LANES)
            a, q, k, v, lf = _even_in(x, w_in, bf, tm=tm, ca=ca, att=att)
            conv = _prompt_conv(a, w_dw_a[i].astype(F32), row2(b_dw_a[i]), nb=nb, sp=sp)
            conv, st_a = _sample_conv(state_conv_a.astype(F32), i, a, w_dw_a[i].astype(F32), row2(b_dw_a[i]),
                                      conv, s0=s0, db=db, ds=ds, dbt=dbt)
            qa, ka, va = _fox_prep(q, k, v, lf, nb=nb, sp=sp, fp=fp)
            at = _flash(qa, ka, va, n=n, fp=fp)
            at = _sample_attn(page_table, i, q, k, v, lf, ck, cv, clft, at, s0=s0, db=db, ds=ds)
            x = _even_out(x, conv, at, row2(ln_a_g[i]), row2(ln_a_b[i]), w_out_even[i].astype(BF16),
                          row2(ln_mix_g[layer]), row2(ln_mix_b[layer]), tm=tm, alpha=alpha)
            kp.append(prompt_rows(k, fp, sp).reshape(nb, s_real, N_HEADS, HEAD_DIM))
            vp.append(prompt_rows(v, fp, sp).reshape(nb, s_real, N_HEADS, HEAD_DIM))
            lfp.append(prompt_rows(lf, fp, sp)[..., :N_HEADS])
            cap.append(prompt_rows(a, sp - (CONV_W - 1), sp))
            ks_.append(sample_rows(k).reshape(db, ds, N_HEADS, HEAD_DIM))
            vs_.append(sample_rows(v).reshape(db, ds, N_HEADS, HEAD_DIM))
            lfs.append(sample_rows(lf)[..., :N_HEADS])
            cas.append(st_a)
        else:
            c = state_sconv.shape[-1]
            y, bg, pd = _odd_in(x, w_in_odd[i].astype(BF16), tm=tm, c=c)
            oc, dd = _prompt_odd(y, bg, pd, w_sconv[i].astype(F32), nb=nb, sp=sp, fp=fp)
            oc, dd, st_s, st_p = _sample_odd(state_sconv.astype(F32), state_pool.astype(F32), i, y, bg, pd,
                                             w_sconv[i].astype(F32), oc, dd, s0=s0, db=db, ds=ds, dbt=dbt)
            wm = jax.scipy.linalg.block_diag(*[w_pool_mix[i, g] for g in range(w_pool_mix.shape[1])]).astype(BF16)
            x = _odd_out(x, oc, dd, wm, row2(pool_scale[i]), w_out_odd[i].astype(BF16),
                         row2(ln_mix_g[layer]), row2(ln_mix_b[layer]), tm=tm, alpha=alpha)
            scp.append(prompt_rows(y, sp - (SCONV_W - 1), sp))
            plp.append(prompt_rows(pd, sp - POOL_HIST, sp))
            scs.append(st_s)
            pls.append(st_p)
        x = _ffn(x, w_ffn_gate[layer].astype(BF16), w_ffn_up[layer].astype(BF16), w_ffn_down[layer].astype(BF16),
                 row2(ln_ffn_g[layer]), row2(ln_ffn_b[layer]), tm=tm, alpha=alpha, nb=nb, sp=sp, fp=fp)

    y_prompt = prompt_rows(x, fp + N_META, sp)
    y_sample = sample_rows(x)
    st = lambda xs: jnp.stack(xs, axis=1)
    return (y_prompt, y_sample, st(kp), st(vp), st(lfp), st(cap), st(scp), st(plp),
            st(ks_), st(vs_), st(lfs), st(cas), st(scs), st(pls))
```

```python
import functools

import jax
import jax.numpy as jnp
from jax import lax
from jax.experimental import pallas as pl
from jax.experimental.pallas import tpu as pltpu

N_META = 16
N_HEADS = 8
HEAD_DIM = 64
CONV_W = 31
CONV_BACK = 32
SCONV_W = 3
POOL_WINDOWS = (2, 4, 8, 16)
POOL_HIST = max(POOL_WINDOWS) - 1
LN_EPS = 1e-5
SEQ_TILE = 256
LANES = 128
SUBLANES = 8
AUX_ONES = 24
FLASH_HEADS = 4
VMEM_LIMIT = 56 * 1024 * 1024
NEG = -1e30
LOG2E = 1.4426950408889634
F32 = jnp.float32
BF16 = jnp.bfloat16
NT_DIMS = (((1,), (1,)), ((), ()))


def _params(*sem):
    return pltpu.CompilerParams(dimension_semantics=sem, vmem_limit_bytes=VMEM_LIMIT)


def _dot(a, b):
    return jnp.dot(a, b, preferred_element_type=F32)


def _ln(z, g, b):
    mu = jnp.mean(z, axis=-1, keepdims=True)
    zc = z - mu
    var = jnp.mean(zc * zc, axis=-1, keepdims=True)
    return zc * lax.rsqrt(var + LN_EPS) * g + b


def _silu(x):
    return x * jax.nn.sigmoid(x)


def _full(shape):
    return pl.BlockSpec(shape, lambda *_: (0,) * len(shape))


def _even_in_kernel(x_ref, w_ref, bf_ref, a_ref, q_ref, k_ref, v_ref, lf_ref, *, ca, att):
    xb = x_ref[...].astype(BF16)

    def mm(lo, hi):
        return _dot(xb, w_ref[:, lo:hi])

    u = mm(0, ca)
    g = mm(ca, 2 * ca)
    a_ref[...] = u * jax.nn.sigmoid(g)
    o = 2 * ca
    q_ref[...] = mm(o, o + att) * (HEAD_DIM ** -0.5)
    k_ref[...] = mm(o + att, o + 2 * att)
    v_ref[...] = mm(o + 2 * att, o + 3 * att)
    z = mm(o + 3 * att, o + 3 * att + LANES) + bf_ref[...]
    lf = jnp.minimum(z, 0.0) - jnp.log1p(jnp.exp(-jnp.abs(z)))
    lane = lax.broadcasted_iota(jnp.int32, lf.shape, 1)
    lf_ref[...] = jnp.where(lane < N_HEADS, lf, 0.0)


def _even_in(x, w, bf, *, tm, ca, att):
    n, d = x.shape
    row = lambda c: pl.BlockSpec((tm, c), lambda i: (i, 0))
    return pl.pallas_call(
        functools.partial(_even_in_kernel, ca=ca, att=att),
        grid=(n // tm,),
        in_specs=[row(d), _full(w.shape), _full(bf.shape)],
        out_specs=[row(ca), row(att), row(att), row(att), row(LANES)],
        out_shape=[jax.ShapeDtypeStruct((n, c), F32) for c in (ca, att, att, att, LANES)],
        compiler_params=_params("parallel"),
        name="even_in_proj",
    )(x, w, bf)


def _odd_in_kernel(x_ref, w_ref, y_ref, bg_ref, pd_ref, *, c):
    xb = x_ref[...].astype(BF16)
    hc = _dot(xb, w_ref[:, 0:c])
    bg_ref[...] = _dot(xb, w_ref[:, c:2 * c])
    cg = _dot(xb, w_ref[:, 2 * c:3 * c])
    y_ref[...] = cg * hc
    pd_ref[...] = _dot(xb, w_ref[:, 3 * c:4 * c])


def _odd_in(x, w, *, tm, c):
    n, d = x.shape
    row = lambda cc: pl.BlockSpec((tm, cc), lambda i: (i, 0))
    return pl.pallas_call(
        functools.partial(_odd_in_kernel, c=c),
        grid=(n // tm,),
        in_specs=[row(d), _full(w.shape)],
        out_specs=[row(c)] * 3,
        out_shape=[jax.ShapeDtypeStruct((n, c), F32)] * 3,
        compiler_params=_params("parallel"),
        name="odd_in_proj",
    )(x, w)


def _even_out_kernel(x_ref, cv_ref, at_ref, ag_ref, ab_ref, w_ref, g_ref, b_ref, o_ref, *, alpha, ca):
    cv = _silu(_ln(cv_ref[...], ag_ref[...], ab_ref[...]))
    mix = _dot(cv.astype(BF16), w_ref[0:ca, :]) + _dot(at_ref[...].astype(BF16), w_ref[ca:, :])
    o_ref[...] = _ln(alpha * x_ref[...] + mix, g_ref[...], b_ref[...])


def _even_out(x, cv, at, ag, ab, w, g, b, *, tm, alpha):
    n, d = x.shape
    ca = cv.shape[1]
    row = lambda c: pl.BlockSpec((tm, c), lambda i: (i, 0))
    return pl.pallas_call(
        functools.partial(_even_out_kernel, alpha=alpha, ca=ca),
        grid=(n // tm,),
        in_specs=[row(d), row(ca), row(at.shape[1]), _full(ag.shape), _full(ab.shape), _full(w.shape),
                  _full(g.shape), _full(b.shape)],
        out_specs=row(d),
        out_shape=jax.ShapeDtypeStruct((n, d), F32),
        compiler_params=_params("parallel"),
        name="even_out_proj",
    )(x, cv, at, ag, ab, w, g, b)


def _odd_out_kernel(x_ref, oc_ref, dd_ref, wm_ref, sc_ref, w_ref, g_ref, b_ref, o_ref, *, alpha, c):
    od = _dot(dd_ref[...].astype(BF16), wm_ref[...]) * sc_ref[...]
    mix = _dot(oc_ref[...].astype(BF16), w_ref[0:c, :]) + _dot(od.astype(BF16), w_ref[c:, :])
    o_ref[...] = _ln(alpha * x_ref[...] + mix, g_ref[...], b_ref[...])


def _odd_out(x, oc, dd, wm, sc, w, g, b, *, tm, alpha):
    n, d = x.shape
    c = oc.shape[1]
    row = lambda cc: pl.BlockSpec((tm, cc), lambda i: (i, 0))
    return pl.pallas_call(
        functools.partial(_odd_out_kernel, alpha=alpha, c=c),
        grid=(n // tm,),
        in_specs=[row(d), row(c), row(c), _full(wm.shape), _full(sc.shape), _full(w.shape),
                  _full(g.shape), _full(b.shape)],
        out_specs=row(d),
        out_shape=jax.ShapeDtypeStruct((n, d), F32),
        compiler_params=_params("parallel"),
        name="odd_out_proj",
    )(x, oc, dd, wm, sc, w, g, b)


def _ffn_kernel(x_ref, wg_ref, wu_ref, wd_ref, g_ref, b_ref, o_ref, h_ref, *, alpha, chunk, nb, sp, fp):
    x = x_ref[...]
    xb = x.astype(BF16)
    for c in range(0, wg_ref.shape[1], chunk):
        gate = _dot(xb, wg_ref[:, c:c + chunk])
        up = _dot(xb, wu_ref[:, c:c + chunk])
        h_ref[:, c:c + chunk] = (_silu(gate) * up).astype(BF16)
    z = alpha * x + _dot(h_ref[...], wd_ref[...])
    out = _ln(z, g_ref[...], b_ref[...])
    tm = x.shape[0]
    r = pl.program_id(0) * tm + lax.broadcasted_iota(jnp.int32, (tm, 1), 0)
    keep = jnp.ones((tm, 1), F32)
    for bi in range(nb):
        keep = jnp.where((r >= bi * sp) & (r < bi * sp + fp), 0.0, keep)
    o_ref[...] = out * keep


def _ffn(x, wg, wu, wd, g, b, *, tm, alpha, nb, sp, fp):
    n, d = x.shape
    dff = wg.shape[1]
    chunk = SEQ_TILE if dff % SEQ_TILE == 0 else dff
    row = pl.BlockSpec((tm, d), lambda i: (i, 0))
    return pl.pallas_call(
        functools.partial(_ffn_kernel, alpha=alpha, chunk=chunk, nb=nb, sp=sp, fp=fp),
        grid=(n // tm,),
        in_specs=[row, _full(wg.shape), _full(wu.shape), _full(wd.shape), _full(g.shape), _full(b.shape)],
        out_specs=row,
        out_shape=jax.ShapeDtypeStruct((n, d), F32),
        scratch_shapes=[pltpu.VMEM((tm, dff), BF16)],
        compiler_params=_params("parallel"),
        name="swiglu_ffn",
    )(x, wg, wu, wd, g, b)


def _conv_tile(win, w_rows):
    rows = win.shape[0]
    t = rows - CONV_BACK
    lead = CONV_BACK - (CONV_W - 1)
    acc = None
    for r in range(SUBLANES):
        rolled = win if r == 0 else pltpu.roll(win, rows - r, axis=0)
        for m in range(CONV_BACK // SUBLANES + 1):
            j = SUBLANES * m + r - lead
            if 0 <= j < CONV_W:
                term = rolled[SUBLANES * m:SUBLANES * m + t] * w_rows[j]
                acc = term if acc is None else acc + term
    return acc


def _prompt_conv_kernel(a_ref, w_ref, b_ref, o_ref, *, n_tiles):
    w_rows = [w_ref[j:j + 1, :] for j in range(CONV_W)]
    bias = b_ref[...]
    c = a_ref.shape[1]
    win0 = jnp.concatenate([jnp.zeros((CONV_BACK, c), F32), a_ref[0:SEQ_TILE, :]], axis=0)
    o_ref[0:SEQ_TILE, :] = _conv_tile(win0, w_rows) + bias

    def body(i, carry):
        s0 = pl.multiple_of(i * SEQ_TILE, SEQ_TILE)
        win = a_ref[pl.ds(s0 - CONV_BACK, SEQ_TILE + CONV_BACK), :]
        o_ref[pl.ds(s0, SEQ_TILE), :] = _conv_tile(win, w_rows) + bias
        return carry

    lax.fori_loop(1, n_tiles, body, 0)


def _prompt_conv(a, w, b, *, nb, sp):
    n, c = a.shape
    blk = pl.BlockSpec((sp, LANES), lambda bi, ci: (bi, ci))
    return pl.pallas_call(
        functools.partial(_prompt_conv_kernel, n_tiles=sp // SEQ_TILE),
        grid=(nb, c // LANES),
        in_specs=[blk, pl.BlockSpec((CONV_W, LANES), lambda bi, ci: (0, ci)),
                  pl.BlockSpec((1, LANES), lambda bi, ci: (0, ci))],
        out_specs=blk,
        out_shape=jax.ShapeDtypeStruct((n, c), F32),
        compiler_params=_params("parallel", "parallel"),
        name="prompt_conv",
    )(a, w, b)


def _sample_conv_kernel(st_ref, a_ref, w_ref, b_ref, cv_in_ref, o_ref, st_out_ref, ext_ref, *, ds):
    del cv_in_ref
    dbt, hist, c = st_ref.shape
    off = CONV_BACK - hist
    ext_ref[:, off:CONV_BACK, :] = st_ref[...]
    ext_ref[:, CONV_BACK:CONV_BACK + ds, :] = a_ref[...].reshape(dbt, ds, c)
    acc = None
    for j in range(CONV_W):
        term = ext_ref[:, off + j:off + j + ds, :] * w_ref[j:j + 1, :]
        acc = term if acc is None else acc + term
    o_ref[...] = (acc + b_ref[...]).reshape(dbt * ds, c)
    st_out_ref[...] = ext_ref[:, off + ds:CONV_BACK + ds, :]


def _sample_conv(state, layer, a, w, b, cv, *, s0, db, ds, dbt):
    n, c = a.shape
    hist = state.shape[2]
    rows = dbt * ds
    row_blk = pl.BlockSpec((rows, c), lambda i: (s0 // rows + i, 0))
    return pl.pallas_call(
        functools.partial(_sample_conv_kernel, ds=ds),
        grid=(db // dbt,),
        in_specs=[pl.BlockSpec((dbt, None, hist, c), lambda i: (i, layer, 0, 0)), row_blk,
                  _full(w.shape), _full(b.shape), pl.BlockSpec(memory_space=pl.ANY)],
        out_specs=[row_blk, pl.BlockSpec((dbt, hist, c), lambda i: (i, 0, 0))],
        out_shape=[jax.ShapeDtypeStruct((n, c), F32), jax.ShapeDtypeStruct((db, hist, c), F32)],
        scratch_shapes=[pltpu.VMEM((dbt, CONV_BACK + ds, c), F32)],
        input_output_aliases={4: 0},
        compiler_params=_params("parallel"),
        name="sample_conv",
    )(state, a, w, b, cv)


def _fox_prep_kernel(q_ref, k_ref, v_ref, lf_ref, qa_ref, ka_ref, vat_ref, carry_ref, *, fp):
    s = pl.program_id(1)

    @pl.when(s == 0)
    def _():
        carry_ref[...] = jnp.zeros_like(carry_ref)

    t = SEQ_TILE
    row = lax.broadcasted_iota(jnp.int32, (t, LANES), 0)
    lane = lax.broadcasted_iota(jnp.int32, (t, LANES), 1)
    c = jnp.where(s * t + row >= fp, lf_ref[...], 0.0)
    sh = 1
    while sh < t:
        c = c + jnp.where(row >= sh, pltpu.roll(c, sh, axis=0), 0.0)
        sh *= 2
    c = c + carry_ref[0:1, :]
    carry_ref[0:1, :] = c[t - 1:t, :]

    c = c * LOG2E
    c1 = c.astype(BF16).astype(F32)
    r1 = c - c1
    c2 = r1.astype(BF16).astype(F32)
    c3 = (r1 - c2).astype(BF16).astype(F32)
    g23 = jnp.where(lane < 2 * N_HEADS, pltpu.roll(c2, N_HEADS, axis=1),
                    jnp.where(lane < 3 * N_HEADS, pltpu.roll(c3, 2 * N_HEADS, axis=1), 0.0))
    g = jnp.where(lane < N_HEADS, c1, g23)
    gk = jnp.where(lane < N_HEADS, jnp.where(s * t + row >= fp, c1, -NEG), g23)
    ones_grp = (lane >= HEAD_DIM + AUX_ONES) & (lane < HEAD_DIM + 2 * AUX_ONES)
    ck = [jnp.where((lane >= HEAD_DIM) & (lane < HEAD_DIM + AUX_ONES), -pltpu.roll(gk, HEAD_DIM, axis=1),
                    jnp.where(ones_grp, 1.0, 0.0))]
    cq = [jnp.where(ones_grp, pltpu.roll(g, HEAD_DIM + AUX_ONES, axis=1), 0.0)]
    ck.append(pltpu.roll(ck[0], HEAD_DIM, axis=1))
    cq.append(pltpu.roll(cq[0], HEAD_DIM, axis=1))

    lane1 = lax.broadcasted_iota(jnp.int32, (1, LANES), 1)
    for h in range(N_HEADS):
        odd = h % 2
        pr = h // 2
        base = 0 if odd else HEAD_DIM
        a = lane1 - base
        mine = (lane1 & (N_HEADS - 1)) == h
        data = ((lane1 >= HEAD_DIM) if odd else (lane1 < HEAD_DIM)).astype(F32)
        sel1 = ((a >= 0) & (a < AUX_ONES) & mine).astype(F32)
        selc = ((a >= AUX_ONES) & (a < 2 * AUX_ONES) & mine).astype(F32)
        sl = slice(pr * LANES, (pr + 1) * LANES)
        qa_ref[h] = (q_ref[:, sl] * (data * LOG2E) + (cq[odd] * selc + sel1)).astype(BF16)
        ka_ref[h] = (k_ref[:, sl] * data + ck[odd]).astype(BF16)
        vat_ref[h] = (v_ref[:, sl] * data).T.astype(BF16)


def _fox_prep(q, k, v, lf, *, nb, sp, fp):
    ns = sp // SEQ_TILE
    att = q.shape[1]
    row = lambda c: pl.BlockSpec((SEQ_TILE, c), lambda bi, si: (bi * ns + si, 0))
    hm = pl.BlockSpec((None, N_HEADS, SEQ_TILE, LANES), lambda bi, si: (bi, 0, si, 0))
    hm_t = pl.BlockSpec((None, N_HEADS, LANES, SEQ_TILE), lambda bi, si: (bi, 0, 0, si))
    return pl.pallas_call(
        functools.partial(_fox_prep_kernel, fp=fp),
        grid=(nb, ns),
        in_specs=[row(att), row(att), row(att), row(LANES)],
        out_specs=[hm, hm, hm_t],
        out_shape=[jax.ShapeDtypeStruct((nb, N_HEADS, sp, LANES), BF16)] * 2
        + [jax.ShapeDtypeStruct((nb, N_HEADS, LANES, sp), BF16)],
        scratch_shapes=[pltpu.VMEM((SUBLANES, LANES), F32)],
        compiler_params=_params("parallel", "arbitrary"),
        name="fox_prep",
    )(q, k, v, lf)


def _flash_kernel(q_ref, k_ref, vt_ref, o_ref, p_ref, acc_ref, *, hg):
    i = pl.program_id(2)
    t = SEQ_TILE
    top = lax.broadcasted_iota(jnp.int32, (LANES, t), 0) < HEAD_DIM

    def scores(j):
        ks = pl.multiple_of(j * t, t)
        return [lax.dot_general(k_ref[h, pl.ds(ks, t), :], q_ref[h], NT_DIMS, preferred_element_type=F32)
                for h in range(hg)]

    def accumulate(j, als):
        ks = pl.multiple_of(j * t, t)
        pvs = [_dot(vt_ref[h, :, pl.ds(ks, t)], p_ref[h]) for h in range(hg)]
        for pr in range(hg // 2):
            acc_ref[pr] = (jnp.where(top, als[2 * pr], als[2 * pr + 1]) * acc_ref[pr]
                           + pvs[2 * pr] + pvs[2 * pr + 1])

    def softmax(ss, ms, ls, masked):
        new_m, new_l, als = [], [], []
        for h in range(hg):
            s = ss[h]
            if masked:
                visible = (lax.broadcasted_iota(jnp.int32, (t, t), 0) <= lax.broadcasted_iota(jnp.int32, (t, t), 1))
                s = jnp.where(visible, s, NEG)
            mn = jnp.maximum(ms[h], jnp.max(s, axis=0, keepdims=True))
            p = jnp.exp2(s - mn)
            al = jnp.exp2(ms[h] - mn)
            new_m.append(mn)
            new_l.append(al * ls[h] + jnp.sum(p, axis=0, keepdims=True))
            als.append(al)
            p_ref[h] = p.astype(BF16)
        return tuple(new_m), tuple(new_l), tuple(als)

    def step(j, state, masked):
        ms, ls, als = state
        ss = scores(j)
        accumulate(jnp.maximum(j - 1, 0), als)
        return softmax(ss, ms, ls, masked)

    p_ref[...] = jnp.zeros_like(p_ref)
    acc_ref[...] = jnp.zeros_like(acc_ref)
    state = ((jnp.full((1, t), NEG, F32),) * hg, (jnp.zeros((1, t), F32),) * hg, (jnp.ones((1, t), F32),) * hg)
    state = lax.fori_loop(0, i, functools.partial(step, masked=False), state)
    _, ls, als = step(i, state, True)
    accumulate(i, als)
    for pr in range(hg // 2):
        out_t = acc_ref[pr] * jnp.where(top, 1.0 / ls[2 * pr], 1.0 / ls[2 * pr + 1])
        o_ref[:, pr * LANES:(pr + 1) * LANES] = out_t.T


def _flash(qa, ka, vat, *, n, hg):
    nb, _, sp, _ = qa.shape
    ns = sp // SEQ_TILE
    return pl.pallas_call(
        functools.partial(_flash_kernel, hg=hg),
        grid=(nb, N_HEADS // hg, ns),
        in_specs=[pl.BlockSpec((None, hg, SEQ_TILE, LANES), lambda bi, gi, qi: (bi, gi, qi, 0)),
                  pl.BlockSpec((None, hg, sp, LANES), lambda bi, gi, qi: (bi, gi, 0, 0)),
                  pl.BlockSpec((None, hg, LANES, sp), lambda bi, gi, qi: (bi, gi, 0, 0))],
        out_specs=pl.BlockSpec((SEQ_TILE, hg * HEAD_DIM), lambda bi, gi, qi: (bi * ns + qi, gi)),
        out_shape=jax.ShapeDtypeStruct((n, N_HEADS * HEAD_DIM), F32),
        scratch_shapes=[pltpu.VMEM((hg, SEQ_TILE, SEQ_TILE), BF16), pltpu.VMEM((hg // 2, LANES, SEQ_TILE), F32)],
        compiler_params=_params("parallel", "parallel", "arbitrary"),
        name="fox_flash",
    )(qa, ka, vat)


def _sample_attn_kernel(pt_ref, q_ref, kn_ref, vn_ref, lfn_ref, *rest, n_pages, page):
    del pt_ref
    k_refs = rest[:n_pages]
    v_refs = rest[n_pages:2 * n_pages]
    lf_refs = rest[2 * n_pages:3 * n_pages]
    o_ref = rest[3 * n_pages + 1]
    ds, c = q_ref.shape
    rows = ds * N_HEADS
    q = q_ref[...]
    sub = lax.broadcasted_iota(jnp.int32, (N_HEADS, c), 0)
    lane = lax.broadcasted_iota(jnp.int32, (N_HEADS, c), 1)
    hm = ((lane >= sub * HEAD_DIM) & (lane < (sub + 1) * HEAD_DIM)).astype(F32)
    qbd = jnp.concatenate([jnp.broadcast_to(q[t:t + 1, :], (N_HEADS, c)) * hm for t in range(ds)],
                          axis=0).astype(BF16)
    hm_t = jnp.concatenate([hm] * ds, axis=0)

    groups = [list(range(g, min(g + 2, n_pages))) for g in range(0, n_pages, 2)]
    s_parts = []
    for grp in groups:
        kp = jnp.concatenate([k_refs[r][...] for r in grp], axis=1).astype(BF16)
        s_parts.append(_dot(qbd, kp))
    s_past = jnp.concatenate(s_parts, axis=1)
    past = n_pages * page

    lf = jnp.concatenate([r[...] for r in lf_refs], axis=1)
    lane_p = lax.broadcasted_iota(jnp.int32, (N_HEADS, past), 1)
    suf = lf
    sh = 1
    while sh < past:
        suf = suf + jnp.where(lane_p < past - sh, pltpu.roll(suf, past - sh, axis=1), 0.0)
        sh *= 2
    s_past = s_past + jnp.concatenate([suf - lf] * ds, axis=0)

    pad = jnp.zeros((LANES - ds, c), F32)
    kn = jnp.concatenate([kn_ref[...], pad], axis=0).astype(BF16)
    vn = jnp.concatenate([vn_ref[...], pad], axis=0).astype(BF16)
    s_new = lax.dot_general(qbd, kn, NT_DIMS, preferred_element_type=F32)
    cn = lfn_ref[...]
    sub_n = lax.broadcasted_iota(jnp.int32, cn.shape, 0)
    sh = 1
    while sh < ds:
        cn = cn + jnp.where(sub_n >= sh, pltpu.roll(cn, sh, axis=0), 0.0)
        sh *= 2
    cn_t = jnp.concatenate([cn, jnp.zeros((LANES - ds, LANES), F32)], axis=0).T[0:N_HEADS, :]
    rown = lax.broadcasted_iota(jnp.int32, (rows, LANES), 0)
    u = lax.broadcasted_iota(jnp.int32, (rows, LANES), 1)
    s_new = jnp.where(u * N_HEADS <= rown, s_new - jnp.concatenate([cn_t] * ds, axis=0), NEG)

    m = jnp.maximum(jnp.max(s_past, axis=1, keepdims=True), jnp.max(s_new, axis=1, keepdims=True))
    p_past = jnp.exp(s_past - m)
    p_new = jnp.exp(s_new - m)
    l = jnp.sum(p_past, axis=1, keepdims=True) + jnp.sum(p_new, axis=1, keepdims=True)
    o = _dot(p_new.astype(BF16), vn)
    for gi, grp in enumerate(groups):
        vp = jnp.concatenate([v_refs[r][...] for r in grp], axis=1).astype(BF16)
        lo = grp[0] * page
        o = o + lax.dot_general(p_past[:, lo:lo + len(grp) * page].astype(BF16), vp, NT_DIMS,
                                preferred_element_type=F32)
    o = o * (1.0 / l) * hm_t
    rr = lax.broadcasted_iota(jnp.int32, (ds, rows), 1)
    tt = lax.broadcasted_iota(jnp.int32, (ds, rows), 0)
    pick = ((rr >= tt * N_HEADS) & (rr < (tt + 1) * N_HEADS)).astype(BF16)
    o_ref[...] = _dot(pick, o.astype(BF16))


def _sample_attn(page_table, layer, q, k, v, lf, cache_k, cache_v, cache_lft, att, *, s0, db, ds):
    n, c = q.shape
    n_pages = page_table.shape[1]
    page = cache_k.shape[3]
    row = lambda cc: pl.BlockSpec((ds, cc), lambda i, pt: (s0 // ds + i, 0))

    def paged(shape, r):
        return pl.BlockSpec((None, None) + shape, lambda i, pt: (pt[i * n_pages + r], layer, 0, 0))

    in_specs = ([row(c), row(c), row(c), row(LANES)]
                + [paged((c, page), r) for r in range(n_pages)]
                + [paged((c, page), r) for r in range(n_pages)]
                + [paged((N_HEADS, page), r) for r in range(n_pages)]
                + [pl.BlockSpec(memory_space=pl.ANY)])
    n_in = len(in_specs)
    return pl.pallas_call(
        functools.partial(_sample_attn_kernel, n_pages=n_pages, page=page),
        grid_spec=pltpu.PrefetchScalarGridSpec(
            num_scalar_prefetch=1, grid=(db,), in_specs=in_specs, out_specs=row(c)),
        out_shape=jax.ShapeDtypeStruct((n, c), F32),
        input_output_aliases={n_in: 0},
        compiler_params=_params("parallel"),
        name="sample_paged_attn",
    )(page_table.reshape(-1), q, k, v, lf, *([cache_k] * n_pages), *([cache_v] * n_pages),
      *([cache_lft] * n_pages), att)


def _odd_tile(wy, wp, bg, w_rows, grp, pos0):
    t = bg.shape[0]
    ny = wy.shape[0]
    y0 = wy[SUBLANES:]
    y1 = pltpu.roll(wy, 1, axis=0)[SUBLANES:]
    y2 = pltpu.roll(wy, 2, axis=0)[SUBLANES:]
    out_c = bg * (w_rows[0] * y2 + w_rows[1] * y1 + w_rows[2] * y0)
    del ny
    back = 2 * SUBLANES
    s2 = wp + pltpu.roll(wp, 1, axis=0)
    s4 = s2 + pltpu.roll(s2, 2, axis=0)
    s8 = s4 + pltpu.roll(s4, 4, axis=0)
    s16 = s8 + pltpu.roll(s8, 8, axis=0)
    win = jnp.where(grp == 0, s2, jnp.where(grp == 1, s4, jnp.where(grp == 2, s8, s16)))[back:]
    wsz = jnp.where(grp == 0, POOL_WINDOWS[0],
                    jnp.where(grp == 1, POOL_WINDOWS[1], jnp.where(grp == 2, POOL_WINDOWS[2], POOL_WINDOWS[3])))
    pos = pos0 + lax.broadcasted_iota(jnp.int32, (t, LANES), 0)
    cnt = jnp.maximum(jnp.minimum(wsz, pos + 1), 1).astype(F32)
    d = win / cnt - wp[back:]
    return out_c, d


def _prompt_odd_kernel(y_ref, bg_ref, pd_ref, w_ref, oc_ref, d_ref, *, n_tiles, fp):
    grp = pl.program_id(1)
    w_rows = [w_ref[j:j + 1, :] for j in range(SCONV_W)]
    c = y_ref.shape[1]
    t = SEQ_TILE
    wy0 = jnp.concatenate([jnp.zeros((SUBLANES, c), F32), y_ref[0:t, :]], axis=0)
    wp0 = jnp.concatenate([jnp.zeros((2 * SUBLANES, c), F32), pd_ref[0:t, :]], axis=0)
    oc, d = _odd_tile(wy0, wp0, bg_ref[0:t, :], w_rows, grp, -fp)
    oc_ref[0:t, :] = oc
    d_ref[0:t, :] = d

    def body(i, carry):
        s0 = pl.multiple_of(i * t, t)
        wy = y_ref[pl.ds(s0 - SUBLANES, t + SUBLANES), :]
        wp = pd_ref[pl.ds(s0 - 2 * SUBLANES, t + 2 * SUBLANES), :]
        oc, d = _odd_tile(wy, wp, bg_ref[pl.ds(s0, t), :], w_rows, grp, s0 - fp)
        oc_ref[pl.ds(s0, t), :] = oc
        d_ref[pl.ds(s0, t), :] = d
        return carry

    lax.fori_loop(1, n_tiles, body, 0)


def _prompt_odd(y, bg, pd, w, *, nb, sp, fp):
    n, c = y.shape
    assert c // LANES == len(POOL_WINDOWS)
    blk = pl.BlockSpec((sp, LANES), lambda bi, ci: (bi, ci))
    return pl.pallas_call(
        functools.partial(_prompt_odd_kernel, n_tiles=sp // SEQ_TILE, fp=fp),
        grid=(nb, c // LANES),
        in_specs=[blk, blk, blk, pl.BlockSpec((SCONV_W, LANES), lambda bi, ci: (0, ci))],
        out_specs=[blk, blk],
        out_shape=[jax.ShapeDtypeStruct((n, c), F32)] * 2,
        compiler_params=_params("parallel", "parallel"),
        name="prompt_sconv_pool",
    )(y, bg, pd, w)


def _sample_odd_kernel(ss_ref, sp_ref, y_ref, bg_ref, pd_ref, w_ref, oc_in_ref, d_in_ref,
                       oc_ref, d_ref, ss_out_ref, sp_out_ref, yext_ref, pext_ref, *, ds):
    del oc_in_ref, d_in_ref
    dbt, _, c = ss_ref.shape
    hs = SCONV_W - 1
    y = y_ref[...].reshape(dbt, ds, c)
    pd = pd_ref[...].reshape(dbt, ds, c)
    bg = bg_ref[...].reshape(dbt, ds, c)
    yext_ref[:, SUBLANES - hs:SUBLANES, :] = ss_ref[...]
    yext_ref[:, SUBLANES:SUBLANES + ds, :] = y
    conv = None
    for j in range(SCONV_W):
        lo = SUBLANES - hs + j
        term = yext_ref[:, lo:lo + ds, :] * w_ref[j:j + 1, :]
        conv = term if conv is None else conv + term
    oc_ref[...] = (bg * conv).reshape(dbt * ds, c)
    ss_out_ref[...] = yext_ref[:, SUBLANES + ds - hs:SUBLANES + ds, :]

    base = 2 * SUBLANES
    pext_ref[:, base - POOL_HIST:base, :] = sp_ref[...]
    pext_ref[:, base:base + ds, :] = pd
    run = pd
    sums = {}
    for i in range(1, max(POOL_WINDOWS)):
        run = run + pext_ref[:, base - i:base - i + ds, :]
        if i + 1 in POOL_WINDOWS:
            sums[i + 1] = run
    gc = c // len(POOL_WINDOWS)
    lane = lax.broadcasted_iota(jnp.int32, (dbt, ds, c), 2)
    mean = sums[POOL_WINDOWS[-1]] / float(POOL_WINDOWS[-1])
    for gi in range(len(POOL_WINDOWS) - 2, -1, -1):
        mean = jnp.where(lane < (gi + 1) * gc, sums[POOL_WINDOWS[gi]] / float(POOL_WINDOWS[gi]), mean)
    d_ref[...] = (mean - pd).reshape(dbt * ds, c)
    sp_out_ref[...] = pext_ref[:, base + ds - POOL_HIST:base + ds, :]


def _sample_odd(state_s, state_p, layer, y, bg, pd, w, oc, dd, *, s0, db, ds, dbt):
    n, c = y.shape
    rows = dbt * ds
    row_blk = pl.BlockSpec((rows, c), lambda i: (s0 // rows + i, 0))
    hs = state_s.shape[2]
    hp = state_p.shape[2]
    anyspec = pl.BlockSpec(memory_space=pl.ANY)
    return pl.pallas_call(
        functools.partial(_sample_odd_kernel, ds=ds),
        grid=(db // dbt,),
        in_specs=[pl.BlockSpec((dbt, None, hs, c), lambda i: (i, layer, 0, 0)),
                  pl.BlockSpec((dbt, None, hp, c), lambda i: (i, layer, 0, 0)),
                  row_blk, row_blk, row_blk, _full(w.shape), anyspec, anyspec],
        out_specs=[row_blk, row_blk, pl.BlockSpec((dbt, hs, c), lambda i: (i, 0, 0)),
                   pl.BlockSpec((dbt, hp, c), lambda i: (i, 0, 0))],
        out_shape=[jax.ShapeDtypeStruct((n, c), F32), jax.ShapeDtypeStruct((n, c), F32),
                   jax.ShapeDtypeStruct((db, hs, c), F32), jax.ShapeDtypeStruct((db, hp, c), F32)],
        scratch_shapes=[pltpu.VMEM((dbt, SUBLANES + ds, c), F32), pltpu.VMEM((dbt, 2 * SUBLANES + ds, c), F32)],
        input_output_aliases={6: 0, 7: 1},
        compiler_params=_params("parallel"),
        name="sample_sconv_pool",
    )(state_s, state_p, y, bg, pd, w, oc, dd)


def kernel(x_prompt, x_sample, cache_k, cache_v, cache_logf, page_table, state_conv_a, state_sconv, state_pool, meta_tokens, w_in_even, b_forget, w_dw_a, b_dw_a, ln_a_g, ln_a_b, w_out_even, w_in_odd, w_sconv, w_pool_mix, pool_scale, w_out_odd, ln_mix_g, ln_mix_b, w_ffn_gate, w_ffn_up, w_ffn_down, ln_ffn_g, ln_ffn_b):
    nb, seq, d = x_prompt.shape
    db, ds, _ = x_sample.shape
    depth = w_ffn_gate.shape[0]
    alpha = float((2 * depth) ** 0.25)
    ca = state_conv_a.shape[-1]
    att = N_HEADS * HEAD_DIM
    assert ds == SUBLANES and cache_k.shape[3] == N_HEADS and cache_k.shape[4] == HEAD_DIM
    assert state_conv_a.shape[2] == CONV_W - 1 and state_pool.shape[2] == POOL_HIST
    assert meta_tokens.shape[0] == N_META

    s_real = N_META + seq
    fp = (-s_real) % SEQ_TILE
    if fp < CONV_BACK:
        fp += SEQ_TILE
    sp = fp + s_real
    s0 = nb * sp
    n = s0 + db * ds
    tm = next(t for t in (512, 256, 128, 64, 32, 16, 8) if n % t == 0)
    dbt = next(t for t in (8, 4, 2, 1) if db % t == 0)

    xp = jnp.concatenate([jnp.zeros((nb, fp, d), F32),
                          jnp.broadcast_to(meta_tokens[None].astype(F32), (nb, N_META, d)),
                          x_prompt], axis=1)
    x = jnp.concatenate([xp.reshape(s0, d), x_sample.reshape(db * ds, d)], axis=0)

    n_phys = cache_k.shape[0]
    n_even = cache_k.shape[1]
    page = cache_k.shape[2]
    ck = jnp.transpose(cache_k, (0, 1, 3, 4, 2)).reshape(n_phys, n_even, att, page)
    cv = jnp.transpose(cache_v, (0, 1, 3, 4, 2)).reshape(n_phys, n_even, att, page)
    clft = jnp.swapaxes(cache_logf, 2, 3)

    row2 = lambda v: v.reshape(1, -1).astype(F32)

    def prompt_rows(arr, lo, hi):
        return arr[:s0].reshape(nb, sp, -1)[:, lo:hi]

    def sample_rows(arr):
        return arr[s0:].reshape(db, ds, -1)

    kp, vp, lfp, cap, scp, plp = [], [], [], [], [], []
    ks_, vs_, lfs, cas, scs, pls = [], [], [], [], [], []
    for layer in range(depth):
        i = layer // 2
        if layer % 2 == 0:
            w_in = jnp.pad(w_in_even[i], ((0, 0), (0, LANES - N_HEADS))).astype(BF16)
            bf = jnp.pad(b_forget[i].astype(F32), (0, LANES - N_HEADS)).reshape(1, LANES)
            a, q, k, v, lf = _even_in(x, w_in, bf, tm=tm, ca=ca, att=att)
            conv = _prompt_conv(a, w_dw_a[i].astype(F32), row2(b_dw_a[i]), nb=nb, sp=sp)
            conv, st_a = _sample_conv(state_conv_a.astype(F32), i, a, w_dw_a[i].astype(F32), row2(b_dw_a[i]),
                                      conv, s0=s0, db=db, ds=ds, dbt=dbt)
            qa, ka, vat = _fox_prep(q, k, v, lf, nb=nb, sp=sp, fp=fp)
            at = _flash(qa, ka, vat, n=n, hg=FLASH_HEADS)
            at = _sample_attn(page_table, i, q, k, v, lf, ck, cv, clft, at, s0=s0, db=db, ds=ds)
            x = _even_out(x, conv, at, row2(ln_a_g[i]), row2(ln_a_b[i]), w_out_even[i].astype(BF16),
                          row2(ln_mix_g[layer]), row2(ln_mix_b[layer]), tm=tm, alpha=alpha)
            kp.append(prompt_rows(k, fp, sp).reshape(nb, s_real, N_HEADS, HEAD_DIM))
            vp.append(prompt_rows(v, fp, sp).reshape(nb, s_real, N_HEADS, HEAD_DIM))
            lfp.append(prompt_rows(lf, fp, sp)[..., :N_HEADS])
            cap.append(prompt_rows(a, sp - (CONV_W - 1), sp))
            ks_.append(sample_rows(k).reshape(db, ds, N_HEADS, HEAD_DIM))
            vs_.append(sample_rows(v).reshape(db, ds, N_HEADS, HEAD_DIM))
            lfs.append(sample_rows(lf)[..., :N_HEADS])
            cas.append(st_a)
        else:
            c = state_sconv.shape[-1]
            y, bg, pd = _odd_in(x, w_in_odd[i].astype(BF16), tm=tm, c=c)
            oc, dd = _prompt_odd(y, bg, pd, w_sconv[i].astype(F32), nb=nb, sp=sp, fp=fp)
            oc, dd, st_s, st_p = _sample_odd(state_sconv.astype(F32), state_pool.astype(F32), i, y, bg, pd,
                                             w_sconv[i].astype(F32), oc, dd, s0=s0, db=db, ds=ds, dbt=dbt)
            wm = jax.scipy.linalg.block_diag(*[w_pool_mix[i, g] for g in range(w_pool_mix.shape[1])]).astype(BF16)
            x = _odd_out(x, oc, dd, wm, row2(pool_scale[i]), w_out_odd[i].astype(BF16),
                         row2(ln_mix_g[layer]), row2(ln_mix_b[layer]), tm=tm, alpha=alpha)
            scp.append(prompt_rows(y, sp - (SCONV_W - 1), sp))
            plp.append(prompt_rows(pd, sp - POOL_HIST, sp))
            scs.append(st_s)
            pls.append(st_p)
        x = _ffn(x, w_ffn_gate[layer].astype(BF16), w_ffn_up[layer].astype(BF16), w_ffn_down[layer].astype(BF16),
                 row2(ln_ffn_g[layer]), row2(ln_ffn_b[layer]), tm=tm, alpha=alpha, nb=nb, sp=sp, fp=fp)

    y_prompt = prompt_rows(x, fp + N_META, sp)
    y_sample = sample_rows(x)
    st = lambda xs: jnp.stack(xs, axis=1)
    return (y_prompt, y_sample, st(kp), st(vp), st(lfp), st(cap), st(scp), st(plp),
            st(ks_), st(vs_), st(lfs), st(cas), st(scs), st(pls))
```

```python
import functools

import jax
import jax.numpy as jnp
from jax import lax
from jax.experimental import pallas as pl
from jax.experimental.pallas import tpu as pltpu

N_META = 16
N_HEADS = 8
HEAD_DIM = 64
CONV_W = 31
CONV_BACK = 32
SCONV_W = 3
POOL_WINDOWS = (2, 4, 8, 16)
POOL_HIST = max(POOL_WINDOWS) - 1
LN_EPS = 1e-5
SEQ_TILE = 256
LANES = 128
SUBLANES = 8
AUX_ONES = 24
FLASH_HEADS = 4
VMEM_LIMIT = 56 * 1024 * 1024
NEG = -1e30
LOG2E = 1.4426950408889634
F32 = jnp.float32
BF16 = jnp.bfloat16
NT_DIMS = (((1,), (1,)), ((), ()))


def _params(*sem):
    return pltpu.CompilerParams(dimension_semantics=sem, vmem_limit_bytes=VMEM_LIMIT)


def _dot(a, b):
    return jnp.dot(a, b, preferred_element_type=F32)


def _ln(z, g, b):
    mu = jnp.mean(z, axis=-1, keepdims=True)
    zc = z - mu
    var = jnp.mean(zc * zc, axis=-1, keepdims=True)
    return zc * lax.rsqrt(var + LN_EPS) * g + b


def _silu(x):
    return x * jax.nn.sigmoid(x)


def _full(shape):
    return pl.BlockSpec(shape, lambda *_: (0,) * len(shape))


def _even_in_kernel(x_ref, w_ref, bf_ref, a_ref, q_ref, k_ref, v_ref, lf_ref, *, ca, att):
    xb = x_ref[...].astype(BF16)

    def mm(lo, hi):
        return _dot(xb, w_ref[:, lo:hi])

    u = mm(0, ca)
    g = mm(ca, 2 * ca)
    a_ref[...] = u * jax.nn.sigmoid(g)
    o = 2 * ca
    q_ref[...] = mm(o, o + att) * (HEAD_DIM ** -0.5)
    k_ref[...] = mm(o + att, o + 2 * att)
    v_ref[...] = mm(o + 2 * att, o + 3 * att)
    z = mm(o + 3 * att, o + 3 * att + LANES) + bf_ref[...]
    lf = jnp.minimum(z, 0.0) - jnp.log1p(jnp.exp(-jnp.abs(z)))
    lane = lax.broadcasted_iota(jnp.int32, lf.shape, 1)
    lf_ref[...] = jnp.where(lane < N_HEADS, lf, 0.0)


def _even_in(x, w, bf, *, tm, ca, att):
    n, d = x.shape
    row = lambda c: pl.BlockSpec((tm, c), lambda i: (i, 0))
    return pl.pallas_call(
        functools.partial(_even_in_kernel, ca=ca, att=att),
        grid=(n // tm,),
        in_specs=[row(d), _full(w.shape), _full(bf.shape)],
        out_specs=[row(ca), row(att), row(att), row(att), row(LANES)],
        out_shape=[jax.ShapeDtypeStruct((n, c), F32) for c in (ca, att, att, att, LANES)],
        compiler_params=_params("parallel"),
        name="even_in_proj",
    )(x, w, bf)


def _odd_in_kernel(x_ref, w_ref, y_ref, bg_ref, pd_ref, *, c):
    xb = x_ref[...].astype(BF16)
    hc = _dot(xb, w_ref[:, 0:c])
    bg_ref[...] = _dot(xb, w_ref[:, c:2 * c])
    cg = _dot(xb, w_ref[:, 2 * c:3 * c])
    y_ref[...] = cg * hc
    pd_ref[...] = _dot(xb, w_ref[:, 3 * c:4 * c])


def _odd_in(x, w, *, tm, c):
    n, d = x.shape
    row = lambda cc: pl.BlockSpec((tm, cc), lambda i: (i, 0))
    return pl.pallas_call(
        functools.partial(_odd_in_kernel, c=c),
        grid=(n // tm,),
        in_specs=[row(d), _full(w.shape)],
        out_specs=[row(c)] * 3,
        out_shape=[jax.ShapeDtypeStruct((n, c), F32)] * 3,
        compiler_params=_params("parallel"),
        name="odd_in_proj",
    )(x, w)


def _ffn_tail(x1, wg_ref, wu_ref, wd_ref, g_ref, b_ref, o_ref, h_ref, *, alpha, chunk, nb, sp, fp):
    xb = x1.astype(BF16)
    for c in range(0, wg_ref.shape[1], chunk):
        gate = _dot(xb, wg_ref[:, c:c + chunk])
        up = _dot(xb, wu_ref[:, c:c + chunk])
        h_ref[:, c:c + chunk] = (_silu(gate) * up).astype(BF16)
    z = alpha * x1 + _dot(h_ref[...], wd_ref[...])
    out = _ln(z, g_ref[...], b_ref[...])
    tm = x1.shape[0]
    r = pl.program_id(0) * tm + lax.broadcasted_iota(jnp.int32, (tm, 1), 0)
    keep = jnp.ones((tm, 1), F32)
    for bi in range(nb):
        keep = jnp.where((r >= bi * sp) & (r < bi * sp + fp), 0.0, keep)
    o_ref[...] = out * keep


def _even_tail_kernel(x_ref, cv_ref, at_ref, ag_ref, ab_ref, w_ref, g1_ref, b1_ref,
                      wg_ref, wu_ref, wd_ref, g2_ref, b2_ref, o_ref, h_ref, *, alpha, ca, **ffn):
    cv = _silu(_ln(cv_ref[...], ag_ref[...], ab_ref[...]))
    mix = _dot(cv.astype(BF16), w_ref[0:ca, :]) + _dot(at_ref[...].astype(BF16), w_ref[ca:, :])
    x1 = _ln(alpha * x_ref[...] + mix, g1_ref[...], b1_ref[...])
    _ffn_tail(x1, wg_ref, wu_ref, wd_ref, g2_ref, b2_ref, o_ref, h_ref, alpha=alpha, **ffn)


def _odd_tail_kernel(x_ref, oc_ref, dd_ref, wm_ref, sc_ref, w_ref, g1_ref, b1_ref,
                     wg_ref, wu_ref, wd_ref, g2_ref, b2_ref, o_ref, h_ref, *, alpha, ca, **ffn):
    od = _dot(dd_ref[...].astype(BF16), wm_ref[...]) * sc_ref[...]
    mix = _dot(oc_ref[...].astype(BF16), w_ref[0:ca, :]) + _dot(od.astype(BF16), w_ref[ca:, :])
    x1 = _ln(alpha * x_ref[...] + mix, g1_ref[...], b1_ref[...])
    _ffn_tail(x1, wg_ref, wu_ref, wd_ref, g2_ref, b2_ref, o_ref, h_ref, alpha=alpha, **ffn)


def _layer_tail(body, name, x, m1, m2, p1, p2, w, g1, b1, wg, wu, wd, g2, b2, *, tm, alpha, nb, sp, fp):
    n, d = x.shape
    ca = m1.shape[1]
    dff = wg.shape[1]
    chunk = SEQ_TILE if dff % SEQ_TILE == 0 else dff
    row = lambda c: pl.BlockSpec((tm, c), lambda i: (i, 0))
    once = lambda a: pl.BlockSpec(a.shape, lambda i: (0,) * a.ndim, pipeline_mode=pl.Buffered(1))
    consts = (p1, p2, w, g1, b1, wg, wu, wd, g2, b2)
    return pl.pallas_call(
        functools.partial(body, alpha=alpha, ca=ca, chunk=chunk, nb=nb, sp=sp, fp=fp),
        grid=(n // tm,),
        in_specs=[row(d), row(ca), row(m2.shape[1])] + [once(a) for a in consts],
        out_specs=row(d),
        out_shape=jax.ShapeDtypeStruct((n, d), F32),
        scratch_shapes=[pltpu.VMEM((tm, dff), BF16)],
        compiler_params=_params("parallel"),
        name=name,
    )(x, m1, m2, *consts)


def _conv_tile(win, w_rows):
    rows = win.shape[0]
    t = rows - CONV_BACK
    lead = CONV_BACK - (CONV_W - 1)
    acc = None
    for r in range(SUBLANES):
        rolled = win if r == 0 else pltpu.roll(win, rows - r, axis=0)
        for m in range(CONV_BACK // SUBLANES + 1):
            j = SUBLANES * m + r - lead
            if 0 <= j < CONV_W:
                term = rolled[SUBLANES * m:SUBLANES * m + t] * w_rows[j]
                acc = term if acc is None else acc + term
    return acc


def _prompt_conv_kernel(a_ref, w_ref, b_ref, o_ref, *, n_tiles):
    w_rows = [w_ref[j:j + 1, :] for j in range(CONV_W)]
    bias = b_ref[...]
    c = a_ref.shape[1]
    win0 = jnp.concatenate([jnp.zeros((CONV_BACK, c), F32), a_ref[0:SEQ_TILE, :]], axis=0)
    o_ref[0:SEQ_TILE, :] = _conv_tile(win0, w_rows) + bias

    def body(i, carry):
        s0 = pl.multiple_of(i * SEQ_TILE, SEQ_TILE)
        win = a_ref[pl.ds(s0 - CONV_BACK, SEQ_TILE + CONV_BACK), :]
        o_ref[pl.ds(s0, SEQ_TILE), :] = _conv_tile(win, w_rows) + bias
        return carry

    lax.fori_loop(1, n_tiles, body, 0)


def _prompt_conv(a, w, b, *, nb, sp):
    n, c = a.shape
    blk = pl.BlockSpec((sp, LANES), lambda bi, ci: (bi, ci))
    return pl.pallas_call(
        functools.partial(_prompt_conv_kernel, n_tiles=sp // SEQ_TILE),
        grid=(nb, c // LANES),
        in_specs=[blk, pl.BlockSpec((CONV_W, LANES), lambda bi, ci: (0, ci)),
                  pl.BlockSpec((1, LANES), lambda bi, ci: (0, ci))],
        out_specs=blk,
        out_shape=jax.ShapeDtypeStruct((n, c), F32),
        compiler_params=_params("parallel", "parallel"),
        name="prompt_conv",
    )(a, w, b)


def _sample_conv_kernel(st_ref, a_ref, w_ref, b_ref, cv_in_ref, o_ref, st_out_ref, ext_ref, *, ds):
    del cv_in_ref
    dbt, hist, c = st_ref.shape
    off = CONV_BACK - hist
    ext_ref[:, off:CONV_BACK, :] = st_ref[...]
    ext_ref[:, CONV_BACK:CONV_BACK + ds, :] = a_ref[...].reshape(dbt, ds, c)
    acc = None
    for j in range(CONV_W):
        term = ext_ref[:, off + j:off + j + ds, :] * w_ref[j:j + 1, :]
        acc = term if acc is None else acc + term
    o_ref[...] = (acc + b_ref[...]).reshape(dbt * ds, c)
    st_out_ref[...] = ext_ref[:, off + ds:CONV_BACK + ds, :]


def _sample_conv(state, layer, a, w, b, cv, *, s0, db, ds, dbt):
    n, c = a.shape
    hist = state.shape[2]
    rows = dbt * ds
    row_blk = pl.BlockSpec((rows, c), lambda i: (s0 // rows + i, 0))
    return pl.pallas_call(
        functools.partial(_sample_conv_kernel, ds=ds),
        grid=(db // dbt,),
        in_specs=[pl.BlockSpec((dbt, None, hist, c), lambda i: (i, layer, 0, 0)), row_blk,
                  _full(w.shape), _full(b.shape), pl.BlockSpec(memory_space=pl.ANY)],
        out_specs=[row_blk, pl.BlockSpec((dbt, hist, c), lambda i: (i, 0, 0))],
        out_shape=[jax.ShapeDtypeStruct((n, c), F32), jax.ShapeDtypeStruct((db, hist, c), F32)],
        scratch_shapes=[pltpu.VMEM((dbt, CONV_BACK + ds, c), F32)],
        input_output_aliases={4: 0},
        compiler_params=_params("parallel"),
        name="sample_conv",
    )(state, a, w, b, cv)


def _fox_prep_kernel(q_ref, k_ref, v_ref, lf_ref, qa_ref, ka_ref, vat_ref, carry_ref, *, fp):
    s = pl.program_id(1)

    @pl.when(s == 0)
    def _():
        carry_ref[...] = jnp.zeros_like(carry_ref)

    t = SEQ_TILE
    row = lax.broadcasted_iota(jnp.int32, (t, LANES), 0)
    lane = lax.broadcasted_iota(jnp.int32, (t, LANES), 1)
    c = jnp.where(s * t + row >= fp, lf_ref[...], 0.0)
    sh = 1
    while sh < t:
        c = c + jnp.where(row >= sh, pltpu.roll(c, sh, axis=0), 0.0)
        sh *= 2
    c = c + carry_ref[0:1, :]
    carry_ref[0:1, :] = c[t - 1:t, :]

    c = c * LOG2E
    c1 = c.astype(BF16).astype(F32)
    r1 = c - c1
    c2 = r1.astype(BF16).astype(F32)
    c3 = (r1 - c2).astype(BF16).astype(F32)
    g23 = jnp.where(lane < 2 * N_HEADS, pltpu.roll(c2, N_HEADS, axis=1),
                    jnp.where(lane < 3 * N_HEADS, pltpu.roll(c3, 2 * N_HEADS, axis=1), 0.0))
    g = jnp.where(lane < N_HEADS, c1, g23)
    gk = jnp.where(lane < N_HEADS, jnp.where(s * t + row >= fp, c1, -NEG), g23)
    ones_grp = (lane >= HEAD_DIM + AUX_ONES) & (lane < HEAD_DIM + 2 * AUX_ONES)
    ck = [jnp.where((lane >= HEAD_DIM) & (lane < HEAD_DIM + AUX_ONES), -pltpu.roll(gk, HEAD_DIM, axis=1),
                    jnp.where(ones_grp, 1.0, 0.0))]
    cq = [jnp.where(ones_grp, pltpu.roll(g, HEAD_DIM + AUX_ONES, axis=1), 0.0)]
    ck.append(pltpu.roll(ck[0], HEAD_DIM, axis=1))
    cq.append(pltpu.roll(cq[0], HEAD_DIM, axis=1))

    lane1 = lax.broadcasted_iota(jnp.int32, (1, LANES), 1)
    for h in range(N_HEADS):
        odd = h % 2
        pr = h // 2
        base = 0 if odd else HEAD_DIM
        a = lane1 - base
        mine = (lane1 & (N_HEADS - 1)) == h
        data = ((lane1 >= HEAD_DIM) if odd else (lane1 < HEAD_DIM)).astype(F32)
        sel1 = ((a >= 0) & (a < AUX_ONES) & mine).astype(F32)
        selc = ((a >= AUX_ONES) & (a < 2 * AUX_ONES) & mine).astype(F32)
        sl = slice(pr * LANES, (pr + 1) * LANES)
        qa_ref[h] = (q_ref[:, sl] * (data * LOG2E) + (cq[odd] * selc + sel1)).astype(BF16)
        ka_ref[h] = (k_ref[:, sl] * data + ck[odd]).astype(BF16)
        vat_ref[h] = (v_ref[:, sl] * data).T.astype(BF16)


def _fox_prep(q, k, v, lf, *, nb, sp, fp):
    ns = sp // SEQ_TILE
    att = q.shape[1]
    row = lambda c: pl.BlockSpec((SEQ_TILE, c), lambda bi, si: (bi * ns + si, 0))
    hm = pl.BlockSpec((None, N_HEADS, SEQ_TILE, LANES), lambda bi, si: (bi, 0, si, 0))
    hm_t = pl.BlockSpec((None, N_HEADS, LANES, SEQ_TILE), lambda bi, si: (bi, 0, 0, si))
    return pl.pallas_call(
        functools.partial(_fox_prep_kernel, fp=fp),
        grid=(nb, ns),
        in_specs=[row(att), row(att), row(att), row(LANES)],
        out_specs=[hm, hm, hm_t],
        out_shape=[jax.ShapeDtypeStruct((nb, N_HEADS, sp, LANES), BF16)] * 2
        + [jax.ShapeDtypeStruct((nb, N_HEADS, LANES, sp), BF16)],
        scratch_shapes=[pltpu.VMEM((SUBLANES, LANES), F32)],
        compiler_params=_params("parallel", "arbitrary"),
        name="fox_prep",
    )(q, k, v, lf)


def _flash_kernel(q_ref, k_ref, vt_ref, o_ref, p_ref, acc_ref, *, hg):
    i = pl.program_id(2)
    t = SEQ_TILE
    top = lax.broadcasted_iota(jnp.int32, (LANES, t), 0) < HEAD_DIM

    def scores(j):
        ks = pl.multiple_of(j * t, t)
        return [lax.dot_general(k_ref[h, pl.ds(ks, t), :], q_ref[h], NT_DIMS, preferred_element_type=F32)
                for h in range(hg)]

    def accumulate(j, als):
        ks = pl.multiple_of(j * t, t)
        pvs = [_dot(vt_ref[h, :, pl.ds(ks, t)], p_ref[h]) for h in range(hg)]
        for pr in range(hg // 2):
            acc_ref[pr] = (jnp.where(top, als[2 * pr], als[2 * pr + 1]) * acc_ref[pr]
                           + pvs[2 * pr] + pvs[2 * pr + 1])

    def softmax(ss, ms, ls, masked):
        new_m, new_l, als = [], [], []
        for h in range(hg):
            s = ss[h]
            if masked:
                visible = (lax.broadcasted_iota(jnp.int32, (t, t), 0) <= lax.broadcasted_iota(jnp.int32, (t, t), 1))
                s = jnp.where(visible, s, NEG)
            mn = jnp.maximum(ms[h], jnp.max(s, axis=0, keepdims=True))
            p = jnp.exp2(s - mn)
            al = jnp.exp2(ms[h] - mn)
            new_m.append(mn)
            new_l.append(al * ls[h] + jnp.sum(p, axis=0, keepdims=True))
            als.append(al)
            p_ref[h] = p.astype(BF16)
        return tuple(new_m), tuple(new_l), tuple(als)

    def step(j, state, masked):
        ms, ls, als = state
        ss = scores(j)
        accumulate(jnp.maximum(j - 1, 0), als)
        return softmax(ss, ms, ls, masked)

    p_ref[...] = jnp.zeros_like(p_ref)
    acc_ref[...] = jnp.zeros_like(acc_ref)
    state = ((jnp.full((1, t), NEG, F32),) * hg, (jnp.zeros((1, t), F32),) * hg, (jnp.ones((1, t), F32),) * hg)
    state = lax.fori_loop(0, i, functools.partial(step, masked=False), state)
    _, ls, als = step(i, state, True)
    accumulate(i, als)
    for pr in range(hg // 2):
        out_t = acc_ref[pr] * jnp.where(top, 1.0 / ls[2 * pr], 1.0 / ls[2 * pr + 1])
        o_ref[:, pr * LANES:(pr + 1) * LANES] = out_t.T


def _flash(qa, ka, vat, *, n, hg):
    nb, _, sp, _ = qa.shape
    ns = sp // SEQ_TILE
    return pl.pallas_call(
        functools.partial(_flash_kernel, hg=hg),
        grid=(nb, N_HEADS // hg, ns),
        in_specs=[pl.BlockSpec((None, hg, SEQ_TILE, LANES), lambda bi, gi, qi: (bi, gi, qi, 0)),
                  pl.BlockSpec((None, hg, sp, LANES), lambda bi, gi, qi: (bi, gi, 0, 0)),
                  pl.BlockSpec((None, hg, LANES, sp), lambda bi, gi, qi: (bi, gi, 0, 0))],
        out_specs=pl.BlockSpec((SEQ_TILE, hg * HEAD_DIM), lambda bi, gi, qi: (bi * ns + qi, gi)),
        out_shape=jax.ShapeDtypeStruct((n, N_HEADS * HEAD_DIM), F32),
        scratch_shapes=[pltpu.VMEM((hg, SEQ_TILE, SEQ_TILE), BF16), pltpu.VMEM((hg // 2, LANES, SEQ_TILE), F32)],
        compiler_params=_params("parallel", "parallel", "arbitrary"),
        name="fox_flash",
    )(qa, ka, vat)


def _sample_attn_kernel(pt_ref, q_ref, kn_ref, vn_ref, lfn_ref, *rest, n_pages, page):
    del pt_ref
    k_refs = rest[:n_pages]
    v_refs = rest[n_pages:2 * n_pages]
    lf_refs = rest[2 * n_pages:3 * n_pages]
    o_ref = rest[3 * n_pages + 1]
    ds, c = q_ref.shape
    rows = ds * N_HEADS
    q = q_ref[...]
    sub = lax.broadcasted_iota(jnp.int32, (N_HEADS, c), 0)
    lane = lax.broadcasted_iota(jnp.int32, (N_HEADS, c), 1)
    hm = ((lane >= sub * HEAD_DIM) & (lane < (sub + 1) * HEAD_DIM)).astype(F32)
    qbd = jnp.concatenate([jnp.broadcast_to(q[t:t + 1, :], (N_HEADS, c)) * hm for t in range(ds)],
                          axis=0).astype(BF16)
    hm_t = jnp.concatenate([hm] * ds, axis=0)

    groups = [list(range(g, min(g + 2, n_pages))) for g in range(0, n_pages, 2)]
    s_parts = []
    for grp in groups:
        kp = jnp.concatenate([k_refs[r][...] for r in grp], axis=1).astype(BF16)
        s_parts.append(_dot(qbd, kp))
    s_past = jnp.concatenate(s_parts, axis=1)
    past = n_pages * page

    lf = jnp.concatenate([r[...] for r in lf_refs], axis=1)
    lane_p = lax.broadcasted_iota(jnp.int32, (N_HEADS, past), 1)
    suf = lf
    sh = 1
    while sh < past:
        suf = suf + jnp.where(lane_p < past - sh, pltpu.roll(suf, past - sh, axis=1), 0.0)
        sh *= 2
    s_past = s_past + jnp.concatenate([suf - lf] * ds, axis=0)

    pad = jnp.zeros((LANES - ds, c), F32)
    kn = jnp.concatenate([kn_ref[...], pad], axis=0).astype(BF16)
    vn = jnp.concatenate([vn_ref[...], pad], axis=0).astype(BF16)
    s_new = lax.dot_general(qbd, kn, NT_DIMS, preferred_element_type=F32)
    cn = lfn_ref[...]
    sub_n = lax.broadcasted_iota(jnp.int32, cn.shape, 0)
    sh = 1
    while sh < ds:
        cn = cn + jnp.where(sub_n >= sh, pltpu.roll(cn, sh, axis=0), 0.0)
        sh *= 2
    cn_t = jnp.concatenate([cn, jnp.zeros((LANES - ds, LANES), F32)], axis=0).T[0:N_HEADS, :]
    rown = lax.broadcasted_iota(jnp.int32, (rows, LANES), 0)
    u = lax.broadcasted_iota(jnp.int32, (rows, LANES), 1)
    s_new = jnp.where(u * N_HEADS <= rown, s_new - jnp.concatenate([cn_t] * ds, axis=0), NEG)

    m = jnp.maximum(jnp.max(s_past, axis=1, keepdims=True), jnp.max(s_new, axis=1, keepdims=True))
    p_past = jnp.exp(s_past - m)
    p_new = jnp.exp(s_new - m)
    l = jnp.sum(p_past, axis=1, keepdims=True) + jnp.sum(p_new, axis=1, keepdims=True)
    o = _dot(p_new.astype(BF16), vn)
    for gi, grp in enumerate(groups):
        vp = jnp.concatenate([v_refs[r][...] for r in grp], axis=1).astype(BF16)
        lo = grp[0] * page
        o = o + lax.dot_general(p_past[:, lo:lo + len(grp) * page].astype(BF16), vp, NT_DIMS,
                                preferred_element_type=F32)
    o = o * (1.0 / l) * hm_t
    rr = lax.broadcasted_iota(jnp.int32, (ds, rows), 1)
    tt = lax.broadcasted_iota(jnp.int32, (ds, rows), 0)
    pick = ((rr >= tt * N_HEADS) & (rr < (tt + 1) * N_HEADS)).astype(BF16)
    o_ref[...] = _dot(pick, o.astype(BF16))


def _sample_attn(page_table, layer, q, k, v, lf, cache_k, cache_v, cache_lft, att, *, s0, db, ds):
    n, c = q.shape
    n_pages = page_table.shape[1]
    page = cache_k.shape[3]
    row = lambda cc: pl.BlockSpec((ds, cc), lambda i, pt: (s0 // ds + i, 0))

    def paged(shape, r):
        return pl.BlockSpec((None, None) + shape, lambda i, pt: (pt[i * n_pages + r], layer, 0, 0))

    in_specs = ([row(c), row(c), row(c), row(LANES)]
                + [paged((c, page), r) for r in range(n_pages)]
                + [paged((c, page), r) for r in range(n_pages)]
                + [paged((N_HEADS, page), r) for r in range(n_pages)]
                + [pl.BlockSpec(memory_space=pl.ANY)])
    n_in = len(in_specs)
    return pl.pallas_call(
        functools.partial(_sample_attn_kernel, n_pages=n_pages, page=page),
        grid_spec=pltpu.PrefetchScalarGridSpec(
            num_scalar_prefetch=1, grid=(db,), in_specs=in_specs, out_specs=row(c)),
        out_shape=jax.ShapeDtypeStruct((n, c), F32),
        input_output_aliases={n_in: 0},
        compiler_params=_params("parallel"),
        name="sample_paged_attn",
    )(page_table.reshape(-1), q, k, v, lf, *([cache_k] * n_pages), *([cache_v] * n_pages),
      *([cache_lft] * n_pages), att)


def _odd_tile(wy, wp, bg, w_rows, grp, pos0):
    t = bg.shape[0]
    ny = wy.shape[0]
    y0 = wy[SUBLANES:]
    y1 = pltpu.roll(wy, 1, axis=0)[SUBLANES:]
    y2 = pltpu.roll(wy, 2, axis=0)[SUBLANES:]
    out_c = bg * (w_rows[0] * y2 + w_rows[1] * y1 + w_rows[2] * y0)
    del ny
    back = 2 * SUBLANES
    s2 = wp + pltpu.roll(wp, 1, axis=0)
    s4 = s2 + pltpu.roll(s2, 2, axis=0)
    s8 = s4 + pltpu.roll(s4, 4, axis=0)
    s16 = s8 + pltpu.roll(s8, 8, axis=0)
    win = jnp.where(grp == 0, s2, jnp.where(grp == 1, s4, jnp.where(grp == 2, s8, s16)))[back:]
    wsz = jnp.where(grp == 0, POOL_WINDOWS[0],
                    jnp.where(grp == 1, POOL_WINDOWS[1], jnp.where(grp == 2, POOL_WINDOWS[2], POOL_WINDOWS[3])))
    pos = pos0 + lax.broadcasted_iota(jnp.int32, (t, LANES), 0)
    cnt = jnp.maximum(jnp.minimum(wsz, pos + 1), 1).astype(F32)
    d = win / cnt - wp[back:]
    return out_c, d


def _prompt_odd_kernel(y_ref, bg_ref, pd_ref, w_ref, oc_ref, d_ref, *, n_tiles, fp):
    grp = pl.program_id(1)
    w_rows = [w_ref[j:j + 1, :] for j in range(SCONV_W)]
    c = y_ref.shape[1]
    t = SEQ_TILE
    wy0 = jnp.concatenate([jnp.zeros((SUBLANES, c), F32), y_ref[0:t, :]], axis=0)
    wp0 = jnp.concatenate([jnp.zeros((2 * SUBLANES, c), F32), pd_ref[0:t, :]], axis=0)
    oc, d = _odd_tile(wy0, wp0, bg_ref[0:t, :], w_rows, grp, -fp)
    oc_ref[0:t, :] = oc
    d_ref[0:t, :] = d

    def body(i, carry):
        s0 = pl.multiple_of(i * t, t)
        wy = y_ref[pl.ds(s0 - SUBLANES, t + SUBLANES), :]
        wp = pd_ref[pl.ds(s0 - 2 * SUBLANES, t + 2 * SUBLANES), :]
        oc, d = _odd_tile(wy, wp, bg_ref[pl.ds(s0, t), :], w_rows, grp, s0 - fp)
        oc_ref[pl.ds(s0, t), :] = oc
        d_ref[pl.ds(s0, t), :] = d
        return carry

    lax.fori_loop(1, n_tiles, body, 0)


def _prompt_odd(y, bg, pd, w, *, nb, sp, fp):
    n, c = y.shape
    assert c // LANES == len(POOL_WINDOWS)
    blk = pl.BlockSpec((sp, LANES), lambda bi, ci: (bi, ci))
    return pl.pallas_call(
        functools.partial(_prompt_odd_kernel, n_tiles=sp // SEQ_TILE, fp=fp),
        grid=(nb, c // LANES),
        in_specs=[blk, blk, blk, pl.BlockSpec((SCONV_W, LANES), lambda bi, ci: (0, ci))],
        out_specs=[blk, blk],
        out_shape=[jax.ShapeDtypeStruct((n, c), F32)] * 2,
        compiler_params=_params("parallel", "parallel"),
        name="prompt_sconv_pool",
    )(y, bg, pd, w)


def _sample_odd_kernel(ss_ref, sp_ref, y_ref, bg_ref, pd_ref, w_ref, oc_in_ref, d_in_ref,
                       oc_ref, d_ref, ss_out_ref, sp_out_ref, yext_ref, pext_ref, *, ds):
    del oc_in_ref, d_in_ref
    dbt, _, c = ss_ref.shape
    hs = SCONV_W - 1
    y = y_ref[...].reshape(dbt, ds, c)
    pd = pd_ref[...].reshape(dbt, ds, c)
    bg = bg_ref[...].reshape(dbt, ds, c)
    yext_ref[:, SUBLANES - hs:SUBLANES, :] = ss_ref[...]
    yext_ref[:, SUBLANES:SUBLANES + ds, :] = y
    conv = None
    for j in range(SCONV_W):
        lo = SUBLANES - hs + j
        term = yext_ref[:, lo:lo + ds, :] * w_ref[j:j + 1, :]
        conv = term if conv is None else conv + term
    oc_ref[...] = (bg * conv).reshape(dbt * ds, c)
    ss_out_ref[...] = yext_ref[:, SUBLANES + ds - hs:SUBLANES + ds, :]

    base = 2 * SUBLANES
    pext_ref[:, base - POOL_HIST:base, :] = sp_ref[...]
    pext_ref[:, base:base + ds, :] = pd
    run = pd
    sums = {}
    for i in range(1, max(POOL_WINDOWS)):
        run = run + pext_ref[:, base - i:base - i + ds, :]
        if i + 1 in POOL_WINDOWS:
            sums[i + 1] = run
    gc = c // len(POOL_WINDOWS)
    lane = lax.broadcasted_iota(jnp.int32, (dbt, ds, c), 2)
    mean = sums[POOL_WINDOWS[-1]] / float(POOL_WINDOWS[-1])
    for gi in range(len(POOL_WINDOWS) - 2, -1, -1):
        mean = jnp.where(lane < (gi + 1) * gc, sums[POOL_WINDOWS[gi]] / float(POOL_WINDOWS[gi]), mean)
    d_ref[...] = (mean - pd).reshape(dbt * ds, c)
    sp_out_ref[...] = pext_ref[:, base + ds - POOL_HIST:base + ds, :]


def _sample_odd(state_s, state_p, layer, y, bg, pd, w, oc, dd, *, s0, db, ds, dbt):
    n, c = y.shape
    rows = dbt * ds
    row_blk = pl.BlockSpec((rows, c), lambda i: (s0 // rows + i, 0))
    hs = state_s.shape[2]
    hp = state_p.shape[2]
    anyspec = pl.BlockSpec(memory_space=pl.ANY)
    return pl.pallas_call(
        functools.partial(_sample_odd_kernel, ds=ds),
        grid=(db // dbt,),
        in_specs=[pl.BlockSpec((dbt, None, hs, c), lambda i: (i, layer, 0, 0)),
                  pl.BlockSpec((dbt, None, hp, c), lambda i: (i, layer, 0, 0)),
                  row_blk, row_blk, row_blk, _full(w.shape), anyspec, anyspec],
        out_specs=[row_blk, row_blk, pl.BlockSpec((dbt, hs, c), lambda i: (i, 0, 0)),
                   pl.BlockSpec((dbt, hp, c), lambda i: (i, 0, 0))],
        out_shape=[jax.ShapeDtypeStruct((n, c), F32), jax.ShapeDtypeStruct((n, c), F32),
                   jax.ShapeDtypeStruct((db, hs, c), F32), jax.ShapeDtypeStruct((db, hp, c), F32)],
        scratch_shapes=[pltpu.VMEM((dbt, SUBLANES + ds, c), F32), pltpu.VMEM((dbt, 2 * SUBLANES + ds, c), F32)],
        input_output_aliases={6: 0, 7: 1},
        compiler_params=_params("parallel"),
        name="sample_sconv_pool",
    )(state_s, state_p, y, bg, pd, w, oc, dd)


def kernel(x_prompt, x_sample, cache_k, cache_v, cache_logf, page_table, state_conv_a, state_sconv, state_pool, meta_tokens, w_in_even, b_forget, w_dw_a, b_dw_a, ln_a_g, ln_a_b, w_out_even, w_in_odd, w_sconv, w_pool_mix, pool_scale, w_out_odd, ln_mix_g, ln_mix_b, w_ffn_gate, w_ffn_up, w_ffn_down, ln_ffn_g, ln_ffn_b):
    nb, seq, d = x_prompt.shape
    db, ds, _ = x_sample.shape
    depth = w_ffn_gate.shape[0]
    alpha = float((2 * depth) ** 0.25)
    ca = state_conv_a.shape[-1]
    att = N_HEADS * HEAD_DIM
    assert ds == SUBLANES and cache_k.shape[3] == N_HEADS and cache_k.shape[4] == HEAD_DIM
    assert state_conv_a.shape[2] == CONV_W - 1 and state_pool.shape[2] == POOL_HIST
    assert meta_tokens.shape[0] == N_META

    s_real = N_META + seq
    fp = (-s_real) % SEQ_TILE
    if fp < CONV_BACK:
        fp += SEQ_TILE
    sp = fp + s_real
    s0 = nb * sp
    n = s0 + db * ds
    tm = next(t for t in (512, 256, 128, 64, 32, 16, 8) if n % t == 0)
    dbt = next(t for t in (8, 4, 2, 1) if db % t == 0)

    head = jnp.concatenate([jnp.zeros((fp, d), F32), meta_tokens.astype(F32)], axis=0)
    x = jnp.concatenate([piece for bi in range(nb) for piece in (head, x_prompt[bi])]
                        + [x_sample.reshape(db * ds, d)], axis=0)

    n_phys = cache_k.shape[0]
    n_even = cache_k.shape[1]
    page = cache_k.shape[2]
    ck = jnp.transpose(cache_k, (0, 1, 3, 4, 2)).reshape(n_phys, n_even, att, page)
    cv = jnp.transpose(cache_v, (0, 1, 3, 4, 2)).reshape(n_phys, n_even, att, page)
    clft = jnp.swapaxes(cache_logf, 2, 3)

    row2 = lambda v: v.reshape(1, -1).astype(F32)

    def prompt_rows(arr, lo, hi):
        return jnp.stack([arr[bi * sp + lo:bi * sp + hi] for bi in range(nb)], axis=0)

    def sample_rows(arr):
        return arr[s0:].reshape(db, ds, -1)

    def tail(body, name, xin, m1, m2, p1, p2, w_out, layer):
        return _layer_tail(body, name, xin, m1, m2, p1, p2, w_out,
                           row2(ln_mix_g[layer]), row2(ln_mix_b[layer]),
                           w_ffn_gate[layer].astype(BF16), w_ffn_up[layer].astype(BF16),
                           w_ffn_down[layer].astype(BF16), row2(ln_ffn_g[layer]), row2(ln_ffn_b[layer]),
                           tm=tm, alpha=alpha, nb=nb, sp=sp, fp=fp)

    kp, vp, lfp, cap, scp, plp = [], [], [], [], [], []
    ks_, vs_, lfs, cas, scs, pls = [], [], [], [], [], []
    for layer in range(depth):
        i = layer // 2
        if layer % 2 == 0:
            w_in = jnp.pad(w_in_even[i], ((0, 0), (0, LANES - N_HEADS))).astype(BF16)
            bf = jnp.pad(b_forget[i].astype(F32), (0, LANES - N_HEADS)).reshape(1, LANES)
            a, q, k, v, lf = _even_in(x, w_in, bf, tm=tm, ca=ca, att=att)
            conv = _prompt_conv(a, w_dw_a[i].astype(F32), row2(b_dw_a[i]), nb=nb, sp=sp)
            conv, st_a = _sample_conv(state_conv_a.astype(F32), i, a, w_dw_a[i].astype(F32), row2(b_dw_a[i]),
                                      conv, s0=s0, db=db, ds=ds, dbt=dbt)
            qa, ka, vat = _fox_prep(q, k, v, lf, nb=nb, sp=sp, fp=fp)
            at = _flash(qa, ka, vat, n=n, hg=FLASH_HEADS)
            at = _sample_attn(page_table, i, q, k, v, lf, ck, cv, clft, at, s0=s0, db=db, ds=ds)
            x = tail(_even_tail_kernel, "even_tail", x, conv, at, row2(ln_a_g[i]), row2(ln_a_b[i]),
                     w_out_even[i].astype(BF16), layer)
            kp.append(prompt_rows(k, fp, sp).reshape(nb, s_real, N_HEADS, HEAD_DIM))
            vp.append(prompt_rows(v, fp, sp).reshape(nb, s_real, N_HEADS, HEAD_DIM))
            lfp.append(prompt_rows(lf, fp, sp)[..., :N_HEADS])
            cap.append(prompt_rows(a, sp - (CONV_W - 1), sp))
            ks_.append(sample_rows(k).reshape(db, ds, N_HEADS, HEAD_DIM))
            vs_.append(sample_rows(v).reshape(db, ds, N_HEADS, HEAD_DIM))
            lfs.append(sample_rows(lf)[..., :N_HEADS])
            cas.append(st_a)
        else:
            c = state_sconv.shape[-1]
            y, bg, pd = _odd_in(x, w_in_odd[i].astype(BF16), tm=tm, c=c)
            oc, dd = _prompt_odd(y, bg, pd, w_sconv[i].astype(F32), nb=nb, sp=sp, fp=fp)
            oc, dd, st_s, st_p = _sample_odd(state_sconv.astype(F32), state_pool.astype(F32), i, y, bg, pd,
                                             w_sconv[i].astype(F32), oc, dd, s0=s0, db=db, ds=ds, dbt=dbt)
            wm = jax.scipy.linalg.block_diag(*[w_pool_mix[i, g] for g in range(w_pool_mix.shape[1])]).astype(BF16)
            x = tail(_odd_tail_kernel, "odd_tail", x, oc, dd, wm, row2(pool_scale[i]),
                     w_out_odd[i].astype(BF16), layer)
            scp.append(prompt_rows(y, sp - (SCONV_W - 1), sp))
            plp.append(prompt_rows(pd, sp - POOL_HIST, sp))
            scs.append(st_s)
            pls.append(st_p)

    y_prompt = prompt_rows(x, fp + N_META, sp)
    y_sample = sample_rows(x)
    st = lambda xs: jnp.stack(xs, axis=1)
    return (y_prompt, y_sample, st(kp), st(vp), st(lfp), st(cap), st(scp), st(plp),
            st(ks_), st(vs_), st(lfs), st(cas), st(scs), st(pls))
```

```python
import functools

import jax
import jax.numpy as jnp
from jax import lax
from jax.experimental import pallas as pl
from jax.experimental.pallas import tpu as pltpu

N_META = 16
N_HEADS = 8
HEAD_DIM = 64
CONV_W = 31
CONV_BACK = 32
SCONV_W = 3
POOL_WINDOWS = (2, 4, 8, 16)
POOL_HIST = max(POOL_WINDOWS) - 1
LN_EPS = 1e-5
SEQ_TILE = 256
LANES = 128
SUBLANES = 8
AUX_ONES = 24
FLASH_HEADS = 4
VMEM_LIMIT = 56 * 1024 * 1024
NEG = -1e30
LOG2E = 1.4426950408889634
F32 = jnp.float32
BF16 = jnp.bfloat16
NT_DIMS = (((1,), (1,)), ((), ()))


def _params(*sem):
    return pltpu.CompilerParams(dimension_semantics=sem, vmem_limit_bytes=VMEM_LIMIT)


def _dot(a, b):
    return jnp.dot(a, b, preferred_element_type=F32)


def _ln(z, g, b):
    mu = jnp.mean(z, axis=-1, keepdims=True)
    zc = z - mu
    var = jnp.mean(zc * zc, axis=-1, keepdims=True)
    return zc * lax.rsqrt(var + LN_EPS) * g + b


def _silu(x):
    return x * jax.nn.sigmoid(x)


def _full(shape):
    return pl.BlockSpec(shape, lambda *_: (0,) * len(shape))


def _slab(a, idx):
    return pl.BlockSpec((None,) + a.shape[1:], lambda *_: (idx,) + (0,) * (a.ndim - 1),
                        pipeline_mode=pl.Buffered(1))


def _even_in_kernel(x_ref, w_ref, bf_ref, a_ref, q_ref, k_ref, v_ref, lf_ref, *, ca, att):
    xb = x_ref[...].astype(BF16)

    def mm(lo, hi):
        return _dot(xb, w_ref[:, lo:hi])

    u = mm(0, ca)
    g = mm(ca, 2 * ca)
    a_ref[...] = u * jax.nn.sigmoid(g)
    o = 2 * ca
    q_ref[...] = mm(o, o + att) * (HEAD_DIM ** -0.5)
    k_ref[...] = mm(o + att, o + 2 * att)
    v_ref[...] = mm(o + 2 * att, o + 3 * att)
    z = mm(o + 3 * att, o + 3 * att + LANES) + bf_ref[...]
    lf = jnp.minimum(z, 0.0) - jnp.log1p(jnp.exp(-jnp.abs(z)))
    lane = lax.broadcasted_iota(jnp.int32, lf.shape, 1)
    lf_ref[...] = jnp.where(lane < N_HEADS, lf, 0.0)


def _even_in(x, w, idx, bf, *, tm, ca, att):
    n, d = x.shape
    row = lambda c: pl.BlockSpec((tm, c), lambda i: (i, 0))
    return pl.pallas_call(
        functools.partial(_even_in_kernel, ca=ca, att=att),
        grid=(n // tm,),
        in_specs=[row(d), _slab(w, idx), _full(bf.shape)],
        out_specs=[row(ca), row(att), row(att), row(att), row(LANES)],
        out_shape=[jax.ShapeDtypeStruct((n, c), F32) for c in (ca, att, att, att, LANES)],
        compiler_params=_params("parallel"),
        name="even_in_proj",
    )(x, w, bf)


def _odd_in_kernel(x_ref, w_ref, y_ref, bg_ref, pd_ref, *, c):
    xb = x_ref[...].astype(BF16)
    hc = _dot(xb, w_ref[:, 0:c])
    bg_ref[...] = _dot(xb, w_ref[:, c:2 * c])
    cg = _dot(xb, w_ref[:, 2 * c:3 * c])
    y_ref[...] = cg * hc
    pd_ref[...] = _dot(xb, w_ref[:, 3 * c:4 * c])


def _odd_in(x, w, idx, *, tm, c):
    n, d = x.shape
    row = lambda cc: pl.BlockSpec((tm, cc), lambda i: (i, 0))
    return pl.pallas_call(
        functools.partial(_odd_in_kernel, c=c),
        grid=(n // tm,),
        in_specs=[row(d), _slab(w, idx)],
        out_specs=[row(c)] * 3,
        out_shape=[jax.ShapeDtypeStruct((n, c), F32)] * 3,
        compiler_params=_params("parallel"),
        name="odd_in_proj",
    )(x, w)


def _ffn_tail(x1, wg_ref, wu_ref, wd_ref, g_ref, b_ref, o_ref, h_ref, *, alpha, chunk, nb, sp, fp):
    xb = x1.astype(BF16)
    for c in range(0, wg_ref.shape[1], chunk):
        gate = _dot(xb, wg_ref[:, c:c + chunk])
        up = _dot(xb, wu_ref[:, c:c + chunk])
        h_ref[:, c:c + chunk] = (_silu(gate) * up).astype(BF16)
    z = alpha * x1 + _dot(h_ref[...], wd_ref[...])
    out = _ln(z, g_ref[...], b_ref[...])
    tm = x1.shape[0]
    r = pl.program_id(0) * tm + lax.broadcasted_iota(jnp.int32, (tm, 1), 0)
    keep = jnp.ones((tm, 1), F32)
    for bi in range(nb):
        keep = jnp.where((r >= bi * sp) & (r < bi * sp + fp), 0.0, keep)
    o_ref[...] = out * keep


def _even_tail_kernel(x_ref, cv_ref, at_ref, ag_ref, ab_ref, w_ref, g1_ref, b1_ref,
                      wg_ref, wu_ref, wd_ref, g2_ref, b2_ref, o_ref, h_ref, *, alpha, ca, **ffn):
    cv = _silu(_ln(cv_ref[...], ag_ref[...], ab_ref[...]))
    mix = _dot(cv.astype(BF16), w_ref[0:ca, :]) + _dot(at_ref[...].astype(BF16), w_ref[ca:, :])
    x1 = _ln(alpha * x_ref[...] + mix, g1_ref[...], b1_ref[...])
    _ffn_tail(x1, wg_ref, wu_ref, wd_ref, g2_ref, b2_ref, o_ref, h_ref, alpha=alpha, **ffn)


def _odd_tail_kernel(x_ref, oc_ref, dd_ref, wm_ref, sc_ref, w_ref, g1_ref, b1_ref,
                     wg_ref, wu_ref, wd_ref, g2_ref, b2_ref, o_ref, h_ref, *, alpha, ca, **ffn):
    od = _dot(dd_ref[...].astype(BF16), wm_ref[...]) * sc_ref[...]
    mix = _dot(oc_ref[...].astype(BF16), w_ref[0:ca, :]) + _dot(od.astype(BF16), w_ref[ca:, :])
    x1 = _ln(alpha * x_ref[...] + mix, g1_ref[...], b1_ref[...])
    _ffn_tail(x1, wg_ref, wu_ref, wd_ref, g2_ref, b2_ref, o_ref, h_ref, alpha=alpha, **ffn)


def _layer_tail(body, name, x, m1, m2, p1, p2, w, g1, b1, wg, wu, wd, g2, b2, *, mix_idx, layer, tm, alpha,
                nb, sp, fp):
    n, d = x.shape
    ca = m1.shape[1]
    dff = wg.shape[2]
    chunk = SEQ_TILE if dff % SEQ_TILE == 0 else dff
    row = lambda c: pl.BlockSpec((tm, c), lambda i: (i, 0))
    once = lambda a: pl.BlockSpec(a.shape, lambda i: (0,) * a.ndim, pipeline_mode=pl.Buffered(1))
    consts = (p1, p2, w, g1, b1, wg, wu, wd, g2, b2)
    const_specs = [once(p1), once(p2), _slab(w, mix_idx), once(g1), once(b1),
                   _slab(wg, layer), _slab(wu, layer), _slab(wd, layer), once(g2), once(b2)]
    return pl.pallas_call(
        functools.partial(body, alpha=alpha, ca=ca, chunk=chunk, nb=nb, sp=sp, fp=fp),
        grid=(n // tm,),
        in_specs=[row(d), row(ca), row(m2.shape[1])] + const_specs,
        out_specs=row(d),
        out_shape=jax.ShapeDtypeStruct((n, d), F32),
        scratch_shapes=[pltpu.VMEM((tm, dff), BF16)],
        compiler_params=_params("parallel"),
        name=name,
    )(x, m1, m2, *consts)


def _conv_tile(win, w_rows):
    rows = win.shape[0]
    t = rows - CONV_BACK
    lead = CONV_BACK - (CONV_W - 1)
    acc = None
    for r in range(SUBLANES):
        rolled = win if r == 0 else pltpu.roll(win, rows - r, axis=0)
        for m in range(CONV_BACK // SUBLANES + 1):
            j = SUBLANES * m + r - lead
            if 0 <= j < CONV_W:
                term = rolled[SUBLANES * m:SUBLANES * m + t] * w_rows[j]
                acc = term if acc is None else acc + term
    return acc


def _prompt_conv_kernel(a_ref, w_ref, b_ref, o_ref, *, n_tiles):
    w_rows = [w_ref[j:j + 1, :] for j in range(CONV_W)]
    bias = b_ref[...]
    c = a_ref.shape[1]
    win0 = jnp.concatenate([jnp.zeros((CONV_BACK, c), F32), a_ref[0:SEQ_TILE, :]], axis=0)
    o_ref[0:SEQ_TILE, :] = _conv_tile(win0, w_rows) + bias

    def body(i, carry):
        s0 = pl.multiple_of(i * SEQ_TILE, SEQ_TILE)
        win = a_ref[pl.ds(s0 - CONV_BACK, SEQ_TILE + CONV_BACK), :]
        o_ref[pl.ds(s0, SEQ_TILE), :] = _conv_tile(win, w_rows) + bias
        return carry

    lax.fori_loop(1, n_tiles, body, 0)


def _prompt_conv(a, w, b, *, nb, sp):
    n, c = a.shape
    blk = pl.BlockSpec((sp, LANES), lambda bi, ci: (bi, ci))
    return pl.pallas_call(
        functools.partial(_prompt_conv_kernel, n_tiles=sp // SEQ_TILE),
        grid=(nb, c // LANES),
        in_specs=[blk, pl.BlockSpec((CONV_W, LANES), lambda bi, ci: (0, ci)),
                  pl.BlockSpec((1, LANES), lambda bi, ci: (0, ci))],
        out_specs=blk,
        out_shape=jax.ShapeDtypeStruct((n, c), F32),
        compiler_params=_params("parallel", "parallel"),
        name="prompt_conv",
    )(a, w, b)


def _sample_conv_kernel(st_ref, a_ref, w_ref, b_ref, cv_in_ref, o_ref, st_out_ref, ext_ref, *, ds):
    del cv_in_ref
    dbt, hist, c = st_ref.shape
    off = CONV_BACK - hist
    ext_ref[:, off:CONV_BACK, :] = st_ref[...]
    ext_ref[:, CONV_BACK:CONV_BACK + ds, :] = a_ref[...].reshape(dbt, ds, c)
    acc = None
    for j in range(CONV_W):
        term = ext_ref[:, off + j:off + j + ds, :] * w_ref[j:j + 1, :]
        acc = term if acc is None else acc + term
    o_ref[...] = (acc + b_ref[...]).reshape(dbt * ds, c)
    st_out_ref[...] = ext_ref[:, off + ds:CONV_BACK + ds, :]


def _sample_conv(state, layer, a, w, b, cv, *, s0, db, ds, dbt):
    n, c = a.shape
    hist = state.shape[2]
    rows = dbt * ds
    row_blk = pl.BlockSpec((rows, c), lambda i: (s0 // rows + i, 0))
    return pl.pallas_call(
        functools.partial(_sample_conv_kernel, ds=ds),
        grid=(db // dbt,),
        in_specs=[pl.BlockSpec((dbt, None, hist, c), lambda i: (i, layer, 0, 0)), row_blk,
                  _full(w.shape), _full(b.shape), pl.BlockSpec(memory_space=pl.ANY)],
        out_specs=[row_blk, pl.BlockSpec((dbt, hist, c), lambda i: (i, 0, 0))],
        out_shape=[jax.ShapeDtypeStruct((n, c), F32), jax.ShapeDtypeStruct((db, hist, c), F32)],
        scratch_shapes=[pltpu.VMEM((dbt, CONV_BACK + ds, c), F32)],
        input_output_aliases={4: 0},
        compiler_params=_params("parallel"),
        name="sample_conv",
    )(state, a, w, b, cv)


def _fox_prep_kernel(q_ref, k_ref, v_ref, lf_ref, *rest, fp, ns, n_prev, emit):
    prev, rest = rest[:2 * n_prev], rest[2 * n_prev:]
    qa_ref, ka_ref, vat_ref = rest[:3]
    carry_ref = rest[5] if emit else rest[3]
    s = pl.program_id(1)

    @pl.when(s < ns)
    def _():
        _fox_operands(s, q_ref, k_ref, v_ref, lf_ref, qa_ref, ka_ref, vat_ref, carry_ref, fp)

    if not emit:
        return
    kt_ref, vt_ref, _, kc_ref, vc_ref = rest[3:]

    @pl.when(s == 0)
    def _():
        kc_ref[...] = jnp.zeros_like(kc_ref)
        vc_ref[...] = jnp.zeros_like(vc_ref)

    t = SEQ_TILE
    k_srcs = list(prev[0::2]) + [k_ref]
    v_srcs = list(prev[1::2]) + [v_ref]
    for srcs, held, dst in ((k_srcs, kc_ref, kt_ref), (v_srcs, vc_ref, vt_ref)):
        for li, src in enumerate(srcs):
            rows = jnp.concatenate([held[li], src[0:fp, :]], axis=0)
            for c in range(src.shape[1] // LANES):
                dst[li, c * LANES:(c + 1) * LANES, :] = rows[:, c * LANES:(c + 1) * LANES].T
            held[li] = src[fp:t, :]


def _fox_operands(s, q_ref, k_ref, v_ref, lf_ref, qa_ref, ka_ref, vat_ref, carry_ref, fp):
    @pl.when(s == 0)
    def _():
        carry_ref[...] = jnp.zeros_like(carry_ref)

    t = SEQ_TILE
    row = lax.broadcasted_iota(jnp.int32, (t, LANES), 0)
    lane = lax.broadcasted_iota(jnp.int32, (t, LANES), 1)
    c = jnp.where(s * t + row >= fp, lf_ref[...], 0.0)
    sh = 1
    while sh < t:
        c = c + jnp.where(row >= sh, pltpu.roll(c, sh, axis=0), 0.0)
        sh *= 2
    c = c + carry_ref[0:1, :]
    carry_ref[0:1, :] = c[t - 1:t, :]

    c = c * LOG2E
    c1 = c.astype(BF16).astype(F32)
    r1 = c - c1
    c2 = r1.astype(BF16).astype(F32)
    c3 = (r1 - c2).astype(BF16).astype(F32)
    g23 = jnp.where(lane < 2 * N_HEADS, pltpu.roll(c2, N_HEADS, axis=1),
                    jnp.where(lane < 3 * N_HEADS, pltpu.roll(c3, 2 * N_HEADS, axis=1), 0.0))
    g = jnp.where(lane < N_HEADS, c1, g23)
    gk = jnp.where(lane < N_HEADS, jnp.where(s * t + row >= fp, c1, -NEG), g23)
    ones_grp = (lane >= HEAD_DIM + AUX_ONES) & (lane < HEAD_DIM + 2 * AUX_ONES)
    ck = [jnp.where((lane >= HEAD_DIM) & (lane < HEAD_DIM + AUX_ONES), -pltpu.roll(gk, HEAD_DIM, axis=1),
                    jnp.where(ones_grp, 1.0, 0.0))]
    cq = [jnp.where(ones_grp, pltpu.roll(g, HEAD_DIM + AUX_ONES, axis=1), 0.0)]
    ck.append(pltpu.roll(ck[0], HEAD_DIM, axis=1))
    cq.append(pltpu.roll(cq[0], HEAD_DIM, axis=1))

    lane1 = lax.broadcasted_iota(jnp.int32, (1, LANES), 1)
    for h in range(N_HEADS):
        odd = h % 2
        pr = h // 2
        base = 0 if odd else HEAD_DIM
        a = lane1 - base
        mine = (lane1 & (N_HEADS - 1)) == h
        data = ((lane1 >= HEAD_DIM) if odd else (lane1 < HEAD_DIM)).astype(F32)
        sel1 = ((a >= 0) & (a < AUX_ONES) & mine).astype(F32)
        selc = ((a >= AUX_ONES) & (a < 2 * AUX_ONES) & mine).astype(F32)
        sl = slice(pr * LANES, (pr + 1) * LANES)
        qa_ref[h] = (q_ref[:, sl] * (data * LOG2E) + (cq[odd] * selc + sel1)).astype(BF16)
        ka_ref[h] = (k_ref[:, sl] * data + ck[odd]).astype(BF16)
        vat_ref[h] = (v_ref[:, sl] * data).T.astype(BF16)


def _fox_prep(q, k, v, lf, prev_kv, emit, *, nb, sp, fp):
    ns = sp // SEQ_TILE
    att = q.shape[1]
    t = SEQ_TILE
    last = ns - 1
    row = lambda c: pl.BlockSpec((t, c), lambda bi, si: (bi * ns + jnp.minimum(si, last), 0))
    hm = pl.BlockSpec((None, N_HEADS, t, LANES), lambda bi, si: (bi, 0, jnp.minimum(si, last), 0))
    hm_t = pl.BlockSpec((None, N_HEADS, LANES, t), lambda bi, si: (bi, 0, 0, jnp.minimum(si, last)))
    out_specs = [hm, hm, hm_t]
    out_shape = [jax.ShapeDtypeStruct((nb, N_HEADS, sp, LANES), BF16)] * 2 + [
        jax.ShapeDtypeStruct((nb, N_HEADS, LANES, sp), BF16)]
    scratch = [pltpu.VMEM((SUBLANES, LANES), F32)]
    n_prev = len(prev_kv) if emit else 0
    if emit:
        assert 0 < fp < t and fp % SUBLANES == 0
        n_layers = n_prev + 1
        cache = pl.BlockSpec((None, n_layers, att, t), lambda bi, si: (bi, 0, 0, jnp.maximum(si - 1, 0)))
        out_specs += [cache, cache]
        out_shape += [jax.ShapeDtypeStruct((nb, n_layers, att, sp - fp), F32)] * 2
        scratch += [pltpu.VMEM((n_layers, t - fp, att), F32)] * 2
    prev = [a for kv in prev_kv for a in kv] if emit else []
    return pl.pallas_call(
        functools.partial(_fox_prep_kernel, fp=fp, ns=ns, n_prev=n_prev, emit=emit),
        grid=(nb, ns + 1 if emit else ns),
        in_specs=[row(att), row(att), row(att), row(LANES)] + [row(att)] * len(prev),
        out_specs=out_specs,
        out_shape=out_shape,
        scratch_shapes=scratch,
        compiler_params=_params("parallel", "arbitrary"),
        name="fox_prep",
    )(q, k, v, lf, *prev)


def _flash_kernel(q_ref, k_ref, vt_ref, o_ref, s_ref, p_ref, acc_ref, *, hg):
    i = pl.program_id(2)
    t = SEQ_TILE
    top = lax.broadcasted_iota(jnp.int32, (LANES, t), 0) < HEAD_DIM

    def scores_to(j, slot):
        ks = pl.multiple_of(j * t, t)
        for h in range(hg):
            s_ref[slot, h] = lax.dot_general(k_ref[h, pl.ds(ks, t), :], q_ref[h], NT_DIMS,
                                             preferred_element_type=F32)

    def accumulate(j, als):
        ks = pl.multiple_of(j * t, t)
        pvs = [_dot(vt_ref[h, :, pl.ds(ks, t)], p_ref[h]) for h in range(hg)]
        for pr in range(hg // 2):
            acc_ref[pr] = (jnp.where(top, als[2 * pr], als[2 * pr + 1]) * acc_ref[pr]
                           + pvs[2 * pr] + pvs[2 * pr + 1])

    def softmax(slot, ms, ls, masked):
        new_m, new_l, als = [], [], []
        for h in range(hg):
            s = s_ref[slot, h]
            if masked:
                visible = (lax.broadcasted_iota(jnp.int32, (t, t), 0) <= lax.broadcasted_iota(jnp.int32, (t, t), 1))
                s = jnp.where(visible, s, NEG)
            mn = jnp.maximum(ms[h], jnp.max(s, axis=0, keepdims=True))
            p = jnp.exp2(s - mn)
            al = jnp.exp2(ms[h] - mn)
            new_m.append(mn)
            new_l.append(al * ls[h] + jnp.sum(p, axis=0, keepdims=True))
            als.append(al)
            p_ref[h] = p.astype(BF16)
        return tuple(new_m), tuple(new_l), tuple(als)

    def stage(j, slot, state):
        ms, ls, als = state
        scores_to(j + 1, 1 - slot)
        accumulate(jnp.maximum(j - 1, 0), als)
        return softmax(slot, ms, ls, False)

    def pair(jj, state):
        return stage(2 * jj + 1, 1, stage(2 * jj, 0, state))

    def finish(slot, state):
        ms, ls, als = state
        accumulate(jnp.maximum(i - 1, 0), als)
        _, ls, als = softmax(slot, ms, ls, True)
        accumulate(i, als)
        for pr in range(hg // 2):
            out_t = acc_ref[pr] * jnp.where(top, 1.0 / ls[2 * pr], 1.0 / ls[2 * pr + 1])
            o_ref[:, pr * LANES:(pr + 1) * LANES] = out_t.T

    p_ref[...] = jnp.zeros_like(p_ref)
    acc_ref[...] = jnp.zeros_like(acc_ref)
    scores_to(0, 0)
    state = ((jnp.full((1, t), NEG, F32),) * hg, (jnp.zeros((1, t), F32),) * hg, (jnp.ones((1, t), F32),) * hg)
    state = lax.fori_loop(0, i // 2, pair, state)
    odd = i % 2 == 1
    state = lax.cond(odd, lambda st: stage(i - 1, 0, st), lambda st: st, state)
    pl.when(odd)(lambda: finish(1, state))
    pl.when(jnp.logical_not(odd))(lambda: finish(0, state))


def _flash(qa, ka, vat, *, n, hg):
    nb, _, sp, _ = qa.shape
    ns = sp // SEQ_TILE
    return pl.pallas_call(
        functools.partial(_flash_kernel, hg=hg),
        grid=(nb, N_HEADS // hg, ns),
        in_specs=[pl.BlockSpec((None, hg, SEQ_TILE, LANES), lambda bi, gi, qi: (bi, gi, qi, 0)),
                  pl.BlockSpec((None, hg, sp, LANES), lambda bi, gi, qi: (bi, gi, 0, 0)),
                  pl.BlockSpec((None, hg, LANES, sp), lambda bi, gi, qi: (bi, gi, 0, 0))],
        out_specs=pl.BlockSpec((SEQ_TILE, hg * HEAD_DIM), lambda bi, gi, qi: (bi * ns + qi, gi)),
        out_shape=jax.ShapeDtypeStruct((n, N_HEADS * HEAD_DIM), F32),
        scratch_shapes=[pltpu.VMEM((2, hg, SEQ_TILE, SEQ_TILE), F32), pltpu.VMEM((hg, SEQ_TILE, SEQ_TILE), BF16),
                        pltpu.VMEM((hg // 2, LANES, SEQ_TILE), F32)],
        compiler_params=_params("parallel", "parallel", "arbitrary"),
        name="fox_flash",
    )(qa, ka, vat)


def _sample_attn_kernel(pt_ref, q_ref, kn_ref, vn_ref, lfn_ref, *rest, n_pages, page):
    del pt_ref
    k_refs = rest[:n_pages]
    v_refs = rest[n_pages:2 * n_pages]
    lf_refs = rest[2 * n_pages:3 * n_pages]
    o_ref = rest[3 * n_pages + 1]
    ds, c = q_ref.shape
    rows = ds * N_HEADS
    q = q_ref[...]
    sub = lax.broadcasted_iota(jnp.int32, (N_HEADS, c), 0)
    lane = lax.broadcasted_iota(jnp.int32, (N_HEADS, c), 1)
    hm = ((lane >= sub * HEAD_DIM) & (lane < (sub + 1) * HEAD_DIM)).astype(F32)
    qbd = jnp.concatenate([jnp.broadcast_to(q[t:t + 1, :], (N_HEADS, c)) * hm for t in range(ds)],
                          axis=0).astype(BF16)
    hm_t = jnp.concatenate([hm] * ds, axis=0)

    groups = [list(range(g, min(g + 2, n_pages))) for g in range(0, n_pages, 2)]
    s_parts = []
    for grp in groups:
        kp = jnp.concatenate([k_refs[r][...] for r in grp], axis=1).astype(BF16)
        s_parts.append(_dot(qbd, kp))
    s_past = jnp.concatenate(s_parts, axis=1)
    past = n_pages * page

    lf = jnp.concatenate([r[...] for r in lf_refs], axis=1)
    lane_p = lax.broadcasted_iota(jnp.int32, (N_HEADS, past), 1)
    suf = lf
    sh = 1
    while sh < past:
        suf = suf + jnp.where(lane_p < past - sh, pltpu.roll(suf, past - sh, axis=1), 0.0)
        sh *= 2
    s_past = s_past + jnp.concatenate([suf - lf] * ds, axis=0)

    pad = jnp.zeros((LANES - ds, c), F32)
    kn = jnp.concatenate([kn_ref[...], pad], axis=0).astype(BF16)
    vn = jnp.concatenate([vn_ref[...], pad], axis=0).astype(BF16)
    s_new = lax.dot_general(qbd, kn, NT_DIMS, preferred_element_type=F32)
    cn = lfn_ref[...]
    sub_n = lax.broadcasted_iota(jnp.int32, cn.shape, 0)
    sh = 1
    while sh < ds:
        cn = cn + jnp.where(sub_n >= sh, pltpu.roll(cn, sh, axis=0), 0.0)
        sh *= 2
    cn_t = jnp.concatenate([cn, jnp.zeros((LANES - ds, LANES), F32)], axis=0).T[0:N_HEADS, :]
    rown = lax.broadcasted_iota(jnp.int32, (rows, LANES), 0)
    u = lax.broadcasted_iota(jnp.int32, (rows, LANES), 1)
    s_new = jnp.where(u * N_HEADS <= rown, s_new - jnp.concatenate([cn_t] * ds, axis=0), NEG)

    m = jnp.maximum(jnp.max(s_past, axis=1, keepdims=True), jnp.max(s_new, axis=1, keepdims=True))
    p_past = jnp.exp(s_past - m)
    p_new = jnp.exp(s_new - m)
    l = jnp.sum(p_past, axis=1, keepdims=True) + jnp.sum(p_new, axis=1, keepdims=True)
    o = _dot(p_new.astype(BF16), vn)
    for gi, grp in enumerate(groups):
        vp = jnp.concatenate([v_refs[r][...] for r in grp], axis=1).astype(BF16)
        lo = grp[0] * page
        o = o + lax.dot_general(p_past[:, lo:lo + len(grp) * page].astype(BF16), vp, NT_DIMS,
                                preferred_element_type=F32)
    o = o * (1.0 / l) * hm_t
    rr = lax.broadcasted_iota(jnp.int32, (ds, rows), 1)
    tt = lax.broadcasted_iota(jnp.int32, (ds, rows), 0)
    pick = ((rr >= tt * N_HEADS) & (rr < (tt + 1) * N_HEADS)).astype(BF16)
    o_ref[...] = _dot(pick, o.astype(BF16))


def _sample_attn(page_table, layer, q, k, v, lf, cache_k, cache_v, cache_lft, att, *, s0, db, ds):
    n, c = q.shape
    n_pages = page_table.shape[1]
    page = cache_k.shape[3]
    row = lambda cc: pl.BlockSpec((ds, cc), lambda i, pt: (s0 // ds + i, 0))

    def paged(shape, r):
        return pl.BlockSpec((None, None) + shape, lambda i, pt: (pt[i * n_pages + r], layer, 0, 0))

    in_specs = ([row(c), row(c), row(c), row(LANES)]
                + [paged((c, page), r) for r in range(n_pages)]
                + [paged((c, page), r) for r in range(n_pages)]
                + [paged((N_HEADS, page), r) for r in range(n_pages)]
                + [pl.BlockSpec(memory_space=pl.ANY)])
    n_in = len(in_specs)
    return pl.pallas_call(
        functools.partial(_sample_attn_kernel, n_pages=n_pages, page=page),
        grid_spec=pltpu.PrefetchScalarGridSpec(
            num_scalar_prefetch=1, grid=(db,), in_specs=in_specs, out_specs=row(c)),
        out_shape=jax.ShapeDtypeStruct((n, c), F32),
        input_output_aliases={n_in: 0},
        compiler_params=_params("parallel"),
        name="sample_paged_attn",
    )(page_table.reshape(-1), q, k, v, lf, *([cache_k] * n_pages), *([cache_v] * n_pages),
      *([cache_lft] * n_pages), att)


def _odd_tile(wy, wp, bg, w_rows, grp, pos0):
    t = bg.shape[0]
    ny = wy.shape[0]
    y0 = wy[SUBLANES:]
    y1 = pltpu.roll(wy, 1, axis=0)[SUBLANES:]
    y2 = pltpu.roll(wy, 2, axis=0)[SUBLANES:]
    out_c = bg * (w_rows[0] * y2 + w_rows[1] * y1 + w_rows[2] * y0)
    del ny
    back = 2 * SUBLANES
    s2 = wp + pltpu.roll(wp, 1, axis=0)
    s4 = s2 + pltpu.roll(s2, 2, axis=0)
    s8 = s4 + pltpu.roll(s4, 4, axis=0)
    s16 = s8 + pltpu.roll(s8, 8, axis=0)
    win = jnp.where(grp == 0, s2, jnp.where(grp == 1, s4, jnp.where(grp == 2, s8, s16)))[back:]
    wsz = jnp.where(grp == 0, POOL_WINDOWS[0],
                    jnp.where(grp == 1, POOL_WINDOWS[1], jnp.where(grp == 2, POOL_WINDOWS[2], POOL_WINDOWS[3])))
    pos = pos0 + lax.broadcasted_iota(jnp.int32, (t, LANES), 0)
    cnt = jnp.maximum(jnp.minimum(wsz, pos + 1), 1).astype(F32)
    d = win / cnt - wp[back:]
    return out_c, d


def _prompt_odd_kernel(y_ref, bg_ref, pd_ref, w_ref, oc_ref, d_ref, *, n_tiles, fp):
    grp = pl.program_id(1)
    w_rows = [w_ref[j:j + 1, :] for j in range(SCONV_W)]
    c = y_ref.shape[1]
    t = SEQ_TILE
    wy0 = jnp.concatenate([jnp.zeros((SUBLANES, c), F32), y_ref[0:t, :]], axis=0)
    wp0 = jnp.concatenate([jnp.zeros((2 * SUBLANES, c), F32), pd_ref[0:t, :]], axis=0)
    oc, d = _odd_tile(wy0, wp0, bg_ref[0:t, :], w_rows, grp, -fp)
    oc_ref[0:t, :] = oc
    d_ref[0:t, :] = d

    def body(i, carry):
        s0 = pl.multiple_of(i * t, t)
        wy = y_ref[pl.ds(s0 - SUBLANES, t + SUBLANES), :]
        wp = pd_ref[pl.ds(s0 - 2 * SUBLANES, t + 2 * SUBLANES), :]
        oc, d = _odd_tile(wy, wp, bg_ref[pl.ds(s0, t), :], w_rows, grp, s0 - fp)
        oc_ref[pl.ds(s0, t), :] = oc
        d_ref[pl.ds(s0, t), :] = d
        return carry

    lax.fori_loop(1, n_tiles, body, 0)


def _prompt_odd(y, bg, pd, w, *, nb, sp, fp):
    n, c = y.shape
    assert c // LANES == len(POOL_WINDOWS)
    blk = pl.BlockSpec((sp, LANES), lambda bi, ci: (bi, ci))
    return pl.pallas_call(
        functools.partial(_prompt_odd_kernel, n_tiles=sp // SEQ_TILE, fp=fp),
        grid=(nb, c // LANES),
        in_specs=[blk, blk, blk, pl.BlockSpec((SCONV_W, LANES), lambda bi, ci: (0, ci))],
        out_specs=[blk, blk],
        out_shape=[jax.ShapeDtypeStruct((n, c), F32)] * 2,
        compiler_params=_params("parallel", "parallel"),
        name="prompt_sconv_pool",
    )(y, bg, pd, w)


def _sample_odd_kernel(ss_ref, sp_ref, y_ref, bg_ref, pd_ref, w_ref, oc_in_ref, d_in_ref,
                       oc_ref, d_ref, ss_out_ref, sp_out_ref, yext_ref, pext_ref, *, ds):
    del oc_in_ref, d_in_ref
    dbt, _, c = ss_ref.shape
    hs = SCONV_W - 1
    y = y_ref[...].reshape(dbt, ds, c)
    pd = pd_ref[...].reshape(dbt, ds, c)
    bg = bg_ref[...].reshape(dbt, ds, c)
    yext_ref[:, SUBLANES - hs:SUBLANES, :] = ss_ref[...]
    yext_ref[:, SUBLANES:SUBLANES + ds, :] = y
    conv = None
    for j in range(SCONV_W):
        lo = SUBLANES - hs + j
        term = yext_ref[:, lo:lo + ds, :] * w_ref[j:j + 1, :]
        conv = term if conv is None else conv + term
    oc_ref[...] = (bg * conv).reshape(dbt * ds, c)
    ss_out_ref[...] = yext_ref[:, SUBLANES + ds - hs:SUBLANES + ds, :]

    base = 2 * SUBLANES
    pext_ref[:, base - POOL_HIST:base, :] = sp_ref[...]
    pext_ref[:, base:base + ds, :] = pd
    run = pd
    sums = {}
    for i in range(1, max(POOL_WINDOWS)):
        run = run + pext_ref[:, base - i:base - i + ds, :]
        if i + 1 in POOL_WINDOWS:
            sums[i + 1] = run
    gc = c // len(POOL_WINDOWS)
    lane = lax.broadcasted_iota(jnp.int32, (dbt, ds, c), 2)
    mean = sums[POOL_WINDOWS[-1]] / float(POOL_WINDOWS[-1])
    for gi in range(len(POOL_WINDOWS) - 2, -1, -1):
        mean = jnp.where(lane < (gi + 1) * gc, sums[POOL_WINDOWS[gi]] / float(POOL_WINDOWS[gi]), mean)
    d_ref[...] = (mean - pd).reshape(dbt * ds, c)
    sp_out_ref[...] = pext_ref[:, base + ds - POOL_HIST:base + ds, :]


def _sample_odd(state_s, state_p, layer, y, bg, pd, w, oc, dd, *, s0, db, ds, dbt):
    n, c = y.shape
    rows = dbt * ds
    row_blk = pl.BlockSpec((rows, c), lambda i: (s0 // rows + i, 0))
    hs = state_s.shape[2]
    hp = state_p.shape[2]
    anyspec = pl.BlockSpec(memory_space=pl.ANY)
    return pl.pallas_call(
        functools.partial(_sample_odd_kernel, ds=ds),
        grid=(db // dbt,),
        in_specs=[pl.BlockSpec((dbt, None, hs, c), lambda i: (i, layer, 0, 0)),
                  pl.BlockSpec((dbt, None, hp, c), lambda i: (i, layer, 0, 0)),
                  row_blk, row_blk, row_blk, _full(w.shape), anyspec, anyspec],
        out_specs=[row_blk, row_blk, pl.BlockSpec((dbt, hs, c), lambda i: (i, 0, 0)),
                   pl.BlockSpec((dbt, hp, c), lambda i: (i, 0, 0))],
        out_shape=[jax.ShapeDtypeStruct((n, c), F32), jax.ShapeDtypeStruct((n, c), F32),
                   jax.ShapeDtypeStruct((db, hs, c), F32), jax.ShapeDtypeStruct((db, hp, c), F32)],
        scratch_shapes=[pltpu.VMEM((dbt, SUBLANES + ds, c), F32), pltpu.VMEM((dbt, 2 * SUBLANES + ds, c), F32)],
        input_output_aliases={6: 0, 7: 1},
        compiler_params=_params("parallel"),
        name="sample_sconv_pool",
    )(state_s, state_p, y, bg, pd, w, oc, dd)


def kernel(x_prompt, x_sample, cache_k, cache_v, cache_logf, page_table, state_conv_a, state_sconv, state_pool, meta_tokens, w_in_even, b_forget, w_dw_a, b_dw_a, ln_a_g, ln_a_b, w_out_even, w_in_odd, w_sconv, w_pool_mix, pool_scale, w_out_odd, ln_mix_g, ln_mix_b, w_ffn_gate, w_ffn_up, w_ffn_down, ln_ffn_g, ln_ffn_b):
    nb, seq, d = x_prompt.shape
    db, ds, _ = x_sample.shape
    depth = w_ffn_gate.shape[0]
    alpha = float((2 * depth) ** 0.25)
    ca = state_conv_a.shape[-1]
    att = N_HEADS * HEAD_DIM
    assert ds == SUBLANES and cache_k.shape[3] == N_HEADS and cache_k.shape[4] == HEAD_DIM
    assert state_conv_a.shape[2] == CONV_W - 1 and state_pool.shape[2] == POOL_HIST
    assert meta_tokens.shape[0] == N_META

    s_real = N_META + seq
    fp = (-s_real) % SEQ_TILE
    if fp < CONV_BACK:
        fp += SEQ_TILE
    sp = fp + s_real
    s0 = nb * sp
    n = s0 + db * ds
    tm = next(t for t in (512, 256, 128, 64, 32, 16, 8) if n % t == 0)
    dbt = next(t for t in (8, 4, 2, 1) if db % t == 0)

    head = jnp.concatenate([jnp.zeros((fp, d), F32), meta_tokens.astype(F32)], axis=0)
    x = jnp.concatenate([piece for bi in range(nb) for piece in (head, x_prompt[bi])]
                        + [x_sample.reshape(db * ds, d)], axis=0)

    n_phys = cache_k.shape[0]
    n_even = cache_k.shape[1]
    page = cache_k.shape[2]
    ck = jnp.transpose(cache_k, (0, 1, 3, 4, 2)).reshape(n_phys, n_even, att, page)
    cv = jnp.transpose(cache_v, (0, 1, 3, 4, 2)).reshape(n_phys, n_even, att, page)
    clft = jnp.swapaxes(cache_logf, 2, 3)

    row2 = lambda v: v.reshape(1, -1).astype(F32)

    def prompt_rows(arr, lo, hi):
        return jnp.stack([arr[bi * sp + lo:bi * sp + hi] for bi in range(nb)], axis=0)

    def sample_rows(arr):
        return arr[s0:].reshape(db, ds, -1)

    wg_all, wu_all, wd_all = (w.astype(BF16) for w in (w_ffn_gate, w_ffn_up, w_ffn_down))
    wo_even, wo_odd = w_out_even.astype(BF16), w_out_odd.astype(BF16)
    wi_even = jnp.pad(w_in_even.astype(BF16), ((0, 0), (0, 0), (0, LANES - N_HEADS)))
    wi_odd = w_in_odd.astype(BF16)

    def tail(body, name, xin, m1, m2, p1, p2, w_out_all, layer):
        return _layer_tail(body, name, xin, m1, m2, p1, p2, w_out_all,
                           row2(ln_mix_g[layer]), row2(ln_mix_b[layer]), wg_all, wu_all, wd_all,
                           row2(ln_ffn_g[layer]), row2(ln_ffn_b[layer]),
                           mix_idx=layer // 2, layer=layer, tm=tm, alpha=alpha, nb=nb, sp=sp, fp=fp)

    prev_kv = []
    lfp, cap, scp, plp = [], [], [], []
    ks_, vs_, lfs, cas, scs, pls = [], [], [], [], [], []
    for layer in range(depth):
        i = layer // 2
        if layer % 2 == 0:
            bf = jnp.pad(b_forget[i].astype(F32), (0, LANES - N_HEADS)).reshape(1, LANES)
            a, q, k, v, lf = _even_in(x, wi_even, i, bf, tm=tm, ca=ca, att=att)
            conv = _prompt_conv(a, w_dw_a[i].astype(F32), row2(b_dw_a[i]), nb=nb, sp=sp)
            conv, st_a = _sample_conv(state_conv_a.astype(F32), i, a, w_dw_a[i].astype(F32), row2(b_dw_a[i]),
                                      conv, s0=s0, db=db, ds=ds, dbt=dbt)
            qa, ka, vat, *caches = _fox_prep(q, k, v, lf, prev_kv, i == n_even - 1, nb=nb, sp=sp, fp=fp)
            prev_kv.append((k, v))
            at = _flash(qa, ka, vat, n=n, hg=FLASH_HEADS)
            at = _sample_attn(page_table, i, q, k, v, lf, ck, cv, clft, at, s0=s0, db=db, ds=ds)
            x = tail(_even_tail_kernel, "even_tail", x, conv, at, row2(ln_a_g[i]), row2(ln_a_b[i]),
                     wo_even, layer)
            lfp.append(prompt_rows(lf, fp, sp)[..., :N_HEADS])
            cap.append(prompt_rows(a, sp - (CONV_W - 1), sp))
            ks_.append(sample_rows(k).reshape(db, ds, N_HEADS, HEAD_DIM))
            vs_.append(sample_rows(v).reshape(db, ds, N_HEADS, HEAD_DIM))
            lfs.append(sample_rows(lf)[..., :N_HEADS])
            cas.append(st_a)
        else:
            c = state_sconv.shape[-1]
            y, bg, pd = _odd_in(x, wi_odd, i, tm=tm, c=c)
            oc, dd = _prompt_odd(y, bg, pd, w_sconv[i].astype(F32), nb=nb, sp=sp, fp=fp)
            oc, dd, st_s, st_p = _sample_odd(state_sconv.astype(F32), state_pool.astype(F32), i, y, bg, pd,
                                             w_sconv[i].astype(F32), oc, dd, s0=s0, db=db, ds=ds, dbt=dbt)
            wm = jax.scipy.linalg.block_diag(*[w_pool_mix[i, g] for g in range(w_pool_mix.shape[1])]).astype(BF16)
            x = tail(_odd_tail_kernel, "odd_tail", x, oc, dd, wm, row2(pool_scale[i]),
                     wo_odd, layer)
            scp.append(prompt_rows(y, sp - (SCONV_W - 1), sp))
            plp.append(prompt_rows(pd, sp - POOL_HIST, sp))
            scs.append(st_s)
            pls.append(st_p)

    y_prompt = prompt_rows(x, fp + N_META, sp)
    y_sample = sample_rows(x)
    st = lambda xs: jnp.stack(xs, axis=1)
    kp, vp = (jnp.transpose(c.reshape(nb, n_even, N_HEADS, HEAD_DIM, s_real), (0, 1, 4, 2, 3)) for c in caches)
    return (y_prompt, y_sample, kp, vp, st(lfp), st(cap), st(scp), st(plp),
            st(ks_), st(vs_), st(lfs), st(cas), st(scs), st(pls))
```

```python
import functools

import jax
import jax.numpy as jnp
from jax import lax
from jax.experimental import pallas as pl
from jax.experimental.pallas import tpu as pltpu

N_META = 16
N_HEADS = 8
HEAD_DIM = 64
CONV_W = 31
CONV_BACK = 32
SCONV_W = 3
POOL_WINDOWS = (2, 4, 8, 16)
POOL_HIST = max(POOL_WINDOWS) - 1
LN_EPS = 1e-5
SEQ_TILE = 256
LANES = 128
SUBLANES = 8
AUX_ONES = 24
FLASH_HEADS = 4
VMEM_LIMIT = 56 * 1024 * 1024
NEG = -1e30
LOG2E = 1.4426950408889634
F32 = jnp.float32
BF16 = jnp.bfloat16
NT_DIMS = (((1,), (1,)), ((), ()))


def _params(*sem):
    return pltpu.CompilerParams(dimension_semantics=sem, vmem_limit_bytes=VMEM_LIMIT)


def _dot(a, b):
    return jnp.dot(a, b, preferred_element_type=F32)


def _ln(z, g, b):
    mu = jnp.mean(z, axis=-1, keepdims=True)
    zc = z - mu
    var = jnp.mean(zc * zc, axis=-1, keepdims=True)
    return zc * lax.rsqrt(var + LN_EPS) * g + b


def _silu(x):
    return x * jax.nn.sigmoid(x)


def _full(shape):
    return pl.BlockSpec(shape, lambda *_: (0,) * len(shape))


def _slab(a, idx):
    return pl.BlockSpec((None,) + a.shape[1:], lambda *_: (idx,) + (0,) * (a.ndim - 1),
                        pipeline_mode=pl.Buffered(1))


def _even_in_kernel(x_ref, w_ref, bf_ref, a_ref, q_ref, k_ref, v_ref, lf_ref, *, ca, att):
    xb = x_ref[...].astype(BF16)

    def mm(lo, hi):
        return _dot(xb, w_ref[:, lo:hi])

    u = mm(0, ca)
    g = mm(ca, 2 * ca)
    a_ref[...] = u * jax.nn.sigmoid(g)
    o = 2 * ca
    q_ref[...] = mm(o, o + att) * (HEAD_DIM ** -0.5)
    k_ref[...] = mm(o + att, o + 2 * att)
    v_ref[...] = mm(o + 2 * att, o + 3 * att)
    z = mm(o + 3 * att, o + 3 * att + LANES) + bf_ref[...]
    lf = jnp.minimum(z, 0.0) - jnp.log1p(jnp.exp(-jnp.abs(z)))
    lane = lax.broadcasted_iota(jnp.int32, lf.shape, 1)
    lf_ref[...] = jnp.where(lane < N_HEADS, lf, 0.0)


def _even_in(x, w, idx, bf, *, tm, ca, att):
    n, d = x.shape
    row = lambda c: pl.BlockSpec((tm, c), lambda i: (i, 0))
    return pl.pallas_call(
        functools.partial(_even_in_kernel, ca=ca, att=att),
        grid=(n // tm,),
        in_specs=[row(d), _slab(w, idx), _full(bf.shape)],
        out_specs=[row(ca), row(att), row(att), row(att), row(LANES)],
        out_shape=[jax.ShapeDtypeStruct((n, c), F32) for c in (ca, att, att, att, LANES)],
        compiler_params=_params("parallel"),
        name="even_in_proj",
    )(x, w, bf)


def _odd_in_kernel(x_ref, w_ref, y_ref, bg_ref, pd_ref, *, c):
    xb = x_ref[...].astype(BF16)
    hc = _dot(xb, w_ref[:, 0:c])
    bg_ref[...] = _dot(xb, w_ref[:, c:2 * c])
    cg = _dot(xb, w_ref[:, 2 * c:3 * c])
    y_ref[...] = cg * hc
    pd_ref[...] = _dot(xb, w_ref[:, 3 * c:4 * c])


def _odd_in(x, w, idx, *, tm, c):
    n, d = x.shape
    row = lambda cc: pl.BlockSpec((tm, cc), lambda i: (i, 0))
    return pl.pallas_call(
        functools.partial(_odd_in_kernel, c=c),
        grid=(n // tm,),
        in_specs=[row(d), _slab(w, idx)],
        out_specs=[row(c)] * 3,
        out_shape=[jax.ShapeDtypeStruct((n, c), F32)] * 3,
        compiler_params=_params("parallel"),
        name="odd_in_proj",
    )(x, w)


def _tail_kernel(x_ref, m1p_ref, m2p_ref, m1s_ref, m2s_ref, p1_ref, p2_ref, w_ref, g1_ref, b1_ref,
                 wg_ref, wu_ref, wd_ref, g2_ref, b2_ref, *rest, even, alpha, chunk, nb, sp, fp, npt, lead):
    i = pl.program_id(0)
    is_sample = i >= npt
    m1 = jnp.where(is_sample, m1s_ref[...], m1p_ref[...])
    m2 = jnp.where(is_sample, m2s_ref[...], m2p_ref[...])
    if even:
        m1 = _silu(_ln(m1, p1_ref[...], p2_ref[...]))
    else:
        m2 = _dot(m2.astype(BF16), p1_ref[...]) * p2_ref[...]
    ca = m1.shape[1]
    mix = _dot(m1.astype(BF16), w_ref[0:ca, :]) + _dot(m2.astype(BF16), w_ref[ca:, :])
    x1 = _ln(alpha * x_ref[...] + mix, g1_ref[...], b1_ref[...])

    h_ref = rest[-1]
    xb = x1.astype(BF16)
    for c in range(0, wg_ref.shape[1], chunk):
        gate = _dot(xb, wg_ref[:, c:c + chunk])
        up = _dot(xb, wu_ref[:, c:c + chunk])
        h_ref[:, c:c + chunk] = (_silu(gate) * up).astype(BF16)
    out = _ln(alpha * x1 + _dot(h_ref[...], wd_ref[...]), g2_ref[...], b2_ref[...])

    tm = x1.shape[0]
    if lead is None:
        r = i * tm + lax.broadcasted_iota(jnp.int32, (tm, 1), 0)
        keep = jnp.ones((tm, 1), F32)
        for bi in range(nb):
            keep = jnp.where((r >= bi * sp) & (r < bi * sp + fp), 0.0, keep)
        rest[0][...] = out * keep
    else:
        yp_ref, ys_ref = rest[:2]

        @pl.when(jnp.logical_and(jnp.logical_not(is_sample), i % (sp // tm) >= lead))
        def _():
            yp_ref[...] = out

        @pl.when(is_sample)
        def _():
            ys_ref[...] = out


def _layer_tail(name, x, m1p, m2p, m1s, m2s, p1, p2, w, g1, b1, wg, wu, wd, g2, b2, *, even, mix_idx, layer, tm,
                alpha, nb, sp, fp, out_rows=None):
    n, d = x.shape
    s0 = m1p.shape[0]
    dff = wg.shape[2]
    chunk = SEQ_TILE if dff % SEQ_TILE == 0 else dff
    npt = s0 // tm
    row = lambda c: pl.BlockSpec((tm, c), lambda i: (i, 0))
    prow = lambda c: pl.BlockSpec((tm, c), lambda i: (jnp.minimum(i, npt - 1), 0))
    srow = lambda c: pl.BlockSpec((tm, c), lambda i: (jnp.maximum(i - npt, 0), 0))
    once = lambda a: pl.BlockSpec(a.shape, lambda i: (0,) * a.ndim, pipeline_mode=pl.Buffered(1))
    consts = (p1, p2, w, g1, b1, wg, wu, wd, g2, b2)
    const_specs = [once(p1), once(p2), _slab(w, mix_idx), once(g1), once(b1),
                   _slab(wg, layer), _slab(wu, layer), _slab(wd, layer), once(g2), once(b2)]
    if out_rows is None:
        lead = None
        out_specs = row(d)
        out_shape = jax.ShapeDtypeStruct((n, d), F32)
    else:
        seq, db_rows = out_rows
        tps = sp // tm
        lead = (sp - seq) // tm
        out_specs = [
            pl.BlockSpec((None, tm, d), lambda i: (jnp.minimum(i // tps, nb - 1),
                                                   jnp.where(i < npt, jnp.maximum(i % tps - lead, 0), tps - lead - 1),
                                                   0)),
            srow(d)]
        out_shape = [jax.ShapeDtypeStruct((nb, seq, d), F32), jax.ShapeDtypeStruct((db_rows, d), F32)]
    return pl.pallas_call(
        functools.partial(_tail_kernel, even=even, alpha=alpha, chunk=chunk, nb=nb, sp=sp, fp=fp, npt=npt, lead=lead),
        grid=(n // tm,),
        in_specs=[row(d), prow(m1p.shape[1]), prow(m2p.shape[1]), srow(m1s.shape[1]), srow(m2s.shape[1])]
        + const_specs,
        out_specs=out_specs,
        out_shape=out_shape,
        scratch_shapes=[pltpu.VMEM((tm, dff), BF16)],
        compiler_params=_params("arbitrary" if out_rows else "parallel"),
        name=name,
    )(x, m1p, m2p, m1s, m2s, *consts)


def _conv_tile(win, w_rows):
    rows = win.shape[0]
    t = rows - CONV_BACK
    lead = CONV_BACK - (CONV_W - 1)
    acc = None
    for r in range(SUBLANES):
        rolled = win if r == 0 else pltpu.roll(win, rows - r, axis=0)
        for m in range(CONV_BACK // SUBLANES + 1):
            j = SUBLANES * m + r - lead
            if 0 <= j < CONV_W:
                term = rolled[SUBLANES * m:SUBLANES * m + t] * w_rows[j]
                acc = term if acc is None else acc + term
    return acc


def _prompt_conv_kernel(a_ref, w_ref, b_ref, o_ref, *, n_tiles):
    w_rows = [w_ref[j:j + 1, :] for j in range(CONV_W)]
    bias = b_ref[...]
    c = a_ref.shape[1]
    win0 = jnp.concatenate([jnp.zeros((CONV_BACK, c), F32), a_ref[0:SEQ_TILE, :]], axis=0)
    o_ref[0:SEQ_TILE, :] = _conv_tile(win0, w_rows) + bias

    def body(i, carry):
        s0 = pl.multiple_of(i * SEQ_TILE, SEQ_TILE)
        win = a_ref[pl.ds(s0 - CONV_BACK, SEQ_TILE + CONV_BACK), :]
        o_ref[pl.ds(s0, SEQ_TILE), :] = _conv_tile(win, w_rows) + bias
        return carry

    lax.fori_loop(1, n_tiles, body, 0)


def _prompt_conv(a, w, b, *, nb, sp):
    n, c = a.shape
    blk = pl.BlockSpec((sp, LANES), lambda bi, ci: (bi, ci))
    return pl.pallas_call(
        functools.partial(_prompt_conv_kernel, n_tiles=sp // SEQ_TILE),
        grid=(nb, c // LANES),
        in_specs=[blk, pl.BlockSpec((CONV_W, LANES), lambda bi, ci: (0, ci)),
                  pl.BlockSpec((1, LANES), lambda bi, ci: (0, ci))],
        out_specs=blk,
        out_shape=jax.ShapeDtypeStruct((nb * sp, c), F32),
        compiler_params=_params("parallel", "parallel"),
        name="prompt_conv",
    )(a, w, b)


def _sample_conv_kernel(st_ref, a_ref, w_ref, b_ref, o_ref, st_out_ref, ext_ref, *, ds):
    dbt, hist, c = st_ref.shape
    off = CONV_BACK - hist
    ext_ref[:, off:CONV_BACK, :] = st_ref[...]
    ext_ref[:, CONV_BACK:CONV_BACK + ds, :] = a_ref[...].reshape(dbt, ds, c)
    acc = None
    for j in range(CONV_W):
        term = ext_ref[:, off + j:off + j + ds, :] * w_ref[j:j + 1, :]
        acc = term if acc is None else acc + term
    o_ref[...] = (acc + b_ref[...]).reshape(dbt * ds, c)
    st_out_ref[...] = ext_ref[:, off + ds:CONV_BACK + ds, :]


def _sample_conv(state, layer, a, w, b, *, s0, db, ds, dbt):
    n, c = a.shape
    hist = state.shape[2]
    rows = dbt * ds
    return pl.pallas_call(
        functools.partial(_sample_conv_kernel, ds=ds),
        grid=(db // dbt,),
        in_specs=[pl.BlockSpec((dbt, None, hist, c), lambda i: (i, layer, 0, 0)),
                  pl.BlockSpec((rows, c), lambda i: (s0 // rows + i, 0)), _full(w.shape), _full(b.shape)],
        out_specs=[pl.BlockSpec((rows, c), lambda i: (i, 0)), pl.BlockSpec((dbt, hist, c), lambda i: (i, 0, 0))],
        out_shape=[jax.ShapeDtypeStruct((db * ds, c), F32), jax.ShapeDtypeStruct((db, hist, c), F32)],
        scratch_shapes=[pltpu.VMEM((dbt, CONV_BACK + ds, c), F32)],
        compiler_params=_params("parallel"),
        name="sample_conv",
    )(state, a, w, b)


def _fox_prep_kernel(q_ref, k_ref, v_ref, lf_ref, *rest, fp, ns, n_prev, emit):
    prev, rest = rest[:2 * n_prev], rest[2 * n_prev:]
    qa_ref, ka_ref, vat_ref = rest[:3]
    carry_ref = rest[5] if emit else rest[3]
    s = pl.program_id(1)

    @pl.when(s < ns)
    def _():
        _fox_operands(s, q_ref, k_ref, v_ref, lf_ref, qa_ref, ka_ref, vat_ref, carry_ref, fp)

    if not emit:
        return
    kt_ref, vt_ref, _, kc_ref, vc_ref = rest[3:]

    @pl.when(s == 0)
    def _():
        kc_ref[...] = jnp.zeros_like(kc_ref)
        vc_ref[...] = jnp.zeros_like(vc_ref)

    t = SEQ_TILE
    k_srcs = list(prev[0::2]) + [k_ref]
    v_srcs = list(prev[1::2]) + [v_ref]
    for srcs, held, dst in ((k_srcs, kc_ref, kt_ref), (v_srcs, vc_ref, vt_ref)):
        for li, src in enumerate(srcs):
            rows = jnp.concatenate([held[li], src[0:fp, :]], axis=0)
            for c in range(src.shape[1] // LANES):
                dst[li, c * LANES:(c + 1) * LANES, :] = rows[:, c * LANES:(c + 1) * LANES].T
            held[li] = src[fp:t, :]


def _fox_operands(s, q_ref, k_ref, v_ref, lf_ref, qa_ref, ka_ref, vat_ref, carry_ref, fp):
    @pl.when(s == 0)
    def _():
        carry_ref[...] = jnp.zeros_like(carry_ref)

    t = SEQ_TILE
    row = lax.broadcasted_iota(jnp.int32, (t, LANES), 0)
    lane = lax.broadcasted_iota(jnp.int32, (t, LANES), 1)
    c = jnp.where(s * t + row >= fp, lf_ref[...], 0.0)
    sh = 1
    while sh < t:
        c = c + jnp.where(row >= sh, pltpu.roll(c, sh, axis=0), 0.0)
        sh *= 2
    c = c + carry_ref[0:1, :]
    carry_ref[0:1, :] = c[t - 1:t, :]

    c = c * LOG2E
    c1 = c.astype(BF16).astype(F32)
    r1 = c - c1
    c2 = r1.astype(BF16).astype(F32)
    c3 = (r1 - c2).astype(BF16).astype(F32)
    g23 = jnp.where(lane < 2 * N_HEADS, pltpu.roll(c2, N_HEADS, axis=1),
                    jnp.where(lane < 3 * N_HEADS, pltpu.roll(c3, 2 * N_HEADS, axis=1), 0.0))
    g = jnp.where(lane < N_HEADS, c1, g23)
    gk = jnp.where(lane < N_HEADS, jnp.where(s * t + row >= fp, c1, -NEG), g23)
    ones_grp = (lane >= HEAD_DIM + AUX_ONES) & (lane < HEAD_DIM + 2 * AUX_ONES)
    ck = [jnp.where((lane >= HEAD_DIM) & (lane < HEAD_DIM + AUX_ONES), -pltpu.roll(gk, HEAD_DIM, axis=1),
                    jnp.where(ones_grp, 1.0, 0.0))]
    cq = [jnp.where(ones_grp, pltpu.roll(g, HEAD_DIM + AUX_ONES, axis=1), 0.0)]
    ck.append(pltpu.roll(ck[0], HEAD_DIM, axis=1))
    cq.append(pltpu.roll(cq[0], HEAD_DIM, axis=1))

    lane1 = lax.broadcasted_iota(jnp.int32, (1, LANES), 1)
    for h in range(N_HEADS):
        odd = h % 2
        pr = h // 2
        base = 0 if odd else HEAD_DIM
        a = lane1 - base
        mine = (lane1 & (N_HEADS - 1)) == h
        data = ((lane1 >= HEAD_DIM) if odd else (lane1 < HEAD_DIM)).astype(F32)
        sel1 = ((a >= 0) & (a < AUX_ONES) & mine).astype(F32)
        selc = ((a >= AUX_ONES) & (a < 2 * AUX_ONES) & mine).astype(F32)
        sl = slice(pr * LANES, (pr + 1) * LANES)
        qa_ref[h] = (q_ref[:, sl] * (data * LOG2E) + (cq[odd] * selc + sel1)).astype(BF16)
        ka_ref[h] = (k_ref[:, sl] * data + ck[odd]).astype(BF16)
        vat_ref[h] = (v_ref[:, sl] * data).T.astype(BF16)


def _fox_prep(q, k, v, lf, prev_kv, emit, *, nb, sp, fp):
    ns = sp // SEQ_TILE
    att = q.shape[1]
    t = SEQ_TILE
    last = ns - 1
    row = lambda c: pl.BlockSpec((t, c), lambda bi, si: (bi * ns + jnp.minimum(si, last), 0))
    hm = pl.BlockSpec((None, N_HEADS, t, LANES), lambda bi, si: (bi, 0, jnp.minimum(si, last), 0))
    hm_t = pl.BlockSpec((None, N_HEADS, LANES, t), lambda bi, si: (bi, 0, 0, jnp.minimum(si, last)))
    out_specs = [hm, hm, hm_t]
    out_shape = [jax.ShapeDtypeStruct((nb, N_HEADS, sp, LANES), BF16)] * 2 + [
        jax.ShapeDtypeStruct((nb, N_HEADS, LANES, sp), BF16)]
    scratch = [pltpu.VMEM((SUBLANES, LANES), F32)]
    n_prev = len(prev_kv) if emit else 0
    if emit:
        assert 0 < fp < t and fp % SUBLANES == 0
        n_layers = n_prev + 1
        cache = pl.BlockSpec((None, n_layers, att, t), lambda bi, si: (bi, 0, 0, jnp.maximum(si - 1, 0)))
        out_specs += [cache, cache]
        out_shape += [jax.ShapeDtypeStruct((nb, n_layers, att, sp - fp), F32)] * 2
        scratch += [pltpu.VMEM((n_layers, t - fp, att), F32)] * 2
    prev = [a for kv in prev_kv for a in kv] if emit else []
    return pl.pallas_call(
        functools.partial(_fox_prep_kernel, fp=fp, ns=ns, n_prev=n_prev, emit=emit),
        grid=(nb, ns + 1 if emit else ns),
        in_specs=[row(att), row(att), row(att), row(LANES)] + [row(att)] * len(prev),
        out_specs=out_specs,
        out_shape=out_shape,
        scratch_shapes=scratch,
        compiler_params=_params("parallel", "arbitrary"),
        name="fox_prep",
    )(q, k, v, lf, *prev)


def _flash_kernel(q_ref, k_ref, vt_ref, o_ref, s_ref, p_ref, acc_ref, *, hg):
    i = pl.program_id(2)
    t = SEQ_TILE
    top = lax.broadcasted_iota(jnp.int32, (LANES, t), 0) < HEAD_DIM

    def scores_to(j, slot):
        ks = pl.multiple_of(j * t, t)
        for h in range(hg):
            s_ref[slot, h] = lax.dot_general(k_ref[h, pl.ds(ks, t), :], q_ref[h], NT_DIMS,
                                             preferred_element_type=F32)

    def accumulate(j, als):
        ks = pl.multiple_of(j * t, t)
        pvs = [_dot(vt_ref[h, :, pl.ds(ks, t)], p_ref[h]) for h in range(hg)]
        for pr in range(hg // 2):
            acc_ref[pr] = (jnp.where(top, als[2 * pr], als[2 * pr + 1]) * acc_ref[pr]
                           + pvs[2 * pr] + pvs[2 * pr + 1])

    def softmax(slot, ms, ls, masked):
        new_m, new_l, als = [], [], []
        for h in range(hg):
            s = s_ref[slot, h]
            if masked:
                visible = (lax.broadcasted_iota(jnp.int32, (t, t), 0) <= lax.broadcasted_iota(jnp.int32, (t, t), 1))
                s = jnp.where(visible, s, NEG)
            mn = jnp.maximum(ms[h], jnp.max(s, axis=0, keepdims=True))
            p = jnp.exp2(s - mn)
            al = jnp.exp2(ms[h] - mn)
            new_m.append(mn)
            new_l.append(al * ls[h] + jnp.sum(p, axis=0, keepdims=True))
            als.append(al)
            p_ref[h] = p.astype(BF16)
        return tuple(new_m), tuple(new_l), tuple(als)

    def stage(j, slot, state):
        ms, ls, als = state
        scores_to(j + 1, 1 - slot)
        accumulate(jnp.maximum(j - 1, 0), als)
        return softmax(slot, ms, ls, False)

    def pair(jj, state):
        return stage(2 * jj + 1, 1, stage(2 * jj, 0, state))

    def finish(slot, state):
        ms, ls, als = state
        accumulate(jnp.maximum(i - 1, 0), als)
        _, ls, als = softmax(slot, ms, ls, True)
        accumulate(i, als)
        for pr in range(hg // 2):
            out_t = acc_ref[pr] * jnp.where(top, 1.0 / ls[2 * pr], 1.0 / ls[2 * pr + 1])
            o_ref[:, pr * LANES:(pr + 1) * LANES] = out_t.T

    p_ref[...] = jnp.zeros_like(p_ref)
    acc_ref[...] = jnp.zeros_like(acc_ref)
    scores_to(0, 0)
    state = ((jnp.full((1, t), NEG, F32),) * hg, (jnp.zeros((1, t), F32),) * hg, (jnp.ones((1, t), F32),) * hg)
    state = lax.fori_loop(0, i // 2, pair, state)
    odd = i % 2 == 1
    state = lax.cond(odd, lambda st: stage(i - 1, 0, st), lambda st: st, state)
    pl.when(odd)(lambda: finish(1, state))
    pl.when(jnp.logical_not(odd))(lambda: finish(0, state))


def _flash(qa, ka, vat, *, hg):
    nb, _, sp, _ = qa.shape
    ns = sp // SEQ_TILE
    return pl.pallas_call(
        functools.partial(_flash_kernel, hg=hg),
        grid=(nb, N_HEADS // hg, ns),
        in_specs=[pl.BlockSpec((None, hg, SEQ_TILE, LANES), lambda bi, gi, qi: (bi, gi, qi, 0)),
                  pl.BlockSpec((None, hg, sp, LANES), lambda bi, gi, qi: (bi, gi, 0, 0)),
                  pl.BlockSpec((None, hg, LANES, sp), lambda bi, gi, qi: (bi, gi, 0, 0))],
        out_specs=pl.BlockSpec((SEQ_TILE, hg * HEAD_DIM), lambda bi, gi, qi: (bi * ns + qi, gi)),
        out_shape=jax.ShapeDtypeStruct((nb * sp, N_HEADS * HEAD_DIM), F32),
        scratch_shapes=[pltpu.VMEM((2, hg, SEQ_TILE, SEQ_TILE), F32), pltpu.VMEM((hg, SEQ_TILE, SEQ_TILE), BF16),
                        pltpu.VMEM((hg // 2, LANES, SEQ_TILE), F32)],
        compiler_params=_params("parallel", "parallel", "arbitrary"),
        name="fox_flash",
    )(qa, ka, vat)


def _sample_attn_kernel(pt_ref, q_ref, kn_ref, vn_ref, lfn_ref, *rest, n_pages, page, eb, ds):
    del pt_ref
    o_ref = rest[3 * eb * n_pages]
    c = q_ref.shape[1]
    rows = ds * N_HEADS
    groups = [list(range(g, min(g + 2, n_pages))) for g in range(0, n_pages, 2)]
    past = n_pages * page
    sub = lax.broadcasted_iota(jnp.int32, (N_HEADS, c), 0)
    lane = lax.broadcasted_iota(jnp.int32, (N_HEADS, c), 1)
    hm = ((lane >= sub * HEAD_DIM) & (lane < (sub + 1) * HEAD_DIM)).astype(F32)
    hm_t = jnp.concatenate([hm] * ds, axis=0)
    pad = jnp.zeros((LANES - ds, c), F32)

    def refs_of(kind, e):
        return rest[(kind * eb + e) * n_pages:(kind * eb + e + 1) * n_pages]

    def scores(e):
        sl = slice(e * ds, (e + 1) * ds)
        q = q_ref[sl, :]
        qbd = jnp.concatenate([jnp.broadcast_to(q[t:t + 1, :], (N_HEADS, c)) * hm for t in range(ds)],
                              axis=0).astype(BF16)
        k_refs, lf_refs = refs_of(0, e), refs_of(2, e)
        s_parts = []
        for grp in groups:
            kp = jnp.concatenate([k_refs[r][...] for r in grp], axis=1).astype(BF16)
            s_parts.append(_dot(qbd, kp))
        s_past = jnp.concatenate(s_parts, axis=1)

        lf = jnp.concatenate([r[...] for r in lf_refs], axis=1)
        lane_p = lax.broadcasted_iota(jnp.int32, (N_HEADS, past), 1)
        suf = lf
        sh = 1
        while sh < past:
            suf = suf + jnp.where(lane_p < past - sh, pltpu.roll(suf, past - sh, axis=1), 0.0)
            sh *= 2
        s_past = s_past + jnp.concatenate([suf - lf] * ds, axis=0)

        kn = jnp.concatenate([kn_ref[sl, :], pad], axis=0).astype(BF16)
        s_new = lax.dot_general(qbd, kn, NT_DIMS, preferred_element_type=F32)
        cn = lfn_ref[sl, :]
        sub_n = lax.broadcasted_iota(jnp.int32, cn.shape, 0)
        sh = 1
        while sh < ds:
            cn = cn + jnp.where(sub_n >= sh, pltpu.roll(cn, sh, axis=0), 0.0)
            sh *= 2
        cn_t = jnp.concatenate([cn, jnp.zeros((LANES - ds, LANES), F32)], axis=0).T[0:N_HEADS, :]
        rown = lax.broadcasted_iota(jnp.int32, (rows, LANES), 0)
        u = lax.broadcasted_iota(jnp.int32, (rows, LANES), 1)
        s_new = jnp.where(u * N_HEADS <= rown, s_new - jnp.concatenate([cn_t] * ds, axis=0), NEG)
        return s_past, s_new

    def softmax(s_past, s_new):
        m = jnp.maximum(jnp.max(s_past, axis=1, keepdims=True), jnp.max(s_new, axis=1, keepdims=True))
        p_past = jnp.exp(s_past - m)
        p_new = jnp.exp(s_new - m)
        l = jnp.sum(p_past, axis=1, keepdims=True) + jnp.sum(p_new, axis=1, keepdims=True)
        return p_past.astype(BF16), p_new.astype(BF16), l

    def output(e, p_past, p_new, l):
        sl = slice(e * ds, (e + 1) * ds)
        v_refs = refs_of(1, e)
        vn = jnp.concatenate([vn_ref[sl, :], pad], axis=0).astype(BF16)
        o = _dot(p_new, vn)
        for grp in groups:
            vp = jnp.concatenate([v_refs[r][...] for r in grp], axis=1).astype(BF16)
            lo = grp[0] * page
            o = o + lax.dot_general(p_past[:, lo:lo + len(grp) * page], vp, NT_DIMS, preferred_element_type=F32)
        o = o * (1.0 / l) * hm_t
        rr = lax.broadcasted_iota(jnp.int32, (ds, rows), 1)
        tt = lax.broadcasted_iota(jnp.int32, (ds, rows), 0)
        pick = ((rr >= tt * N_HEADS) & (rr < (tt + 1) * N_HEADS)).astype(BF16)
        o_ref[sl, :] = _dot(pick, o.astype(BF16))

    ss = [scores(e) for e in range(eb)]
    ps = [softmax(*s) for s in ss]
    for e in range(eb):
        output(e, *ps[e])


def _sample_attn(page_table, layer, q, k, v, lf, cache_k, cache_v, cache_lft, *, s0, db, ds):
    n, c = q.shape
    n_pages = page_table.shape[1]
    page = cache_k.shape[3]
    eb = 2 if db % 2 == 0 else 1
    rows = eb * ds
    row = lambda cc: pl.BlockSpec((rows, cc), lambda i, pt: (s0 // rows + i, 0))

    def paged(shape, e, r):
        return pl.BlockSpec((None, None) + shape, lambda i, pt: (pt[(i * eb + e) * n_pages + r], layer, 0, 0))

    pages = [(e, r) for e in range(eb) for r in range(n_pages)]
    in_specs = ([row(c), row(c), row(c), row(LANES)]
                + [paged((c, page), e, r) for e, r in pages]
                + [paged((c, page), e, r) for e, r in pages]
                + [paged((N_HEADS, page), e, r) for e, r in pages])
    return pl.pallas_call(
        functools.partial(_sample_attn_kernel, n_pages=n_pages, page=page, eb=eb, ds=ds),
        grid_spec=pltpu.PrefetchScalarGridSpec(
            num_scalar_prefetch=1, grid=(db // eb,), in_specs=in_specs,
            out_specs=pl.BlockSpec((rows, c), lambda i, pt: (i, 0))),
        out_shape=jax.ShapeDtypeStruct((db * ds, c), F32),
        compiler_params=_params("parallel"),
        name="sample_paged_attn",
    )(page_table.reshape(-1), q, k, v, lf, *([cache_k] * len(pages)), *([cache_v] * len(pages)),
      *([cache_lft] * len(pages)))


def _odd_tile(wy, wp, bg, w_rows, grp, pos0):
    t = bg.shape[0]
    ny = wy.shape[0]
    y0 = wy[SUBLANES:]
    y1 = pltpu.roll(wy, 1, axis=0)[SUBLANES:]
    y2 = pltpu.roll(wy, 2, axis=0)[SUBLANES:]
    out_c = bg * (w_rows[0] * y2 + w_rows[1] * y1 + w_rows[2] * y0)
    del ny
    back = 2 * SUBLANES
    s2 = wp + pltpu.roll(wp, 1, axis=0)
    s4 = s2 + pltpu.roll(s2, 2, axis=0)
    s8 = s4 + pltpu.roll(s4, 4, axis=0)
    s16 = s8 + pltpu.roll(s8, 8, axis=0)
    win = jnp.where(grp == 0, s2, jnp.where(grp == 1, s4, jnp.where(grp == 2, s8, s16)))[back:]
    wsz = jnp.where(grp == 0, POOL_WINDOWS[0],
                    jnp.where(grp == 1, POOL_WINDOWS[1], jnp.where(grp == 2, POOL_WINDOWS[2], POOL_WINDOWS[3])))
    pos = pos0 + lax.broadcasted_iota(jnp.int32, (t, LANES), 0)
    cnt = jnp.maximum(jnp.minimum(wsz, pos + 1), 1).astype(F32)
    d = win / cnt - wp[back:]
    return out_c, d


def _prompt_odd_kernel(y_ref, bg_ref, pd_ref, w_ref, oc_ref, d_ref, *, n_tiles, fp):
    grp = pl.program_id(1)
    w_rows = [w_ref[j:j + 1, :] for j in range(SCONV_W)]
    c = y_ref.shape[1]
    t = SEQ_TILE
    wy0 = jnp.concatenate([jnp.zeros((SUBLANES, c), F32), y_ref[0:t, :]], axis=0)
    wp0 = jnp.concatenate([jnp.zeros((2 * SUBLANES, c), F32), pd_ref[0:t, :]], axis=0)
    oc, d = _odd_tile(wy0, wp0, bg_ref[0:t, :], w_rows, grp, -fp)
    oc_ref[0:t, :] = oc
    d_ref[0:t, :] = d

    def body(i, carry):
        s0 = pl.multiple_of(i * t, t)
        wy = y_ref[pl.ds(s0 - SUBLANES, t + SUBLANES), :]
        wp = pd_ref[pl.ds(s0 - 2 * SUBLANES, t + 2 * SUBLANES), :]
        oc, d = _odd_tile(wy, wp, bg_ref[pl.ds(s0, t), :], w_rows, grp, s0 - fp)
        oc_ref[pl.ds(s0, t), :] = oc
        d_ref[pl.ds(s0, t), :] = d
        return carry

    lax.fori_loop(1, n_tiles, body, 0)


def _prompt_odd(y, bg, pd, w, *, nb, sp, fp):
    n, c = y.shape
    assert c // LANES == len(POOL_WINDOWS)
    blk = pl.BlockSpec((sp, LANES), lambda bi, ci: (bi, ci))
    return pl.pallas_call(
        functools.partial(_prompt_odd_kernel, n_tiles=sp // SEQ_TILE, fp=fp),
        grid=(nb, c // LANES),
        in_specs=[blk, blk, blk, pl.BlockSpec((SCONV_W, LANES), lambda bi, ci: (0, ci))],
        out_specs=[blk, blk],
        out_shape=[jax.ShapeDtypeStruct((nb * sp, c), F32)] * 2,
        compiler_params=_params("parallel", "parallel"),
        name="prompt_sconv_pool",
    )(y, bg, pd, w)


def _sample_odd_kernel(ss_ref, sp_ref, y_ref, bg_ref, pd_ref, w_ref,
                       oc_ref, d_ref, ss_out_ref, sp_out_ref, yext_ref, pext_ref, *, ds):
    dbt, _, c = ss_ref.shape
    hs = SCONV_W - 1
    y = y_ref[...].reshape(dbt, ds, c)
    pd = pd_ref[...].reshape(dbt, ds, c)
    bg = bg_ref[...].reshape(dbt, ds, c)
    yext_ref[:, SUBLANES - hs:SUBLANES, :] = ss_ref[...]
    yext_ref[:, SUBLANES:SUBLANES + ds, :] = y
    conv = None
    for j in range(SCONV_W):
        lo = SUBLANES - hs + j
        term = yext_ref[:, lo:lo + ds, :] * w_ref[j:j + 1, :]
        conv = term if conv is None else conv + term
    oc_ref[...] = (bg * conv).reshape(dbt * ds, c)
    ss_out_ref[...] = yext_ref[:, SUBLANES + ds - hs:SUBLANES + ds, :]

    base = 2 * SUBLANES
    pext_ref[:, base - POOL_HIST:base, :] = sp_ref[...]
    pext_ref[:, base:base + ds, :] = pd
    run = pd
    sums = {}
    for i in range(1, max(POOL_WINDOWS)):
        run = run + pext_ref[:, base - i:base - i + ds, :]
        if i + 1 in POOL_WINDOWS:
            sums[i + 1] = run
    gc = c // len(POOL_WINDOWS)
    lane = lax.broadcasted_iota(jnp.int32, (dbt, ds, c), 2)
    mean = sums[POOL_WINDOWS[-1]] / float(POOL_WINDOWS[-1])
    for gi in range(len(POOL_WINDOWS) - 2, -1, -1):
        mean = jnp.where(lane < (gi + 1) * gc, sums[POOL_WINDOWS[gi]] / float(POOL_WINDOWS[gi]), mean)
    d_ref[...] = (mean - pd).reshape(dbt * ds, c)
    sp_out_ref[...] = pext_ref[:, base + ds - POOL_HIST:base + ds, :]


def _sample_odd(state_s, state_p, layer, y, bg, pd, w, *, s0, db, ds, dbt):
    n, c = y.shape
    rows = dbt * ds
    row_in = pl.BlockSpec((rows, c), lambda i: (s0 // rows + i, 0))
    row_out = pl.BlockSpec((rows, c), lambda i: (i, 0))
    hs = state_s.shape[2]
    hp = state_p.shape[2]
    return pl.pallas_call(
        functools.partial(_sample_odd_kernel, ds=ds),
        grid=(db // dbt,),
        in_specs=[pl.BlockSpec((dbt, None, hs, c), lambda i: (i, layer, 0, 0)),
                  pl.BlockSpec((dbt, None, hp, c), lambda i: (i, layer, 0, 0)),
                  row_in, row_in, row_in, _full(w.shape)],
        out_specs=[row_out, row_out, pl.BlockSpec((dbt, hs, c), lambda i: (i, 0, 0)),
                   pl.BlockSpec((dbt, hp, c), lambda i: (i, 0, 0))],
        out_shape=[jax.ShapeDtypeStruct((db * ds, c), F32), jax.ShapeDtypeStruct((db * ds, c), F32),
                   jax.ShapeDtypeStruct((db, hs, c), F32), jax.ShapeDtypeStruct((db, hp, c), F32)],
        scratch_shapes=[pltpu.VMEM((dbt, SUBLANES + ds, c), F32), pltpu.VMEM((dbt, 2 * SUBLANES + ds, c), F32)],
        compiler_params=_params("parallel"),
        name="sample_sconv_pool",
    )(state_s, state_p, y, bg, pd, w)


def kernel(x_prompt, x_sample, cache_k, cache_v, cache_logf, page_table, state_conv_a, state_sconv, state_pool, meta_tokens, w_in_even, b_forget, w_dw_a, b_dw_a, ln_a_g, ln_a_b, w_out_even, w_in_odd, w_sconv, w_pool_mix, pool_scale, w_out_odd, ln_mix_g, ln_mix_b, w_ffn_gate, w_ffn_up, w_ffn_down, ln_ffn_g, ln_ffn_b):
    nb, seq, d = x_prompt.shape
    db, ds, _ = x_sample.shape
    depth = w_ffn_gate.shape[0]
    alpha = float((2 * depth) ** 0.25)
    ca = state_conv_a.shape[-1]
    att = N_HEADS * HEAD_DIM
    assert ds == SUBLANES and cache_k.shape[3] == N_HEADS and cache_k.shape[4] == HEAD_DIM
    assert state_conv_a.shape[2] == CONV_W - 1 and state_pool.shape[2] == POOL_HIST
    assert meta_tokens.shape[0] == N_META

    s_real = N_META + seq
    fp = (-s_real) % SEQ_TILE
    if fp < CONV_BACK:
        fp += SEQ_TILE
    sp = fp + s_real
    s0 = nb * sp
    n = s0 + db * ds
    tm = next(t for t in (512, 256, 128, 64, 32, 16, 8) if s0 % t == 0 and (db * ds) % t == 0)
    dbt = next(t for t in (8, 4, 2, 1) if db % t == 0)
    direct_out = seq % SEQ_TILE == 0 and (db * ds) % SEQ_TILE == 0

    head = jnp.concatenate([jnp.zeros((fp, d), F32), meta_tokens.astype(F32)], axis=0)
    x = jnp.concatenate([piece for bi in range(nb) for piece in (head, x_prompt[bi])]
                        + [x_sample.reshape(db * ds, d)], axis=0)

    n_phys = cache_k.shape[0]
    n_even = cache_k.shape[1]
    page = cache_k.shape[2]
    ck = jnp.transpose(cache_k, (0, 1, 3, 4, 2)).reshape(n_phys, n_even, att, page)
    cv = jnp.transpose(cache_v, (0, 1, 3, 4, 2)).reshape(n_phys, n_even, att, page)
    clft = jnp.swapaxes(cache_logf, 2, 3)

    row2 = lambda v: v.reshape(1, -1).astype(F32)

    def prompt_rows(arr, lo, hi):
        return jnp.stack([arr[bi * sp + lo:bi * sp + hi] for bi in range(nb)], axis=0)

    def sample_rows(arr):
        return arr[s0:].reshape(db, ds, -1)

    wg_all, wu_all, wd_all = (w.astype(BF16) for w in (w_ffn_gate, w_ffn_up, w_ffn_down))
    wo_even, wo_odd = w_out_even.astype(BF16), w_out_odd.astype(BF16)
    wi_even = jnp.pad(w_in_even.astype(BF16), ((0, 0), (0, 0), (0, LANES - N_HEADS)))
    wi_odd = w_in_odd.astype(BF16)

    def tail(name, xin, m1p, m2p, m1s, m2s, p1, p2, w_out_all, layer):
        last = direct_out and layer == depth - 1
        return _layer_tail(name, xin, m1p, m2p, m1s, m2s, p1, p2, w_out_all,
                           row2(ln_mix_g[layer]), row2(ln_mix_b[layer]), wg_all, wu_all, wd_all,
                           row2(ln_ffn_g[layer]), row2(ln_ffn_b[layer]), even=layer % 2 == 0,
                           mix_idx=layer // 2, layer=layer, tm=SEQ_TILE if last else tm, alpha=alpha,
                           nb=nb, sp=sp, fp=fp, out_rows=(seq, db * ds) if last else None)

    prev_kv = []
    lfp, cap, scp, plp = [], [], [], []
    ks_, vs_, lfs, cas, scs, pls = [], [], [], [], [], []
    for layer in range(depth):
        i = layer // 2
        if layer % 2 == 0:
            bf = jnp.pad(b_forget[i].astype(F32), (0, LANES - N_HEADS)).reshape(1, LANES)
            a, q, k, v, lf = _even_in(x, wi_even, i, bf, tm=tm, ca=ca, att=att)
            conv_p = _prompt_conv(a, w_dw_a[i].astype(F32), row2(b_dw_a[i]), nb=nb, sp=sp)
            conv_s, st_a = _sample_conv(state_conv_a.astype(F32), i, a, w_dw_a[i].astype(F32), row2(b_dw_a[i]),
                                        s0=s0, db=db, ds=ds, dbt=dbt)
            qa, ka, vat, *caches = _fox_prep(q, k, v, lf, prev_kv, i == n_even - 1, nb=nb, sp=sp, fp=fp)
            prev_kv.append((k, v))
            at_p = _flash(qa, ka, vat, hg=FLASH_HEADS)
            at_s = _sample_attn(page_table, i, q, k, v, lf, ck, cv, clft, s0=s0, db=db, ds=ds)
            x = tail("even_tail", x, conv_p, at_p, conv_s, at_s, row2(ln_a_g[i]), row2(ln_a_b[i]), wo_even, layer)
            lfp.append(prompt_rows(lf, fp, sp)[..., :N_HEADS])
            cap.append(prompt_rows(a, sp - (CONV_W - 1), sp))
            ks_.append(sample_rows(k).reshape(db, ds, N_HEADS, HEAD_DIM))
            vs_.append(sample_rows(v).reshape(db, ds, N_HEADS, HEAD_DIM))
            lfs.append(sample_rows(lf)[..., :N_HEADS])
            cas.append(st_a)
        else:
            c = state_sconv.shape[-1]
            y, bg, pd = _odd_in(x, wi_odd, i, tm=tm, c=c)
            oc_p, dd_p = _prompt_odd(y, bg, pd, w_sconv[i].astype(F32), nb=nb, sp=sp, fp=fp)
            oc_s, dd_s, st_s, st_p = _sample_odd(state_sconv.astype(F32), state_pool.astype(F32), i, y, bg, pd,
                                                 w_sconv[i].astype(F32), s0=s0, db=db, ds=ds, dbt=dbt)
            wm = jax.scipy.linalg.block_diag(*[w_pool_mix[i, g] for g in range(w_pool_mix.shape[1])]).astype(BF16)
            x = tail("odd_tail", x, oc_p, dd_p, oc_s, dd_s, wm, row2(pool_scale[i]), wo_odd, layer)
            scp.append(prompt_rows(y, sp - (SCONV_W - 1), sp))
            plp.append(prompt_rows(pd, sp - POOL_HIST, sp))
            scs.append(st_s)
            pls.append(st_p)

    if direct_out:
        y_prompt, y_sample = x[0], x[1].reshape(db, ds, d)
    else:
        y_prompt, y_sample = prompt_rows(x, fp + N_META, sp), sample_rows(x)
    st = lambda xs: jnp.stack(xs, axis=1)
    kp, vp = (jnp.transpose(c.reshape(nb, n_even, N_HEADS, HEAD_DIM, s_real), (0, 1, 4, 2, 3)) for c in caches)
    return (y_prompt, y_sample, kp, vp, st(lfp), st(cap), st(scp), st(plp),
            st(ks_), st(vs_), st(lfs), st(cas), st(scs), st(pls))
```

```python
import functools

import jax
import jax.numpy as jnp
from jax import lax
from jax.experimental import pallas as pl
from jax.experimental.pallas import tpu as pltpu

N_META = 16
N_HEADS = 8
HEAD_DIM = 64
CONV_W = 31
CONV_BACK = 32
SCONV_W = 3
POOL_WINDOWS = (2, 4, 8, 16)
POOL_HIST = max(POOL_WINDOWS) - 1
LN_EPS = 1e-5
SEQ_TILE = 256
LANES = 128
SUBLANES = 8
AUX_ONES = 24
FLASH_HEADS = 8
VMEM_LIMIT = 56 * 1024 * 1024
NEG = -1e30
LOG2E = 1.4426950408889634
F32 = jnp.float32
BF16 = jnp.bfloat16
NT_DIMS = (((1,), (1,)), ((), ()))


def _params(*sem):
    return pltpu.CompilerParams(dimension_semantics=sem, vmem_limit_bytes=VMEM_LIMIT)


def _dot(a, b):
    return jnp.dot(a, b, preferred_element_type=F32)


def _ln(z, g, b):
    mu = jnp.mean(z, axis=-1, keepdims=True)
    zc = z - mu
    var = jnp.mean(zc * zc, axis=-1, keepdims=True)
    return zc * lax.rsqrt(var + LN_EPS) * g + b


def _silu(x):
    return x * jax.nn.sigmoid(x)


def _full(shape):
    return pl.BlockSpec(shape, lambda *_: (0,) * len(shape))


def _slab(a, idx):
    return pl.BlockSpec((None,) + a.shape[1:], lambda *_: (idx,) + (0,) * (a.ndim - 1),
                        pipeline_mode=pl.Buffered(1))


def _even_in_kernel(x_ref, w_ref, bf_ref, a_ref, q_ref, k_ref, v_ref, lf_ref, *, ca, att):
    xb = x_ref[...].astype(BF16)

    def mm(lo, hi):
        return _dot(xb, w_ref[:, lo:hi])

    u = mm(0, ca)
    g = mm(ca, 2 * ca)
    a_ref[...] = u * jax.nn.sigmoid(g)
    o = 2 * ca
    q_ref[...] = mm(o, o + att) * (HEAD_DIM ** -0.5)
    k_ref[...] = mm(o + att, o + 2 * att)
    v_ref[...] = mm(o + 2 * att, o + 3 * att)
    z = mm(o + 3 * att, o + 3 * att + LANES) + bf_ref[...]
    lf = jnp.minimum(z, 0.0) - jnp.log1p(jnp.exp(-jnp.abs(z)))
    lane = lax.broadcasted_iota(jnp.int32, lf.shape, 1)
    lf_ref[...] = jnp.where(lane < N_HEADS, lf, 0.0)


def _even_in(x, w, idx, bf, *, tm, ca, att):
    n, d = x.shape
    row = lambda c: pl.BlockSpec((tm, c), lambda i: (i, 0))
    return pl.pallas_call(
        functools.partial(_even_in_kernel, ca=ca, att=att),
        grid=(n // tm,),
        in_specs=[row(d), _slab(w, idx), _full(bf.shape)],
        out_specs=[row(ca), row(att), row(att), row(att), row(LANES)],
        out_shape=[jax.ShapeDtypeStruct((n, c), F32) for c in (ca, att, att, att, LANES)],
        compiler_params=_params("parallel"),
        name="even_in_proj",
    )(x, w, bf)


def _odd_in_kernel(x_ref, w_ref, y_ref, bg_ref, pd_ref, *, c):
    xb = x_ref[...].astype(BF16)
    hc = _dot(xb, w_ref[:, 0:c])
    bg_ref[...] = _dot(xb, w_ref[:, c:2 * c])
    cg = _dot(xb, w_ref[:, 2 * c:3 * c])
    y_ref[...] = cg * hc
    pd_ref[...] = _dot(xb, w_ref[:, 3 * c:4 * c])


def _odd_in(x, w, idx, *, tm, c):
    n, d = x.shape
    row = lambda cc: pl.BlockSpec((tm, cc), lambda i: (i, 0))
    return pl.pallas_call(
        functools.partial(_odd_in_kernel, c=c),
        grid=(n // tm,),
        in_specs=[row(d), _slab(w, idx)],
        out_specs=[row(c)] * 3,
        out_shape=[jax.ShapeDtypeStruct((n, c), F32)] * 3,
        compiler_params=_params("parallel"),
        name="odd_in_proj",
    )(x, w)


def _tail_kernel(x_ref, m1p_ref, m2p_ref, m1s_ref, m2s_ref, p1_ref, p2_ref, w_ref, g1_ref, b1_ref,
                 wg_ref, wu_ref, wd_ref, g2_ref, b2_ref, *rest, even, alpha, chunk, nb, sp, fp, npt, lead):
    i = pl.program_id(0)
    is_sample = i >= npt
    m1 = jnp.where(is_sample, m1s_ref[...], m1p_ref[...])
    m2 = jnp.where(is_sample, m2s_ref[...], m2p_ref[...])
    if even:
        m1 = _silu(_ln(m1, p1_ref[...], p2_ref[...]))
    else:
        m2 = _dot(m2.astype(BF16), p1_ref[...]) * p2_ref[...]
    ca = m1.shape[1]
    mix = _dot(m1.astype(BF16), w_ref[0:ca, :]) + _dot(m2.astype(BF16), w_ref[ca:, :])
    x1 = _ln(alpha * x_ref[...] + mix, g1_ref[...], b1_ref[...])

    h_ref = rest[-1]
    xb = x1.astype(BF16)
    for c in range(0, wg_ref.shape[1], chunk):
        gate = _dot(xb, wg_ref[:, c:c + chunk])
        up = _dot(xb, wu_ref[:, c:c + chunk])
        h_ref[:, c:c + chunk] = (_silu(gate) * up).astype(BF16)
    out = _ln(alpha * x1 + _dot(h_ref[...], wd_ref[...]), g2_ref[...], b2_ref[...])

    tm = x1.shape[0]
    if lead is None:
        r = i * tm + lax.broadcasted_iota(jnp.int32, (tm, 1), 0)
        keep = jnp.ones((tm, 1), F32)
        for bi in range(nb):
            keep = jnp.where((r >= bi * sp) & (r < bi * sp + fp), 0.0, keep)
        rest[0][...] = out * keep
    else:
        yp_ref, ys_ref = rest[:2]

        @pl.when(jnp.logical_and(jnp.logical_not(is_sample), i % (sp // tm) >= lead))
        def _():
            yp_ref[...] = out

        @pl.when(is_sample)
        def _():
            ys_ref[...] = out


def _layer_tail(name, x, m1p, m2p, m1s, m2s, p1, p2, w, g1, b1, wg, wu, wd, g2, b2, *, even, mix_idx, layer, tm,
                alpha, nb, sp, fp, out_rows=None):
    n, d = x.shape
    s0 = m1p.shape[0]
    dff = wg.shape[2]
    chunk = SEQ_TILE if dff % SEQ_TILE == 0 else dff
    npt = s0 // tm
    row = lambda c: pl.BlockSpec((tm, c), lambda i: (i, 0))
    prow = lambda c: pl.BlockSpec((tm, c), lambda i: (jnp.minimum(i, npt - 1), 0))
    srow = lambda c: pl.BlockSpec((tm, c), lambda i: (jnp.maximum(i - npt, 0), 0))
    once = lambda a: pl.BlockSpec(a.shape, lambda i: (0,) * a.ndim, pipeline_mode=pl.Buffered(1))
    consts = (p1, p2, w, g1, b1, wg, wu, wd, g2, b2)
    const_specs = [once(p1), once(p2), _slab(w, mix_idx), once(g1), once(b1),
                   _slab(wg, layer), _slab(wu, layer), _slab(wd, layer), once(g2), once(b2)]
    if out_rows is None:
        lead = None
        out_specs = row(d)
        out_shape = jax.ShapeDtypeStruct((n, d), F32)
    else:
        seq, db_rows = out_rows
        tps = sp // tm
        lead = (sp - seq) // tm
        out_specs = [
            pl.BlockSpec((None, tm, d), lambda i: (jnp.minimum(i // tps, nb - 1),
                                                   jnp.where(i < npt, jnp.maximum(i % tps - lead, 0), tps - lead - 1),
                                                   0)),
            srow(d)]
        out_shape = [jax.ShapeDtypeStruct((nb, seq, d), F32), jax.ShapeDtypeStruct((db_rows, d), F32)]
    return pl.pallas_call(
        functools.partial(_tail_kernel, even=even, alpha=alpha, chunk=chunk, nb=nb, sp=sp, fp=fp, npt=npt, lead=lead),
        grid=(n // tm,),
        in_specs=[row(d), prow(m1p.shape[1]), prow(m2p.shape[1]), srow(m1s.shape[1]), srow(m2s.shape[1])]
        + const_specs,
        out_specs=out_specs,
        out_shape=out_shape,
        scratch_shapes=[pltpu.VMEM((tm, dff), BF16)],
        compiler_params=_params("arbitrary" if out_rows else "parallel"),
        name=name,
    )(x, m1p, m2p, m1s, m2s, *consts)


def _conv_tile(win, w_rows):
    rows = win.shape[0]
    t = rows - CONV_BACK
    lead = CONV_BACK - (CONV_W - 1)
    acc = None
    for r in range(SUBLANES):
        rolled = win if r == 0 else pltpu.roll(win, rows - r, axis=0)
        for m in range(CONV_BACK // SUBLANES + 1):
            j = SUBLANES * m + r - lead
            if 0 <= j < CONV_W:
                term = rolled[SUBLANES * m:SUBLANES * m + t] * w_rows[j]
                acc = term if acc is None else acc + term
    return acc


def _prompt_conv_kernel(a_ref, w_ref, b_ref, o_ref, *, n_tiles):
    w_rows = [w_ref[j:j + 1, :] for j in range(CONV_W)]
    bias = b_ref[...]
    c = a_ref.shape[1]
    win0 = jnp.concatenate([jnp.zeros((CONV_BACK, c), F32), a_ref[0:SEQ_TILE, :]], axis=0)
    o_ref[0:SEQ_TILE, :] = _conv_tile(win0, w_rows) + bias

    def body(i, carry):
        s0 = pl.multiple_of(i * SEQ_TILE, SEQ_TILE)
        win = a_ref[pl.ds(s0 - CONV_BACK, SEQ_TILE + CONV_BACK), :]
        o_ref[pl.ds(s0, SEQ_TILE), :] = _conv_tile(win, w_rows) + bias
        return carry

    lax.fori_loop(1, n_tiles, body, 0)


def _prompt_conv(a, w, b, *, nb, sp):
    n, c = a.shape
    blk = pl.BlockSpec((sp, LANES), lambda bi, ci: (bi, ci))
    return pl.pallas_call(
        functools.partial(_prompt_conv_kernel, n_tiles=sp // SEQ_TILE),
        grid=(nb, c // LANES),
        in_specs=[blk, pl.BlockSpec((CONV_W, LANES), lambda bi, ci: (0, ci)),
                  pl.BlockSpec((1, LANES), lambda bi, ci: (0, ci))],
        out_specs=blk,
        out_shape=jax.ShapeDtypeStruct((nb * sp, c), F32),
        compiler_params=_params("parallel", "parallel"),
        name="prompt_conv",
    )(a, w, b)


def _sample_conv_kernel(st_ref, a_ref, w_ref, b_ref, o_ref, st_out_ref, ext_ref, *, ds):
    dbt, hist, c = st_ref.shape
    off = CONV_BACK - hist
    ext_ref[:, off:CONV_BACK, :] = st_ref[...]
    ext_ref[:, CONV_BACK:CONV_BACK + ds, :] = a_ref[...].reshape(dbt, ds, c)
    acc = None
    for j in range(CONV_W):
        term = ext_ref[:, off + j:off + j + ds, :] * w_ref[j:j + 1, :]
        acc = term if acc is None else acc + term
    o_ref[...] = (acc + b_ref[...]).reshape(dbt * ds, c)
    st_out_ref[...] = ext_ref[:, off + ds:CONV_BACK + ds, :]


def _sample_conv(state, layer, a, w, b, *, s0, db, ds, dbt):
    n, c = a.shape
    hist = state.shape[2]
    rows = dbt * ds
    return pl.pallas_call(
        functools.partial(_sample_conv_kernel, ds=ds),
        grid=(db // dbt,),
        in_specs=[pl.BlockSpec((dbt, None, hist, c), lambda i: (i, layer, 0, 0)),
                  pl.BlockSpec((rows, c), lambda i: (s0 // rows + i, 0)), _full(w.shape), _full(b.shape)],
        out_specs=[pl.BlockSpec((rows, c), lambda i: (i, 0)), pl.BlockSpec((dbt, hist, c), lambda i: (i, 0, 0))],
        out_shape=[jax.ShapeDtypeStruct((db * ds, c), F32), jax.ShapeDtypeStruct((db, hist, c), F32)],
        scratch_shapes=[pltpu.VMEM((dbt, CONV_BACK + ds, c), F32)],
        compiler_params=_params("parallel"),
        name="sample_conv",
    )(state, a, w, b)


def _fox_prep_kernel(q_ref, k_ref, v_ref, lf_ref, *rest, fp, ns, n_prev, emit):
    prev, rest = rest[:2 * n_prev], rest[2 * n_prev:]
    qa_ref, ka_ref, vat_ref = rest[:3]
    carry_ref = rest[5] if emit else rest[3]
    s = pl.program_id(1)

    @pl.when(s < ns)
    def _():
        _fox_operands(s, q_ref, k_ref, v_ref, lf_ref, qa_ref, ka_ref, vat_ref, carry_ref, fp)

    if not emit:
        return
    kt_ref, vt_ref, _, kc_ref, vc_ref = rest[3:]

    @pl.when(s == 0)
    def _():
        kc_ref[...] = jnp.zeros_like(kc_ref)
        vc_ref[...] = jnp.zeros_like(vc_ref)

    t = SEQ_TILE
    k_srcs = list(prev[0::2]) + [k_ref]
    v_srcs = list(prev[1::2]) + [v_ref]
    for srcs, held, dst in ((k_srcs, kc_ref, kt_ref), (v_srcs, vc_ref, vt_ref)):
        for li, src in enumerate(srcs):
            rows = jnp.concatenate([held[li], src[0:fp, :]], axis=0)
            for c in range(src.shape[1] // LANES):
                dst[li, c * LANES:(c + 1) * LANES, :] = rows[:, c * LANES:(c + 1) * LANES].T
            held[li] = src[fp:t, :]


def _fox_operands(s, q_ref, k_ref, v_ref, lf_ref, qa_ref, ka_ref, vat_ref, carry_ref, fp):
    @pl.when(s == 0)
    def _():
        carry_ref[...] = jnp.zeros_like(carry_ref)

    t = SEQ_TILE
    row = lax.broadcasted_iota(jnp.int32, (t, LANES), 0)
    lane = lax.broadcasted_iota(jnp.int32, (t, LANES), 1)
    c = jnp.where(s * t + row >= fp, lf_ref[...], 0.0)
    sh = 1
    while sh < t:
        c = c + jnp.where(row >= sh, pltpu.roll(c, sh, axis=0), 0.0)
        sh *= 2
    c = c + carry_ref[0:1, :]
    carry_ref[0:1, :] = c[t - 1:t, :]

    c = c * LOG2E
    c1 = c.astype(BF16).astype(F32)
    r1 = c - c1
    c2 = r1.astype(BF16).astype(F32)
    c3 = (r1 - c2).astype(BF16).astype(F32)
    g23 = jnp.where(lane < 2 * N_HEADS, pltpu.roll(c2, N_HEADS, axis=1),
                    jnp.where(lane < 3 * N_HEADS, pltpu.roll(c3, 2 * N_HEADS, axis=1), 0.0))
    g = jnp.where(lane < N_HEADS, c1, g23)
    gk = jnp.where(lane < N_HEADS, jnp.where(s * t + row >= fp, c1, -NEG), g23)
    ones_grp = (lane >= HEAD_DIM + AUX_ONES) & (lane < HEAD_DIM + 2 * AUX_ONES)
    ck = [jnp.where((lane >= HEAD_DIM) & (lane < HEAD_DIM + AUX_ONES), -pltpu.roll(gk, HEAD_DIM, axis=1),
                    jnp.where(ones_grp, 1.0, 0.0))]
    cq = [jnp.where(ones_grp, pltpu.roll(g, HEAD_DIM + AUX_ONES, axis=1), 0.0)]
    ck.append(pltpu.roll(ck[0], HEAD_DIM, axis=1))
    cq.append(pltpu.roll(cq[0], HEAD_DIM, axis=1))

    lane1 = lax.broadcasted_iota(jnp.int32, (1, LANES), 1)
    for h in range(N_HEADS):
        odd = h % 2
        pr = h // 2
        base = 0 if odd else HEAD_DIM
        a = lane1 - base
        mine = (lane1 & (N_HEADS - 1)) == h
        data = ((lane1 >= HEAD_DIM) if odd else (lane1 < HEAD_DIM)).astype(F32)
        sel1 = ((a >= 0) & (a < AUX_ONES) & mine).astype(F32)
        selc = ((a >= AUX_ONES) & (a < 2 * AUX_ONES) & mine).astype(F32)
        sl = slice(pr * LANES, (pr + 1) * LANES)
        qa_ref[h] = (q_ref[:, sl] * (data * LOG2E) + (cq[odd] * selc + sel1)).astype(BF16)
        ka_ref[h] = (k_ref[:, sl] * data + ck[odd]).astype(BF16)
    for pr in range(N_HEADS // 2):
        v_t = v_ref[:, pr * LANES:(pr + 1) * LANES].T.astype(BF16)
        vat_ref[2 * pr] = v_t[0:HEAD_DIM]
        vat_ref[2 * pr + 1] = v_t[HEAD_DIM:]


def _fox_prep(q, k, v, lf, prev_kv, emit, *, nb, sp, fp):
    ns = sp // SEQ_TILE
    att = q.shape[1]
    t = SEQ_TILE
    last = ns - 1
    row = lambda c: pl.BlockSpec((t, c), lambda bi, si: (bi * ns + jnp.minimum(si, last), 0))
    hm = pl.BlockSpec((None, N_HEADS, t, LANES), lambda bi, si: (bi, 0, jnp.minimum(si, last), 0))
    hm_t = pl.BlockSpec((None, N_HEADS, HEAD_DIM, t), lambda bi, si: (bi, 0, 0, jnp.minimum(si, last)))
    out_specs = [hm, hm, hm_t]
    out_shape = [jax.ShapeDtypeStruct((nb, N_HEADS, sp, LANES), BF16)] * 2 + [
        jax.ShapeDtypeStruct((nb, N_HEADS, HEAD_DIM, sp), BF16)]
    scratch = [pltpu.VMEM((SUBLANES, LANES), F32)]
    n_prev = len(prev_kv) if emit else 0
    if emit:
        assert 0 < fp < t and fp % SUBLANES == 0
        n_layers = n_prev + 1
        cache = pl.BlockSpec((None, n_layers, att, t), lambda bi, si: (bi, 0, 0, jnp.maximum(si - 1, 0)))
        out_specs += [cache, cache]
        out_shape += [jax.ShapeDtypeStruct((nb, n_layers, att, sp - fp), F32)] * 2
        scratch += [pltpu.VMEM((n_layers, t - fp, att), F32)] * 2
    prev = [a for kv in prev_kv for a in kv] if emit else []
    return pl.pallas_call(
        functools.partial(_fox_prep_kernel, fp=fp, ns=ns, n_prev=n_prev, emit=emit),
        grid=(nb, ns + 1 if emit else ns),
        in_specs=[row(att), row(att), row(att), row(LANES)] + [row(att)] * len(prev),
        out_specs=out_specs,
        out_shape=out_shape,
        scratch_shapes=scratch,
        compiler_params=_params("parallel", "arbitrary"),
        name="fox_prep",
    )(q, k, v, lf, *prev)


def _flash_kernel(q_ref, k_ref, vt_ref, o_ref, s_ref, p_ref, acc_ref, *, hg):
    i = pl.program_id(2)
    t = SEQ_TILE

    def scores_to(j, slot):
        ks = pl.multiple_of(j * t, t)
        for h in range(hg):
            s_ref[slot, h] = lax.dot_general(k_ref[h, pl.ds(ks, t), :], q_ref[h], NT_DIMS,
                                             preferred_element_type=F32)

    def accumulate(j, als):
        ks = pl.multiple_of(j * t, t)
        pvs = [_dot(vt_ref[h, :, pl.ds(ks, t)], p_ref[h]) for h in range(hg)]
        for h in range(hg):
            acc_ref[h] = als[h] * acc_ref[h] + pvs[h]

    def softmax(slot, ms, ls, masked):
        new_m, new_l, als = [], [], []
        for h in range(hg):
            s = s_ref[slot, h]
            if masked:
                visible = (lax.broadcasted_iota(jnp.int32, (t, t), 0) <= lax.broadcasted_iota(jnp.int32, (t, t), 1))
                s = jnp.where(visible, s, NEG)
            mn = jnp.maximum(ms[h], jnp.max(s, axis=0, keepdims=True))
            p = jnp.exp2(s - mn)
            al = jnp.exp2(ms[h] - mn)
            new_m.append(mn)
            new_l.append(al * ls[h] + jnp.sum(p, axis=0, keepdims=True))
            als.append(al)
            p_ref[h] = p.astype(BF16)
        return tuple(new_m), tuple(new_l), tuple(als)

    def stage(j, slot, state):
        ms, ls, als = state
        scores_to(j + 1, 1 - slot)
        accumulate(jnp.maximum(j - 1, 0), als)
        return softmax(slot, ms, ls, False)

    def pair(jj, state):
        return stage(2 * jj + 1, 1, stage(2 * jj, 0, state))

    def finish(slot, state):
        ms, ls, als = state
        accumulate(jnp.maximum(i - 1, 0), als)
        _, ls, als = softmax(slot, ms, ls, True)
        accumulate(i, als)
        for pr in range(hg // 2):
            out_t = jnp.concatenate([acc_ref[2 * pr] * (1.0 / ls[2 * pr]),
                                     acc_ref[2 * pr + 1] * (1.0 / ls[2 * pr + 1])], axis=0)
            o_ref[:, pr * LANES:(pr + 1) * LANES] = out_t.T

    p_ref[...] = jnp.zeros_like(p_ref)
    acc_ref[...] = jnp.zeros_like(acc_ref)
    scores_to(0, 0)
    state = ((jnp.full((1, t), NEG, F32),) * hg, (jnp.zeros((1, t), F32),) * hg, (jnp.ones((1, t), F32),) * hg)
    state = lax.fori_loop(0, i // 2, pair, state)
    odd = i % 2 == 1
    state = lax.cond(odd, lambda st: stage(i - 1, 0, st), lambda st: st, state)
    pl.when(odd)(lambda: finish(1, state))
    pl.when(jnp.logical_not(odd))(lambda: finish(0, state))


def _flash(qa, ka, vat, *, hg):
    nb, _, sp, _ = qa.shape
    ns = sp // SEQ_TILE
    return pl.pallas_call(
        functools.partial(_flash_kernel, hg=hg),
        grid=(nb, N_HEADS // hg, ns),
        in_specs=[pl.BlockSpec((None, hg, SEQ_TILE, LANES), lambda bi, gi, qi: (bi, gi, qi, 0)),
                  pl.BlockSpec((None, hg, sp, LANES), lambda bi, gi, qi: (bi, gi, 0, 0)),
                  pl.BlockSpec((None, hg, HEAD_DIM, sp), lambda bi, gi, qi: (bi, gi, 0, 0))],
        out_specs=pl.BlockSpec((SEQ_TILE, hg * HEAD_DIM), lambda bi, gi, qi: (bi * ns + qi, gi)),
        out_shape=jax.ShapeDtypeStruct((nb * sp, N_HEADS * HEAD_DIM), F32),
        scratch_shapes=[pltpu.VMEM((2, hg, SEQ_TILE, SEQ_TILE), F32), pltpu.VMEM((hg, SEQ_TILE, SEQ_TILE), BF16),
                        pltpu.VMEM((hg, HEAD_DIM, SEQ_TILE), F32)],
        compiler_params=_params("parallel", "parallel", "arbitrary"),
        name="fox_flash",
    )(qa, ka, vat)


def _sample_attn_kernel(pt_ref, q_ref, kn_ref, vn_ref, lfn_ref, *rest, n_pages, page, eb, ds):
    del pt_ref
    o_ref = rest[3 * eb * n_pages]
    c = q_ref.shape[1]
    rows = ds * N_HEADS
    groups = [list(range(g, min(g + 2, n_pages))) for g in range(0, n_pages, 2)]
    past = n_pages * page
    sub = lax.broadcasted_iota(jnp.int32, (N_HEADS, c), 0)
    lane = lax.broadcasted_iota(jnp.int32, (N_HEADS, c), 1)
    hm = ((lane >= sub * HEAD_DIM) & (lane < (sub + 1) * HEAD_DIM)).astype(F32)
    hm_t = jnp.concatenate([hm] * ds, axis=0)
    pad = jnp.zeros((LANES - ds, c), F32)

    def refs_of(kind, e):
        return rest[(kind * eb + e) * n_pages:(kind * eb + e + 1) * n_pages]

    def scores(e):
        sl = slice(e * ds, (e + 1) * ds)
        q = q_ref[sl, :]
        qbd = jnp.concatenate([jnp.broadcast_to(q[t:t + 1, :], (N_HEADS, c)) * hm for t in range(ds)],
                              axis=0).astype(BF16)
        k_refs, lf_refs = refs_of(0, e), refs_of(2, e)
        s_parts = []
        for grp in groups:
            kp = jnp.concatenate([k_refs[r][...] for r in grp], axis=1).astype(BF16)
            s_parts.append(_dot(qbd, kp))
        s_past = jnp.concatenate(s_parts, axis=1)

        lf = jnp.concatenate([r[...] for r in lf_refs], axis=1)
        lane_p = lax.broadcasted_iota(jnp.int32, (N_HEADS, past), 1)
        suf = lf
        sh = 1
        while sh < past:
            suf = suf + jnp.where(lane_p < past - sh, pltpu.roll(suf, past - sh, axis=1), 0.0)
            sh *= 2
        s_past = s_past + jnp.concatenate([suf - lf] * ds, axis=0)

        kn = jnp.concatenate([kn_ref[sl, :], pad], axis=0).astype(BF16)
        s_new = lax.dot_general(qbd, kn, NT_DIMS, preferred_element_type=F32)
        cn = lfn_ref[sl, :]
        sub_n = lax.broadcasted_iota(jnp.int32, cn.shape, 0)
        sh = 1
        while sh < ds:
            cn = cn + jnp.where(sub_n >= sh, pltpu.roll(cn, sh, axis=0), 0.0)
            sh *= 2
        cn_t = jnp.concatenate([cn, jnp.zeros((LANES - ds, LANES), F32)], axis=0).T[0:N_HEADS, :]
        rown = lax.broadcasted_iota(jnp.int32, (rows, LANES), 0)
        u = lax.broadcasted_iota(jnp.int32, (rows, LANES), 1)
        s_new = jnp.where(u * N_HEADS <= rown, s_new - jnp.concatenate([cn_t] * ds, axis=0), NEG)
        return s_past, s_new

    def softmax(s_past, s_new):
        m = jnp.maximum(jnp.max(s_past, axis=1, keepdims=True), jnp.max(s_new, axis=1, keepdims=True))
        p_past = jnp.exp(s_past - m)
        p_new = jnp.exp(s_new - m)
        l = jnp.sum(p_past, axis=1, keepdims=True) + jnp.sum(p_new, axis=1, keepdims=True)
        return p_past.astype(BF16), p_new.astype(BF16), l

    def output(e, p_past, p_new, l):
        sl = slice(e * ds, (e + 1) * ds)
        v_refs = refs_of(1, e)
        vn = jnp.concatenate([vn_ref[sl, :], pad], axis=0).astype(BF16)
        o = _dot(p_new, vn)
        for grp in groups:
            vp = jnp.concatenate([v_refs[r][...] for r in grp], axis=1).astype(BF16)
            lo = grp[0] * page
            o = o + lax.dot_general(p_past[:, lo:lo + len(grp) * page], vp, NT_DIMS, preferred_element_type=F32)
        o = o * (1.0 / l) * hm_t
        rr = lax.broadcasted_iota(jnp.int32, (ds, rows), 1)
        tt = lax.broadcasted_iota(jnp.int32, (ds, rows), 0)
        pick = ((rr >= tt * N_HEADS) & (rr < (tt + 1) * N_HEADS)).astype(BF16)
        o_ref[sl, :] = _dot(pick, o.astype(BF16))

    ss = [scores(e) for e in range(eb)]
    ps = [softmax(*s) for s in ss]
    for e in range(eb):
        output(e, *ps[e])


def _sample_attn(page_table, layer, q, k, v, lf, cache_k, cache_v, cache_lft, *, s0, db, ds):
    n, c = q.shape
    n_pages = page_table.shape[1]
    page = cache_k.shape[3]
    eb = 2 if db % 2 == 0 else 1
    rows = eb * ds
    row = lambda cc: pl.BlockSpec((rows, cc), lambda i, pt: (s0 // rows + i, 0))

    def paged(shape, e, r):
        return pl.BlockSpec((None, None) + shape, lambda i, pt: (pt[(i * eb + e) * n_pages + r], layer, 0, 0))

    pages = [(e, r) for e in range(eb) for r in range(n_pages)]
    in_specs = ([row(c), row(c), row(c), row(LANES)]
                + [paged((c, page), e, r) for e, r in pages]
                + [paged((c, page), e, r) for e, r in pages]
                + [paged((N_HEADS, page), e, r) for e, r in pages])
    return pl.pallas_call(
        functools.partial(_sample_attn_kernel, n_pages=n_pages, page=page, eb=eb, ds=ds),
        grid_spec=pltpu.PrefetchScalarGridSpec(
            num_scalar_prefetch=1, grid=(db // eb,), in_specs=in_specs,
            out_specs=pl.BlockSpec((rows, c), lambda i, pt: (i, 0))),
        out_shape=jax.ShapeDtypeStruct((db * ds, c), F32),
        compiler_params=_params("parallel"),
        name="sample_paged_attn",
    )(page_table.reshape(-1), q, k, v, lf, *([cache_k] * len(pages)), *([cache_v] * len(pages)),
      *([cache_lft] * len(pages)))


def _odd_tile(wy, wp, bg, w_rows, grp, pos0):
    t = bg.shape[0]
    ny = wy.shape[0]
    y0 = wy[SUBLANES:]
    y1 = pltpu.roll(wy, 1, axis=0)[SUBLANES:]
    y2 = pltpu.roll(wy, 2, axis=0)[SUBLANES:]
    out_c = bg * (w_rows[0] * y2 + w_rows[1] * y1 + w_rows[2] * y0)
    del ny
    back = 2 * SUBLANES
    s2 = wp + pltpu.roll(wp, 1, axis=0)
    s4 = s2 + pltpu.roll(s2, 2, axis=0)
    s8 = s4 + pltpu.roll(s4, 4, axis=0)
    s16 = s8 + pltpu.roll(s8, 8, axis=0)
    win = jnp.where(grp == 0, s2, jnp.where(grp == 1, s4, jnp.where(grp == 2, s8, s16)))[back:]
    wsz = jnp.where(grp == 0, POOL_WINDOWS[0],
                    jnp.where(grp == 1, POOL_WINDOWS[1], jnp.where(grp == 2, POOL_WINDOWS[2], POOL_WINDOWS[3])))
    pos = pos0 + lax.broadcasted_iota(jnp.int32, (t, LANES), 0)
    cnt = jnp.maximum(jnp.minimum(wsz, pos + 1), 1).astype(F32)
    d = win / cnt - wp[back:]
    return out_c, d


def _prompt_odd_kernel(y_ref, bg_ref, pd_ref, w_ref, oc_ref, d_ref, *, n_tiles, fp):
    grp = pl.program_id(1)
    w_rows = [w_ref[j:j + 1, :] for j in range(SCONV_W)]
    c = y_ref.shape[1]
    t = SEQ_TILE
    wy0 = jnp.concatenate([jnp.zeros((SUBLANES, c), F32), y_ref[0:t, :]], axis=0)
    wp0 = jnp.concatenate([jnp.zeros((2 * SUBLANES, c), F32), pd_ref[0:t, :]], axis=0)
    oc, d = _odd_tile(wy0, wp0, bg_ref[0:t, :], w_rows, grp, -fp)
    oc_ref[0:t, :] = oc
    d_ref[0:t, :] = d

    def body(i, carry):
        s0 = pl.multiple_of(i * t, t)
        wy = y_ref[pl.ds(s0 - SUBLANES, t + SUBLANES), :]
        wp = pd_ref[pl.ds(s0 - 2 * SUBLANES, t + 2 * SUBLANES), :]
        oc, d = _odd_tile(wy, wp, bg_ref[pl.ds(s0, t), :], w_rows, grp, s0 - fp)
        oc_ref[pl.ds(s0, t), :] = oc
        d_ref[pl.ds(s0, t), :] = d
        return carry

    lax.fori_loop(1, n_tiles, body, 0)


def _prompt_odd(y, bg, pd, w, *, nb, sp, fp):
    n, c = y.shape
    assert c // LANES == len(POOL_WINDOWS)
    blk = pl.BlockSpec((sp, LANES), lambda bi, ci: (bi, ci))
    return pl.pallas_call(
        functools.partial(_prompt_odd_kernel, n_tiles=sp // SEQ_TILE, fp=fp),
        grid=(nb, c // LANES),
        in_specs=[blk, blk, blk, pl.BlockSpec((SCONV_W, LANES), lambda bi, ci: (0, ci))],
        out_specs=[blk, blk],
        out_shape=[jax.ShapeDtypeStruct((nb * sp, c), F32)] * 2,
        compiler_params=_params("parallel", "parallel"),
        name="prompt_sconv_pool",
    )(y, bg, pd, w)


def _sample_odd_kernel(ss_ref, sp_ref, y_ref, bg_ref, pd_ref, w_ref,
                       oc_ref, d_ref, ss_out_ref, sp_out_ref, yext_ref, pext_ref, *, ds):
    dbt, _, c = ss_ref.shape
    hs = SCONV_W - 1
    y = y_ref[...].reshape(dbt, ds, c)
    pd = pd_ref[...].reshape(dbt, ds, c)
    bg = bg_ref[...].reshape(dbt, ds, c)
    yext_ref[:, SUBLANES - hs:SUBLANES, :] = ss_ref[...]
    yext_ref[:, SUBLANES:SUBLANES + ds, :] = y
    conv = None
    for j in range(SCONV_W):
        lo = SUBLANES - hs + j
        term = yext_ref[:, lo:lo + ds, :] * w_ref[j:j + 1, :]
        conv = term if conv is None else conv + term
    oc_ref[...] = (bg * conv).reshape(dbt * ds, c)
    ss_out_ref[...] = yext_ref[:, SUBLANES + ds - hs:SUBLANES + ds, :]

    base = 2 * SUBLANES
    pext_ref[:, base - POOL_HIST:base, :] = sp_ref[...]
    pext_ref[:, base:base + ds, :] = pd
    run = pd
    sums = {}
    for i in range(1, max(POOL_WINDOWS)):
        run = run + pext_ref[:, base - i:base - i + ds, :]
        if i + 1 in POOL_WINDOWS:
            sums[i + 1] = run
    gc = c // len(POOL_WINDOWS)
    lane = lax.broadcasted_iota(jnp.int32, (dbt, ds, c), 2)
    mean = sums[POOL_WINDOWS[-1]] / float(POOL_WINDOWS[-1])
    for gi in range(len(POOL_WINDOWS) - 2, -1, -1):
        mean = jnp.where(lane < (gi + 1) * gc, sums[POOL_WINDOWS[gi]] / float(POOL_WINDOWS[gi]), mean)
    d_ref[...] = (mean - pd).reshape(dbt * ds, c)
    sp_out_ref[...] = pext_ref[:, base + ds - POOL_HIST:base + ds, :]


def _sample_odd(state_s, state_p, layer, y, bg, pd, w, *, s0, db, ds, dbt):
    n, c = y.shape
    rows = dbt * ds
    row_in = pl.BlockSpec((rows, c), lambda i: (s0 // rows + i, 0))
    row_out = pl.BlockSpec((rows, c), lambda i: (i, 0))
    hs = state_s.shape[2]
    hp = state_p.shape[2]
    return pl.pallas_call(
        functools.partial(_sample_odd_kernel, ds=ds),
        grid=(db // dbt,),
        in_specs=[pl.BlockSpec((dbt, None, hs, c), lambda i: (i, layer, 0, 0)),
                  pl.BlockSpec((dbt, None, hp, c), lambda i: (i, layer, 0, 0)),
                  row_in, row_in, row_in, _full(w.shape)],
        out_specs=[row_out, row_out, pl.BlockSpec((dbt, hs, c), lambda i: (i, 0, 0)),
                   pl.BlockSpec((dbt, hp, c), lambda i: (i, 0, 0))],
        out_shape=[jax.ShapeDtypeStruct((db * ds, c), F32), jax.ShapeDtypeStruct((db * ds, c), F32),
                   jax.ShapeDtypeStruct((db, hs, c), F32), jax.ShapeDtypeStruct((db, hp, c), F32)],
        scratch_shapes=[pltpu.VMEM((dbt, SUBLANES + ds, c), F32), pltpu.VMEM((dbt, 2 * SUBLANES + ds, c), F32)],
        compiler_params=_params("parallel"),
        name="sample_sconv_pool",
    )(state_s, state_p, y, bg, pd, w)


def kernel(x_prompt, x_sample, cache_k, cache_v, cache_logf, page_table, state_conv_a, state_sconv, state_pool, meta_tokens, w_in_even, b_forget, w_dw_a, b_dw_a, ln_a_g, ln_a_b, w_out_even, w_in_odd, w_sconv, w_pool_mix, pool_scale, w_out_odd, ln_mix_g, ln_mix_b, w_ffn_gate, w_ffn_up, w_ffn_down, ln_ffn_g, ln_ffn_b):
    nb, seq, d = x_prompt.shape
    db, ds, _ = x_sample.shape
    depth = w_ffn_gate.shape[0]
    alpha = float((2 * depth) ** 0.25)
    ca = state_conv_a.shape[-1]
    att = N_HEADS * HEAD_DIM
    assert ds == SUBLANES and cache_k.shape[3] == N_HEADS and cache_k.shape[4] == HEAD_DIM
    assert state_conv_a.shape[2] == CONV_W - 1 and state_pool.shape[2] == POOL_HIST
    assert meta_tokens.shape[0] == N_META

    s_real = N_META + seq
    fp = (-s_real) % SEQ_TILE
    if fp < CONV_BACK:
        fp += SEQ_TILE
    sp = fp + s_real
    s0 = nb * sp
    n = s0 + db * ds
    tm = next(t for t in (512, 256, 128, 64, 32, 16, 8) if s0 % t == 0 and (db * ds) % t == 0)
    dbt = next(t for t in (8, 4, 2, 1) if db % t == 0)
    direct_out = seq % SEQ_TILE == 0 and (db * ds) % SEQ_TILE == 0

    head = jnp.concatenate([jnp.zeros((fp, d), F32), meta_tokens.astype(F32)], axis=0)
    x = jnp.concatenate([piece for bi in range(nb) for piece in (head, x_prompt[bi])]
                        + [x_sample.reshape(db * ds, d)], axis=0)

    n_phys = cache_k.shape[0]
    n_even = cache_k.shape[1]
    page = cache_k.shape[2]
    ck = jnp.transpose(cache_k, (0, 1, 3, 4, 2)).reshape(n_phys, n_even, att, page)
    cv = jnp.transpose(cache_v, (0, 1, 3, 4, 2)).reshape(n_phys, n_even, att, page)
    clft = jnp.swapaxes(cache_logf, 2, 3)

    row2 = lambda v: v.reshape(1, -1).astype(F32)

    def prompt_rows(arr, lo, hi):
        return jnp.stack([arr[bi * sp + lo:bi * sp + hi] for bi in range(nb)], axis=0)

    def sample_rows(arr):
        return arr[s0:].reshape(db, ds, -1)

    wg_all, wu_all, wd_all = (w.astype(BF16) for w in (w_ffn_gate, w_ffn_up, w_ffn_down))
    wo_even, wo_odd = w_out_even.astype(BF16), w_out_odd.astype(BF16)
    wi_even = jnp.pad(w_in_even.astype(BF16), ((0, 0), (0, 0), (0, LANES - N_HEADS)))
    wi_odd = w_in_odd.astype(BF16)

    def tail(name, xin, m1p, m2p, m1s, m2s, p1, p2, w_out_all, layer):
        last = direct_out and layer == depth - 1
        return _layer_tail(name, xin, m1p, m2p, m1s, m2s, p1, p2, w_out_all,
                           row2(ln_mix_g[layer]), row2(ln_mix_b[layer]), wg_all, wu_all, wd_all,
                           row2(ln_ffn_g[layer]), row2(ln_ffn_b[layer]), even=layer % 2 == 0,
                           mix_idx=layer // 2, layer=layer, tm=SEQ_TILE if last else tm, alpha=alpha,
                           nb=nb, sp=sp, fp=fp, out_rows=(seq, db * ds) if last else None)

    prev_kv = []
    lfp, cap, scp, plp = [], [], [], []
    ks_, vs_, lfs, cas, scs, pls = [], [], [], [], [], []
    for layer in range(depth):
        i = layer // 2
        if layer % 2 == 0:
            bf = jnp.pad(b_forget[i].astype(F32), (0, LANES - N_HEADS)).reshape(1, LANES)
            a, q, k, v, lf = _even_in(x, wi_even, i, bf, tm=tm, ca=ca, att=att)
            conv_p = _prompt_conv(a, w_dw_a[i].astype(F32), row2(b_dw_a[i]), nb=nb, sp=sp)
            conv_s, st_a = _sample_conv(state_conv_a.astype(F32), i, a, w_dw_a[i].astype(F32), row2(b_dw_a[i]),
                                        s0=s0, db=db, ds=ds, dbt=dbt)
            qa, ka, vat, *caches = _fox_prep(q, k, v, lf, prev_kv, i == n_even - 1, nb=nb, sp=sp, fp=fp)
            prev_kv.append((k, v))
            at_p = _flash(qa, ka, vat, hg=FLASH_HEADS)
            at_s = _sample_attn(page_table, i, q, k, v, lf, ck, cv, clft, s0=s0, db=db, ds=ds)
            x = tail("even_tail", x, conv_p, at_p, conv_s, at_s, row2(ln_a_g[i]), row2(ln_a_b[i]), wo_even, layer)
            lfp.append(prompt_rows(lf, fp, sp)[..., :N_HEADS])
            cap.append(prompt_rows(a, sp - (CONV_W - 1), sp))
            ks_.append(sample_rows(k).reshape(db, ds, N_HEADS, HEAD_DIM))
            vs_.append(sample_rows(v).reshape(db, ds, N_HEADS, HEAD_DIM))
            lfs.append(sample_rows(lf)[..., :N_HEADS])
            cas.append(st_a)
        else:
            c = state_sconv.shape[-1]
            y, bg, pd = _odd_in(x, wi_odd, i, tm=tm, c=c)
            oc_p, dd_p = _prompt_odd(y, bg, pd, w_sconv[i].astype(F32), nb=nb, sp=sp, fp=fp)
            oc_s, dd_s, st_s, st_p = _sample_odd(state_sconv.astype(F32), state_pool.astype(F32), i, y, bg, pd,
                                                 w_sconv[i].astype(F32), s0=s0, db=db, ds=ds, dbt=dbt)
            wm = jax.scipy.linalg.block_diag(*[w_pool_mix[i, g] for g in range(w_pool_mix.shape[1])]).astype(BF16)
            x = tail("odd_tail", x, oc_p, dd_p, oc_s, dd_s, wm, row2(pool_scale[i]), wo_odd, layer)
            scp.append(prompt_rows(y, sp - (SCONV_W - 1), sp))
            plp.append(prompt_rows(pd, sp - POOL_HIST, sp))
            scs.append(st_s)
            pls.append(st_p)

    if direct_out:
        y_prompt, y_sample = x[0], x[1].reshape(db, ds, d)
    else:
        y_prompt, y_sample = prompt_rows(x, fp + N_META, sp), sample_rows(x)
    st = lambda xs: jnp.stack(xs, axis=1)
    kp, vp = (jnp.transpose(c.reshape(nb, n_even, N_HEADS, HEAD_DIM, s_real), (0, 1, 4, 2, 3)) for c in caches)
    return (y_prompt, y_sample, kp, vp, st(lfp), st(cap), st(scp), st(plp),
            st(ks_), st(vs_), st(lfs), st(cas), st(scs), st(pls))
```

```python
import functools

import jax
import jax.numpy as jnp
from jax import lax
from jax.experimental import pallas as pl
from jax.experimental.pallas import tpu as pltpu

N_META = 16
N_HEADS = 8
HEAD_DIM = 64
CONV_W = 31
CONV_BACK = 32
SCONV_W = 3
POOL_WINDOWS = (2, 4, 8, 16)
POOL_HIST = max(POOL_WINDOWS) - 1
LN_EPS = 1e-5
SEQ_TILE = 256
LANES = 128
SUBLANES = 8
AUX_ONES = 24
FLASH_HEADS = 8
V_ROWS = HEAD_DIM + 16
VMEM_LIMIT = 56 * 1024 * 1024
NEG = -1e30
LOG2E = 1.4426950408889634
F32 = jnp.float32
BF16 = jnp.bfloat16
NT_DIMS = (((1,), (1,)), ((), ()))


def _params(*sem):
    return pltpu.CompilerParams(dimension_semantics=sem, vmem_limit_bytes=VMEM_LIMIT)


def _dot(a, b):
    return jnp.dot(a, b, preferred_element_type=F32)


def _ln(z, g, b):
    mu = jnp.mean(z, axis=-1, keepdims=True)
    zc = z - mu
    var = jnp.mean(zc * zc, axis=-1, keepdims=True)
    return zc * lax.rsqrt(var + LN_EPS) * g + b


def _silu(x):
    return x * jax.nn.sigmoid(x)


def _full(shape):
    return pl.BlockSpec(shape, lambda *_: (0,) * len(shape))


def _slab(a, idx):
    return pl.BlockSpec((None,) + a.shape[1:], lambda *_: (idx,) + (0,) * (a.ndim - 1),
                        pipeline_mode=pl.Buffered(1))


def _even_in_kernel(x_ref, w_ref, bf_ref, a_ref, q_ref, k_ref, v_ref, lf_ref, *, ca, att):
    xb = x_ref[...].astype(BF16)

    def mm(lo, hi):
        return _dot(xb, w_ref[:, lo:hi])

    u = mm(0, ca)
    g = mm(ca, 2 * ca)
    a_ref[...] = u * jax.nn.sigmoid(g)
    o = 2 * ca
    q_ref[...] = mm(o, o + att) * (HEAD_DIM ** -0.5)
    k_ref[...] = mm(o + att, o + 2 * att)
    v_ref[...] = mm(o + 2 * att, o + 3 * att)
    z = mm(o + 3 * att, o + 3 * att + LANES) + bf_ref[...]
    lf = jnp.minimum(z, 0.0) - jnp.log1p(jnp.exp(-jnp.abs(z)))
    lane = lax.broadcasted_iota(jnp.int32, lf.shape, 1)
    lf_ref[...] = jnp.where(lane < N_HEADS, lf, 0.0)


def _even_in(x, w, idx, bf, *, tm, ca, att):
    n, d = x.shape
    row = lambda c: pl.BlockSpec((tm, c), lambda i: (i, 0))
    return pl.pallas_call(
        functools.partial(_even_in_kernel, ca=ca, att=att),
        grid=(n // tm,),
        in_specs=[row(d), _slab(w, idx), _full(bf.shape)],
        out_specs=[row(ca), row(att), row(att), row(att), row(LANES)],
        out_shape=[jax.ShapeDtypeStruct((n, c), F32) for c in (ca, att, att, att, LANES)],
        compiler_params=_params("parallel"),
        name="even_in_proj",
    )(x, w, bf)


def _odd_in_kernel(x_ref, w_ref, y_ref, bg_ref, pd_ref, *, c):
    xb = x_ref[...].astype(BF16)
    hc = _dot(xb, w_ref[:, 0:c])
    bg_ref[...] = _dot(xb, w_ref[:, c:2 * c])
    cg = _dot(xb, w_ref[:, 2 * c:3 * c])
    y_ref[...] = cg * hc
    pd_ref[...] = _dot(xb, w_ref[:, 3 * c:4 * c])


def _odd_in(x, w, idx, *, tm, c):
    n, d = x.shape
    row = lambda cc: pl.BlockSpec((tm, cc), lambda i: (i, 0))
    return pl.pallas_call(
        functools.partial(_odd_in_kernel, c=c),
        grid=(n // tm,),
        in_specs=[row(d), _slab(w, idx)],
        out_specs=[row(c)] * 3,
        out_shape=[jax.ShapeDtypeStruct((n, c), F32)] * 3,
        compiler_params=_params("parallel"),
        name="odd_in_proj",
    )(x, w)


def _tail_kernel(x_ref, m1p_ref, m2p_ref, m1s_ref, m2s_ref, p1_ref, p2_ref, w_ref, g1_ref, b1_ref,
                 wg_ref, wu_ref, wd_ref, g2_ref, b2_ref, *rest, even, alpha, chunk, nb, sp, fp, npt, lead):
    i = pl.program_id(0)
    is_sample = i >= npt
    m1 = jnp.where(is_sample, m1s_ref[...], m1p_ref[...])
    m2 = jnp.where(is_sample, m2s_ref[...], m2p_ref[...])
    if even:
        m1 = _silu(_ln(m1, p1_ref[...], p2_ref[...]))
    else:
        m2 = _dot(m2.astype(BF16), p1_ref[...]) * p2_ref[...]
    ca = m1.shape[1]
    mix = _dot(m1.astype(BF16), w_ref[0:ca, :]) + _dot(m2.astype(BF16), w_ref[ca:, :])
    x1 = _ln(alpha * x_ref[...] + mix, g1_ref[...], b1_ref[...])

    h_ref = rest[-1]
    xb = x1.astype(BF16)
    for c in range(0, wg_ref.shape[1], chunk):
        gate = _dot(xb, wg_ref[:, c:c + chunk])
        up = _dot(xb, wu_ref[:, c:c + chunk])
        h_ref[:, c:c + chunk] = (_silu(gate) * up).astype(BF16)
    out = _ln(alpha * x1 + _dot(h_ref[...], wd_ref[...]), g2_ref[...], b2_ref[...])

    tm = x1.shape[0]
    if lead is None:
        r = i * tm + lax.broadcasted_iota(jnp.int32, (tm, 1), 0)
        keep = jnp.ones((tm, 1), F32)
        for bi in range(nb):
            keep = jnp.where((r >= bi * sp) & (r < bi * sp + fp), 0.0, keep)
        rest[0][...] = out * keep
    else:
        yp_ref, ys_ref = rest[:2]

        @pl.when(jnp.logical_and(jnp.logical_not(is_sample), i % (sp // tm) >= lead))
        def _():
            yp_ref[...] = out

        @pl.when(is_sample)
        def _():
            ys_ref[...] = out


def _layer_tail(name, x, m1p, m2p, m1s, m2s, p1, p2, w, g1, b1, wg, wu, wd, g2, b2, *, even, mix_idx, layer, tm,
                alpha, nb, sp, fp, out_rows=None):
    n, d = x.shape
    s0 = m1p.shape[0]
    dff = wg.shape[2]
    chunk = SEQ_TILE if dff % SEQ_TILE == 0 else dff
    npt = s0 // tm
    row = lambda c: pl.BlockSpec((tm, c), lambda i: (i, 0))
    prow = lambda c: pl.BlockSpec((tm, c), lambda i: (jnp.minimum(i, npt - 1), 0))
    srow = lambda c: pl.BlockSpec((tm, c), lambda i: (jnp.maximum(i - npt, 0), 0))
    once = lambda a: pl.BlockSpec(a.shape, lambda i: (0,) * a.ndim, pipeline_mode=pl.Buffered(1))
    consts = (p1, p2, w, g1, b1, wg, wu, wd, g2, b2)
    const_specs = [once(p1), once(p2), _slab(w, mix_idx), once(g1), once(b1),
                   _slab(wg, layer), _slab(wu, layer), _slab(wd, layer), once(g2), once(b2)]
    if out_rows is None:
        lead = None
        out_specs = row(d)
        out_shape = jax.ShapeDtypeStruct((n, d), F32)
    else:
        seq, db_rows = out_rows
        tps = sp // tm
        lead = (sp - seq) // tm
        out_specs = [
            pl.BlockSpec((None, tm, d), lambda i: (jnp.minimum(i // tps, nb - 1),
                                                   jnp.where(i < npt, jnp.maximum(i % tps - lead, 0), tps - lead - 1),
                                                   0)),
            srow(d)]
        out_shape = [jax.ShapeDtypeStruct((nb, seq, d), F32), jax.ShapeDtypeStruct((db_rows, d), F32)]
    return pl.pallas_call(
        functools.partial(_tail_kernel, even=even, alpha=alpha, chunk=chunk, nb=nb, sp=sp, fp=fp, npt=npt, lead=lead),
        grid=(n // tm,),
        in_specs=[row(d), prow(m1p.shape[1]), prow(m2p.shape[1]), srow(m1s.shape[1]), srow(m2s.shape[1])]
        + const_specs,
        out_specs=out_specs,
        out_shape=out_shape,
        scratch_shapes=[pltpu.VMEM((tm, dff), BF16)],
        compiler_params=_params("arbitrary" if out_rows else "parallel"),
        name=name,
    )(x, m1p, m2p, m1s, m2s, *consts)


def _conv_tile(win, w_rows):
    rows = win.shape[0]
    t = rows - CONV_BACK
    lead = CONV_BACK - (CONV_W - 1)
    acc = None
    for r in range(SUBLANES):
        rolled = win if r == 0 else pltpu.roll(win, rows - r, axis=0)
        for m in range(CONV_BACK // SUBLANES + 1):
            j = SUBLANES * m + r - lead
            if 0 <= j < CONV_W:
                term = rolled[SUBLANES * m:SUBLANES * m + t] * w_rows[j]
                acc = term if acc is None else acc + term
    return acc


def _prompt_conv_kernel(a_ref, w_ref, b_ref, o_ref, *, n_tiles):
    w_rows = [w_ref[j:j + 1, :] for j in range(CONV_W)]
    bias = b_ref[...]
    c = a_ref.shape[1]
    win0 = jnp.concatenate([jnp.zeros((CONV_BACK, c), F32), a_ref[0:SEQ_TILE, :]], axis=0)
    o_ref[0:SEQ_TILE, :] = _conv_tile(win0, w_rows) + bias

    def body(i, carry):
        s0 = pl.multiple_of(i * SEQ_TILE, SEQ_TILE)
        win = a_ref[pl.ds(s0 - CONV_BACK, SEQ_TILE + CONV_BACK), :]
        o_ref[pl.ds(s0, SEQ_TILE), :] = _conv_tile(win, w_rows) + bias
        return carry

    lax.fori_loop(1, n_tiles, body, 0)


def _prompt_conv(a, w, b, *, nb, sp):
    n, c = a.shape
    blk = pl.BlockSpec((sp, LANES), lambda bi, ci: (bi, ci))
    return pl.pallas_call(
        functools.partial(_prompt_conv_kernel, n_tiles=sp // SEQ_TILE),
        grid=(nb, c // LANES),
        in_specs=[blk, pl.BlockSpec((CONV_W, LANES), lambda bi, ci: (0, ci)),
                  pl.BlockSpec((1, LANES), lambda bi, ci: (0, ci))],
        out_specs=blk,
        out_shape=jax.ShapeDtypeStruct((nb * sp, c), F32),
        compiler_params=_params("parallel", "parallel"),
        name="prompt_conv",
    )(a, w, b)


def _sample_conv_kernel(st_ref, a_ref, w_ref, b_ref, o_ref, st_out_ref, ext_ref, *, ds):
    dbt, hist, c = st_ref.shape
    off = CONV_BACK - hist
    ext_ref[:, off:CONV_BACK, :] = st_ref[...]
    ext_ref[:, CONV_BACK:CONV_BACK + ds, :] = a_ref[...].reshape(dbt, ds, c)
    acc = None
    for j in range(CONV_W):
        term = ext_ref[:, off + j:off + j + ds, :] * w_ref[j:j + 1, :]
        acc = term if acc is None else acc + term
    o_ref[...] = (acc + b_ref[...]).reshape(dbt * ds, c)
    st_out_ref[...] = ext_ref[:, off + ds:CONV_BACK + ds, :]


def _sample_conv(state, layer, a, w, b, *, s0, db, ds, dbt):
    n, c = a.shape
    hist = state.shape[2]
    rows = dbt * ds
    return pl.pallas_call(
        functools.partial(_sample_conv_kernel, ds=ds),
        grid=(db // dbt,),
        in_specs=[pl.BlockSpec((dbt, None, hist, c), lambda i: (i, layer, 0, 0)),
                  pl.BlockSpec((rows, c), lambda i: (s0 // rows + i, 0)), _full(w.shape), _full(b.shape)],
        out_specs=[pl.BlockSpec((rows, c), lambda i: (i, 0)), pl.BlockSpec((dbt, hist, c), lambda i: (i, 0, 0))],
        out_shape=[jax.ShapeDtypeStruct((db * ds, c), F32), jax.ShapeDtypeStruct((db, hist, c), F32)],
        scratch_shapes=[pltpu.VMEM((dbt, CONV_BACK + ds, c), F32)],
        compiler_params=_params("parallel"),
        name="sample_conv",
    )(state, a, w, b)


def _fox_prep_kernel(q_ref, k_ref, v_ref, lf_ref, *rest, fp, ns, n_prev, emit):
    prev, rest = rest[:2 * n_prev], rest[2 * n_prev:]
    qa_ref, ka_ref, vat_ref = rest[:3]
    carry_ref = rest[5] if emit else rest[3]
    s = pl.program_id(1)

    @pl.when(s < ns)
    def _():
        _fox_operands(s, q_ref, k_ref, v_ref, lf_ref, qa_ref, ka_ref, vat_ref, carry_ref, fp)

    if not emit:
        return
    kt_ref, vt_ref, _, kc_ref, vc_ref = rest[3:]

    @pl.when(s == 0)
    def _():
        kc_ref[...] = jnp.zeros_like(kc_ref)
        vc_ref[...] = jnp.zeros_like(vc_ref)

    t = SEQ_TILE
    k_srcs = list(prev[0::2]) + [k_ref]
    v_srcs = list(prev[1::2]) + [v_ref]
    for srcs, held, dst in ((k_srcs, kc_ref, kt_ref), (v_srcs, vc_ref, vt_ref)):
        for li, src in enumerate(srcs):
            rows = jnp.concatenate([held[li], src[0:fp, :]], axis=0)
            for c in range(src.shape[1] // LANES):
                dst[li, c * LANES:(c + 1) * LANES, :] = rows[:, c * LANES:(c + 1) * LANES].T
            held[li] = src[fp:t, :]


def _fox_operands(s, q_ref, k_ref, v_ref, lf_ref, qa_ref, ka_ref, vat_ref, carry_ref, fp):
    @pl.when(s == 0)
    def _():
        carry_ref[...] = jnp.zeros_like(carry_ref)

    t = SEQ_TILE
    row = lax.broadcasted_iota(jnp.int32, (t, LANES), 0)
    lane = lax.broadcasted_iota(jnp.int32, (t, LANES), 1)
    c = jnp.where(s * t + row >= fp, lf_ref[...], 0.0)
    sh = 1
    while sh < t:
        c = c + jnp.where(row >= sh, pltpu.roll(c, sh, axis=0), 0.0)
        sh *= 2
    c = c + carry_ref[0:1, :]
    carry_ref[0:1, :] = c[t - 1:t, :]

    c = c * LOG2E
    c1 = c.astype(BF16).astype(F32)
    r1 = c - c1
    c2 = r1.astype(BF16).astype(F32)
    c3 = (r1 - c2).astype(BF16).astype(F32)
    g23 = jnp.where(lane < 2 * N_HEADS, pltpu.roll(c2, N_HEADS, axis=1),
                    jnp.where(lane < 3 * N_HEADS, pltpu.roll(c3, 2 * N_HEADS, axis=1), 0.0))
    g = jnp.where(lane < N_HEADS, c1, g23)
    gk = jnp.where(lane < N_HEADS, jnp.where(s * t + row >= fp, c1, -NEG), g23)
    ones_grp = (lane >= HEAD_DIM + AUX_ONES) & (lane < HEAD_DIM + 2 * AUX_ONES)
    ck = [jnp.where((lane >= HEAD_DIM) & (lane < HEAD_DIM + AUX_ONES), -pltpu.roll(gk, HEAD_DIM, axis=1),
                    jnp.where(ones_grp, 1.0, 0.0))]
    cq = [jnp.where(ones_grp, pltpu.roll(g, HEAD_DIM + AUX_ONES, axis=1), 0.0)]
    ck.append(pltpu.roll(ck[0], HEAD_DIM, axis=1))
    cq.append(pltpu.roll(cq[0], HEAD_DIM, axis=1))

    lane1 = lax.broadcasted_iota(jnp.int32, (1, LANES), 1)
    for h in range(N_HEADS):
        odd = h % 2
        pr = h // 2
        base = 0 if odd else HEAD_DIM
        a = lane1 - base
        mine = (lane1 & (N_HEADS - 1)) == h
        data = ((lane1 >= HEAD_DIM) if odd else (lane1 < HEAD_DIM)).astype(F32)
        sel1 = ((a >= 0) & (a < AUX_ONES) & mine).astype(F32)
        selc = ((a >= AUX_ONES) & (a < 2 * AUX_ONES) & mine).astype(F32)
        sl = slice(pr * LANES, (pr + 1) * LANES)
        qa_ref[h] = (q_ref[:, sl] * (data * LOG2E) + (cq[odd] * selc + sel1)).astype(BF16)
        ka_ref[h] = (k_ref[:, sl] * data + ck[odd]).astype(BF16)
    ones = jnp.ones((V_ROWS - HEAD_DIM, t), BF16)
    for pr in range(N_HEADS // 2):
        v_t = v_ref[:, pr * LANES:(pr + 1) * LANES].T.astype(BF16)
        vat_ref[2 * pr] = jnp.concatenate([v_t[0:HEAD_DIM], ones], axis=0)
        vat_ref[2 * pr + 1] = jnp.concatenate([v_t[HEAD_DIM:], ones], axis=0)


def _fox_prep(q, k, v, lf, prev_kv, emit, *, nb, sp, fp):
    ns = sp // SEQ_TILE
    att = q.shape[1]
    t = SEQ_TILE
    last = ns - 1
    row = lambda c: pl.BlockSpec((t, c), lambda bi, si: (bi * ns + jnp.minimum(si, last), 0))
    hm = pl.BlockSpec((None, N_HEADS, t, LANES), lambda bi, si: (bi, 0, jnp.minimum(si, last), 0))
    hm_t = pl.BlockSpec((None, N_HEADS, V_ROWS, t), lambda bi, si: (bi, 0, 0, jnp.minimum(si, last)))
    out_specs = [hm, hm, hm_t]
    out_shape = [jax.ShapeDtypeStruct((nb, N_HEADS, sp, LANES), BF16)] * 2 + [
        jax.ShapeDtypeStruct((nb, N_HEADS, V_ROWS, sp), BF16)]
    scratch = [pltpu.VMEM((SUBLANES, LANES), F32)]
    n_prev = len(prev_kv) if emit else 0
    if emit:
        assert 0 < fp < t and fp % SUBLANES == 0
        n_layers = n_prev + 1
        cache = pl.BlockSpec((None, n_layers, att, t), lambda bi, si: (bi, 0, 0, jnp.maximum(si - 1, 0)))
        out_specs += [cache, cache]
        out_shape += [jax.ShapeDtypeStruct((nb, n_layers, att, sp - fp), F32)] * 2
        scratch += [pltpu.VMEM((n_layers, t - fp, att), F32)] * 2
    prev = [a for kv in prev_kv for a in kv] if emit else []
    return pl.pallas_call(
        functools.partial(_fox_prep_kernel, fp=fp, ns=ns, n_prev=n_prev, emit=emit),
        grid=(nb, ns + 1 if emit else ns),
        in_specs=[row(att), row(att), row(att), row(LANES)] + [row(att)] * len(prev),
        out_specs=out_specs,
        out_shape=out_shape,
        scratch_shapes=scratch,
        compiler_params=_params("parallel", "arbitrary"),
        name="fox_prep",
    )(q, k, v, lf, *prev)


def _flash_kernel(q_ref, k_ref, vt_ref, o_ref, s_ref, p_ref, acc_ref, *, hg):
    i = pl.program_id(2)
    t = SEQ_TILE

    def scores_to(j, slot):
        ks = pl.multiple_of(j * t, t)
        for h in range(hg):
            s_ref[slot, h] = lax.dot_general(k_ref[h, pl.ds(ks, t), :], q_ref[h], NT_DIMS,
                                             preferred_element_type=F32)

    def accumulate(j, als):
        ks = pl.multiple_of(j * t, t)
        pvs = [_dot(vt_ref[h, :, pl.ds(ks, t)], p_ref[h]) for h in range(hg)]
        for h in range(hg):
            acc_ref[h] = als[h] * acc_ref[h] + pvs[h]

    def softmax(slot, ms, masked):
        new_m, als = [], []
        for h in range(hg):
            s = s_ref[slot, h]
            if masked:
                visible = (lax.broadcasted_iota(jnp.int32, (t, t), 0) <= lax.broadcasted_iota(jnp.int32, (t, t), 1))
                s = jnp.where(visible, s, NEG)
            mn = jnp.maximum(ms[h], jnp.max(s, axis=0, keepdims=True))
            als.append(jnp.exp2(ms[h] - mn))
            new_m.append(mn)
            p_ref[h] = jnp.exp2(s - mn).astype(BF16)
        return tuple(new_m), tuple(als)

    def stage(j, slot, state):
        ms, als = state
        scores_to(j + 1, 1 - slot)
        accumulate(jnp.maximum(j - 1, 0), als)
        return softmax(slot, ms, False)

    def pair(jj, state):
        return stage(2 * jj + 1, 1, stage(2 * jj, 0, state))

    def finish(slot, state):
        ms, als = state
        accumulate(jnp.maximum(i - 1, 0), als)
        _, als = softmax(slot, ms, True)
        accumulate(i, als)
        for pr in range(hg // 2):
            halves = [acc_ref[h, 0:HEAD_DIM, :] * (1.0 / acc_ref[h, HEAD_DIM:HEAD_DIM + 1, :])
                      for h in (2 * pr, 2 * pr + 1)]
            o_ref[:, pr * LANES:(pr + 1) * LANES] = jnp.concatenate(halves, axis=0).T

    p_ref[...] = jnp.zeros_like(p_ref)
    acc_ref[...] = jnp.zeros_like(acc_ref)
    scores_to(0, 0)
    state = ((jnp.full((1, t), NEG, F32),) * hg, (jnp.ones((1, t), F32),) * hg)
    state = lax.fori_loop(0, i // 2, pair, state)
    odd = i % 2 == 1
    state = lax.cond(odd, lambda st: stage(i - 1, 0, st), lambda st: st, state)
    pl.when(odd)(lambda: finish(1, state))
    pl.when(jnp.logical_not(odd))(lambda: finish(0, state))


def _flash(qa, ka, vat, *, hg):
    nb, _, sp, _ = qa.shape
    ns = sp // SEQ_TILE
    return pl.pallas_call(
        functools.partial(_flash_kernel, hg=hg),
        grid=(nb, N_HEADS // hg, ns),
        in_specs=[pl.BlockSpec((None, hg, SEQ_TILE, LANES), lambda bi, gi, qi: (bi, gi, qi, 0)),
                  pl.BlockSpec((None, hg, sp, LANES), lambda bi, gi, qi: (bi, gi, 0, 0)),
                  pl.BlockSpec((None, hg, V_ROWS, sp), lambda bi, gi, qi: (bi, gi, 0, 0))],
        out_specs=pl.BlockSpec((SEQ_TILE, hg * HEAD_DIM), lambda bi, gi, qi: (bi * ns + qi, gi)),
        out_shape=jax.ShapeDtypeStruct((nb * sp, N_HEADS * HEAD_DIM), F32),
        scratch_shapes=[pltpu.VMEM((2, hg, SEQ_TILE, SEQ_TILE), F32), pltpu.VMEM((hg, SEQ_TILE, SEQ_TILE), BF16),
                        pltpu.VMEM((hg, V_ROWS, SEQ_TILE), F32)],
        compiler_params=_params("parallel", "parallel", "arbitrary"),
        name="fox_flash",
    )(qa, ka, vat)


def _sample_attn_kernel(pt_ref, q_ref, kn_ref, vn_ref, lfn_ref, *rest, n_pages, page, eb, ds):
    del pt_ref
    o_ref = rest[3 * eb * n_pages]
    c = q_ref.shape[1]
    rows = ds * N_HEADS
    groups = [list(range(g, min(g + 2, n_pages))) for g in range(0, n_pages, 2)]
    past = n_pages * page
    sub = lax.broadcasted_iota(jnp.int32, (N_HEADS, c), 0)
    lane = lax.broadcasted_iota(jnp.int32, (N_HEADS, c), 1)
    hm = ((lane >= sub * HEAD_DIM) & (lane < (sub + 1) * HEAD_DIM)).astype(F32)
    hm_t = jnp.concatenate([hm] * ds, axis=0)
    pad = jnp.zeros((LANES - ds, c), F32)

    def refs_of(kind, e):
        return rest[(kind * eb + e) * n_pages:(kind * eb + e + 1) * n_pages]

    def scores(e):
        sl = slice(e * ds, (e + 1) * ds)
        q = q_ref[sl, :]
        qbd = jnp.concatenate([jnp.broadcast_to(q[t:t + 1, :], (N_HEADS, c)) * hm for t in range(ds)],
                              axis=0).astype(BF16)
        k_refs, lf_refs = refs_of(0, e), refs_of(2, e)
        s_parts = []
        for grp in groups:
            kp = jnp.concatenate([k_refs[r][...] for r in grp], axis=1).astype(BF16)
            s_parts.append(_dot(qbd, kp))
        s_past = jnp.concatenate(s_parts, axis=1)

        lf = jnp.concatenate([r[...] for r in lf_refs], axis=1)
        lane_p = lax.broadcasted_iota(jnp.int32, (N_HEADS, past), 1)
        suf = lf
        sh = 1
        while sh < past:
            suf = suf + jnp.where(lane_p < past - sh, pltpu.roll(suf, past - sh, axis=1), 0.0)
            sh *= 2
        s_past = s_past + jnp.concatenate([suf - lf] * ds, axis=0)

        kn = jnp.concatenate([kn_ref[sl, :], pad], axis=0).astype(BF16)
        s_new = lax.dot_general(qbd, kn, NT_DIMS, preferred_element_type=F32)
        cn = lfn_ref[sl, :]
        sub_n = lax.broadcasted_iota(jnp.int32, cn.shape, 0)
        sh = 1
        while sh < ds:
            cn = cn + jnp.where(sub_n >= sh, pltpu.roll(cn, sh, axis=0), 0.0)
            sh *= 2
        cn_t = jnp.concatenate([cn, jnp.zeros((LANES - ds, LANES), F32)], axis=0).T[0:N_HEADS, :]
        rown = lax.broadcasted_iota(jnp.int32, (rows, LANES), 0)
        u = lax.broadcasted_iota(jnp.int32, (rows, LANES), 1)
        s_new = jnp.where(u * N_HEADS <= rown, s_new - jnp.concatenate([cn_t] * ds, axis=0), NEG)
        return s_past, s_new

    def softmax(s_past, s_new):
        m = jnp.maximum(jnp.max(s_past, axis=1, keepdims=True), jnp.max(s_new, axis=1, keepdims=True))
        p_past = jnp.exp(s_past - m)
        p_new = jnp.exp(s_new - m)
        l = jnp.sum(p_past, axis=1, keepdims=True) + jnp.sum(p_new, axis=1, keepdims=True)
        return p_past.astype(BF16), p_new.astype(BF16), l

    def output(e, p_past, p_new, l):
        sl = slice(e * ds, (e + 1) * ds)
        v_refs = refs_of(1, e)
        vn = jnp.concatenate([vn_ref[sl, :], pad], axis=0).astype(BF16)
        o = _dot(p_new, vn)
        for grp in groups:
            vp = jnp.concatenate([v_refs[r][...] for r in grp], axis=1).astype(BF16)
            lo = grp[0] * page
            o = o + lax.dot_general(p_past[:, lo:lo + len(grp) * page], vp, NT_DIMS, preferred_element_type=F32)
        o = o * (1.0 / l) * hm_t
        rr = lax.broadcasted_iota(jnp.int32, (ds, rows), 1)
        tt = lax.broadcasted_iota(jnp.int32, (ds, rows), 0)
        pick = ((rr >= tt * N_HEADS) & (rr < (tt + 1) * N_HEADS)).astype(BF16)
        o_ref[sl, :] = _dot(pick, o.astype(BF16))

    ss = [scores(e) for e in range(eb)]
    ps = [softmax(*s) for s in ss]
    for e in range(eb):
        output(e, *ps[e])


def _sample_attn(page_table, layer, q, k, v, lf, cache_k, cache_v, cache_lft, *, s0, db, ds):
    n, c = q.shape
    n_pages = page_table.shape[1]
    page = cache_k.shape[3]
    eb = 2 if db % 2 == 0 else 1
    rows = eb * ds
    row = lambda cc: pl.BlockSpec((rows, cc), lambda i, pt: (s0 // rows + i, 0))

    def paged(shape, e, r):
        return pl.BlockSpec((None, None) + shape, lambda i, pt: (pt[(i * eb + e) * n_pages + r], layer, 0, 0))

    pages = [(e, r) for e in range(eb) for r in range(n_pages)]
    in_specs = ([row(c), row(c), row(c), row(LANES)]
                + [paged((c, page), e, r) for e, r in pages]
                + [paged((c, page), e, r) for e, r in pages]
                + [paged((N_HEADS, page), e, r) for e, r in pages])
    return pl.pallas_call(
        functools.partial(_sample_attn_kernel, n_pages=n_pages, page=page, eb=eb, ds=ds),
        grid_spec=pltpu.PrefetchScalarGridSpec(
            num_scalar_prefetch=1, grid=(db // eb,), in_specs=in_specs,
            out_specs=pl.BlockSpec((rows, c), lambda i, pt: (i, 0))),
        out_shape=jax.ShapeDtypeStruct((db * ds, c), F32),
        compiler_params=_params("parallel"),
        name="sample_paged_attn",
    )(page_table.reshape(-1), q, k, v, lf, *([cache_k] * len(pages)), *([cache_v] * len(pages)),
      *([cache_lft] * len(pages)))


def _odd_tile(wy, wp, bg, w_rows, grp, pos0):
    t = bg.shape[0]
    ny = wy.shape[0]
    y0 = wy[SUBLANES:]
    y1 = pltpu.roll(wy, 1, axis=0)[SUBLANES:]
    y2 = pltpu.roll(wy, 2, axis=0)[SUBLANES:]
    out_c = bg * (w_rows[0] * y2 + w_rows[1] * y1 + w_rows[2] * y0)
    del ny
    back = 2 * SUBLANES
    s2 = wp + pltpu.roll(wp, 1, axis=0)
    s4 = s2 + pltpu.roll(s2, 2, axis=0)
    s8 = s4 + pltpu.roll(s4, 4, axis=0)
    s16 = s8 + pltpu.roll(s8, 8, axis=0)
    win = jnp.where(grp == 0, s2, jnp.where(grp == 1, s4, jnp.where(grp == 2, s8, s16)))[back:]
    wsz = jnp.where(grp == 0, POOL_WINDOWS[0],
                    jnp.where(grp == 1, POOL_WINDOWS[1], jnp.where(grp == 2, POOL_WINDOWS[2], POOL_WINDOWS[3])))
    pos = pos0 + lax.broadcasted_iota(jnp.int32, (t, LANES), 0)
    cnt = jnp.maximum(jnp.minimum(wsz, pos + 1), 1).astype(F32)
    d = win / cnt - wp[back:]
    return out_c, d


def _prompt_odd_kernel(y_ref, bg_ref, pd_ref, w_ref, oc_ref, d_ref, *, n_tiles, fp):
    grp = pl.program_id(1)
    w_rows = [w_ref[j:j + 1, :] for j in range(SCONV_W)]
    c = y_ref.shape[1]
    t = SEQ_TILE
    wy0 = jnp.concatenate([jnp.zeros((SUBLANES, c), F32), y_ref[0:t, :]], axis=0)
    wp0 = jnp.concatenate([jnp.zeros((2 * SUBLANES, c), F32), pd_ref[0:t, :]], axis=0)
    oc, d = _odd_tile(wy0, wp0, bg_ref[0:t, :], w_rows, grp, -fp)
    oc_ref[0:t, :] = oc
    d_ref[0:t, :] = d

    def body(i, carry):
        s0 = pl.multiple_of(i * t, t)
        wy = y_ref[pl.ds(s0 - SUBLANES, t + SUBLANES), :]
        wp = pd_ref[pl.ds(s0 - 2 * SUBLANES, t + 2 * SUBLANES), :]
        oc, d = _odd_tile(wy, wp, bg_ref[pl.ds(s0, t), :], w_rows, grp, s0 - fp)
        oc_ref[pl.ds(s0, t), :] = oc
        d_ref[pl.ds(s0, t), :] = d
        return carry

    lax.fori_loop(1, n_tiles, body, 0)


def _prompt_odd(y, bg, pd, w, *, nb, sp, fp):
    n, c = y.shape
    assert c // LANES == len(POOL_WINDOWS)
    blk = pl.BlockSpec((sp, LANES), lambda bi, ci: (bi, ci))
    return pl.pallas_call(
        functools.partial(_prompt_odd_kernel, n_tiles=sp // SEQ_TILE, fp=fp),
        grid=(nb, c // LANES),
        in_specs=[blk, blk, blk, pl.BlockSpec((SCONV_W, LANES), lambda bi, ci: (0, ci))],
        out_specs=[blk, blk],
        out_shape=[jax.ShapeDtypeStruct((nb * sp, c), F32)] * 2,
        compiler_params=_params("parallel", "parallel"),
        name="prompt_sconv_pool",
    )(y, bg, pd, w)


def _sample_odd_kernel(ss_ref, sp_ref, y_ref, bg_ref, pd_ref, w_ref,
                       oc_ref, d_ref, ss_out_ref, sp_out_ref, yext_ref, pext_ref, *, ds):
    dbt, _, c = ss_ref.shape
    hs = SCONV_W - 1
    y = y_ref[...].reshape(dbt, ds, c)
    pd = pd_ref[...].reshape(dbt, ds, c)
    bg = bg_ref[...].reshape(dbt, ds, c)
    yext_ref[:, SUBLANES - hs:SUBLANES, :] = ss_ref[...]
    yext_ref[:, SUBLANES:SUBLANES + ds, :] = y
    conv = None
    for j in range(SCONV_W):
        lo = SUBLANES - hs + j
        term = yext_ref[:, lo:lo + ds, :] * w_ref[j:j + 1, :]
        conv = term if conv is None else conv + term
    oc_ref[...] = (bg * conv).reshape(dbt * ds, c)
    ss_out_ref[...] = yext_ref[:, SUBLANES + ds - hs:SUBLANES + ds, :]

    base = 2 * SUBLANES
    pext_ref[:, base - POOL_HIST:base, :] = sp_ref[...]
    pext_ref[:, base:base + ds, :] = pd
    run = pd
    sums = {}
    for i in range(1, max(POOL_WINDOWS)):
        run = run + pext_ref[:, base - i:base - i + ds, :]
        if i + 1 in POOL_WINDOWS:
            sums[i + 1] = run
    gc = c // len(POOL_WINDOWS)
    lane = lax.broadcasted_iota(jnp.int32, (dbt, ds, c), 2)
    mean = sums[POOL_WINDOWS[-1]] / float(POOL_WINDOWS[-1])
    for gi in range(len(POOL_WINDOWS) - 2, -1, -1):
        mean = jnp.where(lane < (gi + 1) * gc, sums[POOL_WINDOWS[gi]] / float(POOL_WINDOWS[gi]), mean)
    d_ref[...] = (mean - pd).reshape(dbt * ds, c)
    sp_out_ref[...] = pext_ref[:, base + ds - POOL_HIST:base + ds, :]


def _sample_odd(state_s, state_p, layer, y, bg, pd, w, *, s0, db, ds, dbt):
    n, c = y.shape
    rows = dbt * ds
    row_in = pl.BlockSpec((rows, c), lambda i: (s0 // rows + i, 0))
    row_out = pl.BlockSpec((rows, c), lambda i: (i, 0))
    hs = state_s.shape[2]
    hp = state_p.shape[2]
    return pl.pallas_call(
        functools.partial(_sample_odd_kernel, ds=ds),
        grid=(db // dbt,),
        in_specs=[pl.BlockSpec((dbt, None, hs, c), lambda i: (i, layer, 0, 0)),
                  pl.BlockSpec((dbt, None, hp, c), lambda i: (i, layer, 0, 0)),
                  row_in, row_in, row_in, _full(w.shape)],
        out_specs=[row_out, row_out, pl.BlockSpec((dbt, hs, c), lambda i: (i, 0, 0)),
                   pl.BlockSpec((dbt, hp, c), lambda i: (i, 0, 0))],
        out_shape=[jax.ShapeDtypeStruct((db * ds, c), F32), jax.ShapeDtypeStruct((db * ds, c), F32),
                   jax.ShapeDtypeStruct((db, hs, c), F32), jax.ShapeDtypeStruct((db, hp, c), F32)],
        scratch_shapes=[pltpu.VMEM((dbt, SUBLANES + ds, c), F32), pltpu.VMEM((dbt, 2 * SUBLANES + ds, c), F32)],
        compiler_params=_params("parallel"),
        name="sample_sconv_pool",
    )(state_s, state_p, y, bg, pd, w)


def kernel(x_prompt, x_sample, cache_k, cache_v, cache_logf, page_table, state_conv_a, state_sconv, state_pool, meta_tokens, w_in_even, b_forget, w_dw_a, b_dw_a, ln_a_g, ln_a_b, w_out_even, w_in_odd, w_sconv, w_pool_mix, pool_scale, w_out_odd, ln_mix_g, ln_mix_b, w_ffn_gate, w_ffn_up, w_ffn_down, ln_ffn_g, ln_ffn_b):
    nb, seq, d = x_prompt.shape
    db, ds, _ = x_sample.shape
    depth = w_ffn_gate.shape[0]
    alpha = float((2 * depth) ** 0.25)
    ca = state_conv_a.shape[-1]
    att = N_HEADS * HEAD_DIM
    assert ds == SUBLANES and cache_k.shape[3] == N_HEADS and cache_k.shape[4] == HEAD_DIM
    assert state_conv_a.shape[2] == CONV_W - 1 and state_pool.shape[2] == POOL_HIST
    assert meta_tokens.shape[0] == N_META

    s_real = N_META + seq
    fp = (-s_real) % SEQ_TILE
    if fp < CONV_BACK:
        fp += SEQ_TILE
    sp = fp + s_real
    s0 = nb * sp
    n = s0 + db * ds
    tm = next(t for t in (512, 256, 128, 64, 32, 16, 8) if s0 % t == 0 and (db * ds) % t == 0)
    tm_in = next(t for t in (1024, 512, 256, 128, 64, 32, 16, 8) if n % t == 0)
    dbt = next(t for t in (8, 4, 2, 1) if db % t == 0)
    direct_out = seq % SEQ_TILE == 0 and (db * ds) % SEQ_TILE == 0

    head = jnp.concatenate([jnp.zeros((fp, d), F32), meta_tokens.astype(F32)], axis=0)
    x = jnp.concatenate([piece for bi in range(nb) for piece in (head, x_prompt[bi])]
                        + [x_sample.reshape(db * ds, d)], axis=0)

    n_phys = cache_k.shape[0]
    n_even = cache_k.shape[1]
    page = cache_k.shape[2]
    ck = jnp.transpose(cache_k, (0, 1, 3, 4, 2)).reshape(n_phys, n_even, att, page)
    cv = jnp.transpose(cache_v, (0, 1, 3, 4, 2)).reshape(n_phys, n_even, att, page)
    clft = jnp.swapaxes(cache_logf, 2, 3)

    row2 = lambda v: v.reshape(1, -1).astype(F32)

    def prompt_rows(arr, lo, hi):
        return jnp.stack([arr[bi * sp + lo:bi * sp + hi] for bi in range(nb)], axis=0)

    def sample_rows(arr):
        return arr[s0:].reshape(db, ds, -1)

    wg_all, wu_all, wd_all = (w.astype(BF16) for w in (w_ffn_gate, w_ffn_up, w_ffn_down))
    wo_even, wo_odd = w_out_even.astype(BF16), w_out_odd.astype(BF16)
    wi_even = jnp.pad(w_in_even.astype(BF16), ((0, 0), (0, 0), (0, LANES - N_HEADS)))
    wi_odd = w_in_odd.astype(BF16)

    def tail(name, xin, m1p, m2p, m1s, m2s, p1, p2, w_out_all, layer):
        last = direct_out and layer == depth - 1
        return _layer_tail(name, xin, m1p, m2p, m1s, m2s, p1, p2, w_out_all,
                           row2(ln_mix_g[layer]), row2(ln_mix_b[layer]), wg_all, wu_all, wd_all,
                           row2(ln_ffn_g[layer]), row2(ln_ffn_b[layer]), even=layer % 2 == 0,
                           mix_idx=layer // 2, layer=layer, tm=SEQ_TILE if last else tm, alpha=alpha,
                           nb=nb, sp=sp, fp=fp, out_rows=(seq, db * ds) if last else None)

    prev_kv = []
    lfp, cap, scp, plp = [], [], [], []
    ks_, vs_, lfs, cas, scs, pls = [], [], [], [], [], []
    for layer in range(depth):
        i = layer // 2
        if layer % 2 == 0:
            bf = jnp.pad(b_forget[i].astype(F32), (0, LANES - N_HEADS)).reshape(1, LANES)
            a, q, k, v, lf = _even_in(x, wi_even, i, bf, tm=tm_in, ca=ca, att=att)
            conv_p = _prompt_conv(a, w_dw_a[i].astype(F32), row2(b_dw_a[i]), nb=nb, sp=sp)
            conv_s, st_a = _sample_conv(state_conv_a.astype(F32), i, a, w_dw_a[i].astype(F32), row2(b_dw_a[i]),
                                        s0=s0, db=db, ds=ds, dbt=dbt)
            qa, ka, vat, *caches = _fox_prep(q, k, v, lf, prev_kv, i == n_even - 1, nb=nb, sp=sp, fp=fp)
            prev_kv.append((k, v))
            at_p = _flash(qa, ka, vat, hg=FLASH_HEADS)
            at_s = _sample_attn(page_table, i, q, k, v, lf, ck, cv, clft, s0=s0, db=db, ds=ds)
            x = tail("even_tail", x, conv_p, at_p, conv_s, at_s, row2(ln_a_g[i]), row2(ln_a_b[i]), wo_even, layer)
            lfp.append(prompt_rows(lf, fp, sp)[..., :N_HEADS])
            cap.append(prompt_rows(a, sp - (CONV_W - 1), sp))
            ks_.append(sample_rows(k).reshape(db, ds, N_HEADS, HEAD_DIM))
            vs_.append(sample_rows(v).reshape(db, ds, N_HEADS, HEAD_DIM))
            lfs.append(sample_rows(lf)[..., :N_HEADS])
            cas.append(st_a)
        else:
            c = state_sconv.shape[-1]
            y, bg, pd = _odd_in(x, wi_odd, i, tm=tm_in, c=c)
            oc_p, dd_p = _prompt_odd(y, bg, pd, w_sconv[i].astype(F32), nb=nb, sp=sp, fp=fp)
            oc_s, dd_s, st_s, st_p = _sample_odd(state_sconv.astype(F32), state_pool.astype(F32), i, y, bg, pd,
                                                 w_sconv[i].astype(F32), s0=s0, db=db, ds=ds, dbt=dbt)
            wm = jax.scipy.linalg.block_diag(*[w_pool_mix[i, g] for g in range(w_pool_mix.shape[1])]).astype(BF16)
            x = tail("odd_tail", x, oc_p, dd_p, oc_s, dd_s, wm, row2(pool_scale[i]), wo_odd, layer)
            scp.append(prompt_rows(y, sp - (SCONV_W - 1), sp))
            plp.append(prompt_rows(pd, sp - POOL_HIST, sp))
            scs.append(st_s)
            pls.append(st_p)

    if direct_out:
        y_prompt, y_sample = x[0], x[1].reshape(db, ds, d)
    else:
        y_prompt, y_sample = prompt_rows(x, fp + N_META, sp), sample_rows(x)
    st = lambda xs: jnp.stack(xs, axis=1)
    kp, vp = (jnp.transpose(c.reshape(nb, n_even, N_HEADS, HEAD_DIM, s_real), (0, 1, 4, 2, 3)) for c in caches)
    return (y_prompt, y_sample, kp, vp, st(lfp), st(cap), st(scp), st(plp),
            st(ks_), st(vs_), st(lfs), st(cas), st(scs), st(pls))
```

```python
import functools

import jax
import jax.numpy as jnp
from jax import lax
from jax.experimental import pallas as pl
from jax.experimental.pallas import tpu as pltpu

N_META = 16
N_HEADS = 8
HEAD_DIM = 64
CONV_W = 31
SCONV_W = 3
POOL_WINDOWS = (2, 4, 8, 16)
POOL_HIST = max(POOL_WINDOWS) - 1
LN_EPS = 1e-5
SEQ_TILE = 256
LANES = 128
SUBLANES = 8
CONV_BACK = -(-(CONV_W - 1) // SUBLANES) * SUBLANES
C_TERMS = 3
AUX_ONES = C_TERMS * N_HEADS
FLASH_HEADS = 8
V_ROWS = HEAD_DIM + 2 * SUBLANES
VMEM_LIMIT = 56 * 1024 * 1024
NEG = -1e30
LOG2E = 1.4426950408889634
F32 = jnp.float32
BF16 = jnp.bfloat16
NT_DIMS = (((1,), (1,)), ((), ()))


def _params(*sem):
    return pltpu.CompilerParams(dimension_semantics=sem, vmem_limit_bytes=VMEM_LIMIT)


def _dot(a, b):
    return jnp.dot(a, b, preferred_element_type=F32)


def _ln(z, g, b):
    mu = jnp.mean(z, axis=-1, keepdims=True)
    zc = z - mu
    var = jnp.mean(zc * zc, axis=-1, keepdims=True)
    return zc * lax.rsqrt(var + LN_EPS) * g + b


def _silu(x):
    return x * jax.nn.sigmoid(x)


def _full(shape):
    return pl.BlockSpec(shape, lambda *_: (0,) * len(shape))


def _slab(a, idx):
    return pl.BlockSpec((None,) + a.shape[1:], lambda *_: (idx,) + (0,) * (a.ndim - 1),
                        pipeline_mode=pl.Buffered(1))


def _even_in_kernel(x_ref, w_ref, bf_ref, a_ref, q_ref, k_ref, v_ref, lf_ref, *, ca, att):
    xb = x_ref[...].astype(BF16)

    def mm(lo, hi):
        return _dot(xb, w_ref[:, lo:hi])

    u = mm(0, ca)
    g = mm(ca, 2 * ca)
    a_ref[...] = u * jax.nn.sigmoid(g)
    o = 2 * ca
    q_ref[...] = mm(o, o + att) * (HEAD_DIM ** -0.5)
    k_ref[...] = mm(o + att, o + 2 * att)
    v_ref[...] = mm(o + 2 * att, o + 3 * att)
    z = mm(o + 3 * att, o + 3 * att + LANES) + bf_ref[...]
    lf = jnp.minimum(z, 0.0) - jnp.log1p(jnp.exp(-jnp.abs(z)))
    lane = lax.broadcasted_iota(jnp.int32, lf.shape, 1)
    lf_ref[...] = jnp.where(lane < N_HEADS, lf, 0.0)


def _even_in(x, w, idx, bf, *, tm, ca, att):
    n, d = x.shape
    row = lambda c: pl.BlockSpec((tm, c), lambda i: (i, 0))
    return pl.pallas_call(
        functools.partial(_even_in_kernel, ca=ca, att=att),
        grid=(n // tm,),
        in_specs=[row(d), _slab(w, idx), _full(bf.shape)],
        out_specs=[row(ca), row(att), row(att), row(att), row(LANES)],
        out_shape=[jax.ShapeDtypeStruct((n, c), F32) for c in (ca, att, att, att, LANES)],
        compiler_params=_params("parallel"),
        name="even_in_proj",
    )(x, w, bf)


def _odd_in_kernel(x_ref, w_ref, y_ref, bg_ref, pd_ref, *, c):
    xb = x_ref[...].astype(BF16)
    hc = _dot(xb, w_ref[:, 0:c])
    bg_ref[...] = _dot(xb, w_ref[:, c:2 * c])
    cg = _dot(xb, w_ref[:, 2 * c:3 * c])
    y_ref[...] = cg * hc
    pd_ref[...] = _dot(xb, w_ref[:, 3 * c:4 * c])


def _odd_in(x, w, idx, *, tm, c):
    n, d = x.shape
    row = lambda cc: pl.BlockSpec((tm, cc), lambda i: (i, 0))
    return pl.pallas_call(
        functools.partial(_odd_in_kernel, c=c),
        grid=(n // tm,),
        in_specs=[row(d), _slab(w, idx)],
        out_specs=[row(c)] * 3,
        out_shape=[jax.ShapeDtypeStruct((n, c), F32)] * 3,
        compiler_params=_params("parallel"),
        name="odd_in_proj",
    )(x, w)


def _tail_kernel(x_ref, m1p_ref, m2p_ref, m1s_ref, m2s_ref, p1_ref, p2_ref, w_ref, g1_ref, b1_ref,
                 wg_ref, wu_ref, wd_ref, g2_ref, b2_ref, *rest, even, alpha, chunk, nb, sp, fp, npt, lead):
    i = pl.program_id(0)
    is_sample = i >= npt
    m1 = jnp.where(is_sample, m1s_ref[...], m1p_ref[...])
    m2 = jnp.where(is_sample, m2s_ref[...], m2p_ref[...])
    if even:
        m1 = _silu(_ln(m1, p1_ref[...], p2_ref[...]))
    else:
        m2 = _dot(m2.astype(BF16), p1_ref[...]) * p2_ref[...]
    ca = m1.shape[1]
    mix = _dot(m1.astype(BF16), w_ref[0:ca, :]) + _dot(m2.astype(BF16), w_ref[ca:, :])
    x1 = _ln(alpha * x_ref[...] + mix, g1_ref[...], b1_ref[...])

    h_ref = rest[-1]
    xb = x1.astype(BF16)
    for c in range(0, wg_ref.shape[1], chunk):
        gate = _dot(xb, wg_ref[:, c:c + chunk])
        up = _dot(xb, wu_ref[:, c:c + chunk])
        h_ref[:, c:c + chunk] = (_silu(gate) * up).astype(BF16)
    out = _ln(alpha * x1 + _dot(h_ref[...], wd_ref[...]), g2_ref[...], b2_ref[...])

    tm = x1.shape[0]
    if lead is None:
        r = i * tm + lax.broadcasted_iota(jnp.int32, (tm, 1), 0)
        keep = jnp.ones((tm, 1), F32)
        for bi in range(nb):
            keep = jnp.where((r >= bi * sp) & (r < bi * sp + fp), 0.0, keep)
        rest[0][...] = out * keep
    else:
        yp_ref, ys_ref = rest[:2]

        @pl.when(jnp.logical_and(jnp.logical_not(is_sample), i % (sp // tm) >= lead))
        def _():
            yp_ref[...] = out

        @pl.when(is_sample)
        def _():
            ys_ref[...] = out


def _layer_tail(name, x, m1p, m2p, m1s, m2s, p1, p2, w, g1, b1, wg, wu, wd, g2, b2, *, even, mix_idx, layer, tm,
                alpha, nb, sp, fp, out_rows=None):
    n, d = x.shape
    s0 = m1p.shape[0]
    dff = wg.shape[2]
    chunk = SEQ_TILE if dff % SEQ_TILE == 0 else dff
    npt = s0 // tm
    row = lambda c: pl.BlockSpec((tm, c), lambda i: (i, 0))
    prow = lambda c: pl.BlockSpec((tm, c), lambda i: (jnp.minimum(i, npt - 1), 0))
    srow = lambda c: pl.BlockSpec((tm, c), lambda i: (jnp.maximum(i - npt, 0), 0))
    once = lambda a: pl.BlockSpec(a.shape, lambda i: (0,) * a.ndim, pipeline_mode=pl.Buffered(1))
    consts = (p1, p2, w, g1, b1, wg, wu, wd, g2, b2)
    const_specs = [once(p1), once(p2), _slab(w, mix_idx), once(g1), once(b1),
                   _slab(wg, layer), _slab(wu, layer), _slab(wd, layer), once(g2), once(b2)]
    if out_rows is None:
        lead = None
        out_specs = row(d)
        out_shape = jax.ShapeDtypeStruct((n, d), F32)
    else:
        seq, db_rows = out_rows
        tps = sp // tm
        lead = (sp - seq) // tm
        out_specs = [
            pl.BlockSpec((None, tm, d), lambda i: (jnp.minimum(i // tps, nb - 1),
                                                   jnp.where(i < npt, jnp.maximum(i % tps - lead, 0), tps - lead - 1),
                                                   0)),
            srow(d)]
        out_shape = [jax.ShapeDtypeStruct((nb, seq, d), F32), jax.ShapeDtypeStruct((db_rows, d), F32)]
    return pl.pallas_call(
        functools.partial(_tail_kernel, even=even, alpha=alpha, chunk=chunk, nb=nb, sp=sp, fp=fp, npt=npt, lead=lead),
        grid=(n // tm,),
        in_specs=[row(d), prow(m1p.shape[1]), prow(m2p.shape[1]), srow(m1s.shape[1]), srow(m2s.shape[1])]
        + const_specs,
        out_specs=out_specs,
        out_shape=out_shape,
        scratch_shapes=[pltpu.VMEM((tm, dff), BF16)],
        compiler_params=_params("arbitrary" if out_rows else "parallel"),
        name=name,
    )(x, m1p, m2p, m1s, m2s, *consts)


def _conv_tile(win, w_rows):
    rows = win.shape[0]
    t = rows - CONV_BACK
    lead = CONV_BACK - (CONV_W - 1)
    acc = None
    for r in range(SUBLANES):
        rolled = win if r == 0 else pltpu.roll(win, rows - r, axis=0)
        for m in range(CONV_BACK // SUBLANES + 1):
            j = SUBLANES * m + r - lead
            if 0 <= j < CONV_W:
                term = rolled[SUBLANES * m:SUBLANES * m + t] * w_rows[j]
                acc = term if acc is None else acc + term
    return acc


def _sample_conv_kernel(st_ref, a_ref, w_ref, b_ref, o_ref, st_out_ref, ext_ref, *, ds):
    dbt, hist, c = st_ref.shape
    off = CONV_BACK - hist
    ext_ref[:, off:CONV_BACK, :] = st_ref[...]
    ext_ref[:, CONV_BACK:CONV_BACK + ds, :] = a_ref[...].reshape(dbt, ds, c)
    acc = None
    for j in range(CONV_W):
        term = ext_ref[:, off + j:off + j + ds, :] * w_ref[j:j + 1, :]
        acc = term if acc is None else acc + term
    o_ref[...] = (acc + b_ref[...]).reshape(dbt * ds, c)
    st_out_ref[...] = ext_ref[:, off + ds:CONV_BACK + ds, :]


def _sample_conv(state, layer, a, w, b, *, s0, db, ds, dbt):
    n, c = a.shape
    hist = state.shape[2]
    rows = dbt * ds
    return pl.pallas_call(
        functools.partial(_sample_conv_kernel, ds=ds),
        grid=(db // dbt,),
        in_specs=[pl.BlockSpec((dbt, None, hist, c), lambda i: (i, layer, 0, 0)),
                  pl.BlockSpec((rows, c), lambda i: (s0 // rows + i, 0)), _full(w.shape), _full(b.shape)],
        out_specs=[pl.BlockSpec((rows, c), lambda i: (i, 0)), pl.BlockSpec((dbt, hist, c), lambda i: (i, 0, 0))],
        out_shape=[jax.ShapeDtypeStruct((db * ds, c), F32), jax.ShapeDtypeStruct((db, hist, c), F32)],
        scratch_shapes=[pltpu.VMEM((dbt, CONV_BACK + ds, c), F32)],
        compiler_params=_params("parallel"),
        name="sample_conv",
    )(state, a, w, b)


def _fox_prep_kernel(q_ref, k_ref, v_ref, lf_ref, a_ref, wdw_ref, bdw_ref, *rest, fp, ns, n_prev, emit):
    prev, rest = rest[:2 * n_prev], rest[2 * n_prev:]
    qa_ref, ka_ref, vat_ref, cv_ref = rest[:4]
    carry_ref, ah_ref = rest[6:8] if emit else rest[4:6]
    s = pl.program_id(1)
    t = SEQ_TILE

    @pl.when(s < ns)
    def _():
        _fox_operands(s, q_ref, k_ref, v_ref, lf_ref, qa_ref, ka_ref, vat_ref, carry_ref, fp)

        @pl.when(s == 0)
        def _():
            ah_ref[...] = jnp.zeros_like(ah_ref)

        for cb in range(a_ref.shape[1] // LANES):
            cs = slice(cb * LANES, (cb + 1) * LANES)
            win = jnp.concatenate([ah_ref[:, cs], a_ref[:, cs]], axis=0)
            w_rows = [wdw_ref[j:j + 1, cs] for j in range(CONV_W)]
            cv_ref[:, cs] = _conv_tile(win, w_rows) + bdw_ref[:, cs]
        ah_ref[...] = a_ref[t - CONV_BACK:t, :]

    if not emit:
        return
    kt_ref, vt_ref = rest[4:6]
    kc_ref, vc_ref = rest[8:10]

    @pl.when(s == 0)
    def _():
        kc_ref[...] = jnp.zeros_like(kc_ref)
        vc_ref[...] = jnp.zeros_like(vc_ref)

    k_srcs = list(prev[0::2]) + [k_ref]
    v_srcs = list(prev[1::2]) + [v_ref]
    for srcs, held, dst in ((k_srcs, kc_ref, kt_ref), (v_srcs, vc_ref, vt_ref)):
        for li, src in enumerate(srcs):
            rows = jnp.concatenate([held[li], src[0:fp, :]], axis=0)
            for c in range(src.shape[1] // LANES):
                dst[li, c * LANES:(c + 1) * LANES, :] = rows[:, c * LANES:(c + 1) * LANES].T
            held[li] = src[fp:t, :]


def _fox_operands(s, q_ref, k_ref, v_ref, lf_ref, qa_ref, ka_ref, vat_ref, carry_ref, fp):
    @pl.when(s == 0)
    def _():
        carry_ref[...] = jnp.zeros_like(carry_ref)

    t = SEQ_TILE
    row = lax.broadcasted_iota(jnp.int32, (t, LANES), 0)
    lane = lax.broadcasted_iota(jnp.int32, (t, LANES), 1)
    c = jnp.where(s * t + row >= fp, lf_ref[...], 0.0)
    sh = 1
    while sh < t:
        c = c + jnp.where(row >= sh, pltpu.roll(c, sh, axis=0), 0.0)
        sh *= 2
    c = c + carry_ref[0:1, :]
    carry_ref[0:1, :] = c[t - 1:t, :]

    c = c * LOG2E
    c1 = c.astype(BF16).astype(F32)
    r1 = c - c1
    c2 = r1.astype(BF16).astype(F32)
    c3 = (r1 - c2).astype(BF16).astype(F32)
    g23 = jnp.where(lane < 2 * N_HEADS, pltpu.roll(c2, N_HEADS, axis=1),
                    jnp.where(lane < 3 * N_HEADS, pltpu.roll(c3, 2 * N_HEADS, axis=1), 0.0))
    g = jnp.where(lane < N_HEADS, c1, g23)
    gk = jnp.where(lane < N_HEADS, jnp.where(s * t + row >= fp, c1, -NEG), g23)
    ones_grp = (lane >= HEAD_DIM + AUX_ONES) & (lane < HEAD_DIM + 2 * AUX_ONES)
    ck = [jnp.where((lane >= HEAD_DIM) & (lane < HEAD_DIM + AUX_ONES), -pltpu.roll(gk, HEAD_DIM, axis=1),
                    jnp.where(ones_grp, 1.0, 0.0))]
    cq = [jnp.where(ones_grp, pltpu.roll(g, HEAD_DIM + AUX_ONES, axis=1), 0.0)]
    ck.append(pltpu.roll(ck[0], HEAD_DIM, axis=1))
    cq.append(pltpu.roll(cq[0], HEAD_DIM, axis=1))

    lane1 = lax.broadcasted_iota(jnp.int32, (1, LANES), 1)
    for h in range(N_HEADS):
        odd = h % 2
        pr = h // 2
        base = 0 if odd else HEAD_DIM
        a = lane1 - base
        mine = (lane1 & (N_HEADS - 1)) == h
        data = ((lane1 >= HEAD_DIM) if odd else (lane1 < HEAD_DIM)).astype(F32)
        sel1 = ((a >= 0) & (a < AUX_ONES) & mine).astype(F32)
        selc = ((a >= AUX_ONES) & (a < 2 * AUX_ONES) & mine).astype(F32)
        sl = slice(pr * LANES, (pr + 1) * LANES)
        qa_ref[h] = (q_ref[:, sl] * (data * LOG2E) + (cq[odd] * selc + sel1)).astype(BF16)
        ka_ref[h] = (k_ref[:, sl] * data + ck[odd]).astype(BF16)
    ones = jnp.ones((V_ROWS - HEAD_DIM, t), BF16)
    for pr in range(N_HEADS // 2):
        v_t = v_ref[:, pr * LANES:(pr + 1) * LANES].T.astype(BF16)
        vat_ref[2 * pr] = jnp.concatenate([v_t[0:HEAD_DIM], ones], axis=0)
        vat_ref[2 * pr + 1] = jnp.concatenate([v_t[HEAD_DIM:], ones], axis=0)


def _fox_prep(q, k, v, lf, a, w_dw, b_dw, prev_kv, emit, *, nb, sp, fp):
    ns = sp // SEQ_TILE
    att = q.shape[1]
    ca = a.shape[1]
    t = SEQ_TILE
    last = ns - 1
    row = lambda c: pl.BlockSpec((t, c), lambda bi, si: (bi * ns + jnp.minimum(si, last), 0))
    hm = pl.BlockSpec((None, N_HEADS, t, LANES), lambda bi, si: (bi, 0, jnp.minimum(si, last), 0))
    hm_t = pl.BlockSpec((None, N_HEADS, V_ROWS, t), lambda bi, si: (bi, 0, 0, jnp.minimum(si, last)))
    out_specs = [hm, hm, hm_t, row(ca)]
    out_shape = [jax.ShapeDtypeStruct((nb, N_HEADS, sp, LANES), BF16)] * 2 + [
        jax.ShapeDtypeStruct((nb, N_HEADS, V_ROWS, sp), BF16), jax.ShapeDtypeStruct((nb * sp, ca), F32)]
    scratch = [pltpu.VMEM((SUBLANES, LANES), F32), pltpu.VMEM((CONV_BACK, ca), F32)]
    n_prev = len(prev_kv) if emit else 0
    if emit:
        assert 0 < fp < t and fp % SUBLANES == 0
        n_layers = n_prev + 1
        cache = pl.BlockSpec((None, n_layers, att, t), lambda bi, si: (bi, 0, 0, jnp.maximum(si - 1, 0)))
        out_specs += [cache, cache]
        out_shape += [jax.ShapeDtypeStruct((nb, n_layers, att, sp - fp), F32)] * 2
        scratch += [pltpu.VMEM((n_layers, t - fp, att), F32)] * 2
    prev = [a for kv in prev_kv for a in kv] if emit else []
    return pl.pallas_call(
        functools.partial(_fox_prep_kernel, fp=fp, ns=ns, n_prev=n_prev, emit=emit),
        grid=(nb, ns + 1 if emit else ns),
        in_specs=[row(att), row(att), row(att), row(LANES), row(ca), _full(w_dw.shape), _full(b_dw.shape)]
        + [row(att)] * len(prev),
        out_specs=out_specs,
        out_shape=out_shape,
        scratch_shapes=scratch,
        compiler_params=_params("parallel", "arbitrary"),
        name="fox_prep",
    )(q, k, v, lf, a, w_dw, b_dw, *prev)


def _flash_kernel(q_ref, k_ref, vt_ref, o_ref, s_ref, p_ref, acc_ref, *, hg):
    i = pl.program_id(2)
    t = SEQ_TILE

    def scores_to(j, slot):
        ks = pl.multiple_of(j * t, t)
        for h in range(hg):
            s_ref[slot, h] = lax.dot_general(k_ref[h, pl.ds(ks, t), :], q_ref[h], NT_DIMS,
                                             preferred_element_type=F32)

    def accumulate(j, als):
        ks = pl.multiple_of(j * t, t)
        pvs = [_dot(vt_ref[h, :, pl.ds(ks, t)], p_ref[h]) for h in range(hg)]
        for h in range(hg):
            acc_ref[h] = als[h] * acc_ref[h] + pvs[h]

    def softmax(slot, ms, masked):
        new_m, als = [], []
        for h in range(hg):
            s = s_ref[slot, h]
            if masked:
                visible = (lax.broadcasted_iota(jnp.int32, (t, t), 0) <= lax.broadcasted_iota(jnp.int32, (t, t), 1))
                s = jnp.where(visible, s, NEG)
            mn = jnp.maximum(ms[h], jnp.max(s, axis=0, keepdims=True))
            als.append(jnp.exp2(ms[h] - mn))
            new_m.append(mn)
            p_ref[h] = jnp.exp2(s - mn).astype(BF16)
        return tuple(new_m), tuple(als)

    def stage(j, slot, state):
        ms, als = state
        scores_to(j + 1, 1 - slot)
        accumulate(jnp.maximum(j - 1, 0), als)
        return softmax(slot, ms, False)

    def pair(jj, state):
        return stage(2 * jj + 1, 1, stage(2 * jj, 0, state))

    def finish(slot, state):
        ms, als = state
        accumulate(jnp.maximum(i - 1, 0), als)
        _, als = softmax(slot, ms, True)
        accumulate(i, als)
        for pr in range(hg // 2):
            halves = [acc_ref[h, 0:HEAD_DIM, :] * (1.0 / acc_ref[h, HEAD_DIM:HEAD_DIM + 1, :])
                      for h in (2 * pr, 2 * pr + 1)]
            o_ref[:, pr * LANES:(pr + 1) * LANES] = jnp.concatenate(halves, axis=0).T

    p_ref[...] = jnp.zeros_like(p_ref)
    acc_ref[...] = jnp.zeros_like(acc_ref)
    scores_to(0, 0)
    state = ((jnp.full((1, t), NEG, F32),) * hg, (jnp.ones((1, t), F32),) * hg)
    state = lax.fori_loop(0, i // 2, pair, state)
    odd = i % 2 == 1
    state = lax.cond(odd, lambda st: stage(i - 1, 0, st), lambda st: st, state)
    pl.when(odd)(lambda: finish(1, state))
    pl.when(jnp.logical_not(odd))(lambda: finish(0, state))


def _flash(qa, ka, vat, *, hg):
    nb, _, sp, _ = qa.shape
    ns = sp // SEQ_TILE
    return pl.pallas_call(
        functools.partial(_flash_kernel, hg=hg),
        grid=(nb, N_HEADS // hg, ns),
        in_specs=[pl.BlockSpec((None, hg, SEQ_TILE, LANES), lambda bi, gi, qi: (bi, gi, qi, 0)),
                  pl.BlockSpec((None, hg, sp, LANES), lambda bi, gi, qi: (bi, gi, 0, 0)),
                  pl.BlockSpec((None, hg, V_ROWS, sp), lambda bi, gi, qi: (bi, gi, 0, 0))],
        out_specs=pl.BlockSpec((SEQ_TILE, hg * HEAD_DIM), lambda bi, gi, qi: (bi * ns + qi, gi)),
        out_shape=jax.ShapeDtypeStruct((nb * sp, N_HEADS * HEAD_DIM), F32),
        scratch_shapes=[pltpu.VMEM((2, hg, SEQ_TILE, SEQ_TILE), F32), pltpu.VMEM((hg, SEQ_TILE, SEQ_TILE), BF16),
                        pltpu.VMEM((hg, V_ROWS, SEQ_TILE), F32)],
        compiler_params=_params("parallel", "parallel", "arbitrary"),
        name="fox_flash",
    )(qa, ka, vat)


def _sample_attn_kernel(pt_ref, q_ref, kn_ref, vn_ref, lfn_ref, *rest, n_pages, page, eb, ds):
    del pt_ref
    o_ref = rest[3 * eb * n_pages]
    c = q_ref.shape[1]
    rows = ds * N_HEADS
    groups = [list(range(g, min(g + 2, n_pages))) for g in range(0, n_pages, 2)]
    past = n_pages * page
    sub = lax.broadcasted_iota(jnp.int32, (N_HEADS, c), 0)
    lane = lax.broadcasted_iota(jnp.int32, (N_HEADS, c), 1)
    hm = ((lane >= sub * HEAD_DIM) & (lane < (sub + 1) * HEAD_DIM)).astype(F32)
    hm_t = jnp.concatenate([hm] * ds, axis=0)
    pad = jnp.zeros((LANES - ds, c), F32)

    def refs_of(kind, e):
        return rest[(kind * eb + e) * n_pages:(kind * eb + e + 1) * n_pages]

    def scores(e):
        sl = slice(e * ds, (e + 1) * ds)
        q = q_ref[sl, :]
        qbd = jnp.concatenate([jnp.broadcast_to(q[t:t + 1, :], (N_HEADS, c)) * hm for t in range(ds)],
                              axis=0).astype(BF16)
        k_refs, lf_refs = refs_of(0, e), refs_of(2, e)
        s_parts = []
        for grp in groups:
            kp = jnp.concatenate([k_refs[r][...] for r in grp], axis=1).astype(BF16)
            s_parts.append(_dot(qbd, kp))
        s_past = jnp.concatenate(s_parts, axis=1)

        lf = jnp.concatenate([r[...] for r in lf_refs], axis=1)
        lane_p = lax.broadcasted_iota(jnp.int32, (N_HEADS, past), 1)
        suf = lf
        sh = 1
        while sh < past:
            suf = suf + jnp.where(lane_p < past - sh, pltpu.roll(suf, past - sh, axis=1), 0.0)
            sh *= 2
        s_past = s_past + jnp.concatenate([suf - lf] * ds, axis=0)

        kn = jnp.concatenate([kn_ref[sl, :], pad], axis=0).astype(BF16)
        s_new = lax.dot_general(qbd, kn, NT_DIMS, preferred_element_type=F32)
        cn = lfn_ref[sl, :]
        sub_n = lax.broadcasted_iota(jnp.int32, cn.shape, 0)
        sh = 1
        while sh < ds:
            cn = cn + jnp.where(sub_n >= sh, pltpu.roll(cn, sh, axis=0), 0.0)
            sh *= 2
        cn_t = jnp.concatenate([cn, jnp.zeros((LANES - ds, LANES), F32)], axis=0).T[0:N_HEADS, :]
        rown = lax.broadcasted_iota(jnp.int32, (rows, LANES), 0)
        u = lax.broadcasted_iota(jnp.int32, (rows, LANES), 1)
        s_new = jnp.where(u * N_HEADS <= rown, s_new - jnp.concatenate([cn_t] * ds, axis=0), NEG)
        return s_past, s_new

    def softmax(s_past, s_new):
        m = jnp.maximum(jnp.max(s_past, axis=1, keepdims=True), jnp.max(s_new, axis=1, keepdims=True))
        p_past = jnp.exp(s_past - m)
        p_new = jnp.exp(s_new - m)
        l = jnp.sum(p_past, axis=1, keepdims=True) + jnp.sum(p_new, axis=1, keepdims=True)
        return p_past.astype(BF16), p_new.astype(BF16), l

    def output(e, p_past, p_new, l):
        sl = slice(e * ds, (e + 1) * ds)
        v_refs = refs_of(1, e)
        vn = jnp.concatenate([vn_ref[sl, :], pad], axis=0).astype(BF16)
        o = _dot(p_new, vn)
        for grp in groups:
            vp = jnp.concatenate([v_refs[r][...] for r in grp], axis=1).astype(BF16)
            lo = grp[0] * page
            o = o + lax.dot_general(p_past[:, lo:lo + len(grp) * page], vp, NT_DIMS, preferred_element_type=F32)
        o = o * (1.0 / l) * hm_t
        rr = lax.broadcasted_iota(jnp.int32, (ds, rows), 1)
        tt = lax.broadcasted_iota(jnp.int32, (ds, rows), 0)
        pick = ((rr >= tt * N_HEADS) & (rr < (tt + 1) * N_HEADS)).astype(BF16)
        o_ref[sl, :] = _dot(pick, o.astype(BF16))

    ss = [scores(e) for e in range(eb)]
    ps = [softmax(*s) for s in ss]
    for e in range(eb):
        output(e, *ps[e])


def _sample_attn(page_table, layer, q, k, v, lf, cache_k, cache_v, cache_lft, *, s0, db, ds):
    n, c = q.shape
    n_pages = page_table.shape[1]
    page = cache_k.shape[3]
    eb = 2 if db % 2 == 0 else 1
    rows = eb * ds
    row = lambda cc: pl.BlockSpec((rows, cc), lambda i, pt: (s0 // rows + i, 0))

    def paged(shape, e, r):
        return pl.BlockSpec((None, None) + shape, lambda i, pt: (pt[(i * eb + e) * n_pages + r], layer, 0, 0))

    pages = [(e, r) for e in range(eb) for r in range(n_pages)]
    in_specs = ([row(c), row(c), row(c), row(LANES)]
                + [paged((c, page), e, r) for e, r in pages]
                + [paged((c, page), e, r) for e, r in pages]
                + [paged((N_HEADS, page), e, r) for e, r in pages])
    return pl.pallas_call(
        functools.partial(_sample_attn_kernel, n_pages=n_pages, page=page, eb=eb, ds=ds),
        grid_spec=pltpu.PrefetchScalarGridSpec(
            num_scalar_prefetch=1, grid=(db // eb,), in_specs=in_specs,
            out_specs=pl.BlockSpec((rows, c), lambda i, pt: (i, 0))),
        out_shape=jax.ShapeDtypeStruct((db * ds, c), F32),
        compiler_params=_params("parallel"),
        name="sample_paged_attn",
    )(page_table.reshape(-1), q, k, v, lf, *([cache_k] * len(pages)), *([cache_v] * len(pages)),
      *([cache_lft] * len(pages)))


def _odd_tile(wy, wp, bg, w_rows, grp, pos0):
    t = bg.shape[0]
    y0 = wy[SUBLANES:]
    y1 = pltpu.roll(wy, 1, axis=0)[SUBLANES:]
    y2 = pltpu.roll(wy, 2, axis=0)[SUBLANES:]
    out_c = bg * (w_rows[0] * y2 + w_rows[1] * y1 + w_rows[2] * y0)
    back = 2 * SUBLANES
    s2 = wp + pltpu.roll(wp, 1, axis=0)
    s4 = s2 + pltpu.roll(s2, 2, axis=0)
    s8 = s4 + pltpu.roll(s4, 4, axis=0)
    s16 = s8 + pltpu.roll(s8, 8, axis=0)
    win = jnp.where(grp == 0, s2, jnp.where(grp == 1, s4, jnp.where(grp == 2, s8, s16)))[back:]
    wsz = jnp.where(grp == 0, POOL_WINDOWS[0],
                    jnp.where(grp == 1, POOL_WINDOWS[1], jnp.where(grp == 2, POOL_WINDOWS[2], POOL_WINDOWS[3])))
    pos = pos0 + lax.broadcasted_iota(jnp.int32, (t, LANES), 0)
    cnt = jnp.maximum(jnp.minimum(wsz, pos + 1), 1).astype(F32)
    d = win / cnt - wp[back:]
    return out_c, d


def _prompt_odd_kernel(y_ref, bg_ref, pd_ref, w_ref, oc_ref, d_ref, *, n_tiles, fp):
    grp = pl.program_id(1)
    w_rows = [w_ref[j:j + 1, :] for j in range(SCONV_W)]
    c = y_ref.shape[1]
    t = SEQ_TILE
    wy0 = jnp.concatenate([jnp.zeros((SUBLANES, c), F32), y_ref[0:t, :]], axis=0)
    wp0 = jnp.concatenate([jnp.zeros((2 * SUBLANES, c), F32), pd_ref[0:t, :]], axis=0)
    oc, d = _odd_tile(wy0, wp0, bg_ref[0:t, :], w_rows, grp, -fp)
    oc_ref[0:t, :] = oc
    d_ref[0:t, :] = d

    def body(i, carry):
        s0 = pl.multiple_of(i * t, t)
        wy = y_ref[pl.ds(s0 - SUBLANES, t + SUBLANES), :]
        wp = pd_ref[pl.ds(s0 - 2 * SUBLANES, t + 2 * SUBLANES), :]
        oc, d = _odd_tile(wy, wp, bg_ref[pl.ds(s0, t), :], w_rows, grp, s0 - fp)
        oc_ref[pl.ds(s0, t), :] = oc
        d_ref[pl.ds(s0, t), :] = d
        return carry

    lax.fori_loop(1, n_tiles, body, 0)


def _prompt_odd(y, bg, pd, w, *, nb, sp, fp):
    n, c = y.shape
    assert c // LANES == len(POOL_WINDOWS)
    blk = pl.BlockSpec((sp, LANES), lambda bi, ci: (bi, ci))
    return pl.pallas_call(
        functools.partial(_prompt_odd_kernel, n_tiles=sp // SEQ_TILE, fp=fp),
        grid=(nb, c // LANES),
        in_specs=[blk, blk, blk, pl.BlockSpec((SCONV_W, LANES), lambda bi, ci: (0, ci))],
        out_specs=[blk, blk],
        out_shape=[jax.ShapeDtypeStruct((nb * sp, c), F32)] * 2,
        compiler_params=_params("parallel", "parallel"),
        name="prompt_sconv_pool",
    )(y, bg, pd, w)


def _sample_odd_kernel(ss_ref, sp_ref, y_ref, bg_ref, pd_ref, w_ref,
                       oc_ref, d_ref, ss_out_ref, sp_out_ref, yext_ref, pext_ref, *, ds):
    dbt, _, c = ss_ref.shape
    hs = SCONV_W - 1
    y = y_ref[...].reshape(dbt, ds, c)
    pd = pd_ref[...].reshape(dbt, ds, c)
    bg = bg_ref[...].reshape(dbt, ds, c)
    yext_ref[:, SUBLANES - hs:SUBLANES, :] = ss_ref[...]
    yext_ref[:, SUBLANES:SUBLANES + ds, :] = y
    conv = None
    for j in range(SCONV_W):
        lo = SUBLANES - hs + j
        term = yext_ref[:, lo:lo + ds, :] * w_ref[j:j + 1, :]
        conv = term if conv is None else conv + term
    oc_ref[...] = (bg * conv).reshape(dbt * ds, c)
    ss_out_ref[...] = yext_ref[:, SUBLANES + ds - hs:SUBLANES + ds, :]

    base = 2 * SUBLANES
    pext_ref[:, base - POOL_HIST:base, :] = sp_ref[...]
    pext_ref[:, base:base + ds, :] = pd
    run = pd
    sums = {}
    for i in range(1, max(POOL_WINDOWS)):
        run = run + pext_ref[:, base - i:base - i + ds, :]
        if i + 1 in POOL_WINDOWS:
            sums[i + 1] = run
    gc = c // len(POOL_WINDOWS)
    lane = lax.broadcasted_iota(jnp.int32, (dbt, ds, c), 2)
    mean = sums[POOL_WINDOWS[-1]] / float(POOL_WINDOWS[-1])
    for gi in range(len(POOL_WINDOWS) - 2, -1, -1):
        mean = jnp.where(lane < (gi + 1) * gc, sums[POOL_WINDOWS[gi]] / float(POOL_WINDOWS[gi]), mean)
    d_ref[...] = (mean - pd).reshape(dbt * ds, c)
    sp_out_ref[...] = pext_ref[:, base + ds - POOL_HIST:base + ds, :]


def _sample_odd(state_s, state_p, layer, y, bg, pd, w, *, s0, db, ds, dbt):
    n, c = y.shape
    rows = dbt * ds
    row_in = pl.BlockSpec((rows, c), lambda i: (s0 // rows + i, 0))
    row_out = pl.BlockSpec((rows, c), lambda i: (i, 0))
    hs = state_s.shape[2]
    hp = state_p.shape[2]
    return pl.pallas_call(
        functools.partial(_sample_odd_kernel, ds=ds),
        grid=(db // dbt,),
        in_specs=[pl.BlockSpec((dbt, None, hs, c), lambda i: (i, layer, 0, 0)),
                  pl.BlockSpec((dbt, None, hp, c), lambda i: (i, layer, 0, 0)),
                  row_in, row_in, row_in, _full(w.shape)],
        out_specs=[row_out, row_out, pl.BlockSpec((dbt, hs, c), lambda i: (i, 0, 0)),
                   pl.BlockSpec((dbt, hp, c), lambda i: (i, 0, 0))],
        out_shape=[jax.ShapeDtypeStruct((db * ds, c), F32), jax.ShapeDtypeStruct((db * ds, c), F32),
                   jax.ShapeDtypeStruct((db, hs, c), F32), jax.ShapeDtypeStruct((db, hp, c), F32)],
        scratch_shapes=[pltpu.VMEM((dbt, SUBLANES + ds, c), F32), pltpu.VMEM((dbt, 2 * SUBLANES + ds, c), F32)],
        compiler_params=_params("parallel"),
        name="sample_sconv_pool",
    )(state_s, state_p, y, bg, pd, w)


def kernel(x_prompt, x_sample, cache_k, cache_v, cache_logf, page_table, state_conv_a, state_sconv, state_pool, meta_tokens, w_in_even, b_forget, w_dw_a, b_dw_a, ln_a_g, ln_a_b, w_out_even, w_in_odd, w_sconv, w_pool_mix, pool_scale, w_out_odd, ln_mix_g, ln_mix_b, w_ffn_gate, w_ffn_up, w_ffn_down, ln_ffn_g, ln_ffn_b):
    nb, seq, d = x_prompt.shape
    db, ds, _ = x_sample.shape
    depth = w_ffn_gate.shape[0]
    alpha = float((2 * depth) ** 0.25)
    ca = state_conv_a.shape[-1]
    att = N_HEADS * HEAD_DIM
    assert ds == SUBLANES and cache_k.shape[3] == N_HEADS and cache_k.shape[4] == HEAD_DIM
    assert state_conv_a.shape[2] == CONV_W - 1 and state_pool.shape[2] == POOL_HIST
    assert meta_tokens.shape[0] == N_META

    s_real = N_META + seq
    fp = (-s_real) % SEQ_TILE
    if fp < CONV_BACK:
        fp += SEQ_TILE
    sp = fp + s_real
    s0 = nb * sp
    n = s0 + db * ds
    tm = next(t for t in (512, 256, 128, 64, 32, 16, 8) if s0 % t == 0 and (db * ds) % t == 0)
    tm_in = next(t for t in (1024, 512, 256, 128, 64, 32, 16, 8) if n % t == 0)
    dbt = next(t for t in (8, 4, 2, 1) if db % t == 0)
    direct_out = seq % SEQ_TILE == 0 and (db * ds) % SEQ_TILE == 0

    head = jnp.concatenate([jnp.zeros((fp, d), F32), meta_tokens.astype(F32)], axis=0)
    x = jnp.concatenate([piece for bi in range(nb) for piece in (head, x_prompt[bi])]
                        + [x_sample.reshape(db * ds, d)], axis=0)

    n_phys = cache_k.shape[0]
    n_even = cache_k.shape[1]
    page = cache_k.shape[2]
    ck = jnp.transpose(cache_k, (0, 1, 3, 4, 2)).reshape(n_phys, n_even, att, page)
    cv = jnp.transpose(cache_v, (0, 1, 3, 4, 2)).reshape(n_phys, n_even, att, page)
    clft = jnp.swapaxes(cache_logf, 2, 3)

    row2 = lambda v: v.reshape(1, -1).astype(F32)

    def prompt_rows(arr, lo, hi):
        return jnp.stack([arr[bi * sp + lo:bi * sp + hi] for bi in range(nb)], axis=0)

    def sample_rows(arr):
        return arr[s0:].reshape(db, ds, -1)

    wg_all, wu_all, wd_all = (w.astype(BF16) for w in (w_ffn_gate, w_ffn_up, w_ffn_down))
    wo_even, wo_odd = w_out_even.astype(BF16), w_out_odd.astype(BF16)
    wi_even = jnp.pad(w_in_even.astype(BF16), ((0, 0), (0, 0), (0, LANES - N_HEADS)))
    wi_odd = w_in_odd.astype(BF16)

    def tail(name, xin, m1p, m2p, m1s, m2s, p1, p2, w_out_all, layer):
        last = direct_out and layer == depth - 1
        return _layer_tail(name, xin, m1p, m2p, m1s, m2s, p1, p2, w_out_all,
                           row2(ln_mix_g[layer]), row2(ln_mix_b[layer]), wg_all, wu_all, wd_all,
                           row2(ln_ffn_g[layer]), row2(ln_ffn_b[layer]), even=layer % 2 == 0,
                           mix_idx=layer // 2, layer=layer, tm=SEQ_TILE if last else tm, alpha=alpha,
                           nb=nb, sp=sp, fp=fp, out_rows=(seq, db * ds) if last else None)

    prev_kv = []
    lfp, cap, scp, plp = [], [], [], []
    ks_, vs_, lfs, cas, scs, pls = [], [], [], [], [], []
    for layer in range(depth):
        i = layer // 2
        if layer % 2 == 0:
            bf = jnp.pad(b_forget[i].astype(F32), (0, LANES - N_HEADS)).reshape(1, LANES)
            a, q, k, v, lf = _even_in(x, wi_even, i, bf, tm=tm_in, ca=ca, att=att)
            w_dw, b_dw = w_dw_a[i].astype(F32), row2(b_dw_a[i])
            conv_s, st_a = _sample_conv(state_conv_a.astype(F32), i, a, w_dw, b_dw, s0=s0, db=db, ds=ds, dbt=dbt)
            qa, ka, vat, conv_p, *caches = _fox_prep(q, k, v, lf, a, w_dw, b_dw, prev_kv, i == n_even - 1,
                                                     nb=nb, sp=sp, fp=fp)
            prev_kv.append((k, v))
            at_p = _flash(qa, ka, vat, hg=FLASH_HEADS)
            at_s = _sample_attn(page_table, i, q, k, v, lf, ck, cv, clft, s0=s0, db=db, ds=ds)
            x = tail("even_tail", x, conv_p, at_p, conv_s, at_s, row2(ln_a_g[i]), row2(ln_a_b[i]), wo_even, layer)
            lfp.append(prompt_rows(lf, fp, sp)[..., :N_HEADS])
            cap.append(prompt_rows(a, sp - (CONV_W - 1), sp))
            ks_.append(sample_rows(k).reshape(db, ds, N_HEADS, HEAD_DIM))
            vs_.append(sample_rows(v).reshape(db, ds, N_HEADS, HEAD_DIM))
            lfs.append(sample_rows(lf)[..., :N_HEADS])
            cas.append(st_a)
        else:
            c = state_sconv.shape[-1]
            y, bg, pd = _odd_in(x, wi_odd, i, tm=tm_in, c=c)
            oc_p, dd_p = _prompt_odd(y, bg, pd, w_sconv[i].astype(F32), nb=nb, sp=sp, fp=fp)
            oc_s, dd_s, st_s, st_p = _sample_odd(state_sconv.astype(F32), state_pool.astype(F32), i, y, bg, pd,
                                                 w_sconv[i].astype(F32), s0=s0, db=db, ds=ds, dbt=dbt)
            wm = jax.scipy.linalg.block_diag(*[w_pool_mix[i, g] for g in range(w_pool_mix.shape[1])]).astype(BF16)
            x = tail("odd_tail", x, oc_p, dd_p, oc_s, dd_s, wm, row2(pool_scale[i]), wo_odd, layer)
            scp.append(prompt_rows(y, sp - (SCONV_W - 1), sp))
            plp.append(prompt_rows(pd, sp - POOL_HIST, sp))
            scs.append(st_s)
            pls.append(st_p)

    if direct_out:
        y_prompt, y_sample = x[0], x[1].reshape(db, ds, d)
    else:
        y_prompt, y_sample = prompt_rows(x, fp + N_META, sp), sample_rows(x)
    st = lambda xs: jnp.stack(xs, axis=1)
    kp, vp = (jnp.transpose(c.reshape(nb, n_even, N_HEADS, HEAD_DIM, s_real), (0, 1, 4, 2, 3)) for c in caches)
    return (y_prompt, y_sample, kp, vp, st(lfp), st(cap), st(scp), st(plp),
            st(ks_), st(vs_), st(lfs), st(cas), st(scs), st(pls))
```

```python
import functools

import jax
import jax.numpy as jnp
from jax import lax
from jax.experimental import pallas as pl
from jax.experimental.pallas import tpu as pltpu

N_META = 16
N_HEADS = 8
HEAD_DIM = 64
CONV_W = 31
SCONV_W = 3
POOL_WINDOWS = (2, 4, 8, 16)
POOL_HIST = max(POOL_WINDOWS) - 1
LN_EPS = 1e-5
SEQ_TILE = 256
LANES = 128
SUBLANES = 8
CONV_BACK = -(-(CONV_W - 1) // SUBLANES) * SUBLANES
C_TERMS = 3
AUX_ONES = C_TERMS * N_HEADS
FLASH_HEADS = 8
V_ROWS = HEAD_DIM + 2 * SUBLANES
VMEM_LIMIT = 56 * 1024 * 1024
NEG = -1e30
LOG2E = 1.4426950408889634
F32 = jnp.float32
BF16 = jnp.bfloat16
NT_DIMS = (((1,), (1,)), ((), ()))


def _params(*sem):
    return pltpu.CompilerParams(dimension_semantics=sem, vmem_limit_bytes=VMEM_LIMIT)


def _dot(a, b):
    return jnp.dot(a, b, preferred_element_type=F32)


def _ln(z, g, b):
    mu = jnp.mean(z, axis=-1, keepdims=True)
    zc = z - mu
    var = jnp.mean(zc * zc, axis=-1, keepdims=True)
    return zc * lax.rsqrt(var + LN_EPS) * g + b


def _silu(x):
    return x * jax.nn.sigmoid(x)


def _full(shape):
    return pl.BlockSpec(shape, lambda *_: (0,) * len(shape))


def _slab(a, idx):
    return pl.BlockSpec((None,) + a.shape[1:], lambda *_: (idx,) + (0,) * (a.ndim - 1),
                        pipeline_mode=pl.Buffered(1))


def _even_in_kernel(x_ref, w_ref, bf_ref, a_ref, q_ref, k_ref, v_ref, lf_ref, *, ca, att):
    xb = x_ref[...].astype(BF16)

    def mm(lo, hi):
        return _dot(xb, w_ref[:, lo:hi])

    u = mm(0, ca)
    g = mm(ca, 2 * ca)
    a_ref[...] = u * jax.nn.sigmoid(g)
    o = 2 * ca
    q_ref[...] = mm(o, o + att) * (HEAD_DIM ** -0.5)
    k_ref[...] = mm(o + att, o + 2 * att)
    v_ref[...] = mm(o + 2 * att, o + 3 * att)
    z = mm(o + 3 * att, o + 3 * att + LANES) + bf_ref[...]
    lf = jnp.minimum(z, 0.0) - jnp.log1p(jnp.exp(-jnp.abs(z)))
    lane = lax.broadcasted_iota(jnp.int32, lf.shape, 1)
    lf_ref[...] = jnp.where(lane < N_HEADS, lf, 0.0)


def _even_in(x, w, idx, bf, *, tm, ca, att):
    n, d = x.shape
    row = lambda c: pl.BlockSpec((tm, c), lambda i: (i, 0))
    return pl.pallas_call(
        functools.partial(_even_in_kernel, ca=ca, att=att),
        grid=(n // tm,),
        in_specs=[row(d), _slab(w, idx), _full(bf.shape)],
        out_specs=[row(ca), row(att), row(att), row(att), row(LANES)],
        out_shape=[jax.ShapeDtypeStruct((n, c), F32) for c in (ca, att, att, att, LANES)],
        compiler_params=_params("parallel"),
        name="even_in_proj",
    )(x, w, bf)


def _odd_in_kernel(x_ref, w_ref, y_ref, bg_ref, pd_ref, *, c):
    xb = x_ref[...].astype(BF16)
    hc = _dot(xb, w_ref[:, 0:c])
    bg_ref[...] = _dot(xb, w_ref[:, c:2 * c])
    cg = _dot(xb, w_ref[:, 2 * c:3 * c])
    y_ref[...] = cg * hc
    pd_ref[...] = _dot(xb, w_ref[:, 3 * c:4 * c])


def _odd_in(x, w, idx, *, tm, c):
    n, d = x.shape
    row = lambda cc: pl.BlockSpec((tm, cc), lambda i: (i, 0))
    return pl.pallas_call(
        functools.partial(_odd_in_kernel, c=c),
        grid=(n // tm,),
        in_specs=[row(d), _slab(w, idx)],
        out_specs=[row(c)] * 3,
        out_shape=[jax.ShapeDtypeStruct((n, c), F32)] * 3,
        compiler_params=_params("parallel"),
        name="odd_in_proj",
    )(x, w)


def _tail_kernel(x_ref, m1p_ref, m2p_ref, m1s_ref, m2s_ref, p1_ref, p2_ref, w_ref, g1_ref, b1_ref,
                 wg_ref, wu_ref, wd_ref, g2_ref, b2_ref, *rest, even, alpha, chunk, nb, sp, fp, npt, lead):
    i = pl.program_id(0)
    is_sample = i >= npt
    m1 = jnp.where(is_sample, m1s_ref[...], m1p_ref[...])
    m2 = jnp.where(is_sample, m2s_ref[...], m2p_ref[...])
    if even:
        m1 = _silu(_ln(m1, p1_ref[...], p2_ref[...]))
    else:
        m2 = _dot(m2.astype(BF16), p1_ref[...]) * p2_ref[...]
    ca = m1.shape[1]
    mix = _dot(m1.astype(BF16), w_ref[0:ca, :]) + _dot(m2.astype(BF16), w_ref[ca:, :])
    x1 = _ln(alpha * x_ref[...] + mix, g1_ref[...], b1_ref[...])

    h_ref = rest[-1]
    xb = x1.astype(BF16)
    for c in range(0, wg_ref.shape[1], chunk):
        gate = _dot(xb, wg_ref[:, c:c + chunk])
        up = _dot(xb, wu_ref[:, c:c + chunk])
        h_ref[:, c:c + chunk] = (_silu(gate) * up).astype(BF16)
    out = _ln(alpha * x1 + _dot(h_ref[...], wd_ref[...]), g2_ref[...], b2_ref[...])

    tm = x1.shape[0]
    if lead is None:
        r = i * tm + lax.broadcasted_iota(jnp.int32, (tm, 1), 0)
        keep = jnp.ones((tm, 1), F32)
        for bi in range(nb):
            keep = jnp.where((r >= bi * sp) & (r < bi * sp + fp), 0.0, keep)
        rest[0][...] = out * keep
    else:
        yp_ref, ys_ref = rest[:2]

        @pl.when(jnp.logical_and(jnp.logical_not(is_sample), i % (sp // tm) >= lead))
        def _():
            yp_ref[...] = out

        @pl.when(is_sample)
        def _():
            ys_ref[...] = out


def _layer_tail(name, x, m1p, m2p, m1s, m2s, p1, p2, w, g1, b1, wg, wu, wd, g2, b2, *, even, mix_idx, layer, tm,
                alpha, nb, sp, fp, out_rows=None):
    n, d = x.shape
    s0 = m1p.shape[0]
    dff = wg.shape[2]
    chunk = SEQ_TILE if dff % SEQ_TILE == 0 else dff
    npt = s0 // tm
    row = lambda c: pl.BlockSpec((tm, c), lambda i: (i, 0))
    prow = lambda c: pl.BlockSpec((tm, c), lambda i: (jnp.minimum(i, npt - 1), 0))
    srow = lambda c: pl.BlockSpec((tm, c), lambda i: (jnp.maximum(i - npt, 0), 0))
    once = lambda a: pl.BlockSpec(a.shape, lambda i: (0,) * a.ndim, pipeline_mode=pl.Buffered(1))
    consts = (p1, p2, w, g1, b1, wg, wu, wd, g2, b2)
    const_specs = [once(p1), once(p2), _slab(w, mix_idx), once(g1), once(b1),
                   _slab(wg, layer), _slab(wu, layer), _slab(wd, layer), once(g2), once(b2)]
    if out_rows is None:
        lead = None
        out_specs = row(d)
        out_shape = jax.ShapeDtypeStruct((n, d), F32)
    else:
        seq, db_rows = out_rows
        tps = sp // tm
        lead = (sp - seq) // tm
        out_specs = [
            pl.BlockSpec((None, tm, d), lambda i: (jnp.minimum(i // tps, nb - 1),
                                                   jnp.where(i < npt, jnp.maximum(i % tps - lead, 0), tps - lead - 1),
                                                   0)),
            srow(d)]
        out_shape = [jax.ShapeDtypeStruct((nb, seq, d), F32), jax.ShapeDtypeStruct((db_rows, d), F32)]
    return pl.pallas_call(
        functools.partial(_tail_kernel, even=even, alpha=alpha, chunk=chunk, nb=nb, sp=sp, fp=fp, npt=npt, lead=lead),
        grid=(n // tm,),
        in_specs=[row(d), prow(m1p.shape[1]), prow(m2p.shape[1]), srow(m1s.shape[1]), srow(m2s.shape[1])]
        + const_specs,
        out_specs=out_specs,
        out_shape=out_shape,
        scratch_shapes=[pltpu.VMEM((tm, dff), BF16)],
        compiler_params=_params("arbitrary" if out_rows else "parallel"),
        name=name,
    )(x, m1p, m2p, m1s, m2s, *consts)


def _conv_tile(win, w_rows):
    rows = win.shape[0]
    t = rows - CONV_BACK
    lead = CONV_BACK - (CONV_W - 1)
    acc = None
    for r in range(SUBLANES):
        rolled = win if r == 0 else pltpu.roll(win, rows - r, axis=0)
        for m in range(CONV_BACK // SUBLANES + 1):
            j = SUBLANES * m + r - lead
            if 0 <= j < CONV_W:
                term = rolled[SUBLANES * m:SUBLANES * m + t] * w_rows[j]
                acc = term if acc is None else acc + term
    return acc


def _sample_conv_kernel(st_ref, a_ref, w_ref, b_ref, o_ref, st_out_ref, ext_ref, *, ds):
    dbt, hist, c = st_ref.shape
    off = CONV_BACK - hist
    ext_ref[:, off:CONV_BACK, :] = st_ref[...]
    ext_ref[:, CONV_BACK:CONV_BACK + ds, :] = a_ref[...].reshape(dbt, ds, c)
    acc = None
    for j in range(CONV_W):
        term = ext_ref[:, off + j:off + j + ds, :] * w_ref[j:j + 1, :]
        acc = term if acc is None else acc + term
    o_ref[...] = (acc + b_ref[...]).reshape(dbt * ds, c)
    st_out_ref[...] = ext_ref[:, off + ds:CONV_BACK + ds, :]


def _sample_conv(state, layer, a, w, b, *, s0, db, ds, dbt):
    n, c = a.shape
    hist = state.shape[2]
    rows = dbt * ds
    return pl.pallas_call(
        functools.partial(_sample_conv_kernel, ds=ds),
        grid=(db // dbt,),
        in_specs=[pl.BlockSpec((dbt, None, hist, c), lambda i: (i, layer, 0, 0)),
                  pl.BlockSpec((rows, c), lambda i: (s0 // rows + i, 0)), _full(w.shape), _full(b.shape)],
        out_specs=[pl.BlockSpec((rows, c), lambda i: (i, 0)), pl.BlockSpec((dbt, hist, c), lambda i: (i, 0, 0))],
        out_shape=[jax.ShapeDtypeStruct((db * ds, c), F32), jax.ShapeDtypeStruct((db, hist, c), F32)],
        scratch_shapes=[pltpu.VMEM((dbt, CONV_BACK + ds, c), F32)],
        compiler_params=_params("parallel"),
        name="sample_conv",
    )(state, a, w, b)


def _fox_prep_kernel(q_ref, k_ref, v_ref, lf_ref, a_ref, wdw_ref, bdw_ref, *rest, fp, ns, n_prev, emit):
    prev, rest = rest[:2 * n_prev], rest[2 * n_prev:]
    qa_ref, ka_ref, vat_ref, cv_ref = rest[:4]
    carry_ref, ah_ref = rest[6:8] if emit else rest[4:6]
    s = pl.program_id(1)
    t = SEQ_TILE

    @pl.when(s < ns)
    def _():
        _fox_operands(s, q_ref, k_ref, v_ref, lf_ref, qa_ref, ka_ref, vat_ref, carry_ref, fp)

        @pl.when(s == 0)
        def _():
            ah_ref[...] = jnp.zeros_like(ah_ref)

        for cb in range(a_ref.shape[1] // LANES):
            cs = slice(cb * LANES, (cb + 1) * LANES)
            win = jnp.concatenate([ah_ref[:, cs], a_ref[:, cs]], axis=0)
            w_rows = [wdw_ref[j:j + 1, cs] for j in range(CONV_W)]
            cv_ref[:, cs] = _conv_tile(win, w_rows) + bdw_ref[:, cs]
        ah_ref[...] = a_ref[t - CONV_BACK:t, :]

    if not emit:
        return
    kt_ref, vt_ref = rest[4:6]
    kc_ref, vc_ref = rest[8:10]

    @pl.when(s == 0)
    def _():
        kc_ref[...] = jnp.zeros_like(kc_ref)
        vc_ref[...] = jnp.zeros_like(vc_ref)

    k_srcs = list(prev[0::2]) + [k_ref]
    v_srcs = list(prev[1::2]) + [v_ref]
    for srcs, held, dst in ((k_srcs, kc_ref, kt_ref), (v_srcs, vc_ref, vt_ref)):
        for li, src in enumerate(srcs):
            rows = jnp.concatenate([held[li], src[0:fp, :]], axis=0)
            for c in range(src.shape[1] // LANES):
                dst[li, c * LANES:(c + 1) * LANES, :] = rows[:, c * LANES:(c + 1) * LANES].T
            held[li] = src[fp:t, :]


def _fox_operands(s, q_ref, k_ref, v_ref, lf_ref, qa_ref, ka_ref, vat_ref, carry_ref, fp):
    @pl.when(s == 0)
    def _():
        carry_ref[...] = jnp.zeros_like(carry_ref)

    t = SEQ_TILE
    row = lax.broadcasted_iota(jnp.int32, (t, LANES), 0)
    lane = lax.broadcasted_iota(jnp.int32, (t, LANES), 1)
    c = jnp.where(s * t + row >= fp, lf_ref[...], 0.0)
    sh = 1
    while sh < t:
        c = c + jnp.where(row >= sh, pltpu.roll(c, sh, axis=0), 0.0)
        sh *= 2
    c = c + carry_ref[0:1, :]
    carry_ref[0:1, :] = c[t - 1:t, :]

    c = c * LOG2E
    c1 = c.astype(BF16).astype(F32)
    r1 = c - c1
    c2 = r1.astype(BF16).astype(F32)
    c3 = (r1 - c2).astype(BF16).astype(F32)
    g23 = jnp.where(lane < 2 * N_HEADS, pltpu.roll(c2, N_HEADS, axis=1),
                    jnp.where(lane < 3 * N_HEADS, pltpu.roll(c3, 2 * N_HEADS, axis=1), 0.0))
    g = jnp.where(lane < N_HEADS, c1, g23)
    gk = jnp.where(lane < N_HEADS, jnp.where(s * t + row >= fp, c1, -NEG), g23)
    ones_grp = (lane >= HEAD_DIM + AUX_ONES) & (lane < HEAD_DIM + 2 * AUX_ONES)
    ck = [jnp.where((lane >= HEAD_DIM) & (lane < HEAD_DIM + AUX_ONES), -pltpu.roll(gk, HEAD_DIM, axis=1),
                    jnp.where(ones_grp, 1.0, 0.0))]
    cq = [jnp.where(ones_grp, pltpu.roll(g, HEAD_DIM + AUX_ONES, axis=1), 0.0)]
    ck.append(pltpu.roll(ck[0], HEAD_DIM, axis=1))
    cq.append(pltpu.roll(cq[0], HEAD_DIM, axis=1))

    lane1 = lax.broadcasted_iota(jnp.int32, (1, LANES), 1)
    for h in range(N_HEADS):
        odd = h % 2
        pr = h // 2
        base = 0 if odd else HEAD_DIM
        a = lane1 - base
        mine = (lane1 & (N_HEADS - 1)) == h
        data = ((lane1 >= HEAD_DIM) if odd else (lane1 < HEAD_DIM)).astype(F32)
        sel1 = ((a >= 0) & (a < AUX_ONES) & mine).astype(F32)
        selc = ((a >= AUX_ONES) & (a < 2 * AUX_ONES) & mine).astype(F32)
        sl = slice(pr * LANES, (pr + 1) * LANES)
        qa_ref[h] = (q_ref[:, sl] * (data * LOG2E) + (cq[odd] * selc + sel1)).astype(BF16)
        ka_ref[h] = (k_ref[:, sl] * data + ck[odd]).astype(BF16)
    ones = jnp.ones((V_ROWS - HEAD_DIM, t), BF16)
    for pr in range(N_HEADS // 2):
        v_t = v_ref[:, pr * LANES:(pr + 1) * LANES].T.astype(BF16)
        vat_ref[2 * pr] = jnp.concatenate([v_t[0:HEAD_DIM], ones], axis=0)
        vat_ref[2 * pr + 1] = jnp.concatenate([v_t[HEAD_DIM:], ones], axis=0)


def _fox_prep(q, k, v, lf, a, w_dw, b_dw, prev_kv, emit, *, nb, sp, fp):
    ns = sp // SEQ_TILE
    att = q.shape[1]
    ca = a.shape[1]
    t = SEQ_TILE
    last = ns - 1
    row = lambda c: pl.BlockSpec((t, c), lambda bi, si: (bi * ns + jnp.minimum(si, last), 0))
    hm = pl.BlockSpec((None, N_HEADS, t, LANES), lambda bi, si: (bi, 0, jnp.minimum(si, last), 0))
    hm_t = pl.BlockSpec((None, N_HEADS, V_ROWS, t), lambda bi, si: (bi, 0, 0, jnp.minimum(si, last)))
    out_specs = [hm, hm, hm_t, row(ca)]
    out_shape = [jax.ShapeDtypeStruct((nb, N_HEADS, sp, LANES), BF16)] * 2 + [
        jax.ShapeDtypeStruct((nb, N_HEADS, V_ROWS, sp), BF16), jax.ShapeDtypeStruct((nb * sp, ca), F32)]
    scratch = [pltpu.VMEM((SUBLANES, LANES), F32), pltpu.VMEM((CONV_BACK, ca), F32)]
    n_prev = len(prev_kv) if emit else 0
    if emit:
        assert 0 < fp < t and fp % SUBLANES == 0
        n_layers = n_prev + 1
        cache = pl.BlockSpec((None, n_layers, att, t), lambda bi, si: (bi, 0, 0, jnp.maximum(si - 1, 0)))
        out_specs += [cache, cache]
        out_shape += [jax.ShapeDtypeStruct((nb, n_layers, att, sp - fp), F32)] * 2
        scratch += [pltpu.VMEM((n_layers, t - fp, att), F32)] * 2
    prev = [a for kv in prev_kv for a in kv] if emit else []
    return pl.pallas_call(
        functools.partial(_fox_prep_kernel, fp=fp, ns=ns, n_prev=n_prev, emit=emit),
        grid=(nb, ns + 1 if emit else ns),
        in_specs=[row(att), row(att), row(att), row(LANES), row(ca), _full(w_dw.shape), _full(b_dw.shape)]
        + [row(att)] * len(prev),
        out_specs=out_specs,
        out_shape=out_shape,
        scratch_shapes=scratch,
        compiler_params=_params("parallel", "arbitrary"),
        name="fox_prep",
    )(q, k, v, lf, a, w_dw, b_dw, *prev)


def _flash_kernel(q_ref, k_ref, vt_ref, o_ref, s_ref, p_ref, acc_ref, *, hg):
    i = pl.program_id(2)
    t = SEQ_TILE

    def scores_to(j, slot):
        ks = pl.multiple_of(j * t, t)
        for h in range(hg):
            s_ref[slot, h] = lax.dot_general(k_ref[h, pl.ds(ks, t), :], q_ref[h], NT_DIMS,
                                             preferred_element_type=F32)

    def accumulate(j, als):
        ks = pl.multiple_of(j * t, t)
        pvs = [_dot(vt_ref[h, :, pl.ds(ks, t)], p_ref[h]) for h in range(hg)]
        for h in range(hg):
            acc_ref[h] = als[h] * acc_ref[h] + pvs[h]

    def softmax(slot, ms, masked):
        new_m, als = [], []
        for h in range(hg):
            s = s_ref[slot, h]
            if masked:
                visible = (lax.broadcasted_iota(jnp.int32, (t, t), 0) <= lax.broadcasted_iota(jnp.int32, (t, t), 1))
                s = jnp.where(visible, s, NEG)
            mn = jnp.maximum(ms[h], jnp.max(s, axis=0, keepdims=True))
            als.append(jnp.exp2(ms[h] - mn))
            new_m.append(mn)
            p_ref[h] = jnp.exp2(s - mn).astype(BF16)
        return tuple(new_m), tuple(als)

    def stage(j, slot, state):
        ms, als = state
        scores_to(j + 1, 1 - slot)
        accumulate(jnp.maximum(j - 1, 0), als)
        return softmax(slot, ms, False)

    def pair(jj, state):
        return stage(2 * jj + 1, 1, stage(2 * jj, 0, state))

    def finish(slot, state):
        ms, als = state
        accumulate(jnp.maximum(i - 1, 0), als)
        _, als = softmax(slot, ms, True)
        accumulate(i, als)
        for pr in range(hg // 2):
            halves = [acc_ref[h, 0:HEAD_DIM, :] * (1.0 / acc_ref[h, HEAD_DIM:HEAD_DIM + 1, :])
                      for h in (2 * pr, 2 * pr + 1)]
            o_ref[:, pr * LANES:(pr + 1) * LANES] = jnp.concatenate(halves, axis=0).T

    p_ref[...] = jnp.zeros_like(p_ref)
    acc_ref[...] = jnp.zeros_like(acc_ref)
    scores_to(0, 0)
    state = ((jnp.full((1, t), NEG, F32),) * hg, (jnp.ones((1, t), F32),) * hg)
    state = lax.fori_loop(0, i // 2, pair, state)
    odd = i % 2 == 1
    state = lax.cond(odd, lambda st: stage(i - 1, 0, st), lambda st: st, state)
    pl.when(odd)(lambda: finish(1, state))
    pl.when(jnp.logical_not(odd))(lambda: finish(0, state))


def _flash(qa, ka, vat, *, hg):
    nb, _, sp, _ = qa.shape
    ns = sp // SEQ_TILE
    return pl.pallas_call(
        functools.partial(_flash_kernel, hg=hg),
        grid=(nb, N_HEADS // hg, ns),
        in_specs=[pl.BlockSpec((None, hg, SEQ_TILE, LANES), lambda bi, gi, qi: (bi, gi, qi, 0)),
                  pl.BlockSpec((None, hg, sp, LANES), lambda bi, gi, qi: (bi, gi, 0, 0)),
                  pl.BlockSpec((None, hg, V_ROWS, sp), lambda bi, gi, qi: (bi, gi, 0, 0))],
        out_specs=pl.BlockSpec((SEQ_TILE, hg * HEAD_DIM), lambda bi, gi, qi: (bi * ns + qi, gi)),
        out_shape=jax.ShapeDtypeStruct((nb * sp, N_HEADS * HEAD_DIM), F32),
        scratch_shapes=[pltpu.VMEM((2, hg, SEQ_TILE, SEQ_TILE), F32), pltpu.VMEM((hg, SEQ_TILE, SEQ_TILE), BF16),
                        pltpu.VMEM((hg, V_ROWS, SEQ_TILE), F32)],
        compiler_params=_params("parallel", "parallel", "arbitrary"),
        name="fox_flash",
    )(qa, ka, vat)


def _sample_attn_kernel(pt_ref, q_ref, kn_ref, vn_ref, lfn_ref, *rest, n_pages, page, eb, ds):
    del pt_ref
    o_ref = rest[3 * eb * n_pages]
    c = q_ref.shape[1]
    rows = ds * N_HEADS
    groups = [list(range(g, min(g + 2, n_pages))) for g in range(0, n_pages, 2)]
    past = n_pages * page
    sub = lax.broadcasted_iota(jnp.int32, (N_HEADS, c), 0)
    lane = lax.broadcasted_iota(jnp.int32, (N_HEADS, c), 1)
    hm = ((lane >= sub * HEAD_DIM) & (lane < (sub + 1) * HEAD_DIM)).astype(F32)
    hm_t = jnp.concatenate([hm] * ds, axis=0)
    pad = jnp.zeros((LANES - ds, c), F32)

    def refs_of(kind, e):
        return rest[(kind * eb + e) * n_pages:(kind * eb + e + 1) * n_pages]

    def scores(e):
        sl = slice(e * ds, (e + 1) * ds)
        q = q_ref[sl, :]
        qbd = jnp.concatenate([jnp.broadcast_to(q[t:t + 1, :], (N_HEADS, c)) * hm for t in range(ds)],
                              axis=0).astype(BF16)
        k_refs, lf_refs = refs_of(0, e), refs_of(2, e)
        s_parts = []
        for grp in groups:
            kp = jnp.concatenate([k_refs[r][...] for r in grp], axis=1).astype(BF16)
            s_parts.append(_dot(qbd, kp))
        s_past = jnp.concatenate(s_parts, axis=1)

        lf = jnp.concatenate([r[...] for r in lf_refs], axis=1)
        lane_p = lax.broadcasted_iota(jnp.int32, (N_HEADS, past), 1)
        suf = lf
        sh = 1
        while sh < past:
            suf = suf + jnp.where(lane_p < past - sh, pltpu.roll(suf, past - sh, axis=1), 0.0)
            sh *= 2
        s_past = s_past + jnp.concatenate([suf - lf] * ds, axis=0)

        kn = jnp.concatenate([kn_ref[sl, :], pad], axis=0).astype(BF16)
        s_new = lax.dot_general(qbd, kn, NT_DIMS, preferred_element_type=F32)
        cn = lfn_ref[sl, :]
        sub_n = lax.broadcasted_iota(jnp.int32, cn.shape, 0)
        sh = 1
        while sh < ds:
            cn = cn + jnp.where(sub_n >= sh, pltpu.roll(cn, sh, axis=0), 0.0)
            sh *= 2
        cn_t = jnp.concatenate([cn, jnp.zeros((LANES - ds, LANES), F32)], axis=0).T[0:N_HEADS, :]
        rown = lax.broadcasted_iota(jnp.int32, (rows, LANES), 0)
        u = lax.broadcasted_iota(jnp.int32, (rows, LANES), 1)
        s_new = jnp.where(u * N_HEADS <= rown, s_new - jnp.concatenate([cn_t] * ds, axis=0), NEG)
        return s_past, s_new

    def softmax(s_past, s_new):
        m = jnp.maximum(jnp.max(s_past, axis=1, keepdims=True), jnp.max(s_new, axis=1, keepdims=True))
        p_past = jnp.exp(s_past - m)
        p_new = jnp.exp(s_new - m)
        l = jnp.sum(p_past, axis=1, keepdims=True) + jnp.sum(p_new, axis=1, keepdims=True)
        return p_past.astype(BF16), p_new.astype(BF16), l

    def output(e, p_past, p_new, l):
        sl = slice(e * ds, (e + 1) * ds)
        v_refs = refs_of(1, e)
        vn = jnp.concatenate([vn_ref[sl, :], pad], axis=0).astype(BF16)
        o = _dot(p_new, vn)
        for grp in groups:
            vp = jnp.concatenate([v_refs[r][...] for r in grp], axis=1).astype(BF16)
            lo = grp[0] * page
            o = o + lax.dot_general(p_past[:, lo:lo + len(grp) * page], vp, NT_DIMS, preferred_element_type=F32)
        o = o * (1.0 / l) * hm_t
        rr = lax.broadcasted_iota(jnp.int32, (ds, rows), 1)
        tt = lax.broadcasted_iota(jnp.int32, (ds, rows), 0)
        pick = ((rr >= tt * N_HEADS) & (rr < (tt + 1) * N_HEADS)).astype(BF16)
        o_ref[sl, :] = _dot(pick, o.astype(BF16))

    ss = [scores(e) for e in range(eb)]
    ps = [softmax(*s) for s in ss]
    for e in range(eb):
        output(e, *ps[e])


def _sample_attn(page_table, layer, q, k, v, lf, cache_k, cache_v, cache_lft, *, s0, db, ds):
    n, c = q.shape
    n_pages = page_table.shape[1]
    page = cache_k.shape[3]
    eb = 2 if db % 2 == 0 else 1
    rows = eb * ds
    row = lambda cc: pl.BlockSpec((rows, cc), lambda i, pt: (s0 // rows + i, 0))

    def paged(shape, e, r):
        return pl.BlockSpec((None, None) + shape, lambda i, pt: (pt[(i * eb + e) * n_pages + r], layer, 0, 0))

    pages = [(e, r) for e in range(eb) for r in range(n_pages)]
    in_specs = ([row(c), row(c), row(c), row(LANES)]
                + [paged((c, page), e, r) for e, r in pages]
                + [paged((c, page), e, r) for e, r in pages]
                + [paged((N_HEADS, page), e, r) for e, r in pages])
    return pl.pallas_call(
        functools.partial(_sample_attn_kernel, n_pages=n_pages, page=page, eb=eb, ds=ds),
        grid_spec=pltpu.PrefetchScalarGridSpec(
            num_scalar_prefetch=1, grid=(db // eb,), in_specs=in_specs,
            out_specs=pl.BlockSpec((rows, c), lambda i, pt: (i, 0))),
        out_shape=jax.ShapeDtypeStruct((db * ds, c), F32),
        compiler_params=_params("parallel"),
        name="sample_paged_attn",
    )(page_table.reshape(-1), q, k, v, lf, *([cache_k] * len(pages)), *([cache_v] * len(pages)),
      *([cache_lft] * len(pages)))


def _odd_tile(wy, wp, bg, w_rows, window, pos0):
    t = bg.shape[0]
    y0 = wy[SUBLANES:]
    y1 = pltpu.roll(wy, 1, axis=0)[SUBLANES:]
    y2 = pltpu.roll(wy, 2, axis=0)[SUBLANES:]
    out_c = bg * (w_rows[0] * y2 + w_rows[1] * y1 + w_rows[2] * y0)
    back = 2 * SUBLANES
    win = wp
    step = 1
    while step < window:
        win = win + pltpu.roll(win, step, axis=0)
        step *= 2
    if pos0 is None:
        cnt = float(window)
    else:
        pos = pos0 + lax.broadcasted_iota(jnp.int32, (t, LANES), 0)
        cnt = jnp.maximum(jnp.minimum(window, pos + 1), 1).astype(F32)
    return out_c, win[back:] / cnt - wp[back:]


def _prompt_odd_kernel(y_ref, bg_ref, pd_ref, w_ref, oc_ref, d_ref, *, n_tiles, fp):
    w_rows = [w_ref[j:j + 1, :] for j in range(SCONV_W)]
    c = y_ref.shape[1]
    t = SEQ_TILE
    n_partial = -(-(fp + POOL_HIST) // t)

    def run(window):
        wy0 = jnp.concatenate([jnp.zeros((SUBLANES, c), F32), y_ref[0:t, :]], axis=0)
        wp0 = jnp.concatenate([jnp.zeros((2 * SUBLANES, c), F32), pd_ref[0:t, :]], axis=0)
        oc, d = _odd_tile(wy0, wp0, bg_ref[0:t, :], w_rows, window, -fp)
        oc_ref[0:t, :] = oc
        d_ref[0:t, :] = d

        def body(i, partial):
            s0 = pl.multiple_of(i * t, t)
            wy = y_ref[pl.ds(s0 - SUBLANES, t + SUBLANES), :]
            wp = pd_ref[pl.ds(s0 - 2 * SUBLANES, t + 2 * SUBLANES), :]
            oc, d = _odd_tile(wy, wp, bg_ref[pl.ds(s0, t), :], w_rows, window, s0 - fp if partial else None)
            oc_ref[pl.ds(s0, t), :] = oc
            d_ref[pl.ds(s0, t), :] = d

        lax.fori_loop(1, n_partial, lambda i, carry: body(i, True) or carry, 0)
        lax.fori_loop(n_partial, n_tiles, lambda i, carry: body(i, False) or carry, 0)

    for gi, window in enumerate(POOL_WINDOWS):
        pl.when(pl.program_id(1) == gi)(functools.partial(run, window))


def _prompt_odd(y, bg, pd, w, *, nb, sp, fp):
    n, c = y.shape
    assert c // LANES == len(POOL_WINDOWS)
    blk = pl.BlockSpec((sp, LANES), lambda bi, ci: (bi, ci))
    return pl.pallas_call(
        functools.partial(_prompt_odd_kernel, n_tiles=sp // SEQ_TILE, fp=fp),
        grid=(nb, c // LANES),
        in_specs=[blk, blk, blk, pl.BlockSpec((SCONV_W, LANES), lambda bi, ci: (0, ci))],
        out_specs=[blk, blk],
        out_shape=[jax.ShapeDtypeStruct((nb * sp, c), F32)] * 2,
        compiler_params=_params("parallel", "parallel"),
        name="prompt_sconv_pool",
    )(y, bg, pd, w)


def _sample_odd_kernel(ss_ref, sp_ref, y_ref, bg_ref, pd_ref, w_ref,
                       oc_ref, d_ref, ss_out_ref, sp_out_ref, yext_ref, pext_ref, *, ds):
    dbt, _, c = ss_ref.shape
    hs = SCONV_W - 1
    y = y_ref[...].reshape(dbt, ds, c)
    pd = pd_ref[...].reshape(dbt, ds, c)
    bg = bg_ref[...].reshape(dbt, ds, c)
    yext_ref[:, SUBLANES - hs:SUBLANES, :] = ss_ref[...]
    yext_ref[:, SUBLANES:SUBLANES + ds, :] = y
    conv = None
    for j in range(SCONV_W):
        lo = SUBLANES - hs + j
        term = yext_ref[:, lo:lo + ds, :] * w_ref[j:j + 1, :]
        conv = term if conv is None else conv + term
    oc_ref[...] = (bg * conv).reshape(dbt * ds, c)
    ss_out_ref[...] = yext_ref[:, SUBLANES + ds - hs:SUBLANES + ds, :]

    base = 2 * SUBLANES
    pext_ref[:, base - POOL_HIST:base, :] = sp_ref[...]
    pext_ref[:, base:base + ds, :] = pd
    run = pd
    sums = {}
    for i in range(1, max(POOL_WINDOWS)):
        run = run + pext_ref[:, base - i:base - i + ds, :]
        if i + 1 in POOL_WINDOWS:
            sums[i + 1] = run
    gc = c // len(POOL_WINDOWS)
    lane = lax.broadcasted_iota(jnp.int32, (dbt, ds, c), 2)
    mean = sums[POOL_WINDOWS[-1]] / float(POOL_WINDOWS[-1])
    for gi in range(len(POOL_WINDOWS) - 2, -1, -1):
        mean = jnp.where(lane < (gi + 1) * gc, sums[POOL_WINDOWS[gi]] / float(POOL_WINDOWS[gi]), mean)
    d_ref[...] = (mean - pd).reshape(dbt * ds, c)
    sp_out_ref[...] = pext_ref[:, base + ds - POOL_HIST:base + ds, :]


def _sample_odd(state_s, state_p, layer, y, bg, pd, w, *, s0, db, ds, dbt):
    n, c = y.shape
    rows = dbt * ds
    row_in = pl.BlockSpec((rows, c), lambda i: (s0 // rows + i, 0))
    row_out = pl.BlockSpec((rows, c), lambda i: (i, 0))
    hs = state_s.shape[2]
    hp = state_p.shape[2]
    return pl.pallas_call(
        functools.partial(_sample_odd_kernel, ds=ds),
        grid=(db // dbt,),
        in_specs=[pl.BlockSpec((dbt, None, hs, c), lambda i: (i, layer, 0, 0)),
                  pl.BlockSpec((dbt, None, hp, c), lambda i: (i, layer, 0, 0)),
                  row_in, row_in, row_in, _full(w.shape)],
        out_specs=[row_out, row_out, pl.BlockSpec((dbt, hs, c), lambda i: (i, 0, 0)),
                   pl.BlockSpec((dbt, hp, c), lambda i: (i, 0, 0))],
        out_shape=[jax.ShapeDtypeStruct((db * ds, c), F32), jax.ShapeDtypeStruct((db * ds, c), F32),
                   jax.ShapeDtypeStruct((db, hs, c), F32), jax.ShapeDtypeStruct((db, hp, c), F32)],
        scratch_shapes=[pltpu.VMEM((dbt, SUBLANES + ds, c), F32), pltpu.VMEM((dbt, 2 * SUBLANES + ds, c), F32)],
        compiler_params=_params("parallel"),
        name="sample_sconv_pool",
    )(state_s, state_p, y, bg, pd, w)


def kernel(x_prompt, x_sample, cache_k, cache_v, cache_logf, page_table, state_conv_a, state_sconv, state_pool, meta_tokens, w_in_even, b_forget, w_dw_a, b_dw_a, ln_a_g, ln_a_b, w_out_even, w_in_odd, w_sconv, w_pool_mix, pool_scale, w_out_odd, ln_mix_g, ln_mix_b, w_ffn_gate, w_ffn_up, w_ffn_down, ln_ffn_g, ln_ffn_b):
    nb, seq, d = x_prompt.shape
    db, ds, _ = x_sample.shape
    depth = w_ffn_gate.shape[0]
    alpha = float((2 * depth) ** 0.25)
    ca = state_conv_a.shape[-1]
    att = N_HEADS * HEAD_DIM
    assert ds == SUBLANES and cache_k.shape[3] == N_HEADS and cache_k.shape[4] == HEAD_DIM
    assert state_conv_a.shape[2] == CONV_W - 1 and state_pool.shape[2] == POOL_HIST
    assert meta_tokens.shape[0] == N_META

    s_real = N_META + seq
    fp = (-s_real) % SEQ_TILE
    if fp < CONV_BACK:
        fp += SEQ_TILE
    sp = fp + s_real
    s0 = nb * sp
    n = s0 + db * ds
    tm = next(t for t in (512, 256, 128, 64, 32, 16, 8) if s0 % t == 0 and (db * ds) % t == 0)
    tm_in = next(t for t in (1024, 512, 256, 128, 64, 32, 16, 8) if n % t == 0)
    dbt = next(t for t in (8, 4, 2, 1) if db % t == 0)
    direct_out = seq % SEQ_TILE == 0 and (db * ds) % SEQ_TILE == 0

    head = jnp.concatenate([jnp.zeros((fp, d), F32), meta_tokens.astype(F32)], axis=0)
    x = jnp.concatenate([piece for bi in range(nb) for piece in (head, x_prompt[bi])]
                        + [x_sample.reshape(db * ds, d)], axis=0)

    n_phys = cache_k.shape[0]
    n_even = cache_k.shape[1]
    page = cache_k.shape[2]
    ck = jnp.transpose(cache_k, (0, 1, 3, 4, 2)).reshape(n_phys, n_even, att, page)
    cv = jnp.transpose(cache_v, (0, 1, 3, 4, 2)).reshape(n_phys, n_even, att, page)
    clft = jnp.swapaxes(cache_logf, 2, 3)

    row2 = lambda v: v.reshape(1, -1).astype(F32)

    def prompt_rows(arr, lo, hi):
        return jnp.stack([arr[bi * sp + lo:bi * sp + hi] for bi in range(nb)], axis=0)

    def sample_rows(arr):
        return arr[s0:].reshape(db, ds, -1)

    wg_all, wu_all, wd_all = (w.astype(BF16) for w in (w_ffn_gate, w_ffn_up, w_ffn_down))
    wo_even, wo_odd = w_out_even.astype(BF16), w_out_odd.astype(BF16)
    wi_even = jnp.pad(w_in_even.astype(BF16), ((0, 0), (0, 0), (0, LANES - N_HEADS)))
    wi_odd = w_in_odd.astype(BF16)

    def tail(name, xin, m1p, m2p, m1s, m2s, p1, p2, w_out_all, layer):
        last = direct_out and layer == depth - 1
        return _layer_tail(name, xin, m1p, m2p, m1s, m2s, p1, p2, w_out_all,
                           row2(ln_mix_g[layer]), row2(ln_mix_b[layer]), wg_all, wu_all, wd_all,
                           row2(ln_ffn_g[layer]), row2(ln_ffn_b[layer]), even=layer % 2 == 0,
                           mix_idx=layer // 2, layer=layer, tm=SEQ_TILE if last else tm, alpha=alpha,
                           nb=nb, sp=sp, fp=fp, out_rows=(seq, db * ds) if last else None)

    prev_kv = []
    lfp, cap, scp, plp = [], [], [], []
    ks_, vs_, lfs, cas, scs, pls = [], [], [], [], [], []
    for layer in range(depth):
        i = layer // 2
        if layer % 2 == 0:
            bf = jnp.pad(b_forget[i].astype(F32), (0, LANES - N_HEADS)).reshape(1, LANES)
            a, q, k, v, lf = _even_in(x, wi_even, i, bf, tm=tm_in, ca=ca, att=att)
            w_dw, b_dw = w_dw_a[i].astype(F32), row2(b_dw_a[i])
            conv_s, st_a = _sample_conv(state_conv_a.astype(F32), i, a, w_dw, b_dw, s0=s0, db=db, ds=ds, dbt=dbt)
            qa, ka, vat, conv_p, *caches = _fox_prep(q, k, v, lf, a, w_dw, b_dw, prev_kv, i == n_even - 1,
                                                     nb=nb, sp=sp, fp=fp)
            prev_kv.append((k, v))
            at_p = _flash(qa, ka, vat, hg=FLASH_HEADS)
            at_s = _sample_attn(page_table, i, q, k, v, lf, ck, cv, clft, s0=s0, db=db, ds=ds)
            x = tail("even_tail", x, conv_p, at_p, conv_s, at_s, row2(ln_a_g[i]), row2(ln_a_b[i]), wo_even, layer)
            lfp.append(prompt_rows(lf, fp, sp)[..., :N_HEADS])
            cap.append(prompt_rows(a, sp - (CONV_W - 1), sp))
            ks_.append(sample_rows(k).reshape(db, ds, N_HEADS, HEAD_DIM))
            vs_.append(sample_rows(v).reshape(db, ds, N_HEADS, HEAD_DIM))
            lfs.append(sample_rows(lf)[..., :N_HEADS])
            cas.append(st_a)
        else:
            c = state_sconv.shape[-1]
            y, bg, pd = _odd_in(x, wi_odd, i, tm=tm_in, c=c)
            oc_p, dd_p = _prompt_odd(y, bg, pd, w_sconv[i].astype(F32), nb=nb, sp=sp, fp=fp)
            oc_s, dd_s, st_s, st_p = _sample_odd(state_sconv.astype(F32), state_pool.astype(F32), i, y, bg, pd,
                                                 w_sconv[i].astype(F32), s0=s0, db=db, ds=ds, dbt=dbt)
            wm = jax.scipy.linalg.block_diag(*[w_pool_mix[i, g] for g in range(w_pool_mix.shape[1])]).astype(BF16)
            x = tail("odd_tail", x, oc_p, dd_p, oc_s, dd_s, wm, row2(pool_scale[i]), wo_odd, layer)
            scp.append(prompt_rows(y, sp - (SCONV_W - 1), sp))
            plp.append(prompt_rows(pd, sp - POOL_HIST, sp))
            scs.append(st_s)
            pls.append(st_p)

    if direct_out:
        y_prompt, y_sample = x[0], x[1].reshape(db, ds, d)
    else:
        y_prompt, y_sample = prompt_rows(x, fp + N_META, sp), sample_rows(x)
    st = lambda xs: jnp.stack(xs, axis=1)
    kp, vp = (jnp.transpose(c.reshape(nb, n_even, N_HEADS, HEAD_DIM, s_real), (0, 1, 4, 2, 3)) for c in caches)
    return (y_prompt, y_sample, kp, vp, st(lfp), st(cap), st(scp), st(plp),
            st(ks_), st(vs_), st(lfs), st(cas), st(scs), st(pls))
```

```python
import functools

import jax
import jax.numpy as jnp
from jax import lax
from jax.experimental import pallas as pl
from jax.experimental.pallas import tpu as pltpu

N_META = 16
N_HEADS = 8
HEAD_DIM = 64
CONV_W = 31
SCONV_W = 3
POOL_WINDOWS = (2, 4, 8, 16)
POOL_HIST = max(POOL_WINDOWS) - 1
LN_EPS = 1e-5
SEQ_TILE = 256
LANES = 128
SUBLANES = 8
CONV_BACK = -(-(CONV_W - 1) // SUBLANES) * SUBLANES
C_TERMS = 3
AUX_ONES = C_TERMS * N_HEADS
FLASH_HEADS = 8
V_ROWS = HEAD_DIM + 2 * SUBLANES
VMEM_LIMIT = 56 * 1024 * 1024
NEG = -1e30
LOG2E = 1.4426950408889634
F32 = jnp.float32
BF16 = jnp.bfloat16
NT_DIMS = (((1,), (1,)), ((), ()))


def _params(*sem):
    return pltpu.CompilerParams(dimension_semantics=sem, vmem_limit_bytes=VMEM_LIMIT)


def _dot(a, b):
    return jnp.dot(a, b, preferred_element_type=F32)


def _ln(z, g, b):
    mu = jnp.mean(z, axis=-1, keepdims=True)
    zc = z - mu
    var = jnp.mean(zc * zc, axis=-1, keepdims=True)
    return zc * lax.rsqrt(var + LN_EPS) * g + b


def _silu(x):
    return x * jax.nn.sigmoid(x)


def _full(shape):
    return pl.BlockSpec(shape, lambda *_: (0,) * len(shape))


def _slab(a, idx):
    return pl.BlockSpec((None,) + a.shape[1:], lambda *_: (idx,) + (0,) * (a.ndim - 1),
                        pipeline_mode=pl.Buffered(1))


def _even_in_kernel(x_ref, w_ref, bf_ref, a_ref, q_ref, k_ref, v_ref, lf_ref, *, ca, att):
    xb = x_ref[...].astype(BF16)

    def mm(lo, hi):
        return _dot(xb, w_ref[:, lo:hi])

    u = mm(0, ca)
    g = mm(ca, 2 * ca)
    a_ref[...] = u * jax.nn.sigmoid(g)
    o = 2 * ca
    q_ref[...] = mm(o, o + att) * (HEAD_DIM ** -0.5)
    k_ref[...] = mm(o + att, o + 2 * att)
    v_ref[...] = mm(o + 2 * att, o + 3 * att)
    z = mm(o + 3 * att, o + 3 * att + LANES) + bf_ref[...]
    lf = jnp.minimum(z, 0.0) - jnp.log1p(jnp.exp(-jnp.abs(z)))
    lane = lax.broadcasted_iota(jnp.int32, lf.shape, 1)
    lf_ref[...] = jnp.where(lane < N_HEADS, lf, 0.0)


def _even_in(x, w, idx, bf, *, tm, ca, att):
    n, d = x.shape
    row = lambda c: pl.BlockSpec((tm, c), lambda i: (i, 0))
    return pl.pallas_call(
        functools.partial(_even_in_kernel, ca=ca, att=att),
        grid=(n // tm,),
        in_specs=[row(d), _slab(w, idx), _full(bf.shape)],
        out_specs=[row(ca), row(att), row(att), row(att), row(LANES)],
        out_shape=[jax.ShapeDtypeStruct((n, c), F32) for c in (ca, att, att, att, LANES)],
        compiler_params=_params("parallel"),
        name="even_in_proj",
    )(x, w, bf)


def _odd_in_kernel(x_ref, w_ref, y_ref, bg_ref, pd_ref, *, c):
    xb = x_ref[...].astype(BF16)
    hc = _dot(xb, w_ref[:, 0:c])
    bg_ref[...] = _dot(xb, w_ref[:, c:2 * c])
    cg = _dot(xb, w_ref[:, 2 * c:3 * c])
    y_ref[...] = cg * hc
    pd_ref[...] = _dot(xb, w_ref[:, 3 * c:4 * c])


def _odd_in(x, w, idx, *, tm, c):
    n, d = x.shape
    row = lambda cc: pl.BlockSpec((tm, cc), lambda i: (i, 0))
    return pl.pallas_call(
        functools.partial(_odd_in_kernel, c=c),
        grid=(n // tm,),
        in_specs=[row(d), _slab(w, idx)],
        out_specs=[row(c)] * 3,
        out_shape=[jax.ShapeDtypeStruct((n, c), F32)] * 3,
        compiler_params=_params("parallel"),
        name="odd_in_proj",
    )(x, w)


def _tail_kernel(x_ref, m1p_ref, m2p_ref, m1s_ref, m2s_ref, p1_ref, p2_ref, w_ref, g1_ref, b1_ref,
                 wg_ref, wu_ref, wd_ref, g2_ref, b2_ref, *rest, even, alpha, chunk, nb, sp, fp, npt, lead):
    i = pl.program_id(0)
    is_sample = i >= npt
    m1 = jnp.where(is_sample, m1s_ref[...], m1p_ref[...])
    m2 = jnp.where(is_sample, m2s_ref[...], m2p_ref[...])
    if even:
        m1 = _silu(_ln(m1, p1_ref[...], p2_ref[...]))
    else:
        m2 = _dot(m2.astype(BF16), p1_ref[...]) * p2_ref[...]
    ca = m1.shape[1]
    mix = _dot(m1.astype(BF16), w_ref[0:ca, :]) + _dot(m2.astype(BF16), w_ref[ca:, :])
    x1 = _ln(alpha * x_ref[...] + mix, g1_ref[...], b1_ref[...])

    h_ref = rest[-1]
    xb = x1.astype(BF16)
    for c in range(0, wg_ref.shape[1], chunk):
        gate = _dot(xb, wg_ref[:, c:c + chunk])
        up = _dot(xb, wu_ref[:, c:c + chunk])
        h_ref[:, c:c + chunk] = (_silu(gate) * up).astype(BF16)
    out = _ln(alpha * x1 + _dot(h_ref[...], wd_ref[...]), g2_ref[...], b2_ref[...])

    tm = x1.shape[0]
    if lead is None:
        r = i * tm + lax.broadcasted_iota(jnp.int32, (tm, 1), 0)
        keep = jnp.ones((tm, 1), F32)
        for bi in range(nb):
            keep = jnp.where((r >= bi * sp) & (r < bi * sp + fp), 0.0, keep)
        rest[0][...] = out * keep
    else:
        yp_ref, ys_ref = rest[:2]

        @pl.when(jnp.logical_and(jnp.logical_not(is_sample), i % (sp // tm) >= lead))
        def _():
            yp_ref[...] = out

        @pl.when(is_sample)
        def _():
            ys_ref[...] = out


def _layer_tail(name, x, m1p, m2p, m1s, m2s, p1, p2, w, g1, b1, wg, wu, wd, g2, b2, *, even, mix_idx, layer, tm,
                alpha, nb, sp, fp, out_rows=None):
    n, d = x.shape
    s0 = m1p.shape[0]
    dff = wg.shape[2]
    chunk = SEQ_TILE if dff % SEQ_TILE == 0 else dff
    npt = s0 // tm
    row = lambda c: pl.BlockSpec((tm, c), lambda i: (i, 0))
    prow = lambda c: pl.BlockSpec((tm, c), lambda i: (jnp.minimum(i, npt - 1), 0))
    srow = lambda c: pl.BlockSpec((tm, c), lambda i: (jnp.maximum(i - npt, 0), 0))
    once = lambda a: pl.BlockSpec(a.shape, lambda i: (0,) * a.ndim, pipeline_mode=pl.Buffered(1))
    consts = (p1, p2, w, g1, b1, wg, wu, wd, g2, b2)
    const_specs = [once(p1), once(p2), _slab(w, mix_idx), once(g1), once(b1),
                   _slab(wg, layer), _slab(wu, layer), _slab(wd, layer), once(g2), once(b2)]
    if out_rows is None:
        lead = None
        out_specs = row(d)
        out_shape = jax.ShapeDtypeStruct((n, d), F32)
    else:
        seq, db_rows = out_rows
        tps = sp // tm
        lead = (sp - seq) // tm
        out_specs = [
            pl.BlockSpec((None, tm, d), lambda i: (jnp.minimum(i // tps, nb - 1),
                                                   jnp.where(i < npt, jnp.maximum(i % tps - lead, 0), tps - lead - 1),
                                                   0)),
            srow(d)]
        out_shape = [jax.ShapeDtypeStruct((nb, seq, d), F32), jax.ShapeDtypeStruct((db_rows, d), F32)]
    return pl.pallas_call(
        functools.partial(_tail_kernel, even=even, alpha=alpha, chunk=chunk, nb=nb, sp=sp, fp=fp, npt=npt, lead=lead),
        grid=(n // tm,),
        in_specs=[row(d), prow(m1p.shape[1]), prow(m2p.shape[1]), srow(m1s.shape[1]), srow(m2s.shape[1])]
        + const_specs,
        out_specs=out_specs,
        out_shape=out_shape,
        scratch_shapes=[pltpu.VMEM((tm, dff), BF16)],
        compiler_params=_params("arbitrary" if out_rows else "parallel"),
        name=name,
    )(x, m1p, m2p, m1s, m2s, *consts)


def _conv_tile(win, w_rows):
    rows = win.shape[0]
    t = rows - CONV_BACK
    lead = CONV_BACK - (CONV_W - 1)
    acc = None
    for r in range(SUBLANES):
        rolled = win if r == 0 else pltpu.roll(win, rows - r, axis=0)
        for m in range(CONV_BACK // SUBLANES + 1):
            j = SUBLANES * m + r - lead
            if 0 <= j < CONV_W:
                term = rolled[SUBLANES * m:SUBLANES * m + t] * w_rows[j]
                acc = term if acc is None else acc + term
    return acc


def _sample_conv_kernel(st_ref, a_ref, w_ref, b_ref, o_ref, st_out_ref, ext_ref, *, ds):
    dbt, hist, c = st_ref.shape
    off = CONV_BACK - hist
    rows = CONV_BACK + ds
    ext_ref[:, 0:off, :] = jnp.zeros((dbt, off, c), F32)
    ext_ref[:, off:CONV_BACK, :] = st_ref[...]
    ext_ref[:, CONV_BACK:rows, :] = a_ref[...].reshape(dbt, ds, c)
    flat = ext_ref[...].reshape(dbt * rows, c)
    acc = None
    for r in range(SUBLANES):
        shifted = (flat if r == 0 else pltpu.roll(flat, dbt * rows - r, axis=0)).reshape(dbt, rows, c)
        for m in range(rows // SUBLANES):
            j = SUBLANES * m + r - off
            if 0 <= j < CONV_W:
                term = shifted[:, SUBLANES * m:SUBLANES * m + ds, :] * w_ref[j:j + 1, :]
                acc = term if acc is None else acc + term
    o_ref[...] = (acc + b_ref[...]).reshape(dbt * ds, c)
    st_out_ref[...] = ext_ref[:, off + ds:rows, :]


def _sample_conv(state, layer, a, w, b, *, s0, db, ds, dbt):
    n, c = a.shape
    hist = state.shape[2]
    rows = dbt * ds
    return pl.pallas_call(
        functools.partial(_sample_conv_kernel, ds=ds),
        grid=(db // dbt,),
        in_specs=[pl.BlockSpec((dbt, None, hist, c), lambda i: (i, layer, 0, 0)),
                  pl.BlockSpec((rows, c), lambda i: (s0 // rows + i, 0)), _full(w.shape), _full(b.shape)],
        out_specs=[pl.BlockSpec((rows, c), lambda i: (i, 0)), pl.BlockSpec((dbt, hist, c), lambda i: (i, 0, 0))],
        out_shape=[jax.ShapeDtypeStruct((db * ds, c), F32), jax.ShapeDtypeStruct((db, hist, c), F32)],
        scratch_shapes=[pltpu.VMEM((dbt, CONV_BACK + ds, c), F32)],
        compiler_params=_params("parallel"),
        name="sample_conv",
    )(state, a, w, b)


def _fox_prep_kernel(q_ref, k_ref, v_ref, lf_ref, a_ref, wdw_ref, bdw_ref, *rest, fp, ns, n_prev, emit):
    prev, rest = rest[:2 * n_prev], rest[2 * n_prev:]
    qa_ref, ka_ref, vat_ref, cv_ref = rest[:4]
    carry_ref, ah_ref = rest[6:8] if emit else rest[4:6]
    s = pl.program_id(1)
    t = SEQ_TILE

    @pl.when(s < ns)
    def _():
        _fox_operands(s, q_ref, k_ref, v_ref, lf_ref, qa_ref, ka_ref, vat_ref, carry_ref, fp)

        @pl.when(s == 0)
        def _():
            ah_ref[...] = jnp.zeros_like(ah_ref)

        for cb in range(a_ref.shape[1] // LANES):
            cs = slice(cb * LANES, (cb + 1) * LANES)
            win = jnp.concatenate([ah_ref[:, cs], a_ref[:, cs]], axis=0)
            w_rows = [wdw_ref[j:j + 1, cs] for j in range(CONV_W)]
            cv_ref[:, cs] = _conv_tile(win, w_rows) + bdw_ref[:, cs]
        ah_ref[...] = a_ref[t - CONV_BACK:t, :]

    if not emit:
        return
    kt_ref, vt_ref = rest[4:6]
    kc_ref, vc_ref = rest[8:10]

    @pl.when(s == 0)
    def _():
        kc_ref[...] = jnp.zeros_like(kc_ref)
        vc_ref[...] = jnp.zeros_like(vc_ref)

    k_srcs = list(prev[0::2]) + [k_ref]
    v_srcs = list(prev[1::2]) + [v_ref]
    for srcs, held, dst in ((k_srcs, kc_ref, kt_ref), (v_srcs, vc_ref, vt_ref)):
        for li, src in enumerate(srcs):
            rows = jnp.concatenate([held[li], src[0:fp, :]], axis=0)
            for c in range(src.shape[1] // LANES):
                dst[li, c * LANES:(c + 1) * LANES, :] = rows[:, c * LANES:(c + 1) * LANES].T
            held[li] = src[fp:t, :]


def _fox_operands(s, q_ref, k_ref, v_ref, lf_ref, qa_ref, ka_ref, vat_ref, carry_ref, fp):
    @pl.when(s == 0)
    def _():
        carry_ref[...] = jnp.zeros_like(carry_ref)

    t = SEQ_TILE
    row = lax.broadcasted_iota(jnp.int32, (t, LANES), 0)
    lane = lax.broadcasted_iota(jnp.int32, (t, LANES), 1)
    c = jnp.where(s * t + row >= fp, lf_ref[...], 0.0)
    sh = 1
    while sh < t:
        c = c + jnp.where(row >= sh, pltpu.roll(c, sh, axis=0), 0.0)
        sh *= 2
    c = c + carry_ref[0:1, :]
    carry_ref[0:1, :] = c[t - 1:t, :]

    c = c * LOG2E
    c1 = c.astype(BF16).astype(F32)
    r1 = c - c1
    c2 = r1.astype(BF16).astype(F32)
    c3 = (r1 - c2).astype(BF16).astype(F32)
    g23 = jnp.where(lane < 2 * N_HEADS, pltpu.roll(c2, N_HEADS, axis=1),
                    jnp.where(lane < 3 * N_HEADS, pltpu.roll(c3, 2 * N_HEADS, axis=1), 0.0))
    g = jnp.where(lane < N_HEADS, c1, g23)
    gk = jnp.where(lane < N_HEADS, jnp.where(s * t + row >= fp, c1, -NEG), g23)
    ones_grp = (lane >= HEAD_DIM + AUX_ONES) & (lane < HEAD_DIM + 2 * AUX_ONES)
    ck = [jnp.where((lane >= HEAD_DIM) & (lane < HEAD_DIM + AUX_ONES), -pltpu.roll(gk, HEAD_DIM, axis=1),
                    jnp.where(ones_grp, 1.0, 0.0))]
    cq = [jnp.where(ones_grp, pltpu.roll(g, HEAD_DIM + AUX_ONES, axis=1), 0.0)]
    ck.append(pltpu.roll(ck[0], HEAD_DIM, axis=1))
    cq.append(pltpu.roll(cq[0], HEAD_DIM, axis=1))

    lane1 = lax.broadcasted_iota(jnp.int32, (1, LANES), 1)
    for h in range(N_HEADS):
        odd = h % 2
        pr = h // 2
        base = 0 if odd else HEAD_DIM
        a = lane1 - base
        mine = (lane1 & (N_HEADS - 1)) == h
        data = ((lane1 >= HEAD_DIM) if odd else (lane1 < HEAD_DIM)).astype(F32)
        sel1 = ((a >= 0) & (a < AUX_ONES) & mine).astype(F32)
        selc = ((a >= AUX_ONES) & (a < 2 * AUX_ONES) & mine).astype(F32)
        sl = slice(pr * LANES, (pr + 1) * LANES)
        qa_ref[h] = (q_ref[:, sl] * (data * LOG2E) + (cq[odd] * selc + sel1)).astype(BF16)
        ka_ref[h] = (k_ref[:, sl] * data + ck[odd]).astype(BF16)
    ones = jnp.ones((V_ROWS - HEAD_DIM, t), BF16)
    for pr in range(N_HEADS // 2):
        v_t = v_ref[:, pr * LANES:(pr + 1) * LANES].T.astype(BF16)
        vat_ref[2 * pr] = jnp.concatenate([v_t[0:HEAD_DIM], ones], axis=0)
        vat_ref[2 * pr + 1] = jnp.concatenate([v_t[HEAD_DIM:], ones], axis=0)


def _fox_prep(q, k, v, lf, a, w_dw, b_dw, prev_kv, emit, *, nb, sp, fp):
    ns = sp // SEQ_TILE
    att = q.shape[1]
    ca = a.shape[1]
    t = SEQ_TILE
    last = ns - 1
    row = lambda c: pl.BlockSpec((t, c), lambda bi, si: (bi * ns + jnp.minimum(si, last), 0))
    hm = pl.BlockSpec((None, N_HEADS, t, LANES), lambda bi, si: (bi, 0, jnp.minimum(si, last), 0))
    hm_t = pl.BlockSpec((None, N_HEADS, V_ROWS, t), lambda bi, si: (bi, 0, 0, jnp.minimum(si, last)))
    out_specs = [hm, hm, hm_t, row(ca)]
    out_shape = [jax.ShapeDtypeStruct((nb, N_HEADS, sp, LANES), BF16)] * 2 + [
        jax.ShapeDtypeStruct((nb, N_HEADS, V_ROWS, sp), BF16), jax.ShapeDtypeStruct((nb * sp, ca), F32)]
    scratch = [pltpu.VMEM((SUBLANES, LANES), F32), pltpu.VMEM((CONV_BACK, ca), F32)]
    n_prev = len(prev_kv) if emit else 0
    if emit:
        assert 0 < fp < t and fp % SUBLANES == 0
        n_layers = n_prev + 1
        cache = pl.BlockSpec((None, n_layers, att, t), lambda bi, si: (bi, 0, 0, jnp.maximum(si - 1, 0)))
        out_specs += [cache, cache]
        out_shape += [jax.ShapeDtypeStruct((nb, n_layers, att, sp - fp), F32)] * 2
        scratch += [pltpu.VMEM((n_layers, t - fp, att), F32)] * 2
    prev = [a for kv in prev_kv for a in kv] if emit else []
    return pl.pallas_call(
        functools.partial(_fox_prep_kernel, fp=fp, ns=ns, n_prev=n_prev, emit=emit),
        grid=(nb, ns + 1 if emit else ns),
        in_specs=[row(att), row(att), row(att), row(LANES), row(ca), _full(w_dw.shape), _full(b_dw.shape)]
        + [row(att)] * len(prev),
        out_specs=out_specs,
        out_shape=out_shape,
        scratch_shapes=scratch,
        compiler_params=_params("parallel", "arbitrary"),
        name="fox_prep",
    )(q, k, v, lf, a, w_dw, b_dw, *prev)


def _flash_kernel(q_ref, k_ref, vt_ref, o_ref, s_ref, p_ref, acc_ref, *, hg):
    i = pl.program_id(2)
    t = SEQ_TILE

    def scores_to(j, slot):
        ks = pl.multiple_of(j * t, t)
        for h in range(hg):
            s_ref[slot, h] = lax.dot_general(k_ref[h, pl.ds(ks, t), :], q_ref[h], NT_DIMS,
                                             preferred_element_type=F32)

    def accumulate(j, als):
        ks = pl.multiple_of(j * t, t)
        pvs = [_dot(vt_ref[h, :, pl.ds(ks, t)], p_ref[h]) for h in range(hg)]
        for h in range(hg):
            acc_ref[h] = als[h] * acc_ref[h] + pvs[h]

    def softmax(slot, ms, masked):
        new_m, als = [], []
        for h in range(hg):
            s = s_ref[slot, h]
            if masked:
                visible = (lax.broadcasted_iota(jnp.int32, (t, t), 0) <= lax.broadcasted_iota(jnp.int32, (t, t), 1))
                s = jnp.where(visible, s, NEG)
            mn = jnp.maximum(ms[h], jnp.max(s, axis=0, keepdims=True))
            als.append(jnp.exp2(ms[h] - mn))
            new_m.append(mn)
            p_ref[h] = jnp.exp2(s - mn).astype(BF16)
        return tuple(new_m), tuple(als)

    def stage(j, slot, state):
        ms, als = state
        scores_to(j + 1, 1 - slot)
        accumulate(jnp.maximum(j - 1, 0), als)
        return softmax(slot, ms, False)

    def pair(jj, state):
        return stage(2 * jj + 1, 1, stage(2 * jj, 0, state))

    def finish(slot, state):
        ms, als = state
        accumulate(jnp.maximum(i - 1, 0), als)
        _, als = softmax(slot, ms, True)
        accumulate(i, als)
        for pr in range(hg // 2):
            halves = [acc_ref[h, 0:HEAD_DIM, :] * (1.0 / acc_ref[h, HEAD_DIM:HEAD_DIM + 1, :])
                      for h in (2 * pr, 2 * pr + 1)]
            o_ref[:, pr * LANES:(pr + 1) * LANES] = jnp.concatenate(halves, axis=0).T

    p_ref[...] = jnp.zeros_like(p_ref)
    acc_ref[...] = jnp.zeros_like(acc_ref)
    scores_to(0, 0)
    state = ((jnp.full((1, t), NEG, F32),) * hg, (jnp.ones((1, t), F32),) * hg)
    state = lax.fori_loop(0, i // 2, pair, state)
    odd = i % 2 == 1
    state = lax.cond(odd, lambda st: stage(i - 1, 0, st), lambda st: st, state)
    pl.when(odd)(lambda: finish(1, state))
    pl.when(jnp.logical_not(odd))(lambda: finish(0, state))


def _flash(qa, ka, vat, *, hg):
    nb, _, sp, _ = qa.shape
    ns = sp // SEQ_TILE
    return pl.pallas_call(
        functools.partial(_flash_kernel, hg=hg),
        grid=(nb, N_HEADS // hg, ns),
        in_specs=[pl.BlockSpec((None, hg, SEQ_TILE, LANES), lambda bi, gi, qi: (bi, gi, qi, 0)),
                  pl.BlockSpec((None, hg, sp, LANES), lambda bi, gi, qi: (bi, gi, 0, 0)),
                  pl.BlockSpec((None, hg, V_ROWS, sp), lambda bi, gi, qi: (bi, gi, 0, 0))],
        out_specs=pl.BlockSpec((SEQ_TILE, hg * HEAD_DIM), lambda bi, gi, qi: (bi * ns + qi, gi)),
        out_shape=jax.ShapeDtypeStruct((nb * sp, N_HEADS * HEAD_DIM), F32),
        scratch_shapes=[pltpu.VMEM((2, hg, SEQ_TILE, SEQ_TILE), F32), pltpu.VMEM((hg, SEQ_TILE, SEQ_TILE), BF16),
                        pltpu.VMEM((hg, V_ROWS, SEQ_TILE), F32)],
        compiler_params=_params("parallel", "parallel", "arbitrary"),
        name="fox_flash",
    )(qa, ka, vat)


def _sample_attn_kernel(pt_ref, q_ref, kn_ref, vn_ref, lfn_ref, *rest, n_pages, page, eb, ds):
    del pt_ref
    o_ref = rest[3 * eb * n_pages]
    c = q_ref.shape[1]
    rows = ds * N_HEADS
    groups = [list(range(g, min(g + 2, n_pages))) for g in range(0, n_pages, 2)]
    past = n_pages * page
    sub = lax.broadcasted_iota(jnp.int32, (N_HEADS, c), 0)
    lane = lax.broadcasted_iota(jnp.int32, (N_HEADS, c), 1)
    hm = ((lane >= sub * HEAD_DIM) & (lane < (sub + 1) * HEAD_DIM)).astype(F32)
    hm_t = jnp.concatenate([hm] * ds, axis=0)
    pad = jnp.zeros((LANES - ds, c), F32)

    def refs_of(kind, e):
        return rest[(kind * eb + e) * n_pages:(kind * eb + e + 1) * n_pages]

    def scores(e):
        sl = slice(e * ds, (e + 1) * ds)
        q = q_ref[sl, :]
        qbd = jnp.concatenate([jnp.broadcast_to(q[t:t + 1, :], (N_HEADS, c)) * hm for t in range(ds)],
                              axis=0).astype(BF16)
        k_refs, lf_refs = refs_of(0, e), refs_of(2, e)
        s_parts = []
        for grp in groups:
            kp = jnp.concatenate([k_refs[r][...] for r in grp], axis=1).astype(BF16)
            s_parts.append(_dot(qbd, kp))
        s_past = jnp.concatenate(s_parts, axis=1)

        lf = jnp.concatenate([r[...] for r in lf_refs], axis=1)
        lane_p = lax.broadcasted_iota(jnp.int32, (N_HEADS, past), 1)
        suf = lf
        sh = 1
        while sh < past:
            suf = suf + jnp.where(lane_p < past - sh, pltpu.roll(suf, past - sh, axis=1), 0.0)
            sh *= 2
        s_past = s_past + jnp.concatenate([suf - lf] * ds, axis=0)

        kn = jnp.concatenate([kn_ref[sl, :], pad], axis=0).astype(BF16)
        s_new = lax.dot_general(qbd, kn, NT_DIMS, preferred_element_type=F32)
        cn = lfn_ref[sl, :]
        sub_n = lax.broadcasted_iota(jnp.int32, cn.shape, 0)
        sh = 1
        while sh < ds:
            cn = cn + jnp.where(sub_n >= sh, pltpu.roll(cn, sh, axis=0), 0.0)
            sh *= 2
        cn_t = jnp.concatenate([cn, jnp.zeros((LANES - ds, LANES), F32)], axis=0).T[0:N_HEADS, :]
        rown = lax.broadcasted_iota(jnp.int32, (rows, LANES), 0)
        u = lax.broadcasted_iota(jnp.int32, (rows, LANES), 1)
        s_new = jnp.where(u * N_HEADS <= rown, s_new - jnp.concatenate([cn_t] * ds, axis=0), NEG)
        return s_past, s_new

    def softmax(s_past, s_new):
        m = jnp.maximum(jnp.max(s_past, axis=1, keepdims=True), jnp.max(s_new, axis=1, keepdims=True))
        p_past = jnp.exp(s_past - m)
        p_new = jnp.exp(s_new - m)
        l = jnp.sum(p_past, axis=1, keepdims=True) + jnp.sum(p_new, axis=1, keepdims=True)
        return p_past.astype(BF16), p_new.astype(BF16), l

    def output(e, p_past, p_new, l):
        sl = slice(e * ds, (e + 1) * ds)
        v_refs = refs_of(1, e)
        vn = jnp.concatenate([vn_ref[sl, :], pad], axis=0).astype(BF16)
        o = _dot(p_new, vn)
        for grp in groups:
            vp = jnp.concatenate([v_refs[r][...] for r in grp], axis=1).astype(BF16)
            lo = grp[0] * page
            o = o + lax.dot_general(p_past[:, lo:lo + len(grp) * page], vp, NT_DIMS, preferred_element_type=F32)
        o = o * (1.0 / l) * hm_t
        rr = lax.broadcasted_iota(jnp.int32, (ds, rows), 1)
        tt = lax.broadcasted_iota(jnp.int32, (ds, rows), 0)
        pick = ((rr >= tt * N_HEADS) & (rr < (tt + 1) * N_HEADS)).astype(BF16)
        o_ref[sl, :] = _dot(pick, o.astype(BF16))

    ss = [scores(e) for e in range(eb)]
    ps = [softmax(*s) for s in ss]
    for e in range(eb):
        output(e, *ps[e])


def _sample_attn(page_table, layer, q, k, v, lf, cache_k, cache_v, cache_lft, *, s0, db, ds):
    n, c = q.shape
    n_pages = page_table.shape[1]
    page = cache_k.shape[3]
    eb = 2 if db % 2 == 0 else 1
    rows = eb * ds
    row = lambda cc: pl.BlockSpec((rows, cc), lambda i, pt: (s0 // rows + i, 0))

    def paged(shape, e, r):
        return pl.BlockSpec((None, None) + shape, lambda i, pt: (pt[(i * eb + e) * n_pages + r], layer, 0, 0))

    pages = [(e, r) for e in range(eb) for r in range(n_pages)]
    in_specs = ([row(c), row(c), row(c), row(LANES)]
                + [paged((c, page), e, r) for e, r in pages]
                + [paged((c, page), e, r) for e, r in pages]
                + [paged((N_HEADS, page), e, r) for e, r in pages])
    return pl.pallas_call(
        functools.partial(_sample_attn_kernel, n_pages=n_pages, page=page, eb=eb, ds=ds),
        grid_spec=pltpu.PrefetchScalarGridSpec(
            num_scalar_prefetch=1, grid=(db // eb,), in_specs=in_specs,
            out_specs=pl.BlockSpec((rows, c), lambda i, pt: (i, 0))),
        out_shape=jax.ShapeDtypeStruct((db * ds, c), F32),
        compiler_params=_params("parallel"),
        name="sample_paged_attn",
    )(page_table.reshape(-1), q, k, v, lf, *([cache_k] * len(pages)), *([cache_v] * len(pages)),
      *([cache_lft] * len(pages)))


def _odd_tile(wy, wp, bg, w_rows, window, pos0):
    t = bg.shape[0]
    y0 = wy[SUBLANES:]
    y1 = pltpu.roll(wy, 1, axis=0)[SUBLANES:]
    y2 = pltpu.roll(wy, 2, axis=0)[SUBLANES:]
    out_c = bg * (w_rows[0] * y2 + w_rows[1] * y1 + w_rows[2] * y0)
    back = 2 * SUBLANES
    win = wp
    step = 1
    while step < window:
        win = win + pltpu.roll(win, step, axis=0)
        step *= 2
    if pos0 is None:
        cnt = float(window)
    else:
        pos = pos0 + lax.broadcasted_iota(jnp.int32, (t, LANES), 0)
        cnt = jnp.maximum(jnp.minimum(window, pos + 1), 1).astype(F32)
    return out_c, win[back:] / cnt - wp[back:]


def _prompt_odd_kernel(y_ref, bg_ref, pd_ref, w_ref, oc_ref, d_ref, *, n_tiles, fp):
    w_rows = [w_ref[j:j + 1, :] for j in range(SCONV_W)]
    c = y_ref.shape[1]
    t = SEQ_TILE
    n_partial = -(-(fp + POOL_HIST) // t)

    def run(window):
        wy0 = jnp.concatenate([jnp.zeros((SUBLANES, c), F32), y_ref[0:t, :]], axis=0)
        wp0 = jnp.concatenate([jnp.zeros((2 * SUBLANES, c), F32), pd_ref[0:t, :]], axis=0)
        oc, d = _odd_tile(wy0, wp0, bg_ref[0:t, :], w_rows, window, -fp)
        oc_ref[0:t, :] = oc
        d_ref[0:t, :] = d

        def body(i, partial):
            s0 = pl.multiple_of(i * t, t)
            wy = y_ref[pl.ds(s0 - SUBLANES, t + SUBLANES), :]
            wp = pd_ref[pl.ds(s0 - 2 * SUBLANES, t + 2 * SUBLANES), :]
            oc, d = _odd_tile(wy, wp, bg_ref[pl.ds(s0, t), :], w_rows, window, s0 - fp if partial else None)
            oc_ref[pl.ds(s0, t), :] = oc
            d_ref[pl.ds(s0, t), :] = d

        lax.fori_loop(1, n_partial, lambda i, carry: body(i, True) or carry, 0)
        lax.fori_loop(n_partial, n_tiles, lambda i, carry: body(i, False) or carry, 0)

    for gi, window in enumerate(POOL_WINDOWS):
        pl.when(pl.program_id(1) == gi)(functools.partial(run, window))


def _prompt_odd(y, bg, pd, w, *, nb, sp, fp):
    n, c = y.shape
    assert c // LANES == len(POOL_WINDOWS)
    blk = pl.BlockSpec((sp, LANES), lambda bi, ci: (bi, ci))
    return pl.pallas_call(
        functools.partial(_prompt_odd_kernel, n_tiles=sp // SEQ_TILE, fp=fp),
        grid=(nb, c // LANES),
        in_specs=[blk, blk, blk, pl.BlockSpec((SCONV_W, LANES), lambda bi, ci: (0, ci))],
        out_specs=[blk, blk],
        out_shape=[jax.ShapeDtypeStruct((nb * sp, c), F32)] * 2,
        compiler_params=_params("parallel", "parallel"),
        name="prompt_sconv_pool",
    )(y, bg, pd, w)


def _sample_odd_kernel(ss_ref, sp_ref, y_ref, bg_ref, pd_ref, w_ref,
                       oc_ref, d_ref, ss_out_ref, sp_out_ref, yext_ref, pext_ref, *, ds):
    dbt, _, c = ss_ref.shape
    hs = SCONV_W - 1
    y = y_ref[...].reshape(dbt, ds, c)
    pd = pd_ref[...].reshape(dbt, ds, c)
    bg = bg_ref[...].reshape(dbt, ds, c)
    ny = SUBLANES + ds
    yext_ref[:, 0:SUBLANES - hs, :] = jnp.zeros((dbt, SUBLANES - hs, c), F32)
    yext_ref[:, SUBLANES - hs:SUBLANES, :] = ss_ref[...]
    yext_ref[:, SUBLANES:ny, :] = y
    yflat = yext_ref[...].reshape(dbt * ny, c)
    conv = y * w_ref[SCONV_W - 1:SCONV_W, :]
    for back in range(1, SCONV_W):
        shifted = pltpu.roll(yflat, back, axis=0).reshape(dbt, ny, c)[:, SUBLANES:ny, :]
        conv = conv + shifted * w_ref[SCONV_W - 1 - back:SCONV_W - back, :]
    oc_ref[...] = (bg * conv).reshape(dbt * ds, c)
    ss_out_ref[...] = yext_ref[:, ny - hs:ny, :]

    base = 2 * SUBLANES
    npd = base + ds
    pext_ref[:, 0:base - POOL_HIST, :] = jnp.zeros((dbt, base - POOL_HIST, c), F32)
    pext_ref[:, base - POOL_HIST:base, :] = sp_ref[...]
    pext_ref[:, base:npd, :] = pd
    gc = c // len(POOL_WINDOWS)
    means = []
    for gi, window in enumerate(POOL_WINDOWS):
        win = pext_ref[:, :, gi * gc:(gi + 1) * gc].reshape(dbt * npd, gc)
        step = 1
        while step < window:
            win = win + pltpu.roll(win, step, axis=0)
            step *= 2
        means.append(win.reshape(dbt, npd, gc)[:, base:npd, :] / float(window))
    d_ref[...] = (jnp.concatenate(means, axis=-1) - pd).reshape(dbt * ds, c)
    sp_out_ref[...] = pext_ref[:, npd - POOL_HIST:npd, :]


def _sample_odd(state_s, state_p, layer, y, bg, pd, w, *, s0, db, ds, dbt):
    n, c = y.shape
    rows = dbt * ds
    row_in = pl.BlockSpec((rows, c), lambda i: (s0 // rows + i, 0))
    row_out = pl.BlockSpec((rows, c), lambda i: (i, 0))
    hs = state_s.shape[2]
    hp = state_p.shape[2]
    return pl.pallas_call(
        functools.partial(_sample_odd_kernel, ds=ds),
        grid=(db // dbt,),
        in_specs=[pl.BlockSpec((dbt, None, hs, c), lambda i: (i, layer, 0, 0)),
                  pl.BlockSpec((dbt, None, hp, c), lambda i: (i, layer, 0, 0)),
                  row_in, row_in, row_in, _full(w.shape)],
        out_specs=[row_out, row_out, pl.BlockSpec((dbt, hs, c), lambda i: (i, 0, 0)),
                   pl.BlockSpec((dbt, hp, c), lambda i: (i, 0, 0))],
        out_shape=[jax.ShapeDtypeStruct((db * ds, c), F32), jax.ShapeDtypeStruct((db * ds, c), F32),
                   jax.ShapeDtypeStruct((db, hs, c), F32), jax.ShapeDtypeStruct((db, hp, c), F32)],
        scratch_shapes=[pltpu.VMEM((dbt, SUBLANES + ds, c), F32), pltpu.VMEM((dbt, 2 * SUBLANES + ds, c), F32)],
        compiler_params=_params("parallel"),
        name="sample_sconv_pool",
    )(state_s, state_p, y, bg, pd, w)


def kernel(x_prompt, x_sample, cache_k, cache_v, cache_logf, page_table, state_conv_a, state_sconv, state_pool, meta_tokens, w_in_even, b_forget, w_dw_a, b_dw_a, ln_a_g, ln_a_b, w_out_even, w_in_odd, w_sconv, w_pool_mix, pool_scale, w_out_odd, ln_mix_g, ln_mix_b, w_ffn_gate, w_ffn_up, w_ffn_down, ln_ffn_g, ln_ffn_b):
    nb, seq, d = x_prompt.shape
    db, ds, _ = x_sample.shape
    depth = w_ffn_gate.shape[0]
    alpha = float((2 * depth) ** 0.25)
    ca = state_conv_a.shape[-1]
    att = N_HEADS * HEAD_DIM
    assert ds == SUBLANES and cache_k.shape[3] == N_HEADS and cache_k.shape[4] == HEAD_DIM
    assert state_conv_a.shape[2] == CONV_W - 1 and state_pool.shape[2] == POOL_HIST
    assert meta_tokens.shape[0] == N_META

    s_real = N_META + seq
    fp = (-s_real) % SEQ_TILE
    if fp < CONV_BACK:
        fp += SEQ_TILE
    sp = fp + s_real
    s0 = nb * sp
    n = s0 + db * ds
    tm = next(t for t in (512, 256, 128, 64, 32, 16, 8) if s0 % t == 0 and (db * ds) % t == 0)
    tm_in = next(t for t in (1024, 512, 256, 128, 64, 32, 16, 8) if n % t == 0)
    dbt = next(t for t in (8, 4, 2, 1) if db % t == 0)
    direct_out = seq % SEQ_TILE == 0 and (db * ds) % SEQ_TILE == 0

    head = jnp.concatenate([jnp.zeros((fp, d), F32), meta_tokens.astype(F32)], axis=0)
    x = jnp.concatenate([piece for bi in range(nb) for piece in (head, x_prompt[bi])]
                        + [x_sample.reshape(db * ds, d)], axis=0)

    n_phys = cache_k.shape[0]
    n_even = cache_k.shape[1]
    page = cache_k.shape[2]
    ck = jnp.transpose(cache_k, (0, 1, 3, 4, 2)).reshape(n_phys, n_even, att, page)
    cv = jnp.transpose(cache_v, (0, 1, 3, 4, 2)).reshape(n_phys, n_even, att, page)
    clft = jnp.swapaxes(cache_logf, 2, 3)

    row2 = lambda v: v.reshape(1, -1).astype(F32)

    def prompt_rows(arr, lo, hi):
        return jnp.stack([arr[bi * sp + lo:bi * sp + hi] for bi in range(nb)], axis=0)

    def sample_rows(arr):
        return arr[s0:].reshape(db, ds, -1)

    wg_all, wu_all, wd_all = (w.astype(BF16) for w in (w_ffn_gate, w_ffn_up, w_ffn_down))
    wo_even, wo_odd = w_out_even.astype(BF16), w_out_odd.astype(BF16)
    wi_even = jnp.pad(w_in_even.astype(BF16), ((0, 0), (0, 0), (0, LANES - N_HEADS)))
    wi_odd = w_in_odd.astype(BF16)

    def tail(name, xin, m1p, m2p, m1s, m2s, p1, p2, w_out_all, layer):
        last = direct_out and layer == depth - 1
        return _layer_tail(name, xin, m1p, m2p, m1s, m2s, p1, p2, w_out_all,
                           row2(ln_mix_g[layer]), row2(ln_mix_b[layer]), wg_all, wu_all, wd_all,
                           row2(ln_ffn_g[layer]), row2(ln_ffn_b[layer]), even=layer % 2 == 0,
                           mix_idx=layer // 2, layer=layer, tm=SEQ_TILE if last else tm, alpha=alpha,
                           nb=nb, sp=sp, fp=fp, out_rows=(seq, db * ds) if last else None)

    prev_kv = []
    lfp, cap, scp, plp = [], [], [], []
    ks_, vs_, lfs, cas, scs, pls = [], [], [], [], [], []
    for layer in range(depth):
        i = layer // 2
        if layer % 2 == 0:
            bf = jnp.pad(b_forget[i].astype(F32), (0, LANES - N_HEADS)).reshape(1, LANES)
            a, q, k, v, lf = _even_in(x, wi_even, i, bf, tm=tm_in, ca=ca, att=att)
            w_dw, b_dw = w_dw_a[i].astype(F32), row2(b_dw_a[i])
            conv_s, st_a = _sample_conv(state_conv_a.astype(F32), i, a, w_dw, b_dw, s0=s0, db=db, ds=ds, dbt=dbt)
            qa, ka, vat, conv_p, *caches = _fox_prep(q, k, v, lf, a, w_dw, b_dw, prev_kv, i == n_even - 1,
                                                     nb=nb, sp=sp, fp=fp)
            prev_kv.append((k, v))
            at_p = _flash(qa, ka, vat, hg=FLASH_HEADS)
            at_s = _sample_attn(page_table, i, q, k, v, lf, ck, cv, clft, s0=s0, db=db, ds=ds)
            x = tail("even_tail", x, conv_p, at_p, conv_s, at_s, row2(ln_a_g[i]), row2(ln_a_b[i]), wo_even, layer)
            lfp.append(prompt_rows(lf, fp, sp)[..., :N_HEADS])
            cap.append(prompt_rows(a, sp - (CONV_W - 1), sp))
            ks_.append(sample_rows(k).reshape(db, ds, N_HEADS, HEAD_DIM))
            vs_.append(sample_rows(v).reshape(db, ds, N_HEADS, HEAD_DIM))
            lfs.append(sample_rows(lf)[..., :N_HEADS])
            cas.append(st_a)
        else:
            c = state_sconv.shape[-1]
            y, bg, pd = _odd_in(x, wi_odd, i, tm=tm_in, c=c)
            oc_p, dd_p = _prompt_odd(y, bg, pd, w_sconv[i].astype(F32), nb=nb, sp=sp, fp=fp)
            oc_s, dd_s, st_s, st_p = _sample_odd(state_sconv.astype(F32), state_pool.astype(F32), i, y, bg, pd,
                                                 w_sconv[i].astype(F32), s0=s0, db=db, ds=ds, dbt=dbt)
            wm = jax.scipy.linalg.block_diag(*[w_pool_mix[i, g] for g in range(w_pool_mix.shape[1])]).astype(BF16)
            x = tail("odd_tail", x, oc_p, dd_p, oc_s, dd_s, wm, row2(pool_scale[i]), wo_odd, layer)
            scp.append(prompt_rows(y, sp - (SCONV_W - 1), sp))
            plp.append(prompt_rows(pd, sp - POOL_HIST, sp))
            scs.append(st_s)
            pls.append(st_p)

    if direct_out:
        y_prompt, y_sample = x[0], x[1].reshape(db, ds, d)
    else:
        y_prompt, y_sample = prompt_rows(x, fp + N_META, sp), sample_rows(x)
    st = lambda xs: jnp.stack(xs, axis=1)
    kp, vp = (jnp.transpose(c.reshape(nb, n_even, N_HEADS, HEAD_DIM, s_real), (0, 1, 4, 2, 3)) for c in caches)
    return (y_prompt, y_sample, kp, vp, st(lfp), st(cap), st(scp), st(plp),
            st(ks_), st(vs_), st(lfs), st(cas), st(scs), st(pls))
```

```python
import functools

import jax
import jax.numpy as jnp
from jax import lax
from jax.experimental import pallas as pl
from jax.experimental.pallas import tpu as pltpu

N_META = 16
N_HEADS = 8
HEAD_DIM = 64
CONV_W = 31
SCONV_W = 3
POOL_WINDOWS = (2, 4, 8, 16)
POOL_HIST = max(POOL_WINDOWS) - 1
LN_EPS = 1e-5
SEQ_TILE = 256
LANES = 128
SUBLANES = 8
CONV_BACK = -(-(CONV_W - 1) // SUBLANES) * SUBLANES
C_TERMS = 3
AUX_ONES = C_TERMS * N_HEADS
FLASH_HEADS = 8
V_ROWS = HEAD_DIM + 2 * SUBLANES
VMEM_LIMIT = 56 * 1024 * 1024
NEG = -1e30
LOG2E = 1.4426950408889634
F32 = jnp.float32
BF16 = jnp.bfloat16
NT_DIMS = (((1,), (1,)), ((), ()))


def _params(*sem):
    return pltpu.CompilerParams(dimension_semantics=sem, vmem_limit_bytes=VMEM_LIMIT)


def _dot(a, b):
    return jnp.dot(a, b, preferred_element_type=F32)


def _ln(z, g, b):
    mu = jnp.mean(z, axis=-1, keepdims=True)
    zc = z - mu
    var = jnp.mean(zc * zc, axis=-1, keepdims=True)
    return zc * lax.rsqrt(var + LN_EPS) * g + b


def _silu(x):
    return x * jax.nn.sigmoid(x)


def _full(shape):
    return pl.BlockSpec(shape, lambda *_: (0,) * len(shape))


def _slab(a, idx):
    return pl.BlockSpec((None,) + a.shape[1:], lambda *_: (idx,) + (0,) * (a.ndim - 1),
                        pipeline_mode=pl.Buffered(1))


def _even_in_kernel(x_ref, w_ref, bf_ref, a_ref, q_ref, k_ref, v_ref, lf_ref, *, ca, att):
    xb = x_ref[...].astype(BF16)

    def mm(lo, hi):
        return _dot(xb, w_ref[:, lo:hi])

    u = mm(0, ca)
    g = mm(ca, 2 * ca)
    a_ref[...] = u * jax.nn.sigmoid(g)
    o = 2 * ca
    q_ref[...] = mm(o, o + att) * (HEAD_DIM ** -0.5)
    k_ref[...] = mm(o + att, o + 2 * att)
    v_ref[...] = mm(o + 2 * att, o + 3 * att)
    z = mm(o + 3 * att, o + 3 * att + LANES) + bf_ref[...]
    lf = jnp.minimum(z, 0.0) - jnp.log1p(jnp.exp(-jnp.abs(z)))
    lane = lax.broadcasted_iota(jnp.int32, lf.shape, 1)
    lf_ref[...] = jnp.where(lane < N_HEADS, lf, 0.0)


def _even_in(x, w, idx, bf, *, tm, ca, att):
    n, d = x.shape
    row = lambda c: pl.BlockSpec((tm, c), lambda i: (i, 0))
    return pl.pallas_call(
        functools.partial(_even_in_kernel, ca=ca, att=att),
        grid=(n // tm,),
        in_specs=[row(d), _slab(w, idx), _full(bf.shape)],
        out_specs=[row(ca), row(att), row(att), row(att), row(LANES)],
        out_shape=[jax.ShapeDtypeStruct((n, c), F32) for c in (ca, att, att, att, LANES)],
        compiler_params=_params("parallel"),
        name="even_in_proj",
    )(x, w, bf)


def _odd_in_kernel(x_ref, w_ref, y_ref, bg_ref, pd_ref, *, c):
    xb = x_ref[...].astype(BF16)
    hc = _dot(xb, w_ref[:, 0:c])
    bg_ref[...] = _dot(xb, w_ref[:, c:2 * c])
    cg = _dot(xb, w_ref[:, 2 * c:3 * c])
    y_ref[...] = cg * hc
    pd_ref[...] = _dot(xb, w_ref[:, 3 * c:4 * c])


def _odd_in(x, w, idx, *, tm, c):
    n, d = x.shape
    row = lambda cc: pl.BlockSpec((tm, cc), lambda i: (i, 0))
    return pl.pallas_call(
        functools.partial(_odd_in_kernel, c=c),
        grid=(n // tm,),
        in_specs=[row(d), _slab(w, idx)],
        out_specs=[row(c)] * 3,
        out_shape=[jax.ShapeDtypeStruct((n, c), F32)] * 3,
        compiler_params=_params("parallel"),
        name="odd_in_proj",
    )(x, w)


def _tail_kernel(x_ref, m1p_ref, m2p_ref, m1s_ref, m2s_ref, p1_ref, p2_ref, w_ref, g1_ref, b1_ref,
                 wg_ref, wu_ref, wd_ref, g2_ref, b2_ref, *rest, even, alpha, chunk, nb, sp, fp, npt, lead):
    i = pl.program_id(0)
    is_sample = i >= npt
    m1 = jnp.where(is_sample, m1s_ref[...], m1p_ref[...])
    m2 = jnp.where(is_sample, m2s_ref[...], m2p_ref[...])
    if even:
        m1 = _silu(_ln(m1, p1_ref[...], p2_ref[...]))
    else:
        m2 = _dot(m2.astype(BF16), p1_ref[...]) * p2_ref[...]
    ca = m1.shape[1]
    mix = _dot(m1.astype(BF16), w_ref[0:ca, :]) + _dot(m2.astype(BF16), w_ref[ca:, :])
    x1 = _ln(alpha * x_ref[...] + mix, g1_ref[...], b1_ref[...])

    h_ref = rest[-1]
    xb = x1.astype(BF16)
    for c in range(0, wg_ref.shape[1], chunk):
        gate = _dot(xb, wg_ref[:, c:c + chunk])
        up = _dot(xb, wu_ref[:, c:c + chunk])
        h_ref[:, c:c + chunk] = (_silu(gate) * up).astype(BF16)
    out = _ln(alpha * x1 + _dot(h_ref[...], wd_ref[...]), g2_ref[...], b2_ref[...])

    tm = x1.shape[0]
    if lead is None:
        r = i * tm + lax.broadcasted_iota(jnp.int32, (tm, 1), 0)
        keep = jnp.ones((tm, 1), F32)
        for bi in range(nb):
            keep = jnp.where((r >= bi * sp) & (r < bi * sp + fp), 0.0, keep)
        rest[0][...] = out * keep
    else:
        yp_ref, ys_ref = rest[:2]

        @pl.when(jnp.logical_and(jnp.logical_not(is_sample), i % (sp // tm) >= lead))
        def _():
            yp_ref[...] = out

        @pl.when(is_sample)
        def _():
            ys_ref[...] = out


def _layer_tail(name, x, m1p, m2p, m1s, m2s, p1, p2, w, g1, b1, wg, wu, wd, g2, b2, *, even, mix_idx, layer, tm,
                alpha, nb, sp, fp, out_rows=None):
    n, d = x.shape
    s0 = m1p.shape[0]
    dff = wg.shape[2]
    chunk = SEQ_TILE if dff % SEQ_TILE == 0 else dff
    npt = s0 // tm
    row = lambda c: pl.BlockSpec((tm, c), lambda i: (i, 0))
    prow = lambda c: pl.BlockSpec((tm, c), lambda i: (jnp.minimum(i, npt - 1), 0))
    srow = lambda c: pl.BlockSpec((tm, c), lambda i: (jnp.maximum(i - npt, 0), 0))
    once = lambda a: pl.BlockSpec(a.shape, lambda i: (0,) * a.ndim, pipeline_mode=pl.Buffered(1))
    consts = (p1, p2, w, g1, b1, wg, wu, wd, g2, b2)
    const_specs = [once(p1), once(p2), _slab(w, mix_idx), once(g1), once(b1),
                   _slab(wg, layer), _slab(wu, layer), _slab(wd, layer), once(g2), once(b2)]
    if out_rows is None:
        lead = None
        out_specs = row(d)
        out_shape = jax.ShapeDtypeStruct((n, d), F32)
    else:
        seq, db_rows = out_rows
        tps = sp // tm
        lead = (sp - seq) // tm
        out_specs = [
            pl.BlockSpec((None, tm, d), lambda i: (jnp.minimum(i // tps, nb - 1),
                                                   jnp.where(i < npt, jnp.maximum(i % tps - lead, 0), tps - lead - 1),
                                                   0)),
            srow(d)]
        out_shape = [jax.ShapeDtypeStruct((nb, seq, d), F32), jax.ShapeDtypeStruct((db_rows, d), F32)]
    return pl.pallas_call(
        functools.partial(_tail_kernel, even=even, alpha=alpha, chunk=chunk, nb=nb, sp=sp, fp=fp, npt=npt, lead=lead),
        grid=(n // tm,),
        in_specs=[row(d), prow(m1p.shape[1]), prow(m2p.shape[1]), srow(m1s.shape[1]), srow(m2s.shape[1])]
        + const_specs,
        out_specs=out_specs,
        out_shape=out_shape,
        scratch_shapes=[pltpu.VMEM((tm, dff), BF16)],
        compiler_params=_params("arbitrary" if out_rows else "parallel"),
        name=name,
    )(x, m1p, m2p, m1s, m2s, *consts)


def _conv_tile(win, w_rows):
    rows = win.shape[0]
    t = rows - CONV_BACK
    lead = CONV_BACK - (CONV_W - 1)
    acc = None
    for r in range(SUBLANES):
        rolled = win if r == 0 else pltpu.roll(win, rows - r, axis=0)
        for m in range(CONV_BACK // SUBLANES + 1):
            j = SUBLANES * m + r - lead
            if 0 <= j < CONV_W:
                term = rolled[SUBLANES * m:SUBLANES * m + t] * w_rows[j]
                acc = term if acc is None else acc + term
    return acc


def _sample_conv_kernel(st_ref, a_ref, w_ref, b_ref, o_ref, st_out_ref, ext_ref, *, ds):
    dbt, hist, c = st_ref.shape
    off = CONV_BACK - hist
    rows = CONV_BACK + ds
    ext_ref[:, 0:off, :] = jnp.zeros((dbt, off, c), F32)
    ext_ref[:, off:CONV_BACK, :] = st_ref[...]
    ext_ref[:, CONV_BACK:rows, :] = a_ref[...].reshape(dbt, ds, c)
    flat = ext_ref[...].reshape(dbt * rows, c)
    acc = None
    for r in range(SUBLANES):
        shifted = (flat if r == 0 else pltpu.roll(flat, dbt * rows - r, axis=0)).reshape(dbt, rows, c)
        for m in range(rows // SUBLANES):
            j = SUBLANES * m + r - off
            if 0 <= j < CONV_W:
                term = shifted[:, SUBLANES * m:SUBLANES * m + ds, :] * w_ref[j:j + 1, :]
                acc = term if acc is None else acc + term
    o_ref[...] = (acc + b_ref[...]).reshape(dbt * ds, c)
    st_out_ref[...] = ext_ref[:, off + ds:rows, :]


def _sample_conv(state, layer, a, w, b, *, s0, db, ds, dbt):
    n, c = a.shape
    hist = state.shape[2]
    rows = dbt * ds
    return pl.pallas_call(
        functools.partial(_sample_conv_kernel, ds=ds),
        grid=(db // dbt,),
        in_specs=[pl.BlockSpec((dbt, None, hist, c), lambda i: (i, layer, 0, 0)),
                  pl.BlockSpec((rows, c), lambda i: (s0 // rows + i, 0)), _full(w.shape), _full(b.shape)],
        out_specs=[pl.BlockSpec((rows, c), lambda i: (i, 0)), pl.BlockSpec((dbt, hist, c), lambda i: (i, 0, 0))],
        out_shape=[jax.ShapeDtypeStruct((db * ds, c), F32), jax.ShapeDtypeStruct((db, hist, c), F32)],
        scratch_shapes=[pltpu.VMEM((dbt, CONV_BACK + ds, c), F32)],
        compiler_params=_params("parallel"),
        name="sample_conv",
    )(state, a, w, b)


def _fox_prep_kernel(q_ref, k_ref, v_ref, lf_ref, a_ref, wdw_ref, bdw_ref, *rest, fp, ns, n_prev, emit):
    prev, rest = rest[:2 * n_prev], rest[2 * n_prev:]
    qa_ref, ka_ref, vat_ref, cv_ref = rest[:4]
    carry_ref, ah_ref = rest[6:8] if emit else rest[4:6]
    s = pl.program_id(1)
    t = SEQ_TILE

    @pl.when(s < ns)
    def _():
        _fox_operands(s, q_ref, k_ref, v_ref, lf_ref, qa_ref, ka_ref, vat_ref, carry_ref, fp)

        @pl.when(s == 0)
        def _():
            ah_ref[...] = jnp.zeros_like(ah_ref)

        for cb in range(a_ref.shape[1] // LANES):
            cs = slice(cb * LANES, (cb + 1) * LANES)
            win = jnp.concatenate([ah_ref[:, cs], a_ref[:, cs]], axis=0)
            w_rows = [wdw_ref[j:j + 1, cs] for j in range(CONV_W)]
            cv_ref[:, cs] = _conv_tile(win, w_rows) + bdw_ref[:, cs]
        ah_ref[...] = a_ref[t - CONV_BACK:t, :]

    if not emit:
        return
    kt_ref, vt_ref = rest[4:6]
    kc_ref, vc_ref = rest[8:10]

    @pl.when(s == 0)
    def _():
        kc_ref[...] = jnp.zeros_like(kc_ref)
        vc_ref[...] = jnp.zeros_like(vc_ref)

    k_srcs = list(prev[0::2]) + [k_ref]
    v_srcs = list(prev[1::2]) + [v_ref]
    for srcs, held, dst in ((k_srcs, kc_ref, kt_ref), (v_srcs, vc_ref, vt_ref)):
        for li, src in enumerate(srcs):
            rows = jnp.concatenate([held[li], src[0:fp, :]], axis=0)
            for c in range(src.shape[1] // LANES):
                dst[li, c * LANES:(c + 1) * LANES, :] = rows[:, c * LANES:(c + 1) * LANES].T
            held[li] = src[fp:t, :]


def _fox_operands(s, q_ref, k_ref, v_ref, lf_ref, qa_ref, ka_ref, vat_ref, carry_ref, fp):
    @pl.when(s == 0)
    def _():
        carry_ref[...] = jnp.zeros_like(carry_ref)

    t = SEQ_TILE
    row = lax.broadcasted_iota(jnp.int32, (t, LANES), 0)
    lane = lax.broadcasted_iota(jnp.int32, (t, LANES), 1)
    c = jnp.where(s * t + row >= fp, lf_ref[...], 0.0)
    sh = 1
    while sh < t:
        c = c + jnp.where(row >= sh, pltpu.roll(c, sh, axis=0), 0.0)
        sh *= 2
    c = c + carry_ref[0:1, :]
    carry_ref[0:1, :] = c[t - 1:t, :]

    c = c * LOG2E
    c1 = c.astype(BF16).astype(F32)
    r1 = c - c1
    c2 = r1.astype(BF16).astype(F32)
    c3 = (r1 - c2).astype(BF16).astype(F32)
    g23 = jnp.where(lane < 2 * N_HEADS, pltpu.roll(c2, N_HEADS, axis=1),
                    jnp.where(lane < 3 * N_HEADS, pltpu.roll(c3, 2 * N_HEADS, axis=1), 0.0))
    g = jnp.where(lane < N_HEADS, c1, g23)
    gk = jnp.where(lane < N_HEADS, jnp.where(s * t + row >= fp, c1, -NEG), g23)
    ones_grp = (lane >= HEAD_DIM + AUX_ONES) & (lane < HEAD_DIM + 2 * AUX_ONES)
    ck = [jnp.where((lane >= HEAD_DIM) & (lane < HEAD_DIM + AUX_ONES), -pltpu.roll(gk, HEAD_DIM, axis=1),
                    jnp.where(ones_grp, 1.0, 0.0))]
    cq = [jnp.where(ones_grp, pltpu.roll(g, HEAD_DIM + AUX_ONES, axis=1), 0.0)]
    ck.append(pltpu.roll(ck[0], HEAD_DIM, axis=1))
    cq.append(pltpu.roll(cq[0], HEAD_DIM, axis=1))

    lane1 = lax.broadcasted_iota(jnp.int32, (1, LANES), 1)
    for h in range(N_HEADS):
        odd = h % 2
        pr = h // 2
        base = 0 if odd else HEAD_DIM
        a = lane1 - base
        mine = (lane1 & (N_HEADS - 1)) == h
        data = ((lane1 >= HEAD_DIM) if odd else (lane1 < HEAD_DIM)).astype(F32)
        sel1 = ((a >= 0) & (a < AUX_ONES) & mine).astype(F32)
        selc = ((a >= AUX_ONES) & (a < 2 * AUX_ONES) & mine).astype(F32)
        sl = slice(pr * LANES, (pr + 1) * LANES)
        qa_ref[h] = (q_ref[:, sl] * (data * LOG2E) + (cq[odd] * selc + sel1)).T.astype(BF16)
        ka_ref[h] = (k_ref[:, sl] * data + ck[odd]).astype(BF16)
    ones = jnp.ones((V_ROWS - HEAD_DIM, t), BF16)
    for pr in range(N_HEADS // 2):
        v_t = v_ref[:, pr * LANES:(pr + 1) * LANES].T.astype(BF16)
        vat_ref[2 * pr] = jnp.concatenate([v_t[0:HEAD_DIM], ones], axis=0)
        vat_ref[2 * pr + 1] = jnp.concatenate([v_t[HEAD_DIM:], ones], axis=0)


def _fox_prep(q, k, v, lf, a, w_dw, b_dw, prev_kv, emit, *, nb, sp, fp):
    ns = sp // SEQ_TILE
    att = q.shape[1]
    ca = a.shape[1]
    t = SEQ_TILE
    last = ns - 1
    row = lambda c: pl.BlockSpec((t, c), lambda bi, si: (bi * ns + jnp.minimum(si, last), 0))
    hm = pl.BlockSpec((None, N_HEADS, t, LANES), lambda bi, si: (bi, 0, jnp.minimum(si, last), 0))
    hm_t = lambda r: pl.BlockSpec((None, N_HEADS, r, t), lambda bi, si: (bi, 0, 0, jnp.minimum(si, last)))
    out_specs = [hm_t(LANES), hm, hm_t(V_ROWS), row(ca)]
    out_shape = [jax.ShapeDtypeStruct((nb, N_HEADS, LANES, sp), BF16),
                 jax.ShapeDtypeStruct((nb, N_HEADS, sp, LANES), BF16),
                 jax.ShapeDtypeStruct((nb, N_HEADS, V_ROWS, sp), BF16), jax.ShapeDtypeStruct((nb * sp, ca), F32)]
    scratch = [pltpu.VMEM((SUBLANES, LANES), F32), pltpu.VMEM((CONV_BACK, ca), F32)]
    n_prev = len(prev_kv) if emit else 0
    if emit:
        assert 0 < fp < t and fp % SUBLANES == 0
        n_layers = n_prev + 1
        cache = pl.BlockSpec((None, n_layers, att, t), lambda bi, si: (bi, 0, 0, jnp.maximum(si - 1, 0)))
        out_specs += [cache, cache]
        out_shape += [jax.ShapeDtypeStruct((nb, n_layers, att, sp - fp), F32)] * 2
        scratch += [pltpu.VMEM((n_layers, t - fp, att), F32)] * 2
    prev = [a for kv in prev_kv for a in kv] if emit else []
    return pl.pallas_call(
        functools.partial(_fox_prep_kernel, fp=fp, ns=ns, n_prev=n_prev, emit=emit),
        grid=(nb, ns + 1 if emit else ns),
        in_specs=[row(att), row(att), row(att), row(LANES), row(ca), _full(w_dw.shape), _full(b_dw.shape)]
        + [row(att)] * len(prev),
        out_specs=out_specs,
        out_shape=out_shape,
        scratch_shapes=scratch,
        compiler_params=_params("parallel", "arbitrary"),
        name="fox_prep",
    )(q, k, v, lf, a, w_dw, b_dw, *prev)


def _flash_kernel(q_ref, k_ref, vt_ref, o_ref, s_ref, p_ref, acc_ref, *, hg):
    i = pl.program_id(2)
    t = SEQ_TILE

    def scores_to(j, slot):
        ks = pl.multiple_of(j * t, t)
        for h in range(hg):
            s_ref[slot, h] = _dot(k_ref[h, pl.ds(ks, t), :], q_ref[h])

    def accumulate(j, als):
        ks = pl.multiple_of(j * t, t)
        pvs = [_dot(vt_ref[h, :, pl.ds(ks, t)], p_ref[h]) for h in range(hg)]
        for h in range(hg):
            acc_ref[h] = als[h] * acc_ref[h] + pvs[h]

    def softmax(slot, ms, masked):
        new_m, als = [], []
        for h in range(hg):
            s = s_ref[slot, h]
            if masked:
                visible = (lax.broadcasted_iota(jnp.int32, (t, t), 0) <= lax.broadcasted_iota(jnp.int32, (t, t), 1))
                s = jnp.where(visible, s, NEG)
            mn = jnp.maximum(ms[h], jnp.max(s, axis=0, keepdims=True))
            als.append(jnp.exp2(ms[h] - mn))
            new_m.append(mn)
            p_ref[h] = jnp.exp2(s - mn).astype(BF16)
        return tuple(new_m), tuple(als)

    def stage(j, slot, state):
        ms, als = state
        scores_to(j + 1, 1 - slot)
        accumulate(jnp.maximum(j - 1, 0), als)
        return softmax(slot, ms, False)

    def pair(jj, state):
        return stage(2 * jj + 1, 1, stage(2 * jj, 0, state))

    def finish(slot, state):
        ms, als = state
        accumulate(jnp.maximum(i - 1, 0), als)
        _, als = softmax(slot, ms, True)
        accumulate(i, als)
        for pr in range(hg // 2):
            halves = [acc_ref[h, 0:HEAD_DIM, :] * (1.0 / acc_ref[h, HEAD_DIM:HEAD_DIM + 1, :])
                      for h in (2 * pr, 2 * pr + 1)]
            o_ref[:, pr * LANES:(pr + 1) * LANES] = jnp.concatenate(halves, axis=0).T

    p_ref[...] = jnp.zeros_like(p_ref)
    acc_ref[...] = jnp.zeros_like(acc_ref)
    scores_to(0, 0)
    state = ((jnp.full((1, t), NEG, F32),) * hg, (jnp.ones((1, t), F32),) * hg)
    state = lax.fori_loop(0, i // 2, pair, state)
    odd = i % 2 == 1
    state = lax.cond(odd, lambda st: stage(i - 1, 0, st), lambda st: st, state)
    pl.when(odd)(lambda: finish(1, state))
    pl.when(jnp.logical_not(odd))(lambda: finish(0, state))


def _flash(qat, ka, vat, *, hg):
    nb, _, sp, _ = ka.shape
    ns = sp // SEQ_TILE
    return pl.pallas_call(
        functools.partial(_flash_kernel, hg=hg),
        grid=(nb, N_HEADS // hg, ns),
        in_specs=[pl.BlockSpec((None, hg, LANES, SEQ_TILE), lambda bi, gi, qi: (bi, gi, 0, qi)),
                  pl.BlockSpec((None, hg, sp, LANES), lambda bi, gi, qi: (bi, gi, 0, 0)),
                  pl.BlockSpec((None, hg, V_ROWS, sp), lambda bi, gi, qi: (bi, gi, 0, 0))],
        out_specs=pl.BlockSpec((SEQ_TILE, hg * HEAD_DIM), lambda bi, gi, qi: (bi * ns + qi, gi)),
        out_shape=jax.ShapeDtypeStruct((nb * sp, N_HEADS * HEAD_DIM), F32),
        scratch_shapes=[pltpu.VMEM((2, hg, SEQ_TILE, SEQ_TILE), F32), pltpu.VMEM((hg, SEQ_TILE, SEQ_TILE), BF16),
                        pltpu.VMEM((hg, V_ROWS, SEQ_TILE), F32)],
        compiler_params=_params("parallel", "parallel", "arbitrary"),
        name="fox_flash",
    )(qat, ka, vat)


def _sample_attn_kernel(pt_ref, q_ref, kn_ref, vn_ref, lfn_ref, *rest, n_pages, page, eb, ds):
    del pt_ref
    o_ref = rest[3 * eb * n_pages]
    c = q_ref.shape[1]
    rows = ds * N_HEADS
    groups = [list(range(g, min(g + 2, n_pages))) for g in range(0, n_pages, 2)]
    past = n_pages * page
    sub = lax.broadcasted_iota(jnp.int32, (N_HEADS, c), 0)
    lane = lax.broadcasted_iota(jnp.int32, (N_HEADS, c), 1)
    hm = ((lane >= sub * HEAD_DIM) & (lane < (sub + 1) * HEAD_DIM)).astype(F32)
    hm_t = jnp.concatenate([hm] * ds, axis=0)
    pad = jnp.zeros((LANES - ds, c), F32)

    def refs_of(kind, e):
        return rest[(kind * eb + e) * n_pages:(kind * eb + e + 1) * n_pages]

    def scores(e):
        sl = slice(e * ds, (e + 1) * ds)
        q = q_ref[sl, :]
        qbd = jnp.concatenate([jnp.broadcast_to(q[t:t + 1, :], (N_HEADS, c)) * hm for t in range(ds)],
                              axis=0).astype(BF16)
        k_refs, lf_refs = refs_of(0, e), refs_of(2, e)
        s_parts = []
        for grp in groups:
            kp = jnp.concatenate([k_refs[r][...] for r in grp], axis=1).astype(BF16)
            s_parts.append(_dot(qbd, kp))
        s_past = jnp.concatenate(s_parts, axis=1)

        lf = jnp.concatenate([r[...] for r in lf_refs], axis=1)
        lane_p = lax.broadcasted_iota(jnp.int32, (N_HEADS, past), 1)
        suf = lf
        sh = 1
        while sh < past:
            suf = suf + jnp.where(lane_p < past - sh, pltpu.roll(suf, past - sh, axis=1), 0.0)
            sh *= 2
        s_past = s_past + jnp.concatenate([suf - lf] * ds, axis=0)

        kn = jnp.concatenate([kn_ref[sl, :], pad], axis=0).astype(BF16)
        s_new = lax.dot_general(qbd, kn, NT_DIMS, preferred_element_type=F32)
        cn = lfn_ref[sl, :]
        sub_n = lax.broadcasted_iota(jnp.int32, cn.shape, 0)
        sh = 1
        while sh < ds:
            cn = cn + jnp.where(sub_n >= sh, pltpu.roll(cn, sh, axis=0), 0.0)
            sh *= 2
        cn_t = jnp.concatenate([cn, jnp.zeros((LANES - ds, LANES), F32)], axis=0).T[0:N_HEADS, :]
        rown = lax.broadcasted_iota(jnp.int32, (rows, LANES), 0)
        u = lax.broadcasted_iota(jnp.int32, (rows, LANES), 1)
        s_new = jnp.where(u * N_HEADS <= rown, s_new - jnp.concatenate([cn_t] * ds, axis=0), NEG)
        return s_past, s_new

    def softmax(s_past, s_new):
        m = jnp.maximum(jnp.max(s_past, axis=1, keepdims=True), jnp.max(s_new, axis=1, keepdims=True))
        p_past = jnp.exp(s_past - m)
        p_new = jnp.exp(s_new - m)
        l = jnp.sum(p_past, axis=1, keepdims=True) + jnp.sum(p_new, axis=1, keepdims=True)
        return p_past.astype(BF16), p_new.astype(BF16), l

    def output(e, p_past, p_new, l):
        sl = slice(e * ds, (e + 1) * ds)
        v_refs = refs_of(1, e)
        vn = jnp.concatenate([vn_ref[sl, :], pad], axis=0).astype(BF16)
        o = _dot(p_new, vn)
        for grp in groups:
            vp = jnp.concatenate([v_refs[r][...] for r in grp], axis=1).astype(BF16)
            lo = grp[0] * page
            o = o + lax.dot_general(p_past[:, lo:lo + len(grp) * page], vp, NT_DIMS, preferred_element_type=F32)
        o = o * (1.0 / l) * hm_t
        rr = lax.broadcasted_iota(jnp.int32, (ds, rows), 1)
        tt = lax.broadcasted_iota(jnp.int32, (ds, rows), 0)
        pick = ((rr >= tt * N_HEADS) & (rr < (tt + 1) * N_HEADS)).astype(BF16)
        o_ref[sl, :] = _dot(pick, o.astype(BF16))

    ss = [scores(e) for e in range(eb)]
    ps = [softmax(*s) for s in ss]
    for e in range(eb):
        output(e, *ps[e])


def _sample_attn(page_table, layer, q, k, v, lf, cache_k, cache_v, cache_lft, *, s0, db, ds):
    n, c = q.shape
    n_pages = page_table.shape[1]
    page = cache_k.shape[3]
    eb = 2 if db % 2 == 0 else 1
    rows = eb * ds
    row = lambda cc: pl.BlockSpec((rows, cc), lambda i, pt: (s0 // rows + i, 0))

    def paged(shape, e, r):
        return pl.BlockSpec((None, None) + shape, lambda i, pt: (pt[(i * eb + e) * n_pages + r], layer, 0, 0))

    pages = [(e, r) for e in range(eb) for r in range(n_pages)]
    in_specs = ([row(c), row(c), row(c), row(LANES)]
                + [paged((c, page), e, r) for e, r in pages]
                + [paged((c, page), e, r) for e, r in pages]
                + [paged((N_HEADS, page), e, r) for e, r in pages])
    return pl.pallas_call(
        functools.partial(_sample_attn_kernel, n_pages=n_pages, page=page, eb=eb, ds=ds),
        grid_spec=pltpu.PrefetchScalarGridSpec(
            num_scalar_prefetch=1, grid=(db // eb,), in_specs=in_specs,
            out_specs=pl.BlockSpec((rows, c), lambda i, pt: (i, 0))),
        out_shape=jax.ShapeDtypeStruct((db * ds, c), F32),
        compiler_params=_params("parallel"),
        name="sample_paged_attn",
    )(page_table.reshape(-1), q, k, v, lf, *([cache_k] * len(pages)), *([cache_v] * len(pages)),
      *([cache_lft] * len(pages)))


def _odd_tile(wy, wp, bg, w_rows, window, pos0):
    t = bg.shape[0]
    y0 = wy[SUBLANES:]
    y1 = pltpu.roll(wy, 1, axis=0)[SUBLANES:]
    y2 = pltpu.roll(wy, 2, axis=0)[SUBLANES:]
    out_c = bg * (w_rows[0] * y2 + w_rows[1] * y1 + w_rows[2] * y0)
    back = 2 * SUBLANES
    win = wp
    step = 1
    while step < window:
        win = win + pltpu.roll(win, step, axis=0)
        step *= 2
    if pos0 is None:
        cnt = float(window)
    else:
        pos = pos0 + lax.broadcasted_iota(jnp.int32, (t, LANES), 0)
        cnt = jnp.maximum(jnp.minimum(window, pos + 1), 1).astype(F32)
    return out_c, win[back:] / cnt - wp[back:]


def _prompt_odd_kernel(y_ref, bg_ref, pd_ref, w_ref, oc_ref, d_ref, *, n_tiles, fp):
    w_rows = [w_ref[j:j + 1, :] for j in range(SCONV_W)]
    c = y_ref.shape[1]
    t = SEQ_TILE
    n_partial = -(-(fp + POOL_HIST) // t)

    def run(window):
        wy0 = jnp.concatenate([jnp.zeros((SUBLANES, c), F32), y_ref[0:t, :]], axis=0)
        wp0 = jnp.concatenate([jnp.zeros((2 * SUBLANES, c), F32), pd_ref[0:t, :]], axis=0)
        oc, d = _odd_tile(wy0, wp0, bg_ref[0:t, :], w_rows, window, -fp)
        oc_ref[0:t, :] = oc
        d_ref[0:t, :] = d

        def body(i, partial):
            s0 = pl.multiple_of(i * t, t)
            wy = y_ref[pl.ds(s0 - SUBLANES, t + SUBLANES), :]
            wp = pd_ref[pl.ds(s0 - 2 * SUBLANES, t + 2 * SUBLANES), :]
            oc, d = _odd_tile(wy, wp, bg_ref[pl.ds(s0, t), :], w_rows, window, s0 - fp if partial else None)
            oc_ref[pl.ds(s0, t), :] = oc
            d_ref[pl.ds(s0, t), :] = d

        lax.fori_loop(1, n_partial, lambda i, carry: body(i, True) or carry, 0)
        lax.fori_loop(n_partial, n_tiles, lambda i, carry: body(i, False) or carry, 0)

    for gi, window in enumerate(POOL_WINDOWS):
        pl.when(pl.program_id(1) == gi)(functools.partial(run, window))


def _prompt_odd(y, bg, pd, w, *, nb, sp, fp):
    n, c = y.shape
    assert c // LANES == len(POOL_WINDOWS)
    blk = pl.BlockSpec((sp, LANES), lambda bi, ci: (bi, ci))
    return pl.pallas_call(
        functools.partial(_prompt_odd_kernel, n_tiles=sp // SEQ_TILE, fp=fp),
        grid=(nb, c // LANES),
        in_specs=[blk, blk, blk, pl.BlockSpec((SCONV_W, LANES), lambda bi, ci: (0, ci))],
        out_specs=[blk, blk],
        out_shape=[jax.ShapeDtypeStruct((nb * sp, c), F32)] * 2,
        compiler_params=_params("parallel", "parallel"),
        name="prompt_sconv_pool",
    )(y, bg, pd, w)


def _sample_odd_kernel(ss_ref, sp_ref, y_ref, bg_ref, pd_ref, w_ref,
                       oc_ref, d_ref, ss_out_ref, sp_out_ref, yext_ref, pext_ref, *, ds):
    dbt, _, c = ss_ref.shape
    hs = SCONV_W - 1
    y = y_ref[...].reshape(dbt, ds, c)
    pd = pd_ref[...].reshape(dbt, ds, c)
    bg = bg_ref[...].reshape(dbt, ds, c)
    ny = SUBLANES + ds
    yext_ref[:, 0:SUBLANES - hs, :] = jnp.zeros((dbt, SUBLANES - hs, c), F32)
    yext_ref[:, SUBLANES - hs:SUBLANES, :] = ss_ref[...]
    yext_ref[:, SUBLANES:ny, :] = y
    yflat = yext_ref[...].reshape(dbt * ny, c)
    conv = y * w_ref[SCONV_W - 1:SCONV_W, :]
    for back in range(1, SCONV_W):
        shifted = pltpu.roll(yflat, back, axis=0).reshape(dbt, ny, c)[:, SUBLANES:ny, :]
        conv = conv + shifted * w_ref[SCONV_W - 1 - back:SCONV_W - back, :]
    oc_ref[...] = (bg * conv).reshape(dbt * ds, c)
    ss_out_ref[...] = yext_ref[:, ny - hs:ny, :]

    base = 2 * SUBLANES
    npd = base + ds
    pext_ref[:, 0:base - POOL_HIST, :] = jnp.zeros((dbt, base - POOL_HIST, c), F32)
    pext_ref[:, base - POOL_HIST:base, :] = sp_ref[...]
    pext_ref[:, base:npd, :] = pd
    gc = c // len(POOL_WINDOWS)
    means = []
    for gi, window in enumerate(POOL_WINDOWS):
        win = pext_ref[:, :, gi * gc:(gi + 1) * gc].reshape(dbt * npd, gc)
        step = 1
        while step < window:
            win = win + pltpu.roll(win, step, axis=0)
            step *= 2
        means.append(win.reshape(dbt, npd, gc)[:, base:npd, :] / float(window))
    d_ref[...] = (jnp.concatenate(means, axis=-1) - pd).reshape(dbt * ds, c)
    sp_out_ref[...] = pext_ref[:, npd - POOL_HIST:npd, :]


def _sample_odd(state_s, state_p, layer, y, bg, pd, w, *, s0, db, ds, dbt):
    n, c = y.shape
    rows = dbt * ds
    row_in = pl.BlockSpec((rows, c), lambda i: (s0 // rows + i, 0))
    row_out = pl.BlockSpec((rows, c), lambda i: (i, 0))
    hs = state_s.shape[2]
    hp = state_p.shape[2]
    return pl.pallas_call(
        functools.partial(_sample_odd_kernel, ds=ds),
        grid=(db // dbt,),
        in_specs=[pl.BlockSpec((dbt, None, hs, c), lambda i: (i, layer, 0, 0)),
                  pl.BlockSpec((dbt, None, hp, c), lambda i: (i, layer, 0, 0)),
                  row_in, row_in, row_in, _full(w.shape)],
        out_specs=[row_out, row_out, pl.BlockSpec((dbt, hs, c), lambda i: (i, 0, 0)),
                   pl.BlockSpec((dbt, hp, c), lambda i: (i, 0, 0))],
        out_shape=[jax.ShapeDtypeStruct((db * ds, c), F32), jax.ShapeDtypeStruct((db * ds, c), F32),
                   jax.ShapeDtypeStruct((db, hs, c), F32), jax.ShapeDtypeStruct((db, hp, c), F32)],
        scratch_shapes=[pltpu.VMEM((dbt, SUBLANES + ds, c), F32), pltpu.VMEM((dbt, 2 * SUBLANES + ds, c), F32)],
        compiler_params=_params("parallel"),
        name="sample_sconv_pool",
    )(state_s, state_p, y, bg, pd, w)


def kernel(x_prompt, x_sample, cache_k, cache_v, cache_logf, page_table, state_conv_a, state_sconv, state_pool, meta_tokens, w_in_even, b_forget, w_dw_a, b_dw_a, ln_a_g, ln_a_b, w_out_even, w_in_odd, w_sconv, w_pool_mix, pool_scale, w_out_odd, ln_mix_g, ln_mix_b, w_ffn_gate, w_ffn_up, w_ffn_down, ln_ffn_g, ln_ffn_b):
    nb, seq, d = x_prompt.shape
    db, ds, _ = x_sample.shape
    depth = w_ffn_gate.shape[0]
    alpha = float((2 * depth) ** 0.25)
    ca = state_conv_a.shape[-1]
    att = N_HEADS * HEAD_DIM
    assert ds == SUBLANES and cache_k.shape[3] == N_HEADS and cache_k.shape[4] == HEAD_DIM
    assert state_conv_a.shape[2] == CONV_W - 1 and state_pool.shape[2] == POOL_HIST
    assert meta_tokens.shape[0] == N_META

    s_real = N_META + seq
    fp = (-s_real) % SEQ_TILE
    if fp < CONV_BACK:
        fp += SEQ_TILE
    sp = fp + s_real
    s0 = nb * sp
    n = s0 + db * ds
    tm = next(t for t in (512, 256, 128, 64, 32, 16, 8) if s0 % t == 0 and (db * ds) % t == 0)
    tm_in = next(t for t in (1024, 512, 256, 128, 64, 32, 16, 8) if n % t == 0)
    dbt = next(t for t in (8, 4, 2, 1) if db % t == 0)
    direct_out = seq % SEQ_TILE == 0 and (db * ds) % SEQ_TILE == 0

    head = jnp.concatenate([jnp.zeros((fp, d), F32), meta_tokens.astype(F32)], axis=0)
    x = jnp.concatenate([piece for bi in range(nb) for piece in (head, x_prompt[bi])]
                        + [x_sample.reshape(db * ds, d)], axis=0)

    n_phys = cache_k.shape[0]
    n_even = cache_k.shape[1]
    page = cache_k.shape[2]
    ck = jnp.transpose(cache_k, (0, 1, 3, 4, 2)).reshape(n_phys, n_even, att, page)
    cv = jnp.transpose(cache_v, (0, 1, 3, 4, 2)).reshape(n_phys, n_even, att, page)
    clft = jnp.swapaxes(cache_logf, 2, 3)

    row2 = lambda v: v.reshape(1, -1).astype(F32)

    def prompt_rows(arr, lo, hi):
        return jnp.stack([arr[bi * sp + lo:bi * sp + hi] for bi in range(nb)], axis=0)

    def sample_rows(arr):
        return arr[s0:].reshape(db, ds, -1)

    wg_all, wu_all, wd_all = (w.astype(BF16) for w in (w_ffn_gate, w_ffn_up, w_ffn_down))
    wo_even, wo_odd = w_out_even.astype(BF16), w_out_odd.astype(BF16)
    wi_even = jnp.pad(w_in_even.astype(BF16), ((0, 0), (0, 0), (0, LANES - N_HEADS)))
    wi_odd = w_in_odd.astype(BF16)

    def tail(name, xin, m1p, m2p, m1s, m2s, p1, p2, w_out_all, layer):
        last = direct_out and layer == depth - 1
        return _layer_tail(name, xin, m1p, m2p, m1s, m2s, p1, p2, w_out_all,
                           row2(ln_mix_g[layer]), row2(ln_mix_b[layer]), wg_all, wu_all, wd_all,
                           row2(ln_ffn_g[layer]), row2(ln_ffn_b[layer]), even=layer % 2 == 0,
                           mix_idx=layer // 2, layer=layer, tm=SEQ_TILE if last else tm, alpha=alpha,
                           nb=nb, sp=sp, fp=fp, out_rows=(seq, db * ds) if last else None)

    prev_kv = []
    lfp, cap, scp, plp = [], [], [], []
    ks_, vs_, lfs, cas, scs, pls = [], [], [], [], [], []
    for layer in range(depth):
        i = layer // 2
        if layer % 2 == 0:
            bf = jnp.pad(b_forget[i].astype(F32), (0, LANES - N_HEADS)).reshape(1, LANES)
            a, q, k, v, lf = _even_in(x, wi_even, i, bf, tm=tm_in, ca=ca, att=att)
            w_dw, b_dw = w_dw_a[i].astype(F32), row2(b_dw_a[i])
            conv_s, st_a = _sample_conv(state_conv_a.astype(F32), i, a, w_dw, b_dw, s0=s0, db=db, ds=ds, dbt=dbt)
            qa, ka, vat, conv_p, *caches = _fox_prep(q, k, v, lf, a, w_dw, b_dw, prev_kv, i == n_even - 1,
                                                     nb=nb, sp=sp, fp=fp)
            prev_kv.append((k, v))
            at_p = _flash(qa, ka, vat, hg=FLASH_HEADS)
            at_s = _sample_attn(page_table, i, q, k, v, lf, ck, cv, clft, s0=s0, db=db, ds=ds)
            x = tail("even_tail", x, conv_p, at_p, conv_s, at_s, row2(ln_a_g[i]), row2(ln_a_b[i]), wo_even, layer)
            lfp.append(prompt_rows(lf, fp, sp)[..., :N_HEADS])
            cap.append(prompt_rows(a, sp - (CONV_W - 1), sp))
            ks_.append(sample_rows(k).reshape(db, ds, N_HEADS, HEAD_DIM))
            vs_.append(sample_rows(v).reshape(db, ds, N_HEADS, HEAD_DIM))
            lfs.append(sample_rows(lf)[..., :N_HEADS])
            cas.append(st_a)
        else:
            c = state_sconv.shape[-1]
            y, bg, pd = _odd_in(x, wi_odd, i, tm=tm_in, c=c)
            oc_p, dd_p = _prompt_odd(y, bg, pd, w_sconv[i].astype(F32), nb=nb, sp=sp, fp=fp)
            oc_s, dd_s, st_s, st_p = _sample_odd(state_sconv.astype(F32), state_pool.astype(F32), i, y, bg, pd,
                                                 w_sconv[i].astype(F32), s0=s0, db=db, ds=ds, dbt=dbt)
            wm = jax.scipy.linalg.block_diag(*[w_pool_mix[i, g] for g in range(w_pool_mix.shape[1])]).astype(BF16)
            x = tail("odd_tail", x, oc_p, dd_p, oc_s, dd_s, wm, row2(pool_scale[i]), wo_odd, layer)
            scp.append(prompt_rows(y, sp - (SCONV_W - 1), sp))
            plp.append(prompt_rows(pd, sp - POOL_HIST, sp))
            scs.append(st_s)
            pls.append(st_p)

    if direct_out:
        y_prompt, y_sample = x[0], x[1].reshape(db, ds, d)
    else:
        y_prompt, y_sample = prompt_rows(x, fp + N_META, sp), sample_rows(x)
    st = lambda xs: jnp.stack(xs, axis=1)
    kp, vp = (jnp.transpose(c.reshape(nb, n_even, N_HEADS, HEAD_DIM, s_real), (0, 1, 4, 2, 3)) for c in caches)
    return (y_prompt, y_sample, kp, vp, st(lfp), st(cap), st(scp), st(plp),
            st(ks_), st(vs_), st(lfs), st(cas), st(scs), st(pls))
```

```python
import functools

import jax
import jax.numpy as jnp
from jax import lax
from jax.experimental import pallas as pl
from jax.experimental.pallas import tpu as pltpu

N_META = 16
N_HEADS = 8
HEAD_DIM = 64
CONV_W = 31
SCONV_W = 3
POOL_WINDOWS = (2, 4, 8, 16)
POOL_HIST = max(POOL_WINDOWS) - 1
LN_EPS = 1e-5
SEQ_TILE = 256
LANES = 128
SUBLANES = 8
CONV_BACK = -(-(CONV_W - 1) // SUBLANES) * SUBLANES
C_TERMS = 3
AUX_ONES = C_TERMS * N_HEADS
FLASH_HEADS = 8
V_ROWS = HEAD_DIM + 2 * SUBLANES
VMEM_LIMIT = 56 * 1024 * 1024
NEG = -1e30
LOG2E = 1.4426950408889634
F32 = jnp.float32
BF16 = jnp.bfloat16
NT_DIMS = (((1,), (1,)), ((), ()))


def _params(*sem):
    return pltpu.CompilerParams(dimension_semantics=sem, vmem_limit_bytes=VMEM_LIMIT)


def _dot(a, b):
    return jnp.dot(a, b, preferred_element_type=F32)


def _ln(z, g, b):
    mu = jnp.mean(z, axis=-1, keepdims=True)
    zc = z - mu
    var = jnp.mean(zc * zc, axis=-1, keepdims=True)
    return zc * lax.rsqrt(var + LN_EPS) * g + b


def _silu(x):
    return x * jax.nn.sigmoid(x)


def _full(shape):
    return pl.BlockSpec(shape, lambda *_: (0,) * len(shape))


def _slab(a, idx):
    return pl.BlockSpec((None,) + a.shape[1:], lambda *_: (idx,) + (0,) * (a.ndim - 1),
                        pipeline_mode=pl.Buffered(1))


def _even_in_kernel(x_ref, w_ref, bf_ref, a_ref, q_ref, k_ref, v_ref, lf_ref, *, ca, att):
    xb = x_ref[...].astype(BF16)

    def mm(lo, hi):
        return _dot(xb, w_ref[:, lo:hi])

    u = mm(0, ca)
    g = mm(ca, 2 * ca)
    a_ref[...] = u * jax.nn.sigmoid(g)
    o = 2 * ca
    q_ref[...] = mm(o, o + att) * (HEAD_DIM ** -0.5)
    k_ref[...] = mm(o + att, o + 2 * att)
    v_ref[...] = mm(o + 2 * att, o + 3 * att)
    z = mm(o + 3 * att, o + 3 * att + LANES) + bf_ref[...]
    lf = jnp.minimum(z, 0.0) - jnp.log1p(jnp.exp(-jnp.abs(z)))
    lane = lax.broadcasted_iota(jnp.int32, lf.shape, 1)
    lf_ref[...] = jnp.where(lane < N_HEADS, lf, 0.0)


def _even_in(x, w, idx, bf, *, tm, ca, att):
    n, d = x.shape
    row = lambda c: pl.BlockSpec((tm, c), lambda i: (i, 0))
    return pl.pallas_call(
        functools.partial(_even_in_kernel, ca=ca, att=att),
        grid=(n // tm,),
        in_specs=[row(d), _slab(w, idx), _full(bf.shape)],
        out_specs=[row(ca), row(att), row(att), row(att), row(LANES)],
        out_shape=[jax.ShapeDtypeStruct((n, c), F32) for c in (ca, att, att, att, LANES)],
        compiler_params=_params("parallel"),
        name="even_in_proj",
    )(x, w, bf)


def _odd_in_kernel(x_ref, w_ref, y_ref, bg_ref, pd_ref, *, c):
    xb = x_ref[...].astype(BF16)
    hc = _dot(xb, w_ref[:, 0:c])
    bg_ref[...] = _dot(xb, w_ref[:, c:2 * c])
    cg = _dot(xb, w_ref[:, 2 * c:3 * c])
    y_ref[...] = cg * hc
    pd_ref[...] = _dot(xb, w_ref[:, 3 * c:4 * c])


def _odd_in(x, w, idx, *, tm, c):
    n, d = x.shape
    row = lambda cc: pl.BlockSpec((tm, cc), lambda i: (i, 0))
    return pl.pallas_call(
        functools.partial(_odd_in_kernel, c=c),
        grid=(n // tm,),
        in_specs=[row(d), _slab(w, idx)],
        out_specs=[row(c)] * 3,
        out_shape=[jax.ShapeDtypeStruct((n, c), F32)] * 3,
        compiler_params=_params("parallel"),
        name="odd_in_proj",
    )(x, w)


def _tail_kernel(x_ref, m1p_ref, m2p_ref, m1s_ref, m2s_ref, p1_ref, p2_ref, w_ref, g1_ref, b1_ref,
                 wg_ref, wu_ref, wd_ref, g2_ref, b2_ref, *rest, even, alpha, chunk, nb, sp, fp, npt, lead):
    i = pl.program_id(0)
    is_sample = i >= npt
    m1 = jnp.where(is_sample, m1s_ref[...], m1p_ref[...])
    m2 = jnp.where(is_sample, m2s_ref[...], m2p_ref[...])
    if even:
        m1 = _silu(_ln(m1, p1_ref[...], p2_ref[...]))
    else:
        m2 = _dot(m2.astype(BF16), p1_ref[...]) * p2_ref[...]
    ca = m1.shape[1]
    mix = _dot(m1.astype(BF16), w_ref[0:ca, :]) + _dot(m2.astype(BF16), w_ref[ca:, :])
    x1 = _ln(alpha * x_ref[...] + mix, g1_ref[...], b1_ref[...])

    h_ref = rest[-1]
    xb = x1.astype(BF16)
    for c in range(0, wg_ref.shape[1], chunk):
        gate = _dot(xb, wg_ref[:, c:c + chunk])
        up = _dot(xb, wu_ref[:, c:c + chunk])
        h_ref[:, c:c + chunk] = (_silu(gate) * up).astype(BF16)
    out = _ln(alpha * x1 + _dot(h_ref[...], wd_ref[...]), g2_ref[...], b2_ref[...])

    tm = x1.shape[0]
    if lead is None:
        r = i * tm + lax.broadcasted_iota(jnp.int32, (tm, 1), 0)
        keep = jnp.ones((tm, 1), F32)
        for bi in range(nb):
            keep = jnp.where((r >= bi * sp) & (r < bi * sp + fp), 0.0, keep)
        rest[0][...] = out * keep
    else:
        yp_ref, ys_ref = rest[:2]

        @pl.when(jnp.logical_and(jnp.logical_not(is_sample), i % (sp // tm) >= lead))
        def _():
            yp_ref[...] = out

        @pl.when(is_sample)
        def _():
            ys_ref[...] = out


def _layer_tail(name, x, m1p, m2p, m1s, m2s, p1, p2, w, g1, b1, wg, wu, wd, g2, b2, *, even, mix_idx, layer, tm,
                alpha, nb, sp, fp, out_rows=None):
    n, d = x.shape
    s0 = m1p.shape[0]
    dff = wg.shape[2]
    chunk = SEQ_TILE if dff % SEQ_TILE == 0 else dff
    npt = s0 // tm
    row = lambda c: pl.BlockSpec((tm, c), lambda i: (i, 0))
    prow = lambda c: pl.BlockSpec((tm, c), lambda i: (jnp.minimum(i, npt - 1), 0))
    srow = lambda c: pl.BlockSpec((tm, c), lambda i: (jnp.maximum(i - npt, 0), 0))
    once = lambda a: pl.BlockSpec(a.shape, lambda i: (0,) * a.ndim, pipeline_mode=pl.Buffered(1))
    consts = (p1, p2, w, g1, b1, wg, wu, wd, g2, b2)
    const_specs = [once(p1), once(p2), _slab(w, mix_idx), once(g1), once(b1),
                   _slab(wg, layer), _slab(wu, layer), _slab(wd, layer), once(g2), once(b2)]
    if out_rows is None:
        lead = None
        out_specs = row(d)
        out_shape = jax.ShapeDtypeStruct((n, d), F32)
    else:
        seq, db_rows = out_rows
        tps = sp // tm
        lead = (sp - seq) // tm
        out_specs = [
            pl.BlockSpec((None, tm, d), lambda i: (jnp.minimum(i // tps, nb - 1),
                                                   jnp.where(i < npt, jnp.maximum(i % tps - lead, 0), tps - lead - 1),
                                                   0)),
            srow(d)]
        out_shape = [jax.ShapeDtypeStruct((nb, seq, d), F32), jax.ShapeDtypeStruct((db_rows, d), F32)]
    return pl.pallas_call(
        functools.partial(_tail_kernel, even=even, alpha=alpha, chunk=chunk, nb=nb, sp=sp, fp=fp, npt=npt, lead=lead),
        grid=(n // tm,),
        in_specs=[row(d), prow(m1p.shape[1]), prow(m2p.shape[1]), srow(m1s.shape[1]), srow(m2s.shape[1])]
        + const_specs,
        out_specs=out_specs,
        out_shape=out_shape,
        scratch_shapes=[pltpu.VMEM((tm, dff), BF16)],
        compiler_params=_params("arbitrary" if out_rows else "parallel"),
        name=name,
    )(x, m1p, m2p, m1s, m2s, *consts)


def _conv_tile(win, w_rows):
    rows = win.shape[0]
    t = rows - CONV_BACK
    lead = CONV_BACK - (CONV_W - 1)
    acc = None
    for r in range(SUBLANES):
        rolled = win if r == 0 else pltpu.roll(win, rows - r, axis=0)
        for m in range(CONV_BACK // SUBLANES + 1):
            j = SUBLANES * m + r - lead
            if 0 <= j < CONV_W:
                term = rolled[SUBLANES * m:SUBLANES * m + t] * w_rows[j]
                acc = term if acc is None else acc + term
    return acc


def _sample_conv_kernel(st_ref, a_ref, w_ref, b_ref, o_ref, st_out_ref, ext_ref, *, ds):
    dbt, hist, c = st_ref.shape
    off = CONV_BACK - hist
    rows = CONV_BACK + ds
    ext_ref[:, 0:off, :] = jnp.zeros((dbt, off, c), F32)
    ext_ref[:, off:CONV_BACK, :] = st_ref[...]
    ext_ref[:, CONV_BACK:rows, :] = a_ref[...].reshape(dbt, ds, c)
    flat = ext_ref[...].reshape(dbt * rows, c)
    acc = None
    for r in range(SUBLANES):
        shifted = (flat if r == 0 else pltpu.roll(flat, dbt * rows - r, axis=0)).reshape(dbt, rows, c)
        for m in range(rows // SUBLANES):
            j = SUBLANES * m + r - off
            if 0 <= j < CONV_W:
                term = shifted[:, SUBLANES * m:SUBLANES * m + ds, :] * w_ref[j:j + 1, :]
                acc = term if acc is None else acc + term
    o_ref[...] = (acc + b_ref[...]).reshape(dbt * ds, c)
    st_out_ref[...] = ext_ref[:, off + ds:rows, :]


def _sample_conv(state, layer, a, w, b, *, s0, db, ds, dbt):
    n, c = a.shape
    hist = state.shape[2]
    rows = dbt * ds
    return pl.pallas_call(
        functools.partial(_sample_conv_kernel, ds=ds),
        grid=(db // dbt,),
        in_specs=[pl.BlockSpec((dbt, None, hist, c), lambda i: (i, layer, 0, 0)),
                  pl.BlockSpec((rows, c), lambda i: (s0 // rows + i, 0)), _full(w.shape), _full(b.shape)],
        out_specs=[pl.BlockSpec((rows, c), lambda i: (i, 0)), pl.BlockSpec((dbt, hist, c), lambda i: (i, 0, 0))],
        out_shape=[jax.ShapeDtypeStruct((db * ds, c), F32), jax.ShapeDtypeStruct((db, hist, c), F32)],
        scratch_shapes=[pltpu.VMEM((dbt, CONV_BACK + ds, c), F32)],
        compiler_params=_params("parallel"),
        name="sample_conv",
    )(state, a, w, b)


def _fox_prep_kernel(q_ref, k_ref, v_ref, lf_ref, a_ref, wdw_ref, bdw_ref, *rest, fp, ns, n_prev, emit):
    prev, rest = rest[:2 * n_prev], rest[2 * n_prev:]
    qa_ref, ka_ref, vat_ref, cv_ref = rest[:4]
    carry_ref, ah_ref = rest[6:8] if emit else rest[4:6]
    s = pl.program_id(1)
    t = SEQ_TILE

    @pl.when(s < ns)
    def _():
        _fox_operands(s, q_ref, k_ref, v_ref, lf_ref, qa_ref, ka_ref, vat_ref, carry_ref, fp)

        @pl.when(s == 0)
        def _():
            ah_ref[...] = jnp.zeros_like(ah_ref)

        for cb in range(a_ref.shape[1] // LANES):
            cs = slice(cb * LANES, (cb + 1) * LANES)
            win = jnp.concatenate([ah_ref[:, cs], a_ref[:, cs]], axis=0)
            w_rows = [wdw_ref[j:j + 1, cs] for j in range(CONV_W)]
            cv_ref[:, cs] = _conv_tile(win, w_rows) + bdw_ref[:, cs]
        ah_ref[...] = a_ref[t - CONV_BACK:t, :]

    if not emit:
        return
    kt_ref, vt_ref = rest[4:6]
    kc_ref, vc_ref = rest[8:10]

    @pl.when(s == 0)
    def _():
        kc_ref[...] = jnp.zeros_like(kc_ref)
        vc_ref[...] = jnp.zeros_like(vc_ref)

    k_srcs = list(prev[0::2]) + [k_ref]
    v_srcs = list(prev[1::2]) + [v_ref]
    for srcs, held, dst in ((k_srcs, kc_ref, kt_ref), (v_srcs, vc_ref, vt_ref)):
        for li, src in enumerate(srcs):
            rows = jnp.concatenate([held[li], src[0:fp, :]], axis=0)
            for c in range(src.shape[1] // LANES):
                dst[li, c * LANES:(c + 1) * LANES, :] = rows[:, c * LANES:(c + 1) * LANES].T
            held[li] = src[fp:t, :]


def _fox_operands(s, q_ref, k_ref, v_ref, lf_ref, qa_ref, ka_ref, vat_ref, carry_ref, fp):
    @pl.when(s == 0)
    def _():
        carry_ref[...] = jnp.zeros_like(carry_ref)

    t = SEQ_TILE
    row = lax.broadcasted_iota(jnp.int32, (t, LANES), 0)
    lane = lax.broadcasted_iota(jnp.int32, (t, LANES), 1)
    c = jnp.where(s * t + row >= fp, lf_ref[...], 0.0)
    sh = 1
    while sh < t:
        c = c + jnp.where(row >= sh, pltpu.roll(c, sh, axis=0), 0.0)
        sh *= 2
    c = c + carry_ref[0:1, :]
    carry_ref[0:1, :] = c[t - 1:t, :]

    c = c * LOG2E
    c1 = c.astype(BF16).astype(F32)
    r1 = c - c1
    c2 = r1.astype(BF16).astype(F32)
    c3 = (r1 - c2).astype(BF16).astype(F32)
    g23 = jnp.where(lane < 2 * N_HEADS, pltpu.roll(c2, N_HEADS, axis=1),
                    jnp.where(lane < 3 * N_HEADS, pltpu.roll(c3, 2 * N_HEADS, axis=1), 0.0))
    g = jnp.where(lane < N_HEADS, c1, g23)
    gk = jnp.where(lane < N_HEADS, jnp.where(s * t + row >= fp, c1, -NEG), g23)
    ones_grp = (lane >= HEAD_DIM + AUX_ONES) & (lane < HEAD_DIM + 2 * AUX_ONES)
    ck = [jnp.where((lane >= HEAD_DIM) & (lane < HEAD_DIM + AUX_ONES), -pltpu.roll(gk, HEAD_DIM, axis=1),
                    jnp.where(ones_grp, 1.0, 0.0))]
    cq = [jnp.where(ones_grp, pltpu.roll(g, HEAD_DIM + AUX_ONES, axis=1), 0.0)]
    ck.append(pltpu.roll(ck[0], HEAD_DIM, axis=1))
    cq.append(pltpu.roll(cq[0], HEAD_DIM, axis=1))

    lane1 = lax.broadcasted_iota(jnp.int32, (1, LANES), 1)
    for h in range(N_HEADS):
        odd = h % 2
        pr = h // 2
        base = 0 if odd else HEAD_DIM
        a = lane1 - base
        mine = (lane1 & (N_HEADS - 1)) == h
        data = ((lane1 >= HEAD_DIM) if odd else (lane1 < HEAD_DIM)).astype(F32)
        sel1 = ((a >= 0) & (a < AUX_ONES) & mine).astype(F32)
        selc = ((a >= AUX_ONES) & (a < 2 * AUX_ONES) & mine).astype(F32)
        sl = slice(pr * LANES, (pr + 1) * LANES)
        qa_ref[h] = (q_ref[:, sl] * (data * LOG2E) + (cq[odd] * selc + sel1)).T.astype(BF16)
        ka_ref[h] = (k_ref[:, sl] * data + ck[odd]).astype(BF16)
    ones = jnp.ones((V_ROWS - HEAD_DIM, t), BF16)
    for pr in range(N_HEADS // 2):
        v_t = v_ref[:, pr * LANES:(pr + 1) * LANES].T.astype(BF16)
        vat_ref[2 * pr] = jnp.concatenate([v_t[0:HEAD_DIM], ones], axis=0)
        vat_ref[2 * pr + 1] = jnp.concatenate([v_t[HEAD_DIM:], ones], axis=0)


def _fox_prep(q, k, v, lf, a, w_dw, b_dw, prev_kv, emit, *, nb, sp, fp):
    ns = sp // SEQ_TILE
    att = q.shape[1]
    ca = a.shape[1]
    t = SEQ_TILE
    last = ns - 1
    row = lambda c: pl.BlockSpec((t, c), lambda bi, si: (bi * ns + jnp.minimum(si, last), 0))
    hm = pl.BlockSpec((None, N_HEADS, t, LANES), lambda bi, si: (bi, 0, jnp.minimum(si, last), 0))
    hm_t = lambda r: pl.BlockSpec((None, N_HEADS, r, t), lambda bi, si: (bi, 0, 0, jnp.minimum(si, last)))
    out_specs = [hm_t(LANES), hm, hm_t(V_ROWS), row(ca)]
    out_shape = [jax.ShapeDtypeStruct((nb, N_HEADS, LANES, sp), BF16),
                 jax.ShapeDtypeStruct((nb, N_HEADS, sp, LANES), BF16),
                 jax.ShapeDtypeStruct((nb, N_HEADS, V_ROWS, sp), BF16), jax.ShapeDtypeStruct((nb * sp, ca), F32)]
    scratch = [pltpu.VMEM((SUBLANES, LANES), F32), pltpu.VMEM((CONV_BACK, ca), F32)]
    n_prev = len(prev_kv) if emit else 0
    if emit:
        assert 0 < fp < t and fp % SUBLANES == 0
        n_layers = n_prev + 1
        cache = pl.BlockSpec((None, n_layers, att, t), lambda bi, si: (bi, 0, 0, jnp.maximum(si - 1, 0)))
        out_specs += [cache, cache]
        out_shape += [jax.ShapeDtypeStruct((nb, n_layers, att, sp - fp), F32)] * 2
        scratch += [pltpu.VMEM((n_layers, t - fp, att), F32)] * 2
    prev = [a for kv in prev_kv for a in kv] if emit else []
    return pl.pallas_call(
        functools.partial(_fox_prep_kernel, fp=fp, ns=ns, n_prev=n_prev, emit=emit),
        grid=(nb, ns + 1 if emit else ns),
        in_specs=[row(att), row(att), row(att), row(LANES), row(ca), _full(w_dw.shape), _full(b_dw.shape)]
        + [row(att)] * len(prev),
        out_specs=out_specs,
        out_shape=out_shape,
        scratch_shapes=scratch,
        compiler_params=_params("parallel", "arbitrary"),
        name="fox_prep",
    )(q, k, v, lf, a, w_dw, b_dw, *prev)


def _flash_kernel(q_ref, k_ref, vt_ref, o_ref, s_ref, p_ref, acc_ref, *, hg):
    i = pl.program_id(2)
    t = SEQ_TILE

    def scores_to(j, slot):
        ks = pl.multiple_of(j * t, t)
        for h in range(hg):
            s_ref[slot, h] = _dot(k_ref[h, pl.ds(ks, t), :], q_ref[h])

    def accumulate(j, als):
        ks = pl.multiple_of(j * t, t)
        pvs = [_dot(vt_ref[h, :, pl.ds(ks, t)], p_ref[h]) for h in range(hg)]
        for h in range(hg):
            acc_ref[h] = als[h] * acc_ref[h] + pvs[h]

    def softmax(slot, ms, masked):
        new_m, als = [], []
        for h in range(hg):
            s = s_ref[slot, h]
            if masked:
                visible = (lax.broadcasted_iota(jnp.int32, (t, t), 0) <= lax.broadcasted_iota(jnp.int32, (t, t), 1))
                s = jnp.where(visible, s, NEG)
            mn = jnp.maximum(ms[h], jnp.max(s, axis=0, keepdims=True))
            als.append(jnp.exp2(ms[h] - mn))
            new_m.append(mn)
            p_ref[h] = jnp.exp2(s - mn).astype(BF16)
        return tuple(new_m), tuple(als)

    def stage(j, slot, state):
        ms, als = state
        scores_to(j + 1, 1 - slot)
        accumulate(jnp.maximum(j - 1, 0), als)
        return softmax(slot, ms, False)

    def pair(jj, state):
        return stage(2 * jj + 1, 1, stage(2 * jj, 0, state))

    def finish(slot, state):
        ms, als = state
        accumulate(jnp.maximum(i - 1, 0), als)
        _, als = softmax(slot, ms, True)
        accumulate(i, als)
        for pr in range(hg // 2):
            halves = [acc_ref[h, 0:HEAD_DIM, :] * (1.0 / acc_ref[h, HEAD_DIM:HEAD_DIM + 1, :])
                      for h in (2 * pr, 2 * pr + 1)]
            o_ref[:, pr * LANES:(pr + 1) * LANES] = jnp.concatenate(halves, axis=0).T.astype(o_ref.dtype)

    p_ref[...] = jnp.zeros_like(p_ref)
    acc_ref[...] = jnp.zeros_like(acc_ref)
    scores_to(0, 0)
    state = ((jnp.full((1, t), NEG, F32),) * hg, (jnp.ones((1, t), F32),) * hg)
    state = lax.fori_loop(0, i // 2, pair, state)
    odd = i % 2 == 1
    state = lax.cond(odd, lambda st: stage(i - 1, 0, st), lambda st: st, state)
    pl.when(odd)(lambda: finish(1, state))
    pl.when(jnp.logical_not(odd))(lambda: finish(0, state))


def _flash(qat, ka, vat, *, hg):
    nb, _, sp, _ = ka.shape
    ns = sp // SEQ_TILE
    return pl.pallas_call(
        functools.partial(_flash_kernel, hg=hg),
        grid=(nb, N_HEADS // hg, ns),
        in_specs=[pl.BlockSpec((None, hg, LANES, SEQ_TILE), lambda bi, gi, qi: (bi, gi, 0, qi)),
                  pl.BlockSpec((None, hg, sp, LANES), lambda bi, gi, qi: (bi, gi, 0, 0)),
                  pl.BlockSpec((None, hg, V_ROWS, sp), lambda bi, gi, qi: (bi, gi, 0, 0))],
        out_specs=pl.BlockSpec((SEQ_TILE, hg * HEAD_DIM), lambda bi, gi, qi: (bi * ns + qi, gi)),
        out_shape=jax.ShapeDtypeStruct((nb * sp, N_HEADS * HEAD_DIM), BF16),
        scratch_shapes=[pltpu.VMEM((2, hg, SEQ_TILE, SEQ_TILE), F32), pltpu.VMEM((hg, SEQ_TILE, SEQ_TILE), BF16),
                        pltpu.VMEM((hg, V_ROWS, SEQ_TILE), F32)],
        compiler_params=_params("parallel", "parallel", "arbitrary"),
        name="fox_flash",
    )(qat, ka, vat)


def _sample_attn_kernel(pt_ref, q_ref, kn_ref, vn_ref, lfn_ref, *rest, n_pages, page, eb, ds):
    del pt_ref
    o_ref = rest[3 * eb * n_pages]
    c = q_ref.shape[1]
    rows = ds * N_HEADS
    groups = [list(range(g, min(g + 2, n_pages))) for g in range(0, n_pages, 2)]
    past = n_pages * page
    sub = lax.broadcasted_iota(jnp.int32, (N_HEADS, c), 0)
    lane = lax.broadcasted_iota(jnp.int32, (N_HEADS, c), 1)
    hm = ((lane >= sub * HEAD_DIM) & (lane < (sub + 1) * HEAD_DIM)).astype(F32)
    hm_t = jnp.concatenate([hm] * ds, axis=0)
    pad = jnp.zeros((LANES - ds, c), F32)

    def refs_of(kind, e):
        return rest[(kind * eb + e) * n_pages:(kind * eb + e + 1) * n_pages]

    def scores(e):
        sl = slice(e * ds, (e + 1) * ds)
        q = q_ref[sl, :]
        qbd = jnp.concatenate([jnp.broadcast_to(q[t:t + 1, :], (N_HEADS, c)) * hm for t in range(ds)],
                              axis=0).astype(BF16)
        k_refs, lf_refs = refs_of(0, e), refs_of(2, e)
        s_parts = []
        for grp in groups:
            kp = jnp.concatenate([k_refs[r][...] for r in grp], axis=1).astype(BF16)
            s_parts.append(_dot(qbd, kp))
        s_past = jnp.concatenate(s_parts, axis=1)

        lf = jnp.concatenate([r[...] for r in lf_refs], axis=1)
        lane_p = lax.broadcasted_iota(jnp.int32, (N_HEADS, past), 1)
        suf = lf
        sh = 1
        while sh < past:
            suf = suf + jnp.where(lane_p < past - sh, pltpu.roll(suf, past - sh, axis=1), 0.0)
            sh *= 2
        s_past = s_past + jnp.concatenate([suf - lf] * ds, axis=0)

        kn = jnp.concatenate([kn_ref[sl, :], pad], axis=0).astype(BF16)
        s_new = lax.dot_general(qbd, kn, NT_DIMS, preferred_element_type=F32)
        cn = lfn_ref[sl, :]
        sub_n = lax.broadcasted_iota(jnp.int32, cn.shape, 0)
        sh = 1
        while sh < ds:
            cn = cn + jnp.where(sub_n >= sh, pltpu.roll(cn, sh, axis=0), 0.0)
            sh *= 2
        cn_t = jnp.concatenate([cn, jnp.zeros((LANES - ds, LANES), F32)], axis=0).T[0:N_HEADS, :]
        rown = lax.broadcasted_iota(jnp.int32, (rows, LANES), 0)
        u = lax.broadcasted_iota(jnp.int32, (rows, LANES), 1)
        s_new = jnp.where(u * N_HEADS <= rown, s_new - jnp.concatenate([cn_t] * ds, axis=0), NEG)
        return s_past, s_new

    def softmax(s_past, s_new):
        m = jnp.maximum(jnp.max(s_past, axis=1, keepdims=True), jnp.max(s_new, axis=1, keepdims=True))
        p_past = jnp.exp(s_past - m)
        p_new = jnp.exp(s_new - m)
        l = jnp.sum(p_past, axis=1, keepdims=True) + jnp.sum(p_new, axis=1, keepdims=True)
        return p_past.astype(BF16), p_new.astype(BF16), l

    def output(e, p_past, p_new, l):
        sl = slice(e * ds, (e + 1) * ds)
        v_refs = refs_of(1, e)
        vn = jnp.concatenate([vn_ref[sl, :], pad], axis=0).astype(BF16)
        o = _dot(p_new, vn)
        for grp in groups:
            vp = jnp.concatenate([v_refs[r][...] for r in grp], axis=1).astype(BF16)
            lo = grp[0] * page
            o = o + lax.dot_general(p_past[:, lo:lo + len(grp) * page], vp, NT_DIMS, preferred_element_type=F32)
        o = o * (1.0 / l) * hm_t
        rr = lax.broadcasted_iota(jnp.int32, (ds, rows), 1)
        tt = lax.broadcasted_iota(jnp.int32, (ds, rows), 0)
        pick = ((rr >= tt * N_HEADS) & (rr < (tt + 1) * N_HEADS)).astype(BF16)
        o_ref[sl, :] = _dot(pick, o.astype(BF16)).astype(o_ref.dtype)

    ss = [scores(e) for e in range(eb)]
    ps = [softmax(*s) for s in ss]
    for e in range(eb):
        output(e, *ps[e])


def _sample_attn(page_table, layer, q, k, v, lf, cache_k, cache_v, cache_lft, *, s0, db, ds):
    n, c = q.shape
    n_pages = page_table.shape[1]
    page = cache_k.shape[3]
    eb = 2 if db % 2 == 0 else 1
    rows = eb * ds
    row = lambda cc: pl.BlockSpec((rows, cc), lambda i, pt: (s0 // rows + i, 0))

    def paged(shape, e, r):
        return pl.BlockSpec((None, None) + shape, lambda i, pt: (pt[(i * eb + e) * n_pages + r], layer, 0, 0))

    pages = [(e, r) for e in range(eb) for r in range(n_pages)]
    in_specs = ([row(c), row(c), row(c), row(LANES)]
                + [paged((c, page), e, r) for e, r in pages]
                + [paged((c, page), e, r) for e, r in pages]
                + [paged((N_HEADS, page), e, r) for e, r in pages])
    return pl.pallas_call(
        functools.partial(_sample_attn_kernel, n_pages=n_pages, page=page, eb=eb, ds=ds),
        grid_spec=pltpu.PrefetchScalarGridSpec(
            num_scalar_prefetch=1, grid=(db // eb,), in_specs=in_specs,
            out_specs=pl.BlockSpec((rows, c), lambda i, pt: (i, 0))),
        out_shape=jax.ShapeDtypeStruct((db * ds, c), BF16 if rows % (2 * SUBLANES) == 0 else F32),
        compiler_params=_params("parallel"),
        name="sample_paged_attn",
    )(page_table.reshape(-1), q, k, v, lf, *([cache_k] * len(pages)), *([cache_v] * len(pages)),
      *([cache_lft] * len(pages)))


def _odd_tile(wy, wp, bg, w_rows, window, pos0):
    t = bg.shape[0]
    y0 = wy[SUBLANES:]
    y1 = pltpu.roll(wy, 1, axis=0)[SUBLANES:]
    y2 = pltpu.roll(wy, 2, axis=0)[SUBLANES:]
    out_c = bg * (w_rows[0] * y2 + w_rows[1] * y1 + w_rows[2] * y0)
    back = 2 * SUBLANES
    win = wp
    step = 1
    while step < window:
        win = win + pltpu.roll(win, step, axis=0)
        step *= 2
    if pos0 is None:
        cnt = float(window)
    else:
        pos = pos0 + lax.broadcasted_iota(jnp.int32, (t, LANES), 0)
        cnt = jnp.maximum(jnp.minimum(window, pos + 1), 1).astype(F32)
    return out_c, win[back:] / cnt - wp[back:]


def _prompt_odd_kernel(y_ref, bg_ref, pd_ref, w_ref, oc_ref, d_ref, *, n_tiles, fp):
    w_rows = [w_ref[j:j + 1, :] for j in range(SCONV_W)]
    c = y_ref.shape[1]
    t = SEQ_TILE
    n_partial = -(-(fp + POOL_HIST) // t)

    def run(window):
        wy0 = jnp.concatenate([jnp.zeros((SUBLANES, c), F32), y_ref[0:t, :]], axis=0)
        wp0 = jnp.concatenate([jnp.zeros((2 * SUBLANES, c), F32), pd_ref[0:t, :]], axis=0)
        oc, d = _odd_tile(wy0, wp0, bg_ref[0:t, :], w_rows, window, -fp)
        oc_ref[0:t, :] = oc.astype(oc_ref.dtype)
        d_ref[0:t, :] = d.astype(d_ref.dtype)

        def body(i, partial):
            s0 = pl.multiple_of(i * t, t)
            wy = y_ref[pl.ds(s0 - SUBLANES, t + SUBLANES), :]
            wp = pd_ref[pl.ds(s0 - 2 * SUBLANES, t + 2 * SUBLANES), :]
            oc, d = _odd_tile(wy, wp, bg_ref[pl.ds(s0, t), :], w_rows, window, s0 - fp if partial else None)
            oc_ref[pl.ds(s0, t), :] = oc.astype(oc_ref.dtype)
            d_ref[pl.ds(s0, t), :] = d.astype(d_ref.dtype)

        lax.fori_loop(1, n_partial, lambda i, carry: body(i, True) or carry, 0)
        lax.fori_loop(n_partial, n_tiles, lambda i, carry: body(i, False) or carry, 0)

    for gi, window in enumerate(POOL_WINDOWS):
        pl.when(pl.program_id(1) == gi)(functools.partial(run, window))


def _prompt_odd(y, bg, pd, w, *, nb, sp, fp):
    n, c = y.shape
    assert c // LANES == len(POOL_WINDOWS)
    blk = pl.BlockSpec((sp, LANES), lambda bi, ci: (bi, ci))
    return pl.pallas_call(
        functools.partial(_prompt_odd_kernel, n_tiles=sp // SEQ_TILE, fp=fp),
        grid=(nb, c // LANES),
        in_specs=[blk, blk, blk, pl.BlockSpec((SCONV_W, LANES), lambda bi, ci: (0, ci))],
        out_specs=[blk, blk],
        out_shape=[jax.ShapeDtypeStruct((nb * sp, c), BF16)] * 2,
        compiler_params=_params("parallel", "parallel"),
        name="prompt_sconv_pool",
    )(y, bg, pd, w)


def _sample_odd_kernel(ss_ref, sp_ref, y_ref, bg_ref, pd_ref, w_ref,
                       oc_ref, d_ref, ss_out_ref, sp_out_ref, yext_ref, pext_ref, *, ds):
    dbt, _, c = ss_ref.shape
    hs = SCONV_W - 1
    y = y_ref[...].reshape(dbt, ds, c)
    pd = pd_ref[...].reshape(dbt, ds, c)
    bg = bg_ref[...].reshape(dbt, ds, c)
    ny = SUBLANES + ds
    yext_ref[:, 0:SUBLANES - hs, :] = jnp.zeros((dbt, SUBLANES - hs, c), F32)
    yext_ref[:, SUBLANES - hs:SUBLANES, :] = ss_ref[...]
    yext_ref[:, SUBLANES:ny, :] = y
    yflat = yext_ref[...].reshape(dbt * ny, c)
    conv = y * w_ref[SCONV_W - 1:SCONV_W, :]
    for back in range(1, SCONV_W):
        shifted = pltpu.roll(yflat, back, axis=0).reshape(dbt, ny, c)[:, SUBLANES:ny, :]
        conv = conv + shifted * w_ref[SCONV_W - 1 - back:SCONV_W - back, :]
    oc_ref[...] = (bg * conv).reshape(dbt * ds, c).astype(oc_ref.dtype)
    ss_out_ref[...] = yext_ref[:, ny - hs:ny, :]

    base = 2 * SUBLANES
    npd = base + ds
    pext_ref[:, 0:base - POOL_HIST, :] = jnp.zeros((dbt, base - POOL_HIST, c), F32)
    pext_ref[:, base - POOL_HIST:base, :] = sp_ref[...]
    pext_ref[:, base:npd, :] = pd
    gc = c // len(POOL_WINDOWS)
    means = []
    for gi, window in enumerate(POOL_WINDOWS):
        win = pext_ref[:, :, gi * gc:(gi + 1) * gc].reshape(dbt * npd, gc)
        step = 1
        while step < window:
            win = win + pltpu.roll(win, step, axis=0)
            step *= 2
        means.append(win.reshape(dbt, npd, gc)[:, base:npd, :] / float(window))
    d_ref[...] = (jnp.concatenate(means, axis=-1) - pd).reshape(dbt * ds, c).astype(d_ref.dtype)
    sp_out_ref[...] = pext_ref[:, npd - POOL_HIST:npd, :]


def _sample_odd(state_s, state_p, layer, y, bg, pd, w, *, s0, db, ds, dbt):
    n, c = y.shape
    rows = dbt * ds
    row_in = pl.BlockSpec((rows, c), lambda i: (s0 // rows + i, 0))
    row_out = pl.BlockSpec((rows, c), lambda i: (i, 0))
    hs = state_s.shape[2]
    hp = state_p.shape[2]
    return pl.pallas_call(
        functools.partial(_sample_odd_kernel, ds=ds),
        grid=(db // dbt,),
        in_specs=[pl.BlockSpec((dbt, None, hs, c), lambda i: (i, layer, 0, 0)),
                  pl.BlockSpec((dbt, None, hp, c), lambda i: (i, layer, 0, 0)),
                  row_in, row_in, row_in, _full(w.shape)],
        out_specs=[row_out, row_out, pl.BlockSpec((dbt, hs, c), lambda i: (i, 0, 0)),
                   pl.BlockSpec((dbt, hp, c), lambda i: (i, 0, 0))],
        out_shape=[jax.ShapeDtypeStruct((db * ds, c), BF16 if rows % (2 * SUBLANES) == 0 else F32)] * 2 + [
                   jax.ShapeDtypeStruct((db, hs, c), F32), jax.ShapeDtypeStruct((db, hp, c), F32)],
        scratch_shapes=[pltpu.VMEM((dbt, SUBLANES + ds, c), F32), pltpu.VMEM((dbt, 2 * SUBLANES + ds, c), F32)],
        compiler_params=_params("parallel"),
        name="sample_sconv_pool",
    )(state_s, state_p, y, bg, pd, w)


def kernel(x_prompt, x_sample, cache_k, cache_v, cache_logf, page_table, state_conv_a, state_sconv, state_pool, meta_tokens, w_in_even, b_forget, w_dw_a, b_dw_a, ln_a_g, ln_a_b, w_out_even, w_in_odd, w_sconv, w_pool_mix, pool_scale, w_out_odd, ln_mix_g, ln_mix_b, w_ffn_gate, w_ffn_up, w_ffn_down, ln_ffn_g, ln_ffn_b):
    nb, seq, d = x_prompt.shape
    db, ds, _ = x_sample.shape
    depth = w_ffn_gate.shape[0]
    alpha = float((2 * depth) ** 0.25)
    ca = state_conv_a.shape[-1]
    att = N_HEADS * HEAD_DIM
    assert ds == SUBLANES and cache_k.shape[3] == N_HEADS and cache_k.shape[4] == HEAD_DIM
    assert state_conv_a.shape[2] == CONV_W - 1 and state_pool.shape[2] == POOL_HIST
    assert meta_tokens.shape[0] == N_META

    s_real = N_META + seq
    fp = (-s_real) % SEQ_TILE
    if fp < CONV_BACK:
        fp += SEQ_TILE
    sp = fp + s_real
    s0 = nb * sp
    n = s0 + db * ds
    tm = next(t for t in (512, 256, 128, 64, 32, 16, 8) if s0 % t == 0 and (db * ds) % t == 0)
    tm_in = next(t for t in (1024, 512, 256, 128, 64, 32, 16, 8) if n % t == 0)
    dbt = next(t for t in (8, 4, 2, 1) if db % t == 0)
    direct_out = seq % SEQ_TILE == 0 and (db * ds) % SEQ_TILE == 0

    head = jnp.concatenate([jnp.zeros((fp, d), F32), meta_tokens.astype(F32)], axis=0)
    x = jnp.concatenate([piece for bi in range(nb) for piece in (head, x_prompt[bi])]
                        + [x_sample.reshape(db * ds, d)], axis=0)

    n_phys = cache_k.shape[0]
    n_even = cache_k.shape[1]
    page = cache_k.shape[2]
    ck = jnp.transpose(cache_k, (0, 1, 3, 4, 2)).reshape(n_phys, n_even, att, page)
    cv = jnp.transpose(cache_v, (0, 1, 3, 4, 2)).reshape(n_phys, n_even, att, page)
    clft = jnp.swapaxes(cache_logf, 2, 3)

    row2 = lambda v: v.reshape(1, -1).astype(F32)

    def prompt_rows(arr, lo, hi):
        return jnp.stack([arr[bi * sp + lo:bi * sp + hi] for bi in range(nb)], axis=0)

    def sample_rows(arr):
        return arr[s0:].reshape(db, ds, -1)

    wg_all, wu_all, wd_all = (w.astype(BF16) for w in (w_ffn_gate, w_ffn_up, w_ffn_down))
    wo_even, wo_odd = w_out_even.astype(BF16), w_out_odd.astype(BF16)
    wi_even = jnp.pad(w_in_even.astype(BF16), ((0, 0), (0, 0), (0, LANES - N_HEADS)))
    wi_odd = w_in_odd.astype(BF16)

    def tail(name, xin, m1p, m2p, m1s, m2s, p1, p2, w_out_all, layer):
        last = direct_out and layer == depth - 1
        return _layer_tail(name, xin, m1p, m2p, m1s, m2s, p1, p2, w_out_all,
                           row2(ln_mix_g[layer]), row2(ln_mix_b[layer]), wg_all, wu_all, wd_all,
                           row2(ln_ffn_g[layer]), row2(ln_ffn_b[layer]), even=layer % 2 == 0,
                           mix_idx=layer // 2, layer=layer, tm=SEQ_TILE if last else tm, alpha=alpha,
                           nb=nb, sp=sp, fp=fp, out_rows=(seq, db * ds) if last else None)

    prev_kv = []
    lfp, cap, scp, plp = [], [], [], []
    ks_, vs_, lfs, cas, scs, pls = [], [], [], [], [], []
    for layer in range(depth):
        i = layer // 2
        if layer % 2 == 0:
            bf = jnp.pad(b_forget[i].astype(F32), (0, LANES - N_HEADS)).reshape(1, LANES)
            a, q, k, v, lf = _even_in(x, wi_even, i, bf, tm=tm_in, ca=ca, att=att)
            w_dw, b_dw = w_dw_a[i].astype(F32), row2(b_dw_a[i])
            conv_s, st_a = _sample_conv(state_conv_a.astype(F32), i, a, w_dw, b_dw, s0=s0, db=db, ds=ds, dbt=dbt)
            qa, ka, vat, conv_p, *caches = _fox_prep(q, k, v, lf, a, w_dw, b_dw, prev_kv, i == n_even - 1,
                                                     nb=nb, sp=sp, fp=fp)
            prev_kv.append((k, v))
            at_p = _flash(qa, ka, vat, hg=FLASH_HEADS)
            at_s = _sample_attn(page_table, i, q, k, v, lf, ck, cv, clft, s0=s0, db=db, ds=ds)
            x = tail("even_tail", x, conv_p, at_p, conv_s, at_s, row2(ln_a_g[i]), row2(ln_a_b[i]), wo_even, layer)
            lfp.append(prompt_rows(lf, fp, sp)[..., :N_HEADS])
            cap.append(prompt_rows(a, sp - (CONV_W - 1), sp))
            ks_.append(sample_rows(k).reshape(db, ds, N_HEADS, HEAD_DIM))
            vs_.append(sample_rows(v).reshape(db, ds, N_HEADS, HEAD_DIM))
            lfs.append(sample_rows(lf)[..., :N_HEADS])
            cas.append(st_a)
        else:
            c = state_sconv.shape[-1]
            y, bg, pd = _odd_in(x, wi_odd, i, tm=tm_in, c=c)
            oc_p, dd_p = _prompt_odd(y, bg, pd, w_sconv[i].astype(F32), nb=nb, sp=sp, fp=fp)
            oc_s, dd_s, st_s, st_p = _sample_odd(state_sconv.astype(F32), state_pool.astype(F32), i, y, bg, pd,
                                                 w_sconv[i].astype(F32), s0=s0, db=db, ds=ds, dbt=dbt)
            wm = jax.scipy.linalg.block_diag(*[w_pool_mix[i, g] for g in range(w_pool_mix.shape[1])]).astype(BF16)
            x = tail("odd_tail", x, oc_p, dd_p, oc_s, dd_s, wm, row2(pool_scale[i]), wo_odd, layer)
            scp.append(prompt_rows(y, sp - (SCONV_W - 1), sp))
            plp.append(prompt_rows(pd, sp - POOL_HIST, sp))
            scs.append(st_s)
            pls.append(st_p)

    if direct_out:
        y_prompt, y_sample = x[0], x[1].reshape(db, ds, d)
    else:
        y_prompt, y_sample = prompt_rows(x, fp + N_META, sp), sample_rows(x)
    st = lambda xs: jnp.stack(xs, axis=1)
    kp, vp = (jnp.transpose(c.reshape(nb, n_even, N_HEADS, HEAD_DIM, s_real), (0, 1, 4, 2, 3)) for c in caches)
    return (y_prompt, y_sample, kp, vp, st(lfp), st(cap), st(scp), st(plp),
            st(ks_), st(vs_), st(lfs), st(cas), st(scs), st(pls))
```

```python
import functools

import jax
import jax.numpy as jnp
from jax import lax
from jax.experimental import pallas as pl
from jax.experimental.pallas import tpu as pltpu

N_META = 16
N_HEADS = 8
HEAD_DIM = 64
CONV_W = 31
SCONV_W = 3
POOL_WINDOWS = (2, 4, 8, 16)
POOL_HIST = max(POOL_WINDOWS) - 1
LN_EPS = 1e-5
SEQ_TILE = 256
LANES = 128
SUBLANES = 8
CONV_BACK = -(-(CONV_W - 1) // SUBLANES) * SUBLANES
C_TERMS = 3
AUX_ONES = C_TERMS * N_HEADS
FLASH_HEADS = 8
V_ROWS = HEAD_DIM + 2 * SUBLANES
VMEM_LIMIT = 56 * 1024 * 1024
NEG = -1e30
LOG2E = 1.4426950408889634
F32 = jnp.float32
BF16 = jnp.bfloat16
NT_DIMS = (((1,), (1,)), ((), ()))


def _params(*sem):
    return pltpu.CompilerParams(dimension_semantics=sem, vmem_limit_bytes=VMEM_LIMIT)


def _dot(a, b):
    return jnp.dot(a, b, preferred_element_type=F32)


def _ln(z, g, b):
    mu = jnp.mean(z, axis=-1, keepdims=True)
    zc = z - mu
    var = jnp.mean(zc * zc, axis=-1, keepdims=True)
    return zc * lax.rsqrt(var + LN_EPS) * g + b


def _silu(x):
    return x * jax.nn.sigmoid(x)


def _full(shape):
    return pl.BlockSpec(shape, lambda *_: (0,) * len(shape))


def _slab(a, idx):
    return pl.BlockSpec((None,) + a.shape[1:], lambda *_: (idx,) + (0,) * (a.ndim - 1),
                        pipeline_mode=pl.Buffered(1))


def _even_in_kernel(x_ref, w_ref, bf_ref, a_ref, q_ref, k_ref, v_ref, lf_ref, *, ca, att):
    xb = x_ref[...].astype(BF16)

    def mm(lo, hi):
        return _dot(xb, w_ref[:, lo:hi])

    u = mm(0, ca)
    g = mm(ca, 2 * ca)
    a_ref[...] = u * jax.nn.sigmoid(g)
    o = 2 * ca
    q_ref[...] = mm(o, o + att) * (HEAD_DIM ** -0.5)
    k_ref[...] = mm(o + att, o + 2 * att)
    v_ref[...] = mm(o + 2 * att, o + 3 * att)
    z = mm(o + 3 * att, o + 3 * att + LANES) + bf_ref[...]
    lf = jnp.minimum(z, 0.0) - jnp.log1p(jnp.exp(-jnp.abs(z)))
    lane = lax.broadcasted_iota(jnp.int32, lf.shape, 1)
    lf_ref[...] = jnp.where(lane < N_HEADS, lf, 0.0)


def _even_in(x, w, idx, bf, *, tm, ca, att):
    n, d = x.shape
    row = lambda c: pl.BlockSpec((tm, c), lambda i: (i, 0))
    return pl.pallas_call(
        functools.partial(_even_in_kernel, ca=ca, att=att),
        grid=(n // tm,),
        in_specs=[row(d), _slab(w, idx), _full(bf.shape)],
        out_specs=[row(ca), row(att), row(att), row(att), row(LANES)],
        out_shape=[jax.ShapeDtypeStruct((n, c), F32) for c in (ca, att, att, att, LANES)],
        compiler_params=_params("parallel"),
        name="even_in_proj",
    )(x, w, bf)


def _odd_in_kernel(x_ref, w_ref, y_ref, bg_ref, pd_ref, *, c):
    xb = x_ref[...].astype(BF16)
    hc = _dot(xb, w_ref[:, 0:c])
    bg_ref[...] = _dot(xb, w_ref[:, c:2 * c])
    cg = _dot(xb, w_ref[:, 2 * c:3 * c])
    y_ref[...] = cg * hc
    pd_ref[...] = _dot(xb, w_ref[:, 3 * c:4 * c])


def _odd_in(x, w, idx, *, tm, c):
    n, d = x.shape
    row = lambda cc: pl.BlockSpec((tm, cc), lambda i: (i, 0))
    return pl.pallas_call(
        functools.partial(_odd_in_kernel, c=c),
        grid=(n // tm,),
        in_specs=[row(d), _slab(w, idx)],
        out_specs=[row(c)] * 3,
        out_shape=[jax.ShapeDtypeStruct((n, c), F32)] * 3,
        compiler_params=_params("parallel"),
        name="odd_in_proj",
    )(x, w)


def _tail_kernel(x_ref, m1p_ref, m2p_ref, m1s_ref, m2s_ref, p1_ref, p2_ref, w_ref, g1_ref, b1_ref,
                 wg_ref, wu_ref, wd_ref, g2_ref, b2_ref, *rest, even, alpha, chunk, nb, sp, fp, npt, lead):
    i = pl.program_id(0)
    is_sample = i >= npt
    m1 = jnp.where(is_sample, m1s_ref[...], m1p_ref[...])
    m2 = jnp.where(is_sample, m2s_ref[...], m2p_ref[...])
    if even:
        m1 = _silu(_ln(m1, p1_ref[...], p2_ref[...]))
    else:
        m2 = _dot(m2.astype(BF16), p1_ref[...]) * p2_ref[...]
    ca = m1.shape[1]
    mix = _dot(m1.astype(BF16), w_ref[0:ca, :]) + _dot(m2.astype(BF16), w_ref[ca:, :])
    x1 = _ln(alpha * x_ref[...] + mix, g1_ref[...], b1_ref[...])

    h_ref = rest[-1]
    xb = x1.astype(BF16)
    for c in range(0, wg_ref.shape[1], chunk):
        gate = _dot(xb, wg_ref[:, c:c + chunk])
        up = _dot(xb, wu_ref[:, c:c + chunk])
        h_ref[:, c:c + chunk] = (_silu(gate) * up).astype(BF16)
    out = _ln(alpha * x1 + _dot(h_ref[...], wd_ref[...]), g2_ref[...], b2_ref[...])

    tm = x1.shape[0]
    if lead is None:
        r = i * tm + lax.broadcasted_iota(jnp.int32, (tm, 1), 0)
        keep = jnp.ones((tm, 1), F32)
        for bi in range(nb):
            keep = jnp.where((r >= bi * sp) & (r < bi * sp + fp), 0.0, keep)
        rest[0][...] = out * keep
    else:
        yp_ref, ys_ref = rest[:2]

        @pl.when(jnp.logical_and(jnp.logical_not(is_sample), i % (sp // tm) >= lead))
        def _():
            yp_ref[...] = out

        @pl.when(is_sample)
        def _():
            ys_ref[...] = out


def _layer_tail(name, x, m1p, m2p, m1s, m2s, p1, p2, w, g1, b1, wg, wu, wd, g2, b2, *, even, mix_idx, layer, tm,
                alpha, nb, sp, fp, out_rows=None):
    n, d = x.shape
    s0 = m1p.shape[0]
    dff = wg.shape[2]
    chunk = SEQ_TILE if dff % SEQ_TILE == 0 else dff
    npt = s0 // tm
    row = lambda c: pl.BlockSpec((tm, c), lambda i: (i, 0))
    prow = lambda c: pl.BlockSpec((tm, c), lambda i: (jnp.minimum(i, npt - 1), 0))
    srow = lambda c, **kw: pl.BlockSpec((tm, c), lambda i: (jnp.maximum(i - npt, 0), 0), **kw)
    held = dict(pipeline_mode=pl.Buffered(1))
    once = lambda a: pl.BlockSpec(a.shape, lambda i: (0,) * a.ndim, pipeline_mode=pl.Buffered(1))
    consts = (p1, p2, w, g1, b1, wg, wu, wd, g2, b2)
    const_specs = [once(p1), once(p2), _slab(w, mix_idx), once(g1), once(b1),
                   _slab(wg, layer), _slab(wu, layer), _slab(wd, layer), once(g2), once(b2)]
    if out_rows is None:
        lead = None
        out_specs = row(d)
        out_shape = jax.ShapeDtypeStruct((n, d), F32)
    else:
        seq, db_rows = out_rows
        tps = sp // tm
        lead = (sp - seq) // tm
        out_specs = [
            pl.BlockSpec((None, tm, d), lambda i: (jnp.minimum(i // tps, nb - 1),
                                                   jnp.where(i < npt, jnp.maximum(i % tps - lead, 0), tps - lead - 1),
                                                   0)),
            srow(d)]
        out_shape = [jax.ShapeDtypeStruct((nb, seq, d), F32), jax.ShapeDtypeStruct((db_rows, d), F32)]
    return pl.pallas_call(
        functools.partial(_tail_kernel, even=even, alpha=alpha, chunk=chunk, nb=nb, sp=sp, fp=fp, npt=npt, lead=lead),
        grid=(n // tm,),
        in_specs=[row(d), prow(m1p.shape[1]), prow(m2p.shape[1]), srow(m1s.shape[1], **held),
                  srow(m2s.shape[1], **held)] + const_specs,
        out_specs=out_specs,
        out_shape=out_shape,
        scratch_shapes=[pltpu.VMEM((tm, dff), BF16)],
        compiler_params=_params("arbitrary" if out_rows else "parallel"),
        name=name,
    )(x, m1p, m2p, m1s, m2s, *consts)


def _conv_tile(win, w_rows):
    rows = win.shape[0]
    t = rows - CONV_BACK
    lead = CONV_BACK - (CONV_W - 1)
    acc = None
    for r in range(SUBLANES):
        rolled = win if r == 0 else pltpu.roll(win, rows - r, axis=0)
        for m in range(CONV_BACK // SUBLANES + 1):
            j = SUBLANES * m + r - lead
            if 0 <= j < CONV_W:
                term = rolled[SUBLANES * m:SUBLANES * m + t] * w_rows[j]
                acc = term if acc is None else acc + term
    return acc


def _sample_conv_kernel(st_ref, a_ref, w_ref, b_ref, o_ref, st_out_ref, ext_ref, *, ds):
    dbt, hist, c = st_ref.shape
    off = CONV_BACK - hist
    rows = CONV_BACK + ds
    ext_ref[:, 0:off, :] = jnp.zeros((dbt, off, c), F32)
    ext_ref[:, off:CONV_BACK, :] = st_ref[...]
    ext_ref[:, CONV_BACK:rows, :] = a_ref[...].reshape(dbt, ds, c)
    flat = ext_ref[...].reshape(dbt * rows, c)
    acc = None
    for r in range(SUBLANES):
        shifted = (flat if r == 0 else pltpu.roll(flat, dbt * rows - r, axis=0)).reshape(dbt, rows, c)
        for m in range(rows // SUBLANES):
            j = SUBLANES * m + r - off
            if 0 <= j < CONV_W:
                term = shifted[:, SUBLANES * m:SUBLANES * m + ds, :] * w_ref[j:j + 1, :]
                acc = term if acc is None else acc + term
    o_ref[...] = (acc + b_ref[...]).reshape(dbt * ds, c)
    st_out_ref[...] = ext_ref[:, off + ds:rows, :]


def _sample_conv(state, layer, a, w, b, *, s0, db, ds, dbt):
    n, c = a.shape
    hist = state.shape[2]
    rows = dbt * ds
    return pl.pallas_call(
        functools.partial(_sample_conv_kernel, ds=ds),
        grid=(db // dbt,),
        in_specs=[pl.BlockSpec((dbt, None, hist, c), lambda i: (i, layer, 0, 0)),
                  pl.BlockSpec((rows, c), lambda i: (s0 // rows + i, 0)), _full(w.shape), _full(b.shape)],
        out_specs=[pl.BlockSpec((rows, c), lambda i: (i, 0)), pl.BlockSpec((dbt, hist, c), lambda i: (i, 0, 0))],
        out_shape=[jax.ShapeDtypeStruct((db * ds, c), F32), jax.ShapeDtypeStruct((db, hist, c), F32)],
        scratch_shapes=[pltpu.VMEM((dbt, CONV_BACK + ds, c), F32)],
        compiler_params=_params("parallel"),
        name="sample_conv",
    )(state, a, w, b)


def _fox_prep_kernel(q_ref, k_ref, v_ref, lf_ref, a_ref, wdw_ref, bdw_ref, *rest, fp, ns, n_prev, emit):
    prev, rest = rest[:2 * n_prev], rest[2 * n_prev:]
    qa_ref, ka_ref, vat_ref, cv_ref = rest[:4]
    carry_ref, ah_ref = rest[6:8] if emit else rest[4:6]
    s = pl.program_id(1)
    t = SEQ_TILE

    @pl.when(s < ns)
    def _():
        _fox_operands(s, q_ref, k_ref, v_ref, lf_ref, qa_ref, ka_ref, vat_ref, carry_ref, fp)

        @pl.when(s == 0)
        def _():
            ah_ref[...] = jnp.zeros_like(ah_ref)

        for cb in range(a_ref.shape[1] // LANES):
            cs = slice(cb * LANES, (cb + 1) * LANES)
            win = jnp.concatenate([ah_ref[:, cs], a_ref[:, cs]], axis=0)
            w_rows = [wdw_ref[j:j + 1, cs] for j in range(CONV_W)]
            cv_ref[:, cs] = _conv_tile(win, w_rows) + bdw_ref[:, cs]
        ah_ref[...] = a_ref[t - CONV_BACK:t, :]

    if not emit:
        return
    kt_ref, vt_ref = rest[4:6]
    kc_ref, vc_ref = rest[8:10]

    @pl.when(s == 0)
    def _():
        kc_ref[...] = jnp.zeros_like(kc_ref)
        vc_ref[...] = jnp.zeros_like(vc_ref)

    k_srcs = list(prev[0::2]) + [k_ref]
    v_srcs = list(prev[1::2]) + [v_ref]
    for srcs, held, dst in ((k_srcs, kc_ref, kt_ref), (v_srcs, vc_ref, vt_ref)):
        for li, src in enumerate(srcs):
            rows = jnp.concatenate([held[li], src[0:fp, :]], axis=0)
            for c in range(src.shape[1] // LANES):
                dst[li, c * LANES:(c + 1) * LANES, :] = rows[:, c * LANES:(c + 1) * LANES].T
            held[li] = src[fp:t, :]


def _fox_operands(s, q_ref, k_ref, v_ref, lf_ref, qa_ref, ka_ref, vat_ref, carry_ref, fp):
    @pl.when(s == 0)
    def _():
        carry_ref[...] = jnp.zeros_like(carry_ref)

    t = SEQ_TILE
    row = lax.broadcasted_iota(jnp.int32, (t, LANES), 0)
    lane = lax.broadcasted_iota(jnp.int32, (t, LANES), 1)
    c = jnp.where(s * t + row >= fp, lf_ref[...], 0.0)
    sh = 1
    while sh < t:
        c = c + jnp.where(row >= sh, pltpu.roll(c, sh, axis=0), 0.0)
        sh *= 2
    c = c + carry_ref[0:1, :]
    carry_ref[0:1, :] = c[t - 1:t, :]

    c = c * LOG2E
    c1 = c.astype(BF16).astype(F32)
    r1 = c - c1
    c2 = r1.astype(BF16).astype(F32)
    c3 = (r1 - c2).astype(BF16).astype(F32)
    g23 = jnp.where(lane < 2 * N_HEADS, pltpu.roll(c2, N_HEADS, axis=1),
                    jnp.where(lane < 3 * N_HEADS, pltpu.roll(c3, 2 * N_HEADS, axis=1), 0.0))
    g = jnp.where(lane < N_HEADS, c1, g23)
    gk = jnp.where(lane < N_HEADS, jnp.where(s * t + row >= fp, c1, -NEG), g23)
    ones_grp = (lane >= HEAD_DIM + AUX_ONES) & (lane < HEAD_DIM + 2 * AUX_ONES)
    ck = [jnp.where((lane >= HEAD_DIM) & (lane < HEAD_DIM + AUX_ONES), -pltpu.roll(gk, HEAD_DIM, axis=1),
                    jnp.where(ones_grp, 1.0, 0.0))]
    cq = [jnp.where(ones_grp, pltpu.roll(g, HEAD_DIM + AUX_ONES, axis=1), 0.0)]
    ck.append(pltpu.roll(ck[0], HEAD_DIM, axis=1))
    cq.append(pltpu.roll(cq[0], HEAD_DIM, axis=1))

    lane1 = lax.broadcasted_iota(jnp.int32, (1, LANES), 1)
    for h in range(N_HEADS):
        odd = h % 2
        pr = h // 2
        base = 0 if odd else HEAD_DIM
        a = lane1 - base
        mine = (lane1 & (N_HEADS - 1)) == h
        data = ((lane1 >= HEAD_DIM) if odd else (lane1 < HEAD_DIM)).astype(F32)
        sel1 = ((a >= 0) & (a < AUX_ONES) & mine).astype(F32)
        selc = ((a >= AUX_ONES) & (a < 2 * AUX_ONES) & mine).astype(F32)
        sl = slice(pr * LANES, (pr + 1) * LANES)
        qa_ref[h] = (q_ref[:, sl] * (data * LOG2E) + (cq[odd] * selc + sel1)).T.astype(BF16)
        ka_ref[h] = (k_ref[:, sl] * data + ck[odd]).astype(BF16)
    ones = jnp.ones((V_ROWS - HEAD_DIM, t), BF16)
    for pr in range(N_HEADS // 2):
        v_t = v_ref[:, pr * LANES:(pr + 1) * LANES].T.astype(BF16)
        vat_ref[2 * pr] = jnp.concatenate([v_t[0:HEAD_DIM], ones], axis=0)
        vat_ref[2 * pr + 1] = jnp.concatenate([v_t[HEAD_DIM:], ones], axis=0)


def _fox_prep(q, k, v, lf, a, w_dw, b_dw, prev_kv, emit, *, nb, sp, fp):
    ns = sp // SEQ_TILE
    att = q.shape[1]
    ca = a.shape[1]
    t = SEQ_TILE
    last = ns - 1
    row = lambda c: pl.BlockSpec((t, c), lambda bi, si: (bi * ns + jnp.minimum(si, last), 0))
    hm = pl.BlockSpec((None, N_HEADS, t, LANES), lambda bi, si: (bi, 0, jnp.minimum(si, last), 0))
    hm_t = lambda r: pl.BlockSpec((None, N_HEADS, r, t), lambda bi, si: (bi, 0, 0, jnp.minimum(si, last)))
    out_specs = [hm_t(LANES), hm, hm_t(V_ROWS), row(ca)]
    out_shape = [jax.ShapeDtypeStruct((nb, N_HEADS, LANES, sp), BF16),
                 jax.ShapeDtypeStruct((nb, N_HEADS, sp, LANES), BF16),
                 jax.ShapeDtypeStruct((nb, N_HEADS, V_ROWS, sp), BF16), jax.ShapeDtypeStruct((nb * sp, ca), F32)]
    scratch = [pltpu.VMEM((SUBLANES, LANES), F32), pltpu.VMEM((CONV_BACK, ca), F32)]
    n_prev = len(prev_kv) if emit else 0
    if emit:
        assert 0 < fp < t and fp % SUBLANES == 0
        n_layers = n_prev + 1
        cache = pl.BlockSpec((None, n_layers, att, t), lambda bi, si: (bi, 0, 0, jnp.maximum(si - 1, 0)))
        out_specs += [cache, cache]
        out_shape += [jax.ShapeDtypeStruct((nb, n_layers, att, sp - fp), F32)] * 2
        scratch += [pltpu.VMEM((n_layers, t - fp, att), F32)] * 2
    prev = [a for kv in prev_kv for a in kv] if emit else []
    return pl.pallas_call(
        functools.partial(_fox_prep_kernel, fp=fp, ns=ns, n_prev=n_prev, emit=emit),
        grid=(nb, ns + 1 if emit else ns),
        in_specs=[row(att), row(att), row(att), row(LANES), row(ca), _full(w_dw.shape), _full(b_dw.shape)]
        + [row(att)] * len(prev),
        out_specs=out_specs,
        out_shape=out_shape,
        scratch_shapes=scratch,
        compiler_params=_params("parallel", "arbitrary"),
        name="fox_prep",
    )(q, k, v, lf, a, w_dw, b_dw, *prev)


def _flash_kernel(q_ref, k_ref, vt_ref, o_ref, s_ref, p_ref, acc_ref, *, hg):
    i = pl.program_id(2)
    t = SEQ_TILE

    def scores_to(j, slot):
        ks = pl.multiple_of(j * t, t)
        for h in range(hg):
            s_ref[slot, h] = _dot(k_ref[h, pl.ds(ks, t), :], q_ref[h])

    def accumulate(j, als):
        ks = pl.multiple_of(j * t, t)
        pvs = [_dot(vt_ref[h, :, pl.ds(ks, t)], p_ref[h]) for h in range(hg)]
        for h in range(hg):
            acc_ref[h] = als[h] * acc_ref[h] + pvs[h]

    def softmax(slot, ms, masked):
        new_m, als = [], []
        for h in range(hg):
            s = s_ref[slot, h]
            if masked:
                visible = (lax.broadcasted_iota(jnp.int32, (t, t), 0) <= lax.broadcasted_iota(jnp.int32, (t, t), 1))
                s = jnp.where(visible, s, NEG)
            mn = jnp.maximum(ms[h], jnp.max(s, axis=0, keepdims=True))
            als.append(jnp.exp2(ms[h] - mn))
            new_m.append(mn)
            p_ref[h] = jnp.exp2(s - mn).astype(BF16)
        return tuple(new_m), tuple(als)

    def stage(j, slot, state):
        ms, als = state
        scores_to(j + 1, 1 - slot)
        accumulate(jnp.maximum(j - 1, 0), als)
        return softmax(slot, ms, False)

    def pair(jj, state):
        return stage(2 * jj + 1, 1, stage(2 * jj, 0, state))

    def finish(slot, state):
        ms, als = state
        accumulate(jnp.maximum(i - 1, 0), als)
        _, als = softmax(slot, ms, True)
        accumulate(i, als)
        for pr in range(hg // 2):
            halves = [acc_ref[h, 0:HEAD_DIM, :] * (1.0 / acc_ref[h, HEAD_DIM:HEAD_DIM + 1, :])
                      for h in (2 * pr, 2 * pr + 1)]
            o_ref[:, pr * LANES:(pr + 1) * LANES] = jnp.concatenate(halves, axis=0).T.astype(o_ref.dtype)

    p_ref[...] = jnp.zeros_like(p_ref)
    acc_ref[...] = jnp.zeros_like(acc_ref)
    scores_to(0, 0)
    state = ((jnp.full((1, t), NEG, F32),) * hg, (jnp.ones((1, t), F32),) * hg)
    state = lax.fori_loop(0, i // 2, pair, state)
    odd = i % 2 == 1
    state = lax.cond(odd, lambda st: stage(i - 1, 0, st), lambda st: st, state)
    pl.when(odd)(lambda: finish(1, state))
    pl.when(jnp.logical_not(odd))(lambda: finish(0, state))


def _flash(qat, ka, vat, *, hg):
    nb, _, sp, _ = ka.shape
    ns = sp // SEQ_TILE
    return pl.pallas_call(
        functools.partial(_flash_kernel, hg=hg),
        grid=(nb, N_HEADS // hg, ns),
        in_specs=[pl.BlockSpec((None, hg, LANES, SEQ_TILE), lambda bi, gi, qi: (bi, gi, 0, qi)),
                  pl.BlockSpec((None, hg, sp, LANES), lambda bi, gi, qi: (bi, gi, 0, 0)),
                  pl.BlockSpec((None, hg, V_ROWS, sp), lambda bi, gi, qi: (bi, gi, 0, 0))],
        out_specs=pl.BlockSpec((SEQ_TILE, hg * HEAD_DIM), lambda bi, gi, qi: (bi * ns + qi, gi)),
        out_shape=jax.ShapeDtypeStruct((nb * sp, N_HEADS * HEAD_DIM), BF16),
        scratch_shapes=[pltpu.VMEM((2, hg, SEQ_TILE, SEQ_TILE), F32), pltpu.VMEM((hg, SEQ_TILE, SEQ_TILE), BF16),
                        pltpu.VMEM((hg, V_ROWS, SEQ_TILE), F32)],
        compiler_params=_params("parallel", "parallel", "arbitrary"),
        name="fox_flash",
    )(qat, ka, vat)


def _sample_attn_kernel(pt_ref, q_ref, kn_ref, vn_ref, lfn_ref, *rest, n_pages, page, eb, ds):
    del pt_ref
    o_ref = rest[3 * eb * n_pages]
    c = q_ref.shape[1]
    rows = ds * N_HEADS
    groups = [list(range(g, min(g + 2, n_pages))) for g in range(0, n_pages, 2)]
    past = n_pages * page
    sub = lax.broadcasted_iota(jnp.int32, (N_HEADS, c), 0)
    lane = lax.broadcasted_iota(jnp.int32, (N_HEADS, c), 1)
    hm = ((lane >= sub * HEAD_DIM) & (lane < (sub + 1) * HEAD_DIM)).astype(F32)
    hm_t = jnp.concatenate([hm] * ds, axis=0)
    pad = jnp.zeros((LANES - ds, c), F32)

    def refs_of(kind, e):
        return rest[(kind * eb + e) * n_pages:(kind * eb + e + 1) * n_pages]

    def scores(e):
        sl = slice(e * ds, (e + 1) * ds)
        q = q_ref[sl, :]
        qbd = jnp.concatenate([jnp.broadcast_to(q[t:t + 1, :], (N_HEADS, c)) * hm for t in range(ds)],
                              axis=0).astype(BF16)
        k_refs, lf_refs = refs_of(0, e), refs_of(2, e)
        s_parts = []
        for grp in groups:
            kp = jnp.concatenate([k_refs[r][...] for r in grp], axis=1).astype(BF16)
            s_parts.append(_dot(qbd, kp))
        s_past = jnp.concatenate(s_parts, axis=1)

        lf = jnp.concatenate([r[...] for r in lf_refs], axis=1)
        lane_p = lax.broadcasted_iota(jnp.int32, (N_HEADS, past), 1)
        suf = lf
        sh = 1
        while sh < past:
            suf = suf + jnp.where(lane_p < past - sh, pltpu.roll(suf, past - sh, axis=1), 0.0)
            sh *= 2
        s_past = s_past + jnp.concatenate([suf - lf] * ds, axis=0)

        kn = jnp.concatenate([kn_ref[sl, :], pad], axis=0).astype(BF16)
        s_new = lax.dot_general(qbd, kn, NT_DIMS, preferred_element_type=F32)
        cn = lfn_ref[sl, :]
        sub_n = lax.broadcasted_iota(jnp.int32, cn.shape, 0)
        sh = 1
        while sh < ds:
            cn = cn + jnp.where(sub_n >= sh, pltpu.roll(cn, sh, axis=0), 0.0)
            sh *= 2
        cn_t = jnp.concatenate([cn, jnp.zeros((LANES - ds, LANES), F32)], axis=0).T[0:N_HEADS, :]
        rown = lax.broadcasted_iota(jnp.int32, (rows, LANES), 0)
        u = lax.broadcasted_iota(jnp.int32, (rows, LANES), 1)
        s_new = jnp.where(u * N_HEADS <= rown, s_new - jnp.concatenate([cn_t] * ds, axis=0), NEG)
        return s_past, s_new

    def softmax(s_past, s_new):
        m = jnp.maximum(jnp.max(s_past, axis=1, keepdims=True), jnp.max(s_new, axis=1, keepdims=True))
        p_past = jnp.exp(s_past - m)
        p_new = jnp.exp(s_new - m)
        l = jnp.sum(p_past, axis=1, keepdims=True) + jnp.sum(p_new, axis=1, keepdims=True)
        return p_past.astype(BF16), p_new.astype(BF16), l

    def output(e, p_past, p_new, l):
        sl = slice(e * ds, (e + 1) * ds)
        v_refs = refs_of(1, e)
        vn = jnp.concatenate([vn_ref[sl, :], pad], axis=0).astype(BF16)
        o = _dot(p_new, vn)
        for grp in groups:
            vp = jnp.concatenate([v_refs[r][...] for r in grp], axis=1).astype(BF16)
            lo = grp[0] * page
            o = o + lax.dot_general(p_past[:, lo:lo + len(grp) * page], vp, NT_DIMS, preferred_element_type=F32)
        o = o * (1.0 / l) * hm_t
        rr = lax.broadcasted_iota(jnp.int32, (ds, rows), 1)
        tt = lax.broadcasted_iota(jnp.int32, (ds, rows), 0)
        pick = ((rr >= tt * N_HEADS) & (rr < (tt + 1) * N_HEADS)).astype(BF16)
        o_ref[sl, :] = _dot(pick, o.astype(BF16)).astype(o_ref.dtype)

    ss = [scores(e) for e in range(eb)]
    ps = [softmax(*s) for s in ss]
    for e in range(eb):
        output(e, *ps[e])


def _sample_attn(page_table, layer, q, k, v, lf, cache_k, cache_v, cache_lft, *, s0, db, ds):
    n, c = q.shape
    n_pages = page_table.shape[1]
    page = cache_k.shape[3]
    eb = 2 if db % 2 == 0 else 1
    rows = eb * ds
    row = lambda cc: pl.BlockSpec((rows, cc), lambda i, pt: (s0 // rows + i, 0))

    def paged(shape, e, r):
        return pl.BlockSpec((None, None) + shape, lambda i, pt: (pt[(i * eb + e) * n_pages + r], layer, 0, 0))

    pages = [(e, r) for e in range(eb) for r in range(n_pages)]
    in_specs = ([row(c), row(c), row(c), row(LANES)]
                + [paged((c, page), e, r) for e, r in pages]
                + [paged((c, page), e, r) for e, r in pages]
                + [paged((N_HEADS, page), e, r) for e, r in pages])
    return pl.pallas_call(
        functools.partial(_sample_attn_kernel, n_pages=n_pages, page=page, eb=eb, ds=ds),
        grid_spec=pltpu.PrefetchScalarGridSpec(
            num_scalar_prefetch=1, grid=(db // eb,), in_specs=in_specs,
            out_specs=pl.BlockSpec((rows, c), lambda i, pt: (i, 0))),
        out_shape=jax.ShapeDtypeStruct((db * ds, c), BF16 if rows % (2 * SUBLANES) == 0 else F32),
        compiler_params=_params("parallel"),
        name="sample_paged_attn",
    )(page_table.reshape(-1), q, k, v, lf, *([cache_k] * len(pages)), *([cache_v] * len(pages)),
      *([cache_lft] * len(pages)))


def _odd_tile(wy, wp, bg, w_rows, window, pos0):
    t = bg.shape[0]
    y0 = wy[SUBLANES:]
    y1 = pltpu.roll(wy, 1, axis=0)[SUBLANES:]
    y2 = pltpu.roll(wy, 2, axis=0)[SUBLANES:]
    out_c = bg * (w_rows[0] * y2 + w_rows[1] * y1 + w_rows[2] * y0)
    back = 2 * SUBLANES
    win = wp
    step = 1
    while step < window:
        win = win + pltpu.roll(win, step, axis=0)
        step *= 2
    if pos0 is None:
        cnt = float(window)
    else:
        pos = pos0 + lax.broadcasted_iota(jnp.int32, (t, LANES), 0)
        cnt = jnp.maximum(jnp.minimum(window, pos + 1), 1).astype(F32)
    return out_c, win[back:] / cnt - wp[back:]


def _prompt_odd_kernel(y_ref, bg_ref, pd_ref, w_ref, oc_ref, d_ref, *, n_tiles, fp):
    w_rows = [w_ref[j:j + 1, :] for j in range(SCONV_W)]
    c = y_ref.shape[1]
    t = SEQ_TILE
    n_partial = -(-(fp + POOL_HIST) // t)

    def run(window):
        wy0 = jnp.concatenate([jnp.zeros((SUBLANES, c), F32), y_ref[0:t, :]], axis=0)
        wp0 = jnp.concatenate([jnp.zeros((2 * SUBLANES, c), F32), pd_ref[0:t, :]], axis=0)
        oc, d = _odd_tile(wy0, wp0, bg_ref[0:t, :], w_rows, window, -fp)
        oc_ref[0:t, :] = oc.astype(oc_ref.dtype)
        d_ref[0:t, :] = d.astype(d_ref.dtype)

        def body(i, partial):
            s0 = pl.multiple_of(i * t, t)
            wy = y_ref[pl.ds(s0 - SUBLANES, t + SUBLANES), :]
            wp = pd_ref[pl.ds(s0 - 2 * SUBLANES, t + 2 * SUBLANES), :]
            oc, d = _odd_tile(wy, wp, bg_ref[pl.ds(s0, t), :], w_rows, window, s0 - fp if partial else None)
            oc_ref[pl.ds(s0, t), :] = oc.astype(oc_ref.dtype)
            d_ref[pl.ds(s0, t), :] = d.astype(d_ref.dtype)

        lax.fori_loop(1, n_partial, lambda i, carry: body(i, True) or carry, 0)
        lax.fori_loop(n_partial, n_tiles, lambda i, carry: body(i, False) or carry, 0)

    for gi, window in enumerate(POOL_WINDOWS):
        pl.when(pl.program_id(1) == gi)(functools.partial(run, window))


def _prompt_odd(y, bg, pd, w, *, nb, sp, fp):
    n, c = y.shape
    assert c // LANES == len(POOL_WINDOWS)
    blk = pl.BlockSpec((sp, LANES), lambda bi, ci: (bi, ci))
    return pl.pallas_call(
        functools.partial(_prompt_odd_kernel, n_tiles=sp // SEQ_TILE, fp=fp),
        grid=(nb, c // LANES),
        in_specs=[blk, blk, blk, pl.BlockSpec((SCONV_W, LANES), lambda bi, ci: (0, ci))],
        out_specs=[blk, blk],
        out_shape=[jax.ShapeDtypeStruct((nb * sp, c), BF16)] * 2,
        compiler_params=_params("parallel", "parallel"),
        name="prompt_sconv_pool",
    )(y, bg, pd, w)


def _sample_odd_kernel(ss_ref, sp_ref, y_ref, bg_ref, pd_ref, w_ref,
                       oc_ref, d_ref, ss_out_ref, sp_out_ref, yext_ref, pext_ref, *, ds):
    dbt, _, c = ss_ref.shape
    hs = SCONV_W - 1
    y = y_ref[...].reshape(dbt, ds, c)
    pd = pd_ref[...].reshape(dbt, ds, c)
    bg = bg_ref[...].reshape(dbt, ds, c)
    ny = SUBLANES + ds
    yext_ref[:, 0:SUBLANES - hs, :] = jnp.zeros((dbt, SUBLANES - hs, c), F32)
    yext_ref[:, SUBLANES - hs:SUBLANES, :] = ss_ref[...]
    yext_ref[:, SUBLANES:ny, :] = y
    yflat = yext_ref[...].reshape(dbt * ny, c)
    conv = y * w_ref[SCONV_W - 1:SCONV_W, :]
    for back in range(1, SCONV_W):
        shifted = pltpu.roll(yflat, back, axis=0).reshape(dbt, ny, c)[:, SUBLANES:ny, :]
        conv = conv + shifted * w_ref[SCONV_W - 1 - back:SCONV_W - back, :]
    oc_ref[...] = (bg * conv).reshape(dbt * ds, c).astype(oc_ref.dtype)
    ss_out_ref[...] = yext_ref[:, ny - hs:ny, :]

    base = 2 * SUBLANES
    npd = base + ds
    pext_ref[:, 0:base - POOL_HIST, :] = jnp.zeros((dbt, base - POOL_HIST, c), F32)
    pext_ref[:, base - POOL_HIST:base, :] = sp_ref[...]
    pext_ref[:, base:npd, :] = pd
    gc = c // len(POOL_WINDOWS)
    means = []
    for gi, window in enumerate(POOL_WINDOWS):
        win = pext_ref[:, :, gi * gc:(gi + 1) * gc].reshape(dbt * npd, gc)
        step = 1
        while step < window:
            win = win + pltpu.roll(win, step, axis=0)
            step *= 2
        means.append(win.reshape(dbt, npd, gc)[:, base:npd, :] / float(window))
    d_ref[...] = (jnp.concatenate(means, axis=-1) - pd).reshape(dbt * ds, c).astype(d_ref.dtype)
    sp_out_ref[...] = pext_ref[:, npd - POOL_HIST:npd, :]


def _sample_odd(state_s, state_p, layer, y, bg, pd, w, *, s0, db, ds, dbt):
    n, c = y.shape
    rows = dbt * ds
    row_in = pl.BlockSpec((rows, c), lambda i: (s0 // rows + i, 0))
    row_out = pl.BlockSpec((rows, c), lambda i: (i, 0))
    hs = state_s.shape[2]
    hp = state_p.shape[2]
    return pl.pallas_call(
        functools.partial(_sample_odd_kernel, ds=ds),
        grid=(db // dbt,),
        in_specs=[pl.BlockSpec((dbt, None, hs, c), lambda i: (i, layer, 0, 0)),
                  pl.BlockSpec((dbt, None, hp, c), lambda i: (i, layer, 0, 0)),
                  row_in, row_in, row_in, _full(w.shape)],
        out_specs=[row_out, row_out, pl.BlockSpec((dbt, hs, c), lambda i: (i, 0, 0)),
                   pl.BlockSpec((dbt, hp, c), lambda i: (i, 0, 0))],
        out_shape=[jax.ShapeDtypeStruct((db * ds, c), BF16 if rows % (2 * SUBLANES) == 0 else F32)] * 2 + [
                   jax.ShapeDtypeStruct((db, hs, c), F32), jax.ShapeDtypeStruct((db, hp, c), F32)],
        scratch_shapes=[pltpu.VMEM((dbt, SUBLANES + ds, c), F32), pltpu.VMEM((dbt, 2 * SUBLANES + ds, c), F32)],
        compiler_params=_params("parallel"),
        name="sample_sconv_pool",
    )(state_s, state_p, y, bg, pd, w)


def kernel(x_prompt, x_sample, cache_k, cache_v, cache_logf, page_table, state_conv_a, state_sconv, state_pool, meta_tokens, w_in_even, b_forget, w_dw_a, b_dw_a, ln_a_g, ln_a_b, w_out_even, w_in_odd, w_sconv, w_pool_mix, pool_scale, w_out_odd, ln_mix_g, ln_mix_b, w_ffn_gate, w_ffn_up, w_ffn_down, ln_ffn_g, ln_ffn_b):
    nb, seq, d = x_prompt.shape
    db, ds, _ = x_sample.shape
    depth = w_ffn_gate.shape[0]
    alpha = float((2 * depth) ** 0.25)
    ca = state_conv_a.shape[-1]
    att = N_HEADS * HEAD_DIM
    assert ds == SUBLANES and cache_k.shape[3] == N_HEADS and cache_k.shape[4] == HEAD_DIM
    assert state_conv_a.shape[2] == CONV_W - 1 and state_pool.shape[2] == POOL_HIST
    assert meta_tokens.shape[0] == N_META

    s_real = N_META + seq
    fp = (-s_real) % SEQ_TILE
    if fp < CONV_BACK:
        fp += SEQ_TILE
    sp = fp + s_real
    s0 = nb * sp
    n = s0 + db * ds
    tm = next(t for t in (1024, 512, 256, 128, 64, 32, 16, 8) if s0 % t == 0 and (db * ds) % t == 0)
    tm_in = next(t for t in (1024, 512, 256, 128, 64, 32, 16, 8) if n % t == 0)
    dbt = next(t for t in (8, 4, 2, 1) if db % t == 0)
    direct_out = seq % SEQ_TILE == 0 and (db * ds) % SEQ_TILE == 0

    head = jnp.concatenate([jnp.zeros((fp, d), F32), meta_tokens.astype(F32)], axis=0)
    x = jnp.concatenate([piece for bi in range(nb) for piece in (head, x_prompt[bi])]
                        + [x_sample.reshape(db * ds, d)], axis=0)

    n_phys = cache_k.shape[0]
    n_even = cache_k.shape[1]
    page = cache_k.shape[2]
    ck = jnp.transpose(cache_k, (0, 1, 3, 4, 2)).reshape(n_phys, n_even, att, page)
    cv = jnp.transpose(cache_v, (0, 1, 3, 4, 2)).reshape(n_phys, n_even, att, page)
    clft = jnp.swapaxes(cache_logf, 2, 3)

    row2 = lambda v: v.reshape(1, -1).astype(F32)

    def prompt_rows(arr, lo, hi):
        return jnp.stack([arr[bi * sp + lo:bi * sp + hi] for bi in range(nb)], axis=0)

    def sample_rows(arr):
        return arr[s0:].reshape(db, ds, -1)

    wg_all, wu_all, wd_all = (w.astype(BF16) for w in (w_ffn_gate, w_ffn_up, w_ffn_down))
    wo_even, wo_odd = w_out_even.astype(BF16), w_out_odd.astype(BF16)
    wi_even = jnp.pad(w_in_even.astype(BF16), ((0, 0), (0, 0), (0, LANES - N_HEADS)))
    wi_odd = w_in_odd.astype(BF16)

    def tail(name, xin, m1p, m2p, m1s, m2s, p1, p2, w_out_all, layer):
        last = direct_out and layer == depth - 1
        return _layer_tail(name, xin, m1p, m2p, m1s, m2s, p1, p2, w_out_all,
                           row2(ln_mix_g[layer]), row2(ln_mix_b[layer]), wg_all, wu_all, wd_all,
                           row2(ln_ffn_g[layer]), row2(ln_ffn_b[layer]), even=layer % 2 == 0,
                           mix_idx=layer // 2, layer=layer, tm=SEQ_TILE if last else tm, alpha=alpha,
                           nb=nb, sp=sp, fp=fp, out_rows=(seq, db * ds) if last else None)

    prev_kv = []
    lfp, cap, scp, plp = [], [], [], []
    ks_, vs_, lfs, cas, scs, pls = [], [], [], [], [], []
    for layer in range(depth):
        i = layer // 2
        if layer % 2 == 0:
            bf = jnp.pad(b_forget[i].astype(F32), (0, LANES - N_HEADS)).reshape(1, LANES)
            a, q, k, v, lf = _even_in(x, wi_even, i, bf, tm=tm_in, ca=ca, att=att)
            w_dw, b_dw = w_dw_a[i].astype(F32), row2(b_dw_a[i])
            conv_s, st_a = _sample_conv(state_conv_a.astype(F32), i, a, w_dw, b_dw, s0=s0, db=db, ds=ds, dbt=dbt)
            qa, ka, vat, conv_p, *caches = _fox_prep(q, k, v, lf, a, w_dw, b_dw, prev_kv, i == n_even - 1,
                                                     nb=nb, sp=sp, fp=fp)
            prev_kv.append((k, v))
            at_p = _flash(qa, ka, vat, hg=FLASH_HEADS)
            at_s = _sample_attn(page_table, i, q, k, v, lf, ck, cv, clft, s0=s0, db=db, ds=ds)
            x = tail("even_tail", x, conv_p, at_p, conv_s, at_s, row2(ln_a_g[i]), row2(ln_a_b[i]), wo_even, layer)
            lfp.append(prompt_rows(lf, fp, sp)[..., :N_HEADS])
            cap.append(prompt_rows(a, sp - (CONV_W - 1), sp))
            ks_.append(sample_rows(k).reshape(db, ds, N_HEADS, HEAD_DIM))
            vs_.append(sample_rows(v).reshape(db, ds, N_HEADS, HEAD_DIM))
            lfs.append(sample_rows(lf)[..., :N_HEADS])
            cas.append(st_a)
        else:
            c = state_sconv.shape[-1]
            y, bg, pd = _odd_in(x, wi_odd, i, tm=tm_in, c=c)
            oc_p, dd_p = _prompt_odd(y, bg, pd, w_sconv[i].astype(F32), nb=nb, sp=sp, fp=fp)
            oc_s, dd_s, st_s, st_p = _sample_odd(state_sconv.astype(F32), state_pool.astype(F32), i, y, bg, pd,
                                                 w_sconv[i].astype(F32), s0=s0, db=db, ds=ds, dbt=dbt)
            wm = jax.scipy.linalg.block_diag(*[w_pool_mix[i, g] for g in range(w_pool_mix.shape[1])]).astype(BF16)
            x = tail("odd_tail", x, oc_p, dd_p, oc_s, dd_s, wm, row2(pool_scale[i]), wo_odd, layer)
            scp.append(prompt_rows(y, sp - (SCONV_W - 1), sp))
            plp.append(prompt_rows(pd, sp - POOL_HIST, sp))
            scs.append(st_s)
            pls.append(st_p)

    if direct_out:
        y_prompt, y_sample = x[0], x[1].reshape(db, ds, d)
    else:
        y_prompt, y_sample = prompt_rows(x, fp + N_META, sp), sample_rows(x)
    st = lambda xs: jnp.stack(xs, axis=1)
    kp, vp = (jnp.transpose(c.reshape(nb, n_even, N_HEADS, HEAD_DIM, s_real), (0, 1, 4, 2, 3)) for c in caches)
    return (y_prompt, y_sample, kp, vp, st(lfp), st(cap), st(scp), st(plp),
            st(ks_), st(vs_), st(lfs), st(cas), st(scs), st(pls))
```

```python
import functools

import jax
import jax.numpy as jnp
from jax import lax
from jax.experimental import pallas as pl
from jax.experimental.pallas import tpu as pltpu

N_META = 16
N_HEADS = 8
HEAD_DIM = 64
CONV_W = 31
SCONV_W = 3
POOL_WINDOWS = (2, 4, 8, 16)
POOL_HIST = max(POOL_WINDOWS) - 1
LN_EPS = 1e-5
SEQ_TILE = 256
LANES = 128
SUBLANES = 8
CONV_BACK = -(-(CONV_W - 1) // SUBLANES) * SUBLANES
C_TERMS = 3
AUX_ONES = C_TERMS * N_HEADS
FLASH_HEADS = 8
V_ROWS = HEAD_DIM + 2 * SUBLANES
VMEM_LIMIT = 56 * 1024 * 1024
NEG = -1e30
LOG2E = 1.4426950408889634
F32 = jnp.float32
BF16 = jnp.bfloat16
NT_DIMS = (((1,), (1,)), ((), ()))


def _params(*sem):
    return pltpu.CompilerParams(dimension_semantics=sem, vmem_limit_bytes=VMEM_LIMIT)


def _dot(a, b):
    return jnp.dot(a, b, preferred_element_type=F32)


def _ln(z, g, b):
    mu = jnp.mean(z, axis=-1, keepdims=True)
    zc = z - mu
    var = jnp.mean(zc * zc, axis=-1, keepdims=True)
    return zc * lax.rsqrt(var + LN_EPS) * g + b


def _silu(x):
    return x * jax.nn.sigmoid(x)


def _full(shape):
    return pl.BlockSpec(shape, lambda *_: (0,) * len(shape))


def _slab(a, idx):
    return pl.BlockSpec((None,) + a.shape[1:], lambda *_: (idx,) + (0,) * (a.ndim - 1),
                        pipeline_mode=pl.Buffered(1))


def _first_in_kernel(*refs, sub, tps, npt, ca, att):
    xp, head_ref, xs = refs[:sub], refs[sub], refs[sub + 1:2 * sub + 1]
    w_ref, bf_ref, x0_ref, *outs = refs[2 * sub + 1:]
    t = SEQ_TILE
    for k in range(sub):
        g = pl.program_id(0) * sub + k
        is_sample = g >= npt
        is_head = jnp.logical_and(g % tps == 0, jnp.logical_not(is_sample))
        x0_ref[k * t:(k + 1) * t, :] = jnp.where(is_sample, xs[k][...], jnp.where(is_head, head_ref[...], xp[k][...]))
    _even_in_kernel(x0_ref, w_ref, bf_ref, *outs, ca=ca, att=att)


def _even_in_kernel(x_ref, w_ref, bf_ref, a_ref, q_ref, k_ref, v_ref, lf_ref, *, ca, att):
    xb = x_ref[...].astype(BF16)

    def mm(lo, hi):
        return _dot(xb, w_ref[:, lo:hi])

    u = mm(0, ca)
    g = mm(ca, 2 * ca)
    a_ref[...] = u * jax.nn.sigmoid(g)
    o = 2 * ca
    q_ref[...] = mm(o, o + att) * (HEAD_DIM ** -0.5)
    k_ref[...] = mm(o + att, o + 2 * att)
    v_ref[...] = mm(o + 2 * att, o + 3 * att)
    z = mm(o + 3 * att, o + 3 * att + LANES) + bf_ref[...]
    lf = jnp.minimum(z, 0.0) - jnp.log1p(jnp.exp(-jnp.abs(z)))
    lane = lax.broadcasted_iota(jnp.int32, lf.shape, 1)
    lf_ref[...] = jnp.where(lane < N_HEADS, lf, 0.0)


def _even_in(x, w, idx, bf, *, tm, ca, att):
    n, d = x.shape
    row = lambda c: pl.BlockSpec((tm, c), lambda i: (i, 0))
    return pl.pallas_call(
        functools.partial(_even_in_kernel, ca=ca, att=att),
        grid=(n // tm,),
        in_specs=[row(d), _slab(w, idx), _full(bf.shape)],
        out_specs=[row(ca), row(att), row(att), row(att), row(LANES)],
        out_shape=[jax.ShapeDtypeStruct((n, c), F32) for c in (ca, att, att, att, LANES)],
        compiler_params=_params("parallel"),
        name="even_in_proj",
    )(x, w, bf)


def _first_in(x_prompt, head, x_sample, w, idx, bf, *, tm, sp, ca, att):
    nb, seq, d = x_prompt.shape
    t = SEQ_TILE
    sub, tps = tm // t, sp // t
    npt, ns = nb * tps, x_sample.shape[0] // t
    n = nb * sp + x_sample.shape[0]
    row = lambda c: pl.BlockSpec((tm, c), lambda i: (i, 0))
    xp_spec = lambda k: pl.BlockSpec(
        (None, t, d), lambda i: (jnp.minimum((i * sub + k) // tps, nb - 1),
                                 jnp.clip((i * sub + k) % tps - 1, 0, seq // t - 1), 0))
    xs_spec = lambda k: pl.BlockSpec((t, d), lambda i: (jnp.clip(i * sub + k - npt, 0, ns - 1), 0))
    return pl.pallas_call(
        functools.partial(_first_in_kernel, sub=sub, tps=tps, npt=npt, ca=ca, att=att),
        grid=(n // tm,),
        in_specs=[xp_spec(k) for k in range(sub)] + [_full(head.shape)] + [xs_spec(k) for k in range(sub)]
        + [_slab(w, idx), _full(bf.shape)],
        out_specs=[row(d), row(ca), row(att), row(att), row(att), row(LANES)],
        out_shape=[jax.ShapeDtypeStruct((n, c), F32) for c in (d, ca, att, att, att, LANES)],
        compiler_params=_params("parallel"),
        name="first_in_proj",
    )(*([x_prompt] * sub), head, *([x_sample] * sub), w, bf)


def _odd_in_kernel(x_ref, w_ref, y_ref, bg_ref, pd_ref, *, c):
    xb = x_ref[...].astype(BF16)
    hc = _dot(xb, w_ref[:, 0:c])
    bg_ref[...] = _dot(xb, w_ref[:, c:2 * c])
    cg = _dot(xb, w_ref[:, 2 * c:3 * c])
    y_ref[...] = cg * hc
    pd_ref[...] = _dot(xb, w_ref[:, 3 * c:4 * c])


def _odd_in(x, w, idx, *, tm, c):
    n, d = x.shape
    row = lambda cc: pl.BlockSpec((tm, cc), lambda i: (i, 0))
    return pl.pallas_call(
        functools.partial(_odd_in_kernel, c=c),
        grid=(n // tm,),
        in_specs=[row(d), _slab(w, idx)],
        out_specs=[row(c)] * 3,
        out_shape=[jax.ShapeDtypeStruct((n, c), F32)] * 3,
        compiler_params=_params("parallel"),
        name="odd_in_proj",
    )(x, w)


def _tail_kernel(x_ref, m1p_ref, m2p_ref, m1s_ref, m2s_ref, p1_ref, p2_ref, w_ref, g1_ref, b1_ref,
                 wg_ref, wu_ref, wd_ref, g2_ref, b2_ref, *rest, even, alpha, chunk, nb, sp, fp, npt, lead):
    i = pl.program_id(0)
    is_sample = i >= npt
    m1 = jnp.where(is_sample, m1s_ref[...], m1p_ref[...])
    m2 = jnp.where(is_sample, m2s_ref[...], m2p_ref[...])
    if even:
        m1 = _silu(_ln(m1, p1_ref[...], p2_ref[...]))
    else:
        m2 = _dot(m2.astype(BF16), p1_ref[...]) * p2_ref[...]
    ca = m1.shape[1]
    mix = _dot(m1.astype(BF16), w_ref[0:ca, :]) + _dot(m2.astype(BF16), w_ref[ca:, :])
    x1 = _ln(alpha * x_ref[...] + mix, g1_ref[...], b1_ref[...])

    h_ref = rest[-1]
    xb = x1.astype(BF16)
    for c in range(0, wg_ref.shape[1], chunk):
        gate = _dot(xb, wg_ref[:, c:c + chunk])
        up = _dot(xb, wu_ref[:, c:c + chunk])
        h_ref[:, c:c + chunk] = (_silu(gate) * up).astype(BF16)
    out = _ln(alpha * x1 + _dot(h_ref[...], wd_ref[...]), g2_ref[...], b2_ref[...])

    tm = x1.shape[0]
    if lead is None:
        r = i * tm + lax.broadcasted_iota(jnp.int32, (tm, 1), 0)
        keep = jnp.ones((tm, 1), F32)
        for bi in range(nb):
            keep = jnp.where((r >= bi * sp) & (r < bi * sp + fp), 0.0, keep)
        rest[0][...] = out * keep
    else:
        yp_ref, ys_ref = rest[:2]

        @pl.when(jnp.logical_and(jnp.logical_not(is_sample), i % (sp // tm) >= lead))
        def _():
            yp_ref[...] = out

        @pl.when(is_sample)
        def _():
            ys_ref[...] = out


def _layer_tail(name, x, m1p, m2p, m1s, m2s, p1, p2, w, g1, b1, wg, wu, wd, g2, b2, *, even, mix_idx, layer, tm,
                alpha, nb, sp, fp, out_rows=None):
    n, d = x.shape
    s0 = m1p.shape[0]
    dff = wg.shape[2]
    chunk = SEQ_TILE if dff % SEQ_TILE == 0 else dff
    npt = s0 // tm
    row = lambda c: pl.BlockSpec((tm, c), lambda i: (i, 0))
    prow = lambda c: pl.BlockSpec((tm, c), lambda i: (jnp.minimum(i, npt - 1), 0))
    srow = lambda c, **kw: pl.BlockSpec((tm, c), lambda i: (jnp.maximum(i - npt, 0), 0), **kw)
    held = dict(pipeline_mode=pl.Buffered(1))
    once = lambda a: pl.BlockSpec(a.shape, lambda i: (0,) * a.ndim, pipeline_mode=pl.Buffered(1))
    consts = (p1, p2, w, g1, b1, wg, wu, wd, g2, b2)
    const_specs = [once(p1), once(p2), _slab(w, mix_idx), once(g1), once(b1),
                   _slab(wg, layer), _slab(wu, layer), _slab(wd, layer), once(g2), once(b2)]
    if out_rows is None:
        lead = None
        out_specs = row(d)
        out_shape = jax.ShapeDtypeStruct((n, d), F32)
    else:
        seq, db_rows = out_rows
        tps = sp // tm
        lead = (sp - seq) // tm
        out_specs = [
            pl.BlockSpec((None, tm, d), lambda i: (jnp.minimum(i // tps, nb - 1),
                                                   jnp.where(i < npt, jnp.maximum(i % tps - lead, 0), tps - lead - 1),
                                                   0)),
            srow(d)]
        out_shape = [jax.ShapeDtypeStruct((nb, seq, d), F32), jax.ShapeDtypeStruct((db_rows, d), F32)]
    return pl.pallas_call(
        functools.partial(_tail_kernel, even=even, alpha=alpha, chunk=chunk, nb=nb, sp=sp, fp=fp, npt=npt, lead=lead),
        grid=(n // tm,),
        in_specs=[row(d), prow(m1p.shape[1]), prow(m2p.shape[1]), srow(m1s.shape[1], **held),
                  srow(m2s.shape[1], **held)] + const_specs,
        out_specs=out_specs,
        out_shape=out_shape,
        scratch_shapes=[pltpu.VMEM((tm, dff), BF16)],
        compiler_params=_params("arbitrary" if out_rows else "parallel"),
        name=name,
    )(x, m1p, m2p, m1s, m2s, *consts)


def _conv_tile(win, w_rows):
    rows = win.shape[0]
    t = rows - CONV_BACK
    lead = CONV_BACK - (CONV_W - 1)
    acc = None
    for r in range(SUBLANES):
        rolled = win if r == 0 else pltpu.roll(win, rows - r, axis=0)
        for m in range(CONV_BACK // SUBLANES + 1):
            j = SUBLANES * m + r - lead
            if 0 <= j < CONV_W:
                term = rolled[SUBLANES * m:SUBLANES * m + t] * w_rows[j]
                acc = term if acc is None else acc + term
    return acc


def _sample_conv_kernel(st_ref, a_ref, w_ref, b_ref, o_ref, st_out_ref, ext_ref, *, ds):
    dbt, hist, c = st_ref.shape
    off = CONV_BACK - hist
    rows = CONV_BACK + ds
    ext_ref[:, 0:off, :] = jnp.zeros((dbt, off, c), F32)
    ext_ref[:, off:CONV_BACK, :] = st_ref[...]
    ext_ref[:, CONV_BACK:rows, :] = a_ref[...].reshape(dbt, ds, c)
    flat = ext_ref[...].reshape(dbt * rows, c)
    acc = None
    for r in range(SUBLANES):
        shifted = (flat if r == 0 else pltpu.roll(flat, dbt * rows - r, axis=0)).reshape(dbt, rows, c)
        for m in range(rows // SUBLANES):
            j = SUBLANES * m + r - off
            if 0 <= j < CONV_W:
                term = shifted[:, SUBLANES * m:SUBLANES * m + ds, :] * w_ref[j:j + 1, :]
                acc = term if acc is None else acc + term
    o_ref[...] = (acc + b_ref[...]).reshape(dbt * ds, c)
    st_out_ref[...] = ext_ref[:, off + ds:rows, :]


def _sample_conv(state, layer, a, w, b, *, s0, db, ds, dbt):
    n, c = a.shape
    hist = state.shape[2]
    rows = dbt * ds
    return pl.pallas_call(
        functools.partial(_sample_conv_kernel, ds=ds),
        grid=(db // dbt,),
        in_specs=[pl.BlockSpec((dbt, None, hist, c), lambda i: (i, layer, 0, 0)),
                  pl.BlockSpec((rows, c), lambda i: (s0 // rows + i, 0)), _full(w.shape), _full(b.shape)],
        out_specs=[pl.BlockSpec((rows, c), lambda i: (i, 0)), pl.BlockSpec((dbt, hist, c), lambda i: (i, 0, 0))],
        out_shape=[jax.ShapeDtypeStruct((db * ds, c), F32), jax.ShapeDtypeStruct((db, hist, c), F32)],
        scratch_shapes=[pltpu.VMEM((dbt, CONV_BACK + ds, c), F32)],
        compiler_params=_params("parallel"),
        name="sample_conv",
    )(state, a, w, b)


def _fox_prep_kernel(q_ref, k_ref, v_ref, lf_ref, a_ref, wdw_ref, bdw_ref, *rest, fp, ns, n_prev, emit):
    prev, rest = rest[:2 * n_prev], rest[2 * n_prev:]
    qa_ref, ka_ref, vat_ref, cv_ref = rest[:4]
    carry_ref, ah_ref = rest[6:8] if emit else rest[4:6]
    s = pl.program_id(1)
    t = SEQ_TILE

    @pl.when(s < ns)
    def _():
        _fox_operands(s, q_ref, k_ref, v_ref, lf_ref, qa_ref, ka_ref, vat_ref, carry_ref, fp)

        @pl.when(s == 0)
        def _():
            ah_ref[...] = jnp.zeros_like(ah_ref)

        for cb in range(a_ref.shape[1] // LANES):
            cs = slice(cb * LANES, (cb + 1) * LANES)
            win = jnp.concatenate([ah_ref[:, cs], a_ref[:, cs]], axis=0)
            w_rows = [wdw_ref[j:j + 1, cs] for j in range(CONV_W)]
            cv_ref[:, cs] = _conv_tile(win, w_rows) + bdw_ref[:, cs]
        ah_ref[...] = a_ref[t - CONV_BACK:t, :]

    if not emit:
        return
    kt_ref, vt_ref = rest[4:6]
    kc_ref, vc_ref = rest[8:10]

    @pl.when(s == 0)
    def _():
        kc_ref[...] = jnp.zeros_like(kc_ref)
        vc_ref[...] = jnp.zeros_like(vc_ref)

    k_srcs = list(prev[0::2]) + [k_ref]
    v_srcs = list(prev[1::2]) + [v_ref]
    for srcs, held, dst in ((k_srcs, kc_ref, kt_ref), (v_srcs, vc_ref, vt_ref)):
        for li, src in enumerate(srcs):
            rows = jnp.concatenate([held[li], src[0:fp, :]], axis=0)
            for c in range(src.shape[1] // LANES):
                dst[li, c * LANES:(c + 1) * LANES, :] = rows[:, c * LANES:(c + 1) * LANES].T
            held[li] = src[fp:t, :]


def _fox_operands(s, q_ref, k_ref, v_ref, lf_ref, qa_ref, ka_ref, vat_ref, carry_ref, fp):
    @pl.when(s == 0)
    def _():
        carry_ref[...] = jnp.zeros_like(carry_ref)

    t = SEQ_TILE
    row = lax.broadcasted_iota(jnp.int32, (t, LANES), 0)
    lane = lax.broadcasted_iota(jnp.int32, (t, LANES), 1)
    c = jnp.where(s * t + row >= fp, lf_ref[...], 0.0)
    sh = 1
    while sh < t:
        c = c + jnp.where(row >= sh, pltpu.roll(c, sh, axis=0), 0.0)
        sh *= 2
    c = c + carry_ref[0:1, :]
    carry_ref[0:1, :] = c[t - 1:t, :]

    c = c * LOG2E
    c1 = c.astype(BF16).astype(F32)
    r1 = c - c1
    c2 = r1.astype(BF16).astype(F32)
    c3 = (r1 - c2).astype(BF16).astype(F32)
    g23 = jnp.where(lane < 2 * N_HEADS, pltpu.roll(c2, N_HEADS, axis=1),
                    jnp.where(lane < 3 * N_HEADS, pltpu.roll(c3, 2 * N_HEADS, axis=1), 0.0))
    g = jnp.where(lane < N_HEADS, c1, g23)
    gk = jnp.where(lane < N_HEADS, jnp.where(s * t + row >= fp, c1, -NEG), g23)
    ones_grp = (lane >= HEAD_DIM + AUX_ONES) & (lane < HEAD_DIM + 2 * AUX_ONES)
    ck = [jnp.where((lane >= HEAD_DIM) & (lane < HEAD_DIM + AUX_ONES), -pltpu.roll(gk, HEAD_DIM, axis=1),
                    jnp.where(ones_grp, 1.0, 0.0))]
    cq = [jnp.where(ones_grp, pltpu.roll(g, HEAD_DIM + AUX_ONES, axis=1), 0.0)]
    ck.append(pltpu.roll(ck[0], HEAD_DIM, axis=1))
    cq.append(pltpu.roll(cq[0], HEAD_DIM, axis=1))

    lane1 = lax.broadcasted_iota(jnp.int32, (1, LANES), 1)
    for h in range(N_HEADS):
        odd = h % 2
        pr = h // 2
        base = 0 if odd else HEAD_DIM
        a = lane1 - base
        mine = (lane1 & (N_HEADS - 1)) == h
        data = ((lane1 >= HEAD_DIM) if odd else (lane1 < HEAD_DIM)).astype(F32)
        sel1 = ((a >= 0) & (a < AUX_ONES) & mine).astype(F32)
        selc = ((a >= AUX_ONES) & (a < 2 * AUX_ONES) & mine).astype(F32)
        sl = slice(pr * LANES, (pr + 1) * LANES)
        qa_ref[h] = (q_ref[:, sl] * (data * LOG2E) + (cq[odd] * selc + sel1)).T.astype(BF16)
        ka_ref[h] = (k_ref[:, sl] * data + ck[odd]).astype(BF16)
    ones = jnp.ones((V_ROWS - HEAD_DIM, t), BF16)
    for pr in range(N_HEADS // 2):
        v_t = v_ref[:, pr * LANES:(pr + 1) * LANES].T.astype(BF16)
        vat_ref[2 * pr] = jnp.concatenate([v_t[0:HEAD_DIM], ones], axis=0)
        vat_ref[2 * pr + 1] = jnp.concatenate([v_t[HEAD_DIM:], ones], axis=0)


def _fox_prep(q, k, v, lf, a, w_dw, b_dw, prev_kv, emit, *, nb, sp, fp):
    ns = sp // SEQ_TILE
    att = q.shape[1]
    ca = a.shape[1]
    t = SEQ_TILE
    last = ns - 1
    row = lambda c: pl.BlockSpec((t, c), lambda bi, si: (bi * ns + jnp.minimum(si, last), 0))
    hm = pl.BlockSpec((None, N_HEADS, t, LANES), lambda bi, si: (bi, 0, jnp.minimum(si, last), 0))
    hm_t = lambda r: pl.BlockSpec((None, N_HEADS, r, t), lambda bi, si: (bi, 0, 0, jnp.minimum(si, last)))
    out_specs = [hm_t(LANES), hm, hm_t(V_ROWS), row(ca)]
    out_shape = [jax.ShapeDtypeStruct((nb, N_HEADS, LANES, sp), BF16),
                 jax.ShapeDtypeStruct((nb, N_HEADS, sp, LANES), BF16),
                 jax.ShapeDtypeStruct((nb, N_HEADS, V_ROWS, sp), BF16), jax.ShapeDtypeStruct((nb * sp, ca), F32)]
    scratch = [pltpu.VMEM((SUBLANES, LANES), F32), pltpu.VMEM((CONV_BACK, ca), F32)]
    n_prev = len(prev_kv) if emit else 0
    if emit:
        assert 0 < fp < t and fp % SUBLANES == 0
        n_layers = n_prev + 1
        cache = pl.BlockSpec((None, n_layers, att, t), lambda bi, si: (bi, 0, 0, jnp.maximum(si - 1, 0)))
        out_specs += [cache, cache]
        out_shape += [jax.ShapeDtypeStruct((nb, n_layers, att, sp - fp), F32)] * 2
        scratch += [pltpu.VMEM((n_layers, t - fp, att), F32)] * 2
    prev = [a for kv in prev_kv for a in kv] if emit else []
    return pl.pallas_call(
        functools.partial(_fox_prep_kernel, fp=fp, ns=ns, n_prev=n_prev, emit=emit),
        grid=(nb, ns + 1 if emit else ns),
        in_specs=[row(att), row(att), row(att), row(LANES), row(ca), _full(w_dw.shape), _full(b_dw.shape)]
        + [row(att)] * len(prev),
        out_specs=out_specs,
        out_shape=out_shape,
        scratch_shapes=scratch,
        compiler_params=_params("parallel", "arbitrary"),
        name="fox_prep",
    )(q, k, v, lf, a, w_dw, b_dw, *prev)


def _flash_kernel(q_ref, k_ref, vt_ref, o_ref, s_ref, p_ref, acc_ref, *, hg):
    i = pl.program_id(2)
    t = SEQ_TILE

    def scores_to(j, slot):
        ks = pl.multiple_of(j * t, t)
        for h in range(hg):
            s_ref[slot, h] = _dot(k_ref[h, pl.ds(ks, t), :], q_ref[h])

    def accumulate(j, als):
        ks = pl.multiple_of(j * t, t)
        pvs = [_dot(vt_ref[h, :, pl.ds(ks, t)], p_ref[h]) for h in range(hg)]
        for h in range(hg):
            acc_ref[h] = als[h] * acc_ref[h] + pvs[h]

    def softmax(slot, ms, masked):
        new_m, als = [], []
        for h in range(hg):
            s = s_ref[slot, h]
            if masked:
                visible = (lax.broadcasted_iota(jnp.int32, (t, t), 0) <= lax.broadcasted_iota(jnp.int32, (t, t), 1))
                s = jnp.where(visible, s, NEG)
            mn = jnp.maximum(ms[h], jnp.max(s, axis=0, keepdims=True))
            als.append(jnp.exp2(ms[h] - mn))
            new_m.append(mn)
            p_ref[h] = jnp.exp2(s - mn).astype(BF16)
        return tuple(new_m), tuple(als)

    def stage(j, slot, state):
        ms, als = state
        scores_to(j + 1, 1 - slot)
        accumulate(jnp.maximum(j - 1, 0), als)
        return softmax(slot, ms, False)

    def pair(jj, state):
        return stage(2 * jj + 1, 1, stage(2 * jj, 0, state))

    def finish(slot, state):
        ms, als = state
        accumulate(jnp.maximum(i - 1, 0), als)
        _, als = softmax(slot, ms, True)
        accumulate(i, als)
        for pr in range(hg // 2):
            halves = [acc_ref[h, 0:HEAD_DIM, :] * (1.0 / acc_ref[h, HEAD_DIM:HEAD_DIM + 1, :])
                      for h in (2 * pr, 2 * pr + 1)]
            o_ref[:, pr * LANES:(pr + 1) * LANES] = jnp.concatenate(halves, axis=0).T.astype(o_ref.dtype)

    p_ref[...] = jnp.zeros_like(p_ref)
    acc_ref[...] = jnp.zeros_like(acc_ref)
    scores_to(0, 0)
    state = ((jnp.full((1, t), NEG, F32),) * hg, (jnp.ones((1, t), F32),) * hg)
    state = lax.fori_loop(0, i // 2, pair, state)
    odd = i % 2 == 1
    state = lax.cond(odd, lambda st: stage(i - 1, 0, st), lambda st: st, state)
    pl.when(odd)(lambda: finish(1, state))
    pl.when(jnp.logical_not(odd))(lambda: finish(0, state))


def _flash(qat, ka, vat, *, hg):
    nb, _, sp, _ = ka.shape
    ns = sp // SEQ_TILE
    return pl.pallas_call(
        functools.partial(_flash_kernel, hg=hg),
        grid=(nb, N_HEADS // hg, ns),
        in_specs=[pl.BlockSpec((None, hg, LANES, SEQ_TILE), lambda bi, gi, qi: (bi, gi, 0, qi)),
                  pl.BlockSpec((None, hg, sp, LANES), lambda bi, gi, qi: (bi, gi, 0, 0)),
                  pl.BlockSpec((None, hg, V_ROWS, sp), lambda bi, gi, qi: (bi, gi, 0, 0))],
        out_specs=pl.BlockSpec((SEQ_TILE, hg * HEAD_DIM), lambda bi, gi, qi: (bi * ns + qi, gi)),
        out_shape=jax.ShapeDtypeStruct((nb * sp, N_HEADS * HEAD_DIM), BF16),
        scratch_shapes=[pltpu.VMEM((2, hg, SEQ_TILE, SEQ_TILE), F32), pltpu.VMEM((hg, SEQ_TILE, SEQ_TILE), BF16),
                        pltpu.VMEM((hg, V_ROWS, SEQ_TILE), F32)],
        compiler_params=_params("parallel", "parallel", "arbitrary"),
        name="fox_flash",
    )(qat, ka, vat)


def _sample_attn_kernel(pt_ref, q_ref, kn_ref, vn_ref, lfn_ref, *rest, n_pages, page, eb, ds):
    del pt_ref
    o_ref = rest[3 * eb * n_pages]
    c = q_ref.shape[1]
    rows = ds * N_HEADS
    groups = [list(range(g, min(g + 2, n_pages))) for g in range(0, n_pages, 2)]
    past = n_pages * page
    sub = lax.broadcasted_iota(jnp.int32, (N_HEADS, c), 0)
    lane = lax.broadcasted_iota(jnp.int32, (N_HEADS, c), 1)
    hm = ((lane >= sub * HEAD_DIM) & (lane < (sub + 1) * HEAD_DIM)).astype(F32)
    hm_t = jnp.concatenate([hm] * ds, axis=0)
    pad = jnp.zeros((LANES - ds, c), F32)

    def refs_of(kind, e):
        return rest[(kind * eb + e) * n_pages:(kind * eb + e + 1) * n_pages]

    def scores(e):
        sl = slice(e * ds, (e + 1) * ds)
        q = q_ref[sl, :]
        qbd = jnp.concatenate([jnp.broadcast_to(q[t:t + 1, :], (N_HEADS, c)) * hm for t in range(ds)],
                              axis=0).astype(BF16)
        k_refs, lf_refs = refs_of(0, e), refs_of(2, e)
        s_parts = []
        for grp in groups:
            kp = jnp.concatenate([k_refs[r][...] for r in grp], axis=1).astype(BF16)
            s_parts.append(_dot(qbd, kp))
        s_past = jnp.concatenate(s_parts, axis=1)

        lf = jnp.concatenate([r[...] for r in lf_refs], axis=1)
        lane_p = lax.broadcasted_iota(jnp.int32, (N_HEADS, past), 1)
        suf = lf
        sh = 1
        while sh < past:
            suf = suf + jnp.where(lane_p < past - sh, pltpu.roll(suf, past - sh, axis=1), 0.0)
            sh *= 2
        s_past = s_past + jnp.concatenate([suf - lf] * ds, axis=0)

        kn = jnp.concatenate([kn_ref[sl, :], pad], axis=0).astype(BF16)
        s_new = lax.dot_general(qbd, kn, NT_DIMS, preferred_element_type=F32)
        cn = lfn_ref[sl, :]
        sub_n = lax.broadcasted_iota(jnp.int32, cn.shape, 0)
        sh = 1
        while sh < ds:
            cn = cn + jnp.where(sub_n >= sh, pltpu.roll(cn, sh, axis=0), 0.0)
            sh *= 2
        cn_t = jnp.concatenate([cn, jnp.zeros((LANES - ds, LANES), F32)], axis=0).T[0:N_HEADS, :]
        rown = lax.broadcasted_iota(jnp.int32, (rows, LANES), 0)
        u = lax.broadcasted_iota(jnp.int32, (rows, LANES), 1)
        s_new = jnp.where(u * N_HEADS <= rown, s_new - jnp.concatenate([cn_t] * ds, axis=0), NEG)
        return s_past, s_new

    def softmax(s_past, s_new):
        m = jnp.maximum(jnp.max(s_past, axis=1, keepdims=True), jnp.max(s_new, axis=1, keepdims=True))
        p_past = jnp.exp(s_past - m)
        p_new = jnp.exp(s_new - m)
        l = jnp.sum(p_past, axis=1, keepdims=True) + jnp.sum(p_new, axis=1, keepdims=True)
        return p_past.astype(BF16), p_new.astype(BF16), l

    def output(e, p_past, p_new, l):
        sl = slice(e * ds, (e + 1) * ds)
        v_refs = refs_of(1, e)
        vn = jnp.concatenate([vn_ref[sl, :], pad], axis=0).astype(BF16)
        o = _dot(p_new, vn)
        for grp in groups:
            vp = jnp.concatenate([v_refs[r][...] for r in grp], axis=1).astype(BF16)
            lo = grp[0] * page
            o = o + lax.dot_general(p_past[:, lo:lo + len(grp) * page], vp, NT_DIMS, preferred_element_type=F32)
        o = o * (1.0 / l) * hm_t
        rr = lax.broadcasted_iota(jnp.int32, (ds, rows), 1)
        tt = lax.broadcasted_iota(jnp.int32, (ds, rows), 0)
        pick = ((rr >= tt * N_HEADS) & (rr < (tt + 1) * N_HEADS)).astype(BF16)
        o_ref[sl, :] = _dot(pick, o.astype(BF16)).astype(o_ref.dtype)

    ss = [scores(e) for e in range(eb)]
    ps = [softmax(*s) for s in ss]
    for e in range(eb):
        output(e, *ps[e])


def _sample_attn(page_table, layer, q, k, v, lf, cache_k, cache_v, cache_lft, *, s0, db, ds):
    n, c = q.shape
    n_pages = page_table.shape[1]
    page = cache_k.shape[3]
    eb = 2 if db % 2 == 0 else 1
    rows = eb * ds
    row = lambda cc: pl.BlockSpec((rows, cc), lambda i, pt: (s0 // rows + i, 0))

    def paged(shape, e, r):
        return pl.BlockSpec((None, None) + shape, lambda i, pt: (pt[(i * eb + e) * n_pages + r], layer, 0, 0))

    pages = [(e, r) for e in range(eb) for r in range(n_pages)]
    in_specs = ([row(c), row(c), row(c), row(LANES)]
                + [paged((c, page), e, r) for e, r in pages]
                + [paged((c, page), e, r) for e, r in pages]
                + [paged((N_HEADS, page), e, r) for e, r in pages])
    return pl.pallas_call(
        functools.partial(_sample_attn_kernel, n_pages=n_pages, page=page, eb=eb, ds=ds),
        grid_spec=pltpu.PrefetchScalarGridSpec(
            num_scalar_prefetch=1, grid=(db // eb,), in_specs=in_specs,
            out_specs=pl.BlockSpec((rows, c), lambda i, pt: (i, 0))),
        out_shape=jax.ShapeDtypeStruct((db * ds, c), BF16 if rows % (2 * SUBLANES) == 0 else F32),
        compiler_params=_params("parallel"),
        name="sample_paged_attn",
    )(page_table.reshape(-1), q, k, v, lf, *([cache_k] * len(pages)), *([cache_v] * len(pages)),
      *([cache_lft] * len(pages)))


def _odd_tile(wy, wp, bg, w_rows, window, pos0):
    t = bg.shape[0]
    y0 = wy[SUBLANES:]
    y1 = pltpu.roll(wy, 1, axis=0)[SUBLANES:]
    y2 = pltpu.roll(wy, 2, axis=0)[SUBLANES:]
    out_c = bg * (w_rows[0] * y2 + w_rows[1] * y1 + w_rows[2] * y0)
    back = 2 * SUBLANES
    win = wp
    step = 1
    while step < window:
        win = win + pltpu.roll(win, step, axis=0)
        step *= 2
    if pos0 is None:
        cnt = float(window)
    else:
        pos = pos0 + lax.broadcasted_iota(jnp.int32, (t, LANES), 0)
        cnt = jnp.maximum(jnp.minimum(window, pos + 1), 1).astype(F32)
    return out_c, win[back:] / cnt - wp[back:]


def _prompt_odd_kernel(y_ref, bg_ref, pd_ref, w_ref, oc_ref, d_ref, *, n_tiles, fp):
    w_rows = [w_ref[j:j + 1, :] for j in range(SCONV_W)]
    c = y_ref.shape[1]
    t = SEQ_TILE
    n_partial = -(-(fp + POOL_HIST) // t)

    def run(window):
        wy0 = jnp.concatenate([jnp.zeros((SUBLANES, c), F32), y_ref[0:t, :]], axis=0)
        wp0 = jnp.concatenate([jnp.zeros((2 * SUBLANES, c), F32), pd_ref[0:t, :]], axis=0)
        oc, d = _odd_tile(wy0, wp0, bg_ref[0:t, :], w_rows, window, -fp)
        oc_ref[0:t, :] = oc.astype(oc_ref.dtype)
        d_ref[0:t, :] = d.astype(d_ref.dtype)

        def body(i, partial):
            s0 = pl.multiple_of(i * t, t)
            wy = y_ref[pl.ds(s0 - SUBLANES, t + SUBLANES), :]
            wp = pd_ref[pl.ds(s0 - 2 * SUBLANES, t + 2 * SUBLANES), :]
            oc, d = _odd_tile(wy, wp, bg_ref[pl.ds(s0, t), :], w_rows, window, s0 - fp if partial else None)
            oc_ref[pl.ds(s0, t), :] = oc.astype(oc_ref.dtype)
            d_ref[pl.ds(s0, t), :] = d.astype(d_ref.dtype)

        lax.fori_loop(1, n_partial, lambda i, carry: body(i, True) or carry, 0)
        lax.fori_loop(n_partial, n_tiles, lambda i, carry: body(i, False) or carry, 0)

    for gi, window in enumerate(POOL_WINDOWS):
        pl.when(pl.program_id(1) == gi)(functools.partial(run, window))


def _prompt_odd(y, bg, pd, w, *, nb, sp, fp):
    n, c = y.shape
    assert c // LANES == len(POOL_WINDOWS)
    blk = pl.BlockSpec((sp, LANES), lambda bi, ci: (bi, ci))
    return pl.pallas_call(
        functools.partial(_prompt_odd_kernel, n_tiles=sp // SEQ_TILE, fp=fp),
        grid=(nb, c // LANES),
        in_specs=[blk, blk, blk, pl.BlockSpec((SCONV_W, LANES), lambda bi, ci: (0, ci))],
        out_specs=[blk, blk],
        out_shape=[jax.ShapeDtypeStruct((nb * sp, c), BF16)] * 2,
        compiler_params=_params("parallel", "parallel"),
        name="prompt_sconv_pool",
    )(y, bg, pd, w)


def _sample_odd_kernel(ss_ref, sp_ref, y_ref, bg_ref, pd_ref, w_ref,
                       oc_ref, d_ref, ss_out_ref, sp_out_ref, yext_ref, pext_ref, *, ds):
    dbt, _, c = ss_ref.shape
    hs = SCONV_W - 1
    y = y_ref[...].reshape(dbt, ds, c)
    pd = pd_ref[...].reshape(dbt, ds, c)
    bg = bg_ref[...].reshape(dbt, ds, c)
    ny = SUBLANES + ds
    yext_ref[:, 0:SUBLANES - hs, :] = jnp.zeros((dbt, SUBLANES - hs, c), F32)
    yext_ref[:, SUBLANES - hs:SUBLANES, :] = ss_ref[...]
    yext_ref[:, SUBLANES:ny, :] = y
    yflat = yext_ref[...].reshape(dbt * ny, c)
    conv = y * w_ref[SCONV_W - 1:SCONV_W, :]
    for back in range(1, SCONV_W):
        shifted = pltpu.roll(yflat, back, axis=0).reshape(dbt, ny, c)[:, SUBLANES:ny, :]
        conv = conv + shifted * w_ref[SCONV_W - 1 - back:SCONV_W - back, :]
    oc_ref[...] = (bg * conv).reshape(dbt * ds, c).astype(oc_ref.dtype)
    ss_out_ref[...] = yext_ref[:, ny - hs:ny, :]

    base = 2 * SUBLANES
    npd = base + ds
    pext_ref[:, 0:base - POOL_HIST, :] = jnp.zeros((dbt, base - POOL_HIST, c), F32)
    pext_ref[:, base - POOL_HIST:base, :] = sp_ref[...]
    pext_ref[:, base:npd, :] = pd
    gc = c // len(POOL_WINDOWS)
    means = []
    for gi, window in enumerate(POOL_WINDOWS):
        win = pext_ref[:, :, gi * gc:(gi + 1) * gc].reshape(dbt * npd, gc)
        step = 1
        while step < window:
            win = win + pltpu.roll(win, step, axis=0)
            step *= 2
        means.append(win.reshape(dbt, npd, gc)[:, base:npd, :] / float(window))
    d_ref[...] = (jnp.concatenate(means, axis=-1) - pd).reshape(dbt * ds, c).astype(d_ref.dtype)
    sp_out_ref[...] = pext_ref[:, npd - POOL_HIST:npd, :]


def _sample_odd(state_s, state_p, layer, y, bg, pd, w, *, s0, db, ds, dbt):
    n, c = y.shape
    rows = dbt * ds
    row_in = pl.BlockSpec((rows, c), lambda i: (s0 // rows + i, 0))
    row_out = pl.BlockSpec((rows, c), lambda i: (i, 0))
    hs = state_s.shape[2]
    hp = state_p.shape[2]
    return pl.pallas_call(
        functools.partial(_sample_odd_kernel, ds=ds),
        grid=(db // dbt,),
        in_specs=[pl.BlockSpec((dbt, None, hs, c), lambda i: (i, layer, 0, 0)),
                  pl.BlockSpec((dbt, None, hp, c), lambda i: (i, layer, 0, 0)),
                  row_in, row_in, row_in, _full(w.shape)],
        out_specs=[row_out, row_out, pl.BlockSpec((dbt, hs, c), lambda i: (i, 0, 0)),
                   pl.BlockSpec((dbt, hp, c), lambda i: (i, 0, 0))],
        out_shape=[jax.ShapeDtypeStruct((db * ds, c), BF16 if rows % (2 * SUBLANES) == 0 else F32)] * 2 + [
                   jax.ShapeDtypeStruct((db, hs, c), F32), jax.ShapeDtypeStruct((db, hp, c), F32)],
        scratch_shapes=[pltpu.VMEM((dbt, SUBLANES + ds, c), F32), pltpu.VMEM((dbt, 2 * SUBLANES + ds, c), F32)],
        compiler_params=_params("parallel"),
        name="sample_sconv_pool",
    )(state_s, state_p, y, bg, pd, w)


def kernel(x_prompt, x_sample, cache_k, cache_v, cache_logf, page_table, state_conv_a, state_sconv, state_pool, meta_tokens, w_in_even, b_forget, w_dw_a, b_dw_a, ln_a_g, ln_a_b, w_out_even, w_in_odd, w_sconv, w_pool_mix, pool_scale, w_out_odd, ln_mix_g, ln_mix_b, w_ffn_gate, w_ffn_up, w_ffn_down, ln_ffn_g, ln_ffn_b):
    nb, seq, d = x_prompt.shape
    db, ds, _ = x_sample.shape
    depth = w_ffn_gate.shape[0]
    alpha = float((2 * depth) ** 0.25)
    ca = state_conv_a.shape[-1]
    att = N_HEADS * HEAD_DIM
    assert ds == SUBLANES and cache_k.shape[3] == N_HEADS and cache_k.shape[4] == HEAD_DIM
    assert state_conv_a.shape[2] == CONV_W - 1 and state_pool.shape[2] == POOL_HIST
    assert meta_tokens.shape[0] == N_META

    s_real = N_META + seq
    fp = (-s_real) % SEQ_TILE
    if fp < CONV_BACK:
        fp += SEQ_TILE
    sp = fp + s_real
    s0 = nb * sp
    n = s0 + db * ds
    tm = next(t for t in (1024, 512, 256, 128, 64, 32, 16, 8) if s0 % t == 0 and (db * ds) % t == 0)
    tm_in = next(t for t in (1024, 512, 256, 128, 64, 32, 16, 8) if n % t == 0)
    dbt = next(t for t in (8, 4, 2, 1) if db % t == 0)
    direct_out = seq % SEQ_TILE == 0 and (db * ds) % SEQ_TILE == 0

    head = jnp.concatenate([jnp.zeros((fp, d), F32), meta_tokens.astype(F32)], axis=0)
    fused_first = direct_out and tm_in % SEQ_TILE == 0
    x = None if fused_first else jnp.concatenate(
        [piece for bi in range(nb) for piece in (head, x_prompt[bi])] + [x_sample.reshape(db * ds, d)], axis=0)

    n_phys = cache_k.shape[0]
    n_even = cache_k.shape[1]
    page = cache_k.shape[2]
    ck = jnp.transpose(cache_k, (0, 1, 3, 4, 2)).reshape(n_phys, n_even, att, page)
    cv = jnp.transpose(cache_v, (0, 1, 3, 4, 2)).reshape(n_phys, n_even, att, page)
    clft = jnp.swapaxes(cache_logf, 2, 3)

    row2 = lambda v: v.reshape(1, -1).astype(F32)

    def prompt_rows(arr, lo, hi):
        return jnp.stack([arr[bi * sp + lo:bi * sp + hi] for bi in range(nb)], axis=0)

    def sample_rows(arr):
        return arr[s0:].reshape(db, ds, -1)

    wg_all, wu_all, wd_all = (w.astype(BF16) for w in (w_ffn_gate, w_ffn_up, w_ffn_down))
    wo_even, wo_odd = w_out_even.astype(BF16), w_out_odd.astype(BF16)
    wi_even = jnp.pad(w_in_even.astype(BF16), ((0, 0), (0, 0), (0, LANES - N_HEADS)))
    wi_odd = w_in_odd.astype(BF16)

    def tail(name, xin, m1p, m2p, m1s, m2s, p1, p2, w_out_all, layer):
        last = direct_out and layer == depth - 1
        return _layer_tail(name, xin, m1p, m2p, m1s, m2s, p1, p2, w_out_all,
                           row2(ln_mix_g[layer]), row2(ln_mix_b[layer]), wg_all, wu_all, wd_all,
                           row2(ln_ffn_g[layer]), row2(ln_ffn_b[layer]), even=layer % 2 == 0,
                           mix_idx=layer // 2, layer=layer, tm=SEQ_TILE if last else tm, alpha=alpha,
                           nb=nb, sp=sp, fp=fp, out_rows=(seq, db * ds) if last else None)

    prev_kv = []
    lfp, cap, scp, plp = [], [], [], []
    ks_, vs_, lfs, cas, scs, pls = [], [], [], [], [], []
    for layer in range(depth):
        i = layer // 2
        if layer % 2 == 0:
            bf = jnp.pad(b_forget[i].astype(F32), (0, LANES - N_HEADS)).reshape(1, LANES)
            if x is None:
                x, a, q, k, v, lf = _first_in(x_prompt.astype(F32), head, x_sample.reshape(db * ds, d).astype(F32),
                                              wi_even, i, bf, tm=tm_in, sp=sp, ca=ca, att=att)
            else:
                a, q, k, v, lf = _even_in(x, wi_even, i, bf, tm=tm_in, ca=ca, att=att)
            w_dw, b_dw = w_dw_a[i].astype(F32), row2(b_dw_a[i])
            conv_s, st_a = _sample_conv(state_conv_a.astype(F32), i, a, w_dw, b_dw, s0=s0, db=db, ds=ds, dbt=dbt)
            qa, ka, vat, conv_p, *caches = _fox_prep(q, k, v, lf, a, w_dw, b_dw, prev_kv, i == n_even - 1,
                                                     nb=nb, sp=sp, fp=fp)
            prev_kv.append((k, v))
            at_p = _flash(qa, ka, vat, hg=FLASH_HEADS)
            at_s = _sample_attn(page_table, i, q, k, v, lf, ck, cv, clft, s0=s0, db=db, ds=ds)
            x = tail("even_tail", x, conv_p, at_p, conv_s, at_s, row2(ln_a_g[i]), row2(ln_a_b[i]), wo_even, layer)
            lfp.append(prompt_rows(lf, fp, sp)[..., :N_HEADS])
            cap.append(prompt_rows(a, sp - (CONV_W - 1), sp))
            ks_.append(sample_rows(k).reshape(db, ds, N_HEADS, HEAD_DIM))
            vs_.append(sample_rows(v).reshape(db, ds, N_HEADS, HEAD_DIM))
            lfs.append(sample_rows(lf)[..., :N_HEADS])
            cas.append(st_a)
        else:
            c = state_sconv.shape[-1]
            y, bg, pd = _odd_in(x, wi_odd, i, tm=tm_in, c=c)
            oc_p, dd_p = _prompt_odd(y, bg, pd, w_sconv[i].astype(F32), nb=nb, sp=sp, fp=fp)
            oc_s, dd_s, st_s, st_p = _sample_odd(state_sconv.astype(F32), state_pool.astype(F32), i, y, bg, pd,
                                                 w_sconv[i].astype(F32), s0=s0, db=db, ds=ds, dbt=dbt)
            wm = jax.scipy.linalg.block_diag(*[w_pool_mix[i, g] for g in range(w_pool_mix.shape[1])]).astype(BF16)
            x = tail("odd_tail", x, oc_p, dd_p, oc_s, dd_s, wm, row2(pool_scale[i]), wo_odd, layer)
            scp.append(prompt_rows(y, sp - (SCONV_W - 1), sp))
            plp.append(prompt_rows(pd, sp - POOL_HIST, sp))
            scs.append(st_s)
            pls.append(st_p)

    if direct_out:
        y_prompt, y_sample = x[0], x[1].reshape(db, ds, d)
    else:
        y_prompt, y_sample = prompt_rows(x, fp + N_META, sp), sample_rows(x)
    st = lambda xs: jnp.stack(xs, axis=1)
    kp, vp = (jnp.transpose(c.reshape(nb, n_even, N_HEADS, HEAD_DIM, s_real), (0, 1, 4, 2, 3)) for c in caches)
    return (y_prompt, y_sample, kp, vp, st(lfp), st(cap), st(scp), st(plp),
            st(ks_), st(vs_), st(lfs), st(cas), st(scs), st(pls))
```

```python
import functools

import jax
import jax.numpy as jnp
from jax import lax
from jax.experimental import pallas as pl
from jax.experimental.pallas import tpu as pltpu

N_META = 16
N_HEADS = 8
HEAD_DIM = 64
CONV_W = 31
SCONV_W = 3
POOL_WINDOWS = (2, 4, 8, 16)
POOL_HIST = max(POOL_WINDOWS) - 1
LN_EPS = 1e-5
SEQ_TILE = 256
LANES = 128
SUBLANES = 8
CONV_BACK = -(-(CONV_W - 1) // SUBLANES) * SUBLANES
C_TERMS = 3
AUX_ONES = C_TERMS * N_HEADS
FLASH_HEADS = 8
V_ROWS = HEAD_DIM + 2 * SUBLANES
VMEM_LIMIT = 56 * 1024 * 1024
NEG = -1e30
LOG2E = 1.4426950408889634
F32 = jnp.float32
BF16 = jnp.bfloat16
NT_DIMS = (((1,), (1,)), ((), ()))


def _params(*sem):
    return pltpu.CompilerParams(dimension_semantics=sem, vmem_limit_bytes=VMEM_LIMIT)


def _dot(a, b):
    return jnp.dot(a, b, preferred_element_type=F32)


def _ln(z, g, b):
    mu = jnp.mean(z, axis=-1, keepdims=True)
    zc = z - mu
    var = jnp.mean(zc * zc, axis=-1, keepdims=True)
    return zc * lax.rsqrt(var + LN_EPS) * g + b


def _silu(x):
    return x * jax.nn.sigmoid(x)


def _full(shape):
    return pl.BlockSpec(shape, lambda *_: (0,) * len(shape))


def _slab(a, idx):
    return pl.BlockSpec((None,) + a.shape[1:], lambda *_: (idx,) + (0,) * (a.ndim - 1),
                        pipeline_mode=pl.Buffered(1))


def _first_in_kernel(*refs, sub, tps, npt, ca, att):
    xp, head_ref, xs = refs[:sub], refs[sub], refs[sub + 1:2 * sub + 1]
    w_ref, bf_ref, x0_ref, *outs = refs[2 * sub + 1:]
    t = SEQ_TILE
    for k in range(sub):
        g = pl.program_id(0) * sub + k
        is_sample = g >= npt
        is_head = jnp.logical_and(g % tps == 0, jnp.logical_not(is_sample))
        x0_ref[k * t:(k + 1) * t, :] = jnp.where(is_sample, xs[k][...], jnp.where(is_head, head_ref[...], xp[k][...]))
    _even_in_kernel(x0_ref, w_ref, bf_ref, *outs, ca=ca, att=att)


def _even_in_kernel(x_ref, w_ref, bf_ref, a_ref, q_ref, k_ref, v_ref, lf_ref, *, ca, att):
    xb = x_ref[...].astype(BF16)

    def mm(lo, hi):
        return _dot(xb, w_ref[:, lo:hi])

    u = mm(0, ca)
    g = mm(ca, 2 * ca)
    a_ref[...] = u * jax.nn.sigmoid(g)
    o = 2 * ca
    q_ref[...] = mm(o, o + att) * (HEAD_DIM ** -0.5)
    k_ref[...] = mm(o + att, o + 2 * att)
    v_ref[...] = mm(o + 2 * att, o + 3 * att)
    z = mm(o + 3 * att, o + 3 * att + LANES) + bf_ref[...]
    lf = jnp.minimum(z, 0.0) - jnp.log1p(jnp.exp(-jnp.abs(z)))
    lane = lax.broadcasted_iota(jnp.int32, lf.shape, 1)
    lf_ref[...] = jnp.where(lane < N_HEADS, lf, 0.0)


def _even_in(x, w, idx, bf, *, tm, ca, att):
    n, d = x.shape
    row = lambda c: pl.BlockSpec((tm, c), lambda i: (i, 0))
    return pl.pallas_call(
        functools.partial(_even_in_kernel, ca=ca, att=att),
        grid=(n // tm,),
        in_specs=[row(d), _slab(w, idx), _full(bf.shape)],
        out_specs=[row(ca), row(att), row(att), row(att), row(LANES)],
        out_shape=[jax.ShapeDtypeStruct((n, c), F32) for c in (ca, att, att, att, LANES)],
        compiler_params=_params("parallel"),
        name="even_in_proj",
    )(x, w, bf)


def _first_in(x_prompt, head, x_sample, w, idx, bf, *, tm, sp, ca, att):
    nb, seq, d = x_prompt.shape
    t = SEQ_TILE
    sub, tps = tm // t, sp // t
    npt, ns = nb * tps, x_sample.shape[0] // t
    n = nb * sp + x_sample.shape[0]
    row = lambda c: pl.BlockSpec((tm, c), lambda i: (i, 0))
    xp_spec = lambda k: pl.BlockSpec(
        (None, t, d), lambda i: (jnp.minimum((i * sub + k) // tps, nb - 1),
                                 jnp.clip((i * sub + k) % tps - 1, 0, seq // t - 1), 0))
    xs_spec = lambda k: pl.BlockSpec((t, d), lambda i: (jnp.clip(i * sub + k - npt, 0, ns - 1), 0))
    return pl.pallas_call(
        functools.partial(_first_in_kernel, sub=sub, tps=tps, npt=npt, ca=ca, att=att),
        grid=(n // tm,),
        in_specs=[xp_spec(k) for k in range(sub)] + [_full(head.shape)] + [xs_spec(k) for k in range(sub)]
        + [_slab(w, idx), _full(bf.shape)],
        out_specs=[row(d), row(ca), row(att), row(att), row(att), row(LANES)],
        out_shape=[jax.ShapeDtypeStruct((n, c), F32) for c in (d, ca, att, att, att, LANES)],
        compiler_params=_params("parallel"),
        name="first_in_proj",
    )(*([x_prompt] * sub), head, *([x_sample] * sub), w, bf)


def _odd_in_kernel(x_ref, w_ref, y_ref, bg_ref, pd_ref, *, c):
    xb = x_ref[...].astype(BF16)
    hc = _dot(xb, w_ref[:, 0:c])
    bg_ref[...] = _dot(xb, w_ref[:, c:2 * c])
    cg = _dot(xb, w_ref[:, 2 * c:3 * c])
    y_ref[...] = cg * hc
    pd_ref[...] = _dot(xb, w_ref[:, 3 * c:4 * c])


def _odd_in(x, w, idx, *, tm, c):
    n, d = x.shape
    row = lambda cc: pl.BlockSpec((tm, cc), lambda i: (i, 0))
    return pl.pallas_call(
        functools.partial(_odd_in_kernel, c=c),
        grid=(n // tm,),
        in_specs=[row(d), _slab(w, idx)],
        out_specs=[row(c)] * 3,
        out_shape=[jax.ShapeDtypeStruct((n, c), F32)] * 3,
        compiler_params=_params("parallel"),
        name="odd_in_proj",
    )(x, w)


def _tail_kernel(x_ref, m1p_ref, m2p_ref, m1s_ref, m2s_ref, p1_ref, p2_ref, w_ref, g1_ref, b1_ref,
                 wg_ref, wu_ref, wd_ref, g2_ref, b2_ref, *rest, even, alpha, chunk, nb, sp, fp, npt, lead):
    i = pl.program_id(0)
    is_sample = i >= npt
    m1 = jnp.where(is_sample, m1s_ref[...], m1p_ref[...])
    m2 = jnp.where(is_sample, m2s_ref[...], m2p_ref[...])
    if even:
        m1 = _silu(_ln(m1, p1_ref[...], p2_ref[...]))
    else:
        m2 = _dot(m2.astype(BF16), p1_ref[...]) * p2_ref[...]
    ca = m1.shape[1]
    mix = _dot(m1.astype(BF16), w_ref[0:ca, :]) + _dot(m2.astype(BF16), w_ref[ca:, :])
    x1 = _ln(alpha * x_ref[...] + mix, g1_ref[...], b1_ref[...])

    h_ref = rest[-1]
    xb = x1.astype(BF16)
    for c in range(0, wg_ref.shape[1], chunk):
        gate = _dot(xb, wg_ref[:, c:c + chunk])
        up = _dot(xb, wu_ref[:, c:c + chunk])
        h_ref[:, c:c + chunk] = (_silu(gate) * up).astype(BF16)
    out = _ln(alpha * x1 + _dot(h_ref[...], wd_ref[...]), g2_ref[...], b2_ref[...])

    tm = x1.shape[0]
    if lead is None:
        r = i * tm + lax.broadcasted_iota(jnp.int32, (tm, 1), 0)
        keep = jnp.ones((tm, 1), F32)
        for bi in range(nb):
            keep = jnp.where((r >= bi * sp) & (r < bi * sp + fp), 0.0, keep)
        rest[0][...] = out * keep
    else:
        yp_ref, ys_ref = rest[:2]

        @pl.when(jnp.logical_and(jnp.logical_not(is_sample), i % (sp // tm) >= lead))
        def _():
            yp_ref[...] = out

        @pl.when(is_sample)
        def _():
            ys_ref[...] = out


def _layer_tail(name, x, m1p, m2p, m1s, m2s, p1, p2, w, g1, b1, wg, wu, wd, g2, b2, *, even, mix_idx, layer, tm,
                alpha, nb, sp, fp, out_rows=None):
    n, d = x.shape
    s0 = m1p.shape[0]
    dff = wg.shape[2]
    chunk = SEQ_TILE if dff % SEQ_TILE == 0 else dff
    npt = s0 // tm
    row = lambda c: pl.BlockSpec((tm, c), lambda i: (i, 0))
    prow = lambda c: pl.BlockSpec((tm, c), lambda i: (jnp.minimum(i, npt - 1), 0))
    srow = lambda c, **kw: pl.BlockSpec((tm, c), lambda i: (jnp.maximum(i - npt, 0), 0), **kw)
    held = dict(pipeline_mode=pl.Buffered(1))
    once = lambda a: pl.BlockSpec(a.shape, lambda i: (0,) * a.ndim, pipeline_mode=pl.Buffered(1))
    consts = (p1, p2, w, g1, b1, wg, wu, wd, g2, b2)
    const_specs = [once(p1), once(p2), _slab(w, mix_idx), once(g1), once(b1),
                   _slab(wg, layer), _slab(wu, layer), _slab(wd, layer), once(g2), once(b2)]
    if out_rows is None:
        lead = None
        out_specs = row(d)
        out_shape = jax.ShapeDtypeStruct((n, d), F32)
    else:
        seq, db_rows = out_rows
        tps = sp // tm
        lead = (sp - seq) // tm
        out_specs = [
            pl.BlockSpec((None, tm, d), lambda i: (jnp.minimum(i // tps, nb - 1),
                                                   jnp.where(i < npt, jnp.maximum(i % tps - lead, 0), tps - lead - 1),
                                                   0)),
            srow(d)]
        out_shape = [jax.ShapeDtypeStruct((nb, seq, d), F32), jax.ShapeDtypeStruct((db_rows, d), F32)]
    return pl.pallas_call(
        functools.partial(_tail_kernel, even=even, alpha=alpha, chunk=chunk, nb=nb, sp=sp, fp=fp, npt=npt, lead=lead),
        grid=(n // tm,),
        in_specs=[row(d), prow(m1p.shape[1]), prow(m2p.shape[1]), srow(m1s.shape[1], **held),
                  srow(m2s.shape[1], **held)] + const_specs,
        out_specs=out_specs,
        out_shape=out_shape,
        scratch_shapes=[pltpu.VMEM((tm, dff), BF16)],
        compiler_params=_params("arbitrary" if out_rows else "parallel"),
        name=name,
    )(x, m1p, m2p, m1s, m2s, *consts)


def _conv_tile(win, w_rows):
    rows = win.shape[0]
    t = rows - CONV_BACK
    lead = CONV_BACK - (CONV_W - 1)
    acc = None
    for r in range(SUBLANES):
        rolled = win if r == 0 else pltpu.roll(win, rows - r, axis=0)
        for m in range(CONV_BACK // SUBLANES + 1):
            j = SUBLANES * m + r - lead
            if 0 <= j < CONV_W:
                term = rolled[SUBLANES * m:SUBLANES * m + t] * w_rows[j]
                acc = term if acc is None else acc + term
    return acc


def _sample_conv_kernel(st_ref, a_ref, w_ref, b_ref, o_ref, st_out_ref, ext_ref, *, ds):
    dbt, hist, c = st_ref.shape
    off = CONV_BACK - hist
    rows = CONV_BACK + ds
    ext_ref[:, 0:off, :] = jnp.zeros((dbt, off, c), F32)
    ext_ref[:, off:CONV_BACK, :] = st_ref[...]
    ext_ref[:, CONV_BACK:rows, :] = a_ref[...].reshape(dbt, ds, c)
    flat = ext_ref[...].reshape(dbt * rows, c)
    acc = None
    for r in range(SUBLANES):
        shifted = (flat if r == 0 else pltpu.roll(flat, dbt * rows - r, axis=0)).reshape(dbt, rows, c)
        for m in range(rows // SUBLANES):
            j = SUBLANES * m + r - off
            if 0 <= j < CONV_W:
                term = shifted[:, SUBLANES * m:SUBLANES * m + ds, :] * w_ref[j:j + 1, :]
                acc = term if acc is None else acc + term
    o_ref[...] = (acc + b_ref[...]).reshape(dbt * ds, c)
    st_out_ref[...] = ext_ref[:, off + ds:rows, :]


def _sample_conv(state, layer, a, w, b, *, s0, db, ds, dbt):
    c = a.shape[1]
    hist = state.shape[2]
    rows = dbt * ds
    return pl.pallas_call(
        functools.partial(_sample_conv_kernel, ds=ds),
        grid=(db // dbt,),
        in_specs=[pl.BlockSpec((dbt, None, hist, c), lambda i: (i, layer, 0, 0)),
                  pl.BlockSpec((rows, c), lambda i: (s0 // rows + i, 0)), _full(w.shape), _full(b.shape)],
        out_specs=[pl.BlockSpec((rows, c), lambda i: (i, 0)), pl.BlockSpec((dbt, hist, c), lambda i: (i, 0, 0))],
        out_shape=[jax.ShapeDtypeStruct((db * ds, c), F32), jax.ShapeDtypeStruct((db, hist, c), F32)],
        scratch_shapes=[pltpu.VMEM((dbt, CONV_BACK + ds, c), F32)],
        compiler_params=_params("parallel"),
        name="sample_conv",
    )(state, a, w, b)


def _fox_prep_kernel(q_ref, k_ref, v_ref, lf_ref, a_ref, wdw_ref, bdw_ref, *rest, fp, ns, n_prev, emit):
    prev, rest = rest[:2 * n_prev], rest[2 * n_prev:]
    qa_ref, ka_ref, vat_ref, cv_ref = rest[:4]
    carry_ref, ah_ref = rest[6:8] if emit else rest[4:6]
    s = pl.program_id(1)
    t = SEQ_TILE

    @pl.when(s < ns)
    def _():
        _fox_operands(s, q_ref, k_ref, v_ref, lf_ref, qa_ref, ka_ref, vat_ref, carry_ref, fp)

        @pl.when(s == 0)
        def _():
            ah_ref[...] = jnp.zeros_like(ah_ref)

        for cb in range(a_ref.shape[1] // LANES):
            cs = slice(cb * LANES, (cb + 1) * LANES)
            win = jnp.concatenate([ah_ref[:, cs], a_ref[:, cs]], axis=0)
            w_rows = [wdw_ref[j:j + 1, cs] for j in range(CONV_W)]
            cv_ref[:, cs] = _conv_tile(win, w_rows) + bdw_ref[:, cs]
        ah_ref[...] = a_ref[t - CONV_BACK:t, :]

    if not emit:
        return
    kt_ref, vt_ref = rest[4:6]
    kc_ref, vc_ref = rest[8:10]

    @pl.when(s == 0)
    def _():
        kc_ref[...] = jnp.zeros_like(kc_ref)
        vc_ref[...] = jnp.zeros_like(vc_ref)

    k_srcs = list(prev[0::2]) + [k_ref]
    v_srcs = list(prev[1::2]) + [v_ref]
    for srcs, held, dst in ((k_srcs, kc_ref, kt_ref), (v_srcs, vc_ref, vt_ref)):
        for li, src in enumerate(srcs):
            rows = jnp.concatenate([held[li], src[0:fp, :]], axis=0)
            for c in range(src.shape[1] // LANES):
                dst[li, c * LANES:(c + 1) * LANES, :] = rows[:, c * LANES:(c + 1) * LANES].T
            held[li] = src[fp:t, :]


def _fox_operands(s, q_ref, k_ref, v_ref, lf_ref, qa_ref, ka_ref, vat_ref, carry_ref, fp):
    @pl.when(s == 0)
    def _():
        carry_ref[...] = jnp.zeros_like(carry_ref)

    t = SEQ_TILE
    row = lax.broadcasted_iota(jnp.int32, (t, LANES), 0)
    lane = lax.broadcasted_iota(jnp.int32, (t, LANES), 1)
    c = jnp.where(s * t + row >= fp, lf_ref[...], 0.0)
    sh = 1
    while sh < t:
        c = c + jnp.where(row >= sh, pltpu.roll(c, sh, axis=0), 0.0)
        sh *= 2
    c = c + carry_ref[0:1, :]
    carry_ref[0:1, :] = c[t - 1:t, :]

    c = c * LOG2E
    c1 = c.astype(BF16).astype(F32)
    r1 = c - c1
    c2 = r1.astype(BF16).astype(F32)
    c3 = (r1 - c2).astype(BF16).astype(F32)
    g23 = jnp.where(lane < 2 * N_HEADS, pltpu.roll(c2, N_HEADS, axis=1),
                    jnp.where(lane < 3 * N_HEADS, pltpu.roll(c3, 2 * N_HEADS, axis=1), 0.0))
    g = jnp.where(lane < N_HEADS, c1, g23)
    gk = jnp.where(lane < N_HEADS, jnp.where(s * t + row >= fp, c1, -NEG), g23)
    ones_grp = (lane >= HEAD_DIM + AUX_ONES) & (lane < HEAD_DIM + 2 * AUX_ONES)
    ck = [jnp.where((lane >= HEAD_DIM) & (lane < HEAD_DIM + AUX_ONES), -pltpu.roll(gk, HEAD_DIM, axis=1),
                    jnp.where(ones_grp, 1.0, 0.0))]
    cq = [jnp.where(ones_grp, pltpu.roll(g, HEAD_DIM + AUX_ONES, axis=1), 0.0)]
    ck.append(pltpu.roll(ck[0], HEAD_DIM, axis=1))
    cq.append(pltpu.roll(cq[0], HEAD_DIM, axis=1))

    lane1 = lax.broadcasted_iota(jnp.int32, (1, LANES), 1)
    for h in range(N_HEADS):
        odd = h % 2
        pr = h // 2
        base = 0 if odd else HEAD_DIM
        a = lane1 - base
        mine = (lane1 & (N_HEADS - 1)) == h
        data = ((lane1 >= HEAD_DIM) if odd else (lane1 < HEAD_DIM)).astype(F32)
        sel1 = ((a >= 0) & (a < AUX_ONES) & mine).astype(F32)
        selc = ((a >= AUX_ONES) & (a < 2 * AUX_ONES) & mine).astype(F32)
        sl = slice(pr * LANES, (pr + 1) * LANES)
        qa_ref[h] = (q_ref[:, sl] * (data * LOG2E) + (cq[odd] * selc + sel1)).T.astype(BF16)
        ka_ref[h] = (k_ref[:, sl] * data + ck[odd]).astype(BF16)
    ones = jnp.ones((V_ROWS - HEAD_DIM, t), BF16)
    for pr in range(N_HEADS // 2):
        v_t = v_ref[:, pr * LANES:(pr + 1) * LANES].T.astype(BF16)
        vat_ref[2 * pr] = jnp.concatenate([v_t[0:HEAD_DIM], ones], axis=0)
        vat_ref[2 * pr + 1] = jnp.concatenate([v_t[HEAD_DIM:], ones], axis=0)


def _fox_prep(q, k, v, lf, a, w_dw, b_dw, prev_kv, emit, *, nb, sp, fp):
    ns = sp // SEQ_TILE
    att = q.shape[1]
    ca = a.shape[1]
    t = SEQ_TILE
    last = ns - 1
    row = lambda c: pl.BlockSpec((t, c), lambda bi, si: (bi * ns + jnp.minimum(si, last), 0))
    hm = pl.BlockSpec((None, N_HEADS, t, LANES), lambda bi, si: (bi, 0, jnp.minimum(si, last), 0))
    hm_t = lambda r: pl.BlockSpec((None, N_HEADS, r, t), lambda bi, si: (bi, 0, 0, jnp.minimum(si, last)))
    out_specs = [hm_t(LANES), hm, hm_t(V_ROWS), row(ca)]
    out_shape = [jax.ShapeDtypeStruct((nb, N_HEADS, LANES, sp), BF16),
                 jax.ShapeDtypeStruct((nb, N_HEADS, sp, LANES), BF16),
                 jax.ShapeDtypeStruct((nb, N_HEADS, V_ROWS, sp), BF16), jax.ShapeDtypeStruct((nb * sp, ca), F32)]
    scratch = [pltpu.VMEM((SUBLANES, LANES), F32), pltpu.VMEM((CONV_BACK, ca), F32)]
    n_prev = len(prev_kv) if emit else 0
    if emit:
        assert 0 < fp < t and fp % SUBLANES == 0
        n_layers = n_prev + 1
        cache = pl.BlockSpec((None, n_layers, att, t), lambda bi, si: (bi, 0, 0, jnp.maximum(si - 1, 0)))
        out_specs += [cache, cache]
        out_shape += [jax.ShapeDtypeStruct((nb, n_layers, att, sp - fp), F32)] * 2
        scratch += [pltpu.VMEM((n_layers, t - fp, att), F32)] * 2
    prev = [a for kv in prev_kv for a in kv] if emit else []
    return pl.pallas_call(
        functools.partial(_fox_prep_kernel, fp=fp, ns=ns, n_prev=n_prev, emit=emit),
        grid=(nb, ns + 1 if emit else ns),
        in_specs=[row(att), row(att), row(att), row(LANES), row(ca), _full(w_dw.shape), _full(b_dw.shape)]
        + [row(att)] * len(prev),
        out_specs=out_specs,
        out_shape=out_shape,
        scratch_shapes=scratch,
        compiler_params=_params("parallel", "arbitrary"),
        name="fox_prep",
    )(q, k, v, lf, a, w_dw, b_dw, *prev)


def _flash_kernel(q_ref, k_ref, vt_ref, o_ref, s_ref, p_ref, acc_ref, *, hg):
    i = pl.program_id(2)
    t = SEQ_TILE

    def scores_to(j, slot):
        ks = pl.multiple_of(j * t, t)
        for h in range(hg):
            s_ref[slot, h] = _dot(k_ref[h, pl.ds(ks, t), :], q_ref[h])

    def accumulate(j, als):
        ks = pl.multiple_of(j * t, t)
        pvs = [_dot(vt_ref[h, :, pl.ds(ks, t)], p_ref[h]) for h in range(hg)]
        for h in range(hg):
            acc_ref[h] = als[h] * acc_ref[h] + pvs[h]

    def softmax(slot, ms, masked):
        new_m, als = [], []
        for h in range(hg):
            s = s_ref[slot, h]
            if masked:
                visible = (lax.broadcasted_iota(jnp.int32, (t, t), 0) <= lax.broadcasted_iota(jnp.int32, (t, t), 1))
                s = jnp.where(visible, s, NEG)
            mn = jnp.maximum(ms[h], jnp.max(s, axis=0, keepdims=True))
            als.append(jnp.exp2(ms[h] - mn))
            new_m.append(mn)
            p_ref[h] = jnp.exp2(s - mn).astype(BF16)
        return tuple(new_m), tuple(als)

    def stage(j, slot, state):
        ms, als = state
        scores_to(j + 1, 1 - slot)
        accumulate(jnp.maximum(j - 1, 0), als)
        return softmax(slot, ms, False)

    def pair(jj, state):
        return stage(2 * jj + 1, 1, stage(2 * jj, 0, state))

    def finish(slot, state):
        ms, als = state
        accumulate(jnp.maximum(i - 1, 0), als)
        _, als = softmax(slot, ms, True)
        accumulate(i, als)
        for pr in range(hg // 2):
            halves = [acc_ref[h, 0:HEAD_DIM, :] * (1.0 / acc_ref[h, HEAD_DIM:HEAD_DIM + 1, :])
                      for h in (2 * pr, 2 * pr + 1)]
            o_ref[:, pr * LANES:(pr + 1) * LANES] = jnp.concatenate(halves, axis=0).T.astype(o_ref.dtype)

    p_ref[...] = jnp.zeros_like(p_ref)
    acc_ref[...] = jnp.zeros_like(acc_ref)
    scores_to(0, 0)
    state = ((jnp.full((1, t), NEG, F32),) * hg, (jnp.ones((1, t), F32),) * hg)
    state = lax.fori_loop(0, i // 2, pair, state)
    odd = i % 2 == 1
    state = lax.cond(odd, lambda st: stage(i - 1, 0, st), lambda st: st, state)
    pl.when(odd)(lambda: finish(1, state))
    pl.when(jnp.logical_not(odd))(lambda: finish(0, state))


def _flash(qat, ka, vat, *, hg):
    nb, _, sp, _ = ka.shape
    ns = sp // SEQ_TILE
    return pl.pallas_call(
        functools.partial(_flash_kernel, hg=hg),
        grid=(nb, N_HEADS // hg, ns),
        in_specs=[pl.BlockSpec((None, hg, LANES, SEQ_TILE), lambda bi, gi, qi: (bi, gi, 0, qi)),
                  pl.BlockSpec((None, hg, sp, LANES), lambda bi, gi, qi: (bi, gi, 0, 0)),
                  pl.BlockSpec((None, hg, V_ROWS, sp), lambda bi, gi, qi: (bi, gi, 0, 0))],
        out_specs=pl.BlockSpec((SEQ_TILE, hg * HEAD_DIM), lambda bi, gi, qi: (bi * ns + qi, gi)),
        out_shape=jax.ShapeDtypeStruct((nb * sp, N_HEADS * HEAD_DIM), BF16),
        scratch_shapes=[pltpu.VMEM((2, hg, SEQ_TILE, SEQ_TILE), F32), pltpu.VMEM((hg, SEQ_TILE, SEQ_TILE), BF16),
                        pltpu.VMEM((hg, V_ROWS, SEQ_TILE), F32)],
        compiler_params=_params("parallel", "parallel", "arbitrary"),
        name="fox_flash",
    )(qat, ka, vat)


def _sample_attn_kernel(pt_ref, q_ref, kn_ref, vn_ref, lfn_ref, *rest, n_pages, page, eb, ds):
    del pt_ref
    o_ref = rest[3 * eb * n_pages]
    c = q_ref.shape[1]
    rows = ds * N_HEADS
    groups = [list(range(g, min(g + 2, n_pages))) for g in range(0, n_pages, 2)]
    past = n_pages * page
    sub = lax.broadcasted_iota(jnp.int32, (N_HEADS, c), 0)
    lane = lax.broadcasted_iota(jnp.int32, (N_HEADS, c), 1)
    hm = ((lane >= sub * HEAD_DIM) & (lane < (sub + 1) * HEAD_DIM)).astype(F32)
    hm_t = jnp.concatenate([hm] * ds, axis=0)
    pad = jnp.zeros((LANES - ds, c), F32)

    def refs_of(kind, e):
        return rest[(kind * eb + e) * n_pages:(kind * eb + e + 1) * n_pages]

    def scores(e):
        sl = slice(e * ds, (e + 1) * ds)
        q = q_ref[sl, :]
        qbd = jnp.concatenate([jnp.broadcast_to(q[t:t + 1, :], (N_HEADS, c)) * hm for t in range(ds)],
                              axis=0).astype(BF16)
        k_refs, lf_refs = refs_of(0, e), refs_of(2, e)
        s_parts = []
        for grp in groups:
            kp = jnp.concatenate([k_refs[r][...] for r in grp], axis=1).astype(BF16)
            s_parts.append(_dot(qbd, kp))
        s_past = jnp.concatenate(s_parts, axis=1)

        lf = jnp.concatenate([r[...] for r in lf_refs], axis=1)
        lane_p = lax.broadcasted_iota(jnp.int32, (N_HEADS, past), 1)
        suf = lf
        sh = 1
        while sh < past:
            suf = suf + jnp.where(lane_p < past - sh, pltpu.roll(suf, past - sh, axis=1), 0.0)
            sh *= 2
        s_past = s_past + jnp.concatenate([suf - lf] * ds, axis=0)

        kn = jnp.concatenate([kn_ref[sl, :], pad], axis=0).astype(BF16)
        s_new = lax.dot_general(qbd, kn, NT_DIMS, preferred_element_type=F32)
        cn = lfn_ref[sl, :]
        sub_n = lax.broadcasted_iota(jnp.int32, cn.shape, 0)
        sh = 1
        while sh < ds:
            cn = cn + jnp.where(sub_n >= sh, pltpu.roll(cn, sh, axis=0), 0.0)
            sh *= 2
        cn_t = jnp.concatenate([cn, jnp.zeros((LANES - ds, LANES), F32)], axis=0).T[0:N_HEADS, :]
        rown = lax.broadcasted_iota(jnp.int32, (rows, LANES), 0)
        u = lax.broadcasted_iota(jnp.int32, (rows, LANES), 1)
        s_new = jnp.where(u * N_HEADS <= rown, s_new - jnp.concatenate([cn_t] * ds, axis=0), NEG)
        return s_past, s_new

    def softmax(s_past, s_new):
        m = jnp.maximum(jnp.max(s_past, axis=1, keepdims=True), jnp.max(s_new, axis=1, keepdims=True))
        p_past = jnp.exp(s_past - m)
        p_new = jnp.exp(s_new - m)
        l = jnp.sum(p_past, axis=1, keepdims=True) + jnp.sum(p_new, axis=1, keepdims=True)
        return p_past.astype(BF16), p_new.astype(BF16), l

    def output(e, p_past, p_new, l):
        sl = slice(e * ds, (e + 1) * ds)
        v_refs = refs_of(1, e)
        vn = jnp.concatenate([vn_ref[sl, :], pad], axis=0).astype(BF16)
        o = _dot(p_new, vn)
        for grp in groups:
            vp = jnp.concatenate([v_refs[r][...] for r in grp], axis=1).astype(BF16)
            lo = grp[0] * page
            o = o + lax.dot_general(p_past[:, lo:lo + len(grp) * page], vp, NT_DIMS, preferred_element_type=F32)
        o = o * (1.0 / l) * hm_t
        rr = lax.broadcasted_iota(jnp.int32, (ds, rows), 1)
        tt = lax.broadcasted_iota(jnp.int32, (ds, rows), 0)
        pick = ((rr >= tt * N_HEADS) & (rr < (tt + 1) * N_HEADS)).astype(BF16)
        o_ref[sl, :] = _dot(pick, o.astype(BF16)).astype(o_ref.dtype)

    ss = [scores(e) for e in range(eb)]
    ps = [softmax(*s) for s in ss]
    for e in range(eb):
        output(e, *ps[e])


def _sample_attn(page_table, layer, q, k, v, lf, cache_k, cache_v, cache_lft, *, s0, db, ds):
    c = q.shape[1]
    n_pages = page_table.shape[1]
    page = cache_k.shape[3]
    eb = 2 if db % 2 == 0 else 1
    rows = eb * ds
    row = lambda cc: pl.BlockSpec((rows, cc), lambda i, pt: (s0 // rows + i, 0))

    def paged(shape, e, r):
        return pl.BlockSpec((None, None) + shape, lambda i, pt: (pt[(i * eb + e) * n_pages + r], layer, 0, 0))

    pages = [(e, r) for e in range(eb) for r in range(n_pages)]
    in_specs = ([row(c), row(c), row(c), row(LANES)]
                + [paged((c, page), e, r) for e, r in pages]
                + [paged((c, page), e, r) for e, r in pages]
                + [paged((N_HEADS, page), e, r) for e, r in pages])
    return pl.pallas_call(
        functools.partial(_sample_attn_kernel, n_pages=n_pages, page=page, eb=eb, ds=ds),
        grid_spec=pltpu.PrefetchScalarGridSpec(
            num_scalar_prefetch=1, grid=(db // eb,), in_specs=in_specs,
            out_specs=pl.BlockSpec((rows, c), lambda i, pt: (i, 0))),
        out_shape=jax.ShapeDtypeStruct((db * ds, c), BF16 if rows % (2 * SUBLANES) == 0 else F32),
        compiler_params=_params("parallel"),
        name="sample_paged_attn",
    )(page_table.reshape(-1), q, k, v, lf, *([cache_k] * len(pages)), *([cache_v] * len(pages)),
      *([cache_lft] * len(pages)))


def _odd_tile(wy, wp, bg, w_rows, window, pos0):
    t = bg.shape[0]
    y0 = wy[SUBLANES:]
    y1 = pltpu.roll(wy, 1, axis=0)[SUBLANES:]
    y2 = pltpu.roll(wy, 2, axis=0)[SUBLANES:]
    out_c = bg * (w_rows[0] * y2 + w_rows[1] * y1 + w_rows[2] * y0)
    back = 2 * SUBLANES
    win = wp
    step = 1
    while step < window:
        win = win + pltpu.roll(win, step, axis=0)
        step *= 2
    if pos0 is None:
        cnt = float(window)
    else:
        pos = pos0 + lax.broadcasted_iota(jnp.int32, (t, LANES), 0)
        cnt = jnp.maximum(jnp.minimum(window, pos + 1), 1).astype(F32)
    return out_c, win[back:] / cnt - wp[back:]


def _prompt_odd_kernel(y_ref, bg_ref, pd_ref, w_ref, oc_ref, d_ref, *, n_tiles, fp):
    w_rows = [w_ref[j:j + 1, :] for j in range(SCONV_W)]
    c = y_ref.shape[1]
    t = SEQ_TILE
    n_partial = -(-(fp + POOL_HIST) // t)

    def run(window):
        wy0 = jnp.concatenate([jnp.zeros((SUBLANES, c), F32), y_ref[0:t, :]], axis=0)
        wp0 = jnp.concatenate([jnp.zeros((2 * SUBLANES, c), F32), pd_ref[0:t, :]], axis=0)
        oc, d = _odd_tile(wy0, wp0, bg_ref[0:t, :], w_rows, window, -fp)
        oc_ref[0:t, :] = oc.astype(oc_ref.dtype)
        d_ref[0:t, :] = d.astype(d_ref.dtype)

        def body(i, partial):
            s0 = pl.multiple_of(i * t, t)
            wy = y_ref[pl.ds(s0 - SUBLANES, t + SUBLANES), :]
            wp = pd_ref[pl.ds(s0 - 2 * SUBLANES, t + 2 * SUBLANES), :]
            oc, d = _odd_tile(wy, wp, bg_ref[pl.ds(s0, t), :], w_rows, window, s0 - fp if partial else None)
            oc_ref[pl.ds(s0, t), :] = oc.astype(oc_ref.dtype)
            d_ref[pl.ds(s0, t), :] = d.astype(d_ref.dtype)

        lax.fori_loop(1, n_partial, lambda i, carry: body(i, True) or carry, 0)
        lax.fori_loop(n_partial, n_tiles, lambda i, carry: body(i, False) or carry, 0)

    for gi, window in enumerate(POOL_WINDOWS):
        pl.when(pl.program_id(1) == gi)(functools.partial(run, window))


def _prompt_odd(y, bg, pd, w, *, nb, sp, fp):
    c = y.shape[1]
    assert c // LANES == len(POOL_WINDOWS)
    blk = pl.BlockSpec((sp, LANES), lambda bi, ci: (bi, ci))
    return pl.pallas_call(
        functools.partial(_prompt_odd_kernel, n_tiles=sp // SEQ_TILE, fp=fp),
        grid=(nb, c // LANES),
        in_specs=[blk, blk, blk, pl.BlockSpec((SCONV_W, LANES), lambda bi, ci: (0, ci))],
        out_specs=[blk, blk],
        out_shape=[jax.ShapeDtypeStruct((nb * sp, c), BF16)] * 2,
        compiler_params=_params("parallel", "parallel"),
        name="prompt_sconv_pool",
    )(y, bg, pd, w)


def _sample_odd_kernel(ss_ref, sp_ref, y_ref, bg_ref, pd_ref, w_ref,
                       oc_ref, d_ref, ss_out_ref, sp_out_ref, yext_ref, pext_ref, *, ds):
    dbt, _, c = ss_ref.shape
    hs = SCONV_W - 1
    y = y_ref[...].reshape(dbt, ds, c)
    pd = pd_ref[...].reshape(dbt, ds, c)
    bg = bg_ref[...].reshape(dbt, ds, c)
    ny = SUBLANES + ds
    yext_ref[:, 0:SUBLANES - hs, :] = jnp.zeros((dbt, SUBLANES - hs, c), F32)
    yext_ref[:, SUBLANES - hs:SUBLANES, :] = ss_ref[...]
    yext_ref[:, SUBLANES:ny, :] = y
    yflat = yext_ref[...].reshape(dbt * ny, c)
    conv = y * w_ref[SCONV_W - 1:SCONV_W, :]
    for back in range(1, SCONV_W):
        shifted = pltpu.roll(yflat, back, axis=0).reshape(dbt, ny, c)[:, SUBLANES:ny, :]
        conv = conv + shifted * w_ref[SCONV_W - 1 - back:SCONV_W - back, :]
    oc_ref[...] = (bg * conv).reshape(dbt * ds, c).astype(oc_ref.dtype)
    ss_out_ref[...] = yext_ref[:, ny - hs:ny, :]

    base = 2 * SUBLANES
    npd = base + ds
    pext_ref[:, 0:base - POOL_HIST, :] = jnp.zeros((dbt, base - POOL_HIST, c), F32)
    pext_ref[:, base - POOL_HIST:base, :] = sp_ref[...]
    pext_ref[:, base:npd, :] = pd
    gc = c // len(POOL_WINDOWS)
    means = []
    for gi, window in enumerate(POOL_WINDOWS):
        win = pext_ref[:, :, gi * gc:(gi + 1) * gc].reshape(dbt * npd, gc)
        step = 1
        while step < window:
            win = win + pltpu.roll(win, step, axis=0)
            step *= 2
        means.append(win.reshape(dbt, npd, gc)[:, base:npd, :] / float(window))
    d_ref[...] = (jnp.concatenate(means, axis=-1) - pd).reshape(dbt * ds, c).astype(d_ref.dtype)
    sp_out_ref[...] = pext_ref[:, npd - POOL_HIST:npd, :]


def _sample_odd(state_s, state_p, layer, y, bg, pd, w, *, s0, db, ds, dbt):
    c = y.shape[1]
    rows = dbt * ds
    row_in = pl.BlockSpec((rows, c), lambda i: (s0 // rows + i, 0))
    row_out = pl.BlockSpec((rows, c), lambda i: (i, 0))
    hs = state_s.shape[2]
    hp = state_p.shape[2]
    return pl.pallas_call(
        functools.partial(_sample_odd_kernel, ds=ds),
        grid=(db // dbt,),
        in_specs=[pl.BlockSpec((dbt, None, hs, c), lambda i: (i, layer, 0, 0)),
                  pl.BlockSpec((dbt, None, hp, c), lambda i: (i, layer, 0, 0)),
                  row_in, row_in, row_in, _full(w.shape)],
        out_specs=[row_out, row_out, pl.BlockSpec((dbt, hs, c), lambda i: (i, 0, 0)),
                   pl.BlockSpec((dbt, hp, c), lambda i: (i, 0, 0))],
        out_shape=[jax.ShapeDtypeStruct((db * ds, c), BF16 if rows % (2 * SUBLANES) == 0 else F32)] * 2 + [
                   jax.ShapeDtypeStruct((db, hs, c), F32), jax.ShapeDtypeStruct((db, hp, c), F32)],
        scratch_shapes=[pltpu.VMEM((dbt, SUBLANES + ds, c), F32), pltpu.VMEM((dbt, 2 * SUBLANES + ds, c), F32)],
        compiler_params=_params("parallel"),
        name="sample_sconv_pool",
    )(state_s, state_p, y, bg, pd, w)


def kernel(x_prompt, x_sample, cache_k, cache_v, cache_logf, page_table, state_conv_a, state_sconv, state_pool, meta_tokens, w_in_even, b_forget, w_dw_a, b_dw_a, ln_a_g, ln_a_b, w_out_even, w_in_odd, w_sconv, w_pool_mix, pool_scale, w_out_odd, ln_mix_g, ln_mix_b, w_ffn_gate, w_ffn_up, w_ffn_down, ln_ffn_g, ln_ffn_b):
    nb, seq, d = x_prompt.shape
    db, ds, _ = x_sample.shape
    depth = w_ffn_gate.shape[0]
    alpha = float((2 * depth) ** 0.25)
    ca = state_conv_a.shape[-1]
    att = N_HEADS * HEAD_DIM
    assert ds == SUBLANES and cache_k.shape[3] == N_HEADS and cache_k.shape[4] == HEAD_DIM
    assert state_conv_a.shape[2] == CONV_W - 1 and state_pool.shape[2] == POOL_HIST
    assert meta_tokens.shape[0] == N_META

    s_real = N_META + seq
    fp = (-s_real) % SEQ_TILE
    if fp < CONV_BACK:
        fp += SEQ_TILE
    sp = fp + s_real
    s0 = nb * sp
    n = s0 + db * ds
    tm = next(t for t in (1024, 512, 256, 128, 64, 32, 16, 8) if s0 % t == 0 and (db * ds) % t == 0)
    tm_in = next(t for t in (1024, 512, 256, 128, 64, 32, 16, 8) if n % t == 0)
    dbt = next(t for t in (16, 8, 4, 2, 1) if db % t == 0)
    direct_out = seq % SEQ_TILE == 0 and (db * ds) % SEQ_TILE == 0

    head = jnp.concatenate([jnp.zeros((fp, d), F32), meta_tokens.astype(F32)], axis=0)
    fused_first = direct_out and tm_in % SEQ_TILE == 0
    x = None if fused_first else jnp.concatenate(
        [piece for bi in range(nb) for piece in (head, x_prompt[bi])] + [x_sample.reshape(db * ds, d)], axis=0)

    n_phys = cache_k.shape[0]
    n_even = cache_k.shape[1]
    page = cache_k.shape[2]
    ck = jnp.transpose(cache_k, (0, 1, 3, 4, 2)).reshape(n_phys, n_even, att, page)
    cv = jnp.transpose(cache_v, (0, 1, 3, 4, 2)).reshape(n_phys, n_even, att, page)
    clft = jnp.swapaxes(cache_logf, 2, 3)

    row2 = lambda v: v.reshape(1, -1).astype(F32)

    def prompt_rows(arr, lo, hi):
        return jnp.stack([arr[bi * sp + lo:bi * sp + hi] for bi in range(nb)], axis=0)

    def sample_rows(arr):
        return arr[s0:].reshape(db, ds, -1)

    wg_all, wu_all, wd_all = (w.astype(BF16) for w in (w_ffn_gate, w_ffn_up, w_ffn_down))
    wo_even, wo_odd = w_out_even.astype(BF16), w_out_odd.astype(BF16)
    wi_even = jnp.pad(w_in_even.astype(BF16), ((0, 0), (0, 0), (0, LANES - N_HEADS)))
    wi_odd = w_in_odd.astype(BF16)

    def tail(name, xin, m1p, m2p, m1s, m2s, p1, p2, w_out_all, layer):
        last = direct_out and layer == depth - 1
        return _layer_tail(name, xin, m1p, m2p, m1s, m2s, p1, p2, w_out_all,
                           row2(ln_mix_g[layer]), row2(ln_mix_b[layer]), wg_all, wu_all, wd_all,
                           row2(ln_ffn_g[layer]), row2(ln_ffn_b[layer]), even=layer % 2 == 0,
                           mix_idx=layer // 2, layer=layer, tm=SEQ_TILE if last else tm, alpha=alpha,
                           nb=nb, sp=sp, fp=fp, out_rows=(seq, db * ds) if last else None)

    prev_kv = []
    lfp, cap, scp, plp = [], [], [], []
    ks_, vs_, lfs, cas, scs, pls = [], [], [], [], [], []
    for layer in range(depth):
        i = layer // 2
        if layer % 2 == 0:
            bf = jnp.pad(b_forget[i].astype(F32), (0, LANES - N_HEADS)).reshape(1, LANES)
            if x is None:
                x, a, q, k, v, lf = _first_in(x_prompt.astype(F32), head, x_sample.reshape(db * ds, d).astype(F32),
                                              wi_even, i, bf, tm=tm_in, sp=sp, ca=ca, att=att)
            else:
                a, q, k, v, lf = _even_in(x, wi_even, i, bf, tm=tm_in, ca=ca, att=att)
            w_dw, b_dw = w_dw_a[i].astype(F32), row2(b_dw_a[i])
            conv_s, st_a = _sample_conv(state_conv_a.astype(F32), i, a, w_dw, b_dw, s0=s0, db=db, ds=ds, dbt=dbt)
            qa, ka, vat, conv_p, *caches = _fox_prep(q, k, v, lf, a, w_dw, b_dw, prev_kv, i == n_even - 1,
                                                     nb=nb, sp=sp, fp=fp)
            prev_kv.append((k, v))
            at_p = _flash(qa, ka, vat, hg=FLASH_HEADS)
            at_s = _sample_attn(page_table, i, q, k, v, lf, ck, cv, clft, s0=s0, db=db, ds=ds)
            x = tail("even_tail", x, conv_p, at_p, conv_s, at_s, row2(ln_a_g[i]), row2(ln_a_b[i]), wo_even, layer)
            lfp.append(prompt_rows(lf, fp, sp)[..., :N_HEADS])
            cap.append(prompt_rows(a, sp - (CONV_W - 1), sp))
            ks_.append(sample_rows(k).reshape(db, ds, N_HEADS, HEAD_DIM))
            vs_.append(sample_rows(v).reshape(db, ds, N_HEADS, HEAD_DIM))
            lfs.append(sample_rows(lf)[..., :N_HEADS])
            cas.append(st_a)
        else:
            c = state_sconv.shape[-1]
            y, bg, pd = _odd_in(x, wi_odd, i, tm=tm_in, c=c)
            oc_p, dd_p = _prompt_odd(y, bg, pd, w_sconv[i].astype(F32), nb=nb, sp=sp, fp=fp)
            oc_s, dd_s, st_s, st_p = _sample_odd(state_sconv.astype(F32), state_pool.astype(F32), i, y, bg, pd,
                                                 w_sconv[i].astype(F32), s0=s0, db=db, ds=ds, dbt=dbt)
            wm = jax.scipy.linalg.block_diag(*[w_pool_mix[i, g] for g in range(w_pool_mix.shape[1])]).astype(BF16)
            x = tail("odd_tail", x, oc_p, dd_p, oc_s, dd_s, wm, row2(pool_scale[i]), wo_odd, layer)
            scp.append(prompt_rows(y, sp - (SCONV_W - 1), sp))
            plp.append(prompt_rows(pd, sp - POOL_HIST, sp))
            scs.append(st_s)
            pls.append(st_p)

    if direct_out:
        y_prompt, y_sample = x[0], x[1].reshape(db, ds, d)
    else:
        y_prompt, y_sample = prompt_rows(x, fp + N_META, sp), sample_rows(x)
    st = lambda xs: jnp.stack(xs, axis=1)
    kp, vp = (jnp.transpose(c.reshape(nb, n_even, N_HEADS, HEAD_DIM, s_real), (0, 1, 4, 2, 3)) for c in caches)
    return (y_prompt, y_sample, kp, vp, st(lfp), st(cap), st(scp), st(plp),
            st(ks_), st(vs_), st(lfs), st(cas), st(scs), st(pls))
```

```python
import functools

import jax
import jax.numpy as jnp
from jax import lax
from jax.experimental import pallas as pl
from jax.experimental.pallas import tpu as pltpu

N_META = 16
N_HEADS = 8
HEAD_DIM = 64
CONV_W = 31
SCONV_W = 3
POOL_WINDOWS = (2, 4, 8, 16)
POOL_HIST = max(POOL_WINDOWS) - 1
LN_EPS = 1e-5
SEQ_TILE = 256
LANES = 128
SUBLANES = 8
CONV_BACK = -(-(CONV_W - 1) // SUBLANES) * SUBLANES
C_TERMS = 3
AUX_ONES = C_TERMS * N_HEADS
FLASH_HEADS = 8
V_ROWS = HEAD_DIM + 2 * SUBLANES
VMEM_LIMIT = 56 * 1024 * 1024
NEG = -1e30
LOG2E = 1.4426950408889634
F32 = jnp.float32
BF16 = jnp.bfloat16
NT_DIMS = (((1,), (1,)), ((), ()))


def _params(*sem):
    return pltpu.CompilerParams(dimension_semantics=sem, vmem_limit_bytes=VMEM_LIMIT)


def _dot(a, b):
    return jnp.dot(a, b, preferred_element_type=F32)


def _ln(z, g, b):
    mu = jnp.mean(z, axis=-1, keepdims=True)
    zc = z - mu
    var = jnp.mean(zc * zc, axis=-1, keepdims=True)
    return zc * lax.rsqrt(var + LN_EPS) * g + b


def _silu(x):
    return x * jax.nn.sigmoid(x)


def _full(shape):
    return pl.BlockSpec(shape, lambda *_: (0,) * len(shape))


def _slab(a, idx):
    return pl.BlockSpec((None,) + a.shape[1:], lambda *_: (idx,) + (0,) * (a.ndim - 1),
                        pipeline_mode=pl.Buffered(1))


def _first_in_kernel(*refs, sub, tps, npt, ca, att):
    xp, head_ref, xs = refs[:sub], refs[sub], refs[sub + 1:2 * sub + 1]
    w_ref, bf_ref, x0_ref, *outs = refs[2 * sub + 1:]
    t = SEQ_TILE
    for k in range(sub):
        g = pl.program_id(0) * sub + k
        is_sample = g >= npt
        is_head = jnp.logical_and(g % tps == 0, jnp.logical_not(is_sample))
        x0_ref[k * t:(k + 1) * t, :] = jnp.where(is_sample, xs[k][...], jnp.where(is_head, head_ref[...], xp[k][...]))
    _even_in_kernel(x0_ref, w_ref, bf_ref, *outs, ca=ca, att=att)


def _even_in_kernel(x_ref, w_ref, bf_ref, a_ref, q_ref, k_ref, v_ref, lf_ref, *, ca, att):
    xb = x_ref[...].astype(BF16)

    def mm(lo, hi):
        return _dot(xb, w_ref[:, lo:hi])

    u = mm(0, ca)
    g = mm(ca, 2 * ca)
    a_ref[...] = u * jax.nn.sigmoid(g)
    o = 2 * ca
    q_ref[...] = mm(o, o + att) * (HEAD_DIM ** -0.5)
    k_ref[...] = mm(o + att, o + 2 * att)
    v_ref[...] = mm(o + 2 * att, o + 3 * att)
    z = mm(o + 3 * att, o + 3 * att + LANES) + bf_ref[...]
    lf = jnp.minimum(z, 0.0) - jnp.log1p(jnp.exp(-jnp.abs(z)))
    lane = lax.broadcasted_iota(jnp.int32, lf.shape, 1)
    lf_ref[...] = jnp.where(lane < N_HEADS, lf, 0.0)


def _even_in(x, w, idx, bf, *, tm, ca, att):
    n, d = x.shape
    row = lambda c: pl.BlockSpec((tm, c), lambda i: (i, 0))
    return pl.pallas_call(
        functools.partial(_even_in_kernel, ca=ca, att=att),
        grid=(n // tm,),
        in_specs=[row(d), _slab(w, idx), _full(bf.shape)],
        out_specs=[row(ca), row(att), row(att), row(att), row(LANES)],
        out_shape=[jax.ShapeDtypeStruct((n, c), F32) for c in (ca, att, att, att, LANES)],
        compiler_params=_params("parallel"),
        name="even_in_proj",
    )(x, w, bf)


def _first_in(x_prompt, head, x_sample, w, idx, bf, *, tm, sp, ca, att):
    nb, seq, d = x_prompt.shape
    t = SEQ_TILE
    sub, tps = tm // t, sp // t
    npt, ns = nb * tps, x_sample.shape[0] // t
    n = nb * sp + x_sample.shape[0]
    row = lambda c: pl.BlockSpec((tm, c), lambda i: (i, 0))
    xp_spec = lambda k: pl.BlockSpec(
        (None, t, d), lambda i: (jnp.minimum((i * sub + k) // tps, nb - 1),
                                 jnp.clip((i * sub + k) % tps - 1, 0, seq // t - 1), 0))
    xs_spec = lambda k: pl.BlockSpec((t, d), lambda i: (jnp.clip(i * sub + k - npt, 0, ns - 1), 0))
    return pl.pallas_call(
        functools.partial(_first_in_kernel, sub=sub, tps=tps, npt=npt, ca=ca, att=att),
        grid=(n // tm,),
        in_specs=[xp_spec(k) for k in range(sub)] + [_full(head.shape)] + [xs_spec(k) for k in range(sub)]
        + [_slab(w, idx), _full(bf.shape)],
        out_specs=[row(d), row(ca), row(att), row(att), row(att), row(LANES)],
        out_shape=[jax.ShapeDtypeStruct((n, c), F32) for c in (d, ca, att, att, att, LANES)],
        compiler_params=_params("parallel"),
        name="first_in_proj",
    )(*([x_prompt] * sub), head, *([x_sample] * sub), w, bf)


def _odd_in_kernel(x_ref, w_ref, y_ref, bg_ref, pd_ref, *, c):
    xb = x_ref[...].astype(BF16)
    hc = _dot(xb, w_ref[:, 0:c])
    bg_ref[...] = _dot(xb, w_ref[:, c:2 * c])
    cg = _dot(xb, w_ref[:, 2 * c:3 * c])
    y_ref[...] = cg * hc
    pd_ref[...] = _dot(xb, w_ref[:, 3 * c:4 * c])


def _odd_in(x, w, idx, *, tm, c):
    n, d = x.shape
    row = lambda cc: pl.BlockSpec((tm, cc), lambda i: (i, 0))
    return pl.pallas_call(
        functools.partial(_odd_in_kernel, c=c),
        grid=(n // tm,),
        in_specs=[row(d), _slab(w, idx)],
        out_specs=[row(c)] * 3,
        out_shape=[jax.ShapeDtypeStruct((n, c), F32)] * 3,
        compiler_params=_params("parallel"),
        name="odd_in_proj",
    )(x, w)


def _tail_kernel(x_ref, m1p_ref, m2p_ref, m1s_ref, m2s_ref, p1_ref, p2_ref, w_ref, g1_ref, b1_ref,
                 wg_ref, wu_ref, wd_ref, g2_ref, b2_ref, *rest, even, alpha, chunk, nb, sp, fp, npt, lead):
    i = pl.program_id(0)
    is_sample = i >= npt
    m1 = jnp.where(is_sample, m1s_ref[...], m1p_ref[...])
    m2 = jnp.where(is_sample, m2s_ref[...], m2p_ref[...])
    if even:
        m1 = _silu(_ln(m1, p1_ref[...], p2_ref[...]))
    else:
        m2 = _dot(m2.astype(BF16), p1_ref[...]) * p2_ref[...]
    ca = m1.shape[1]
    mix = _dot(m1.astype(BF16), w_ref[0:ca, :]) + _dot(m2.astype(BF16), w_ref[ca:, :])
    x1 = _ln(alpha * x_ref[...] + mix, g1_ref[...], b1_ref[...])

    h_ref = rest[-1]
    xb = x1.astype(BF16)
    for c in range(0, wg_ref.shape[1], chunk):
        gate = _dot(xb, wg_ref[:, c:c + chunk])
        up = _dot(xb, wu_ref[:, c:c + chunk])
        h_ref[:, c:c + chunk] = (_silu(gate) * up).astype(BF16)
    out = _ln(alpha * x1 + _dot(h_ref[...], wd_ref[...]), g2_ref[...], b2_ref[...])

    tm = x1.shape[0]
    if lead is None:
        r = i * tm + lax.broadcasted_iota(jnp.int32, (tm, 1), 0)
        keep = jnp.ones((tm, 1), F32)
        for bi in range(nb):
            keep = jnp.where((r >= bi * sp) & (r < bi * sp + fp), 0.0, keep)
        rest[0][...] = out * keep
    else:
        yp_ref, ys_ref = rest[:2]

        @pl.when(jnp.logical_and(jnp.logical_not(is_sample), i % (sp // tm) >= lead))
        def _():
            yp_ref[...] = out

        @pl.when(is_sample)
        def _():
            ys_ref[...] = out


def _layer_tail(name, x, m1p, m2p, m1s, m2s, p1, p2, w, g1, b1, wg, wu, wd, g2, b2, *, even, mix_idx, layer, tm,
                alpha, nb, sp, fp, out_rows=None):
    n, d = x.shape
    s0 = m1p.shape[0]
    dff = wg.shape[2]
    chunk = SEQ_TILE if dff % SEQ_TILE == 0 else dff
    npt = s0 // tm
    row = lambda c: pl.BlockSpec((tm, c), lambda i: (i, 0))
    prow = lambda c: pl.BlockSpec((tm, c), lambda i: (jnp.minimum(i, npt - 1), 0))
    srow = lambda c, **kw: pl.BlockSpec((tm, c), lambda i: (jnp.maximum(i - npt, 0), 0), **kw)
    held = dict(pipeline_mode=pl.Buffered(1))
    once = lambda a: pl.BlockSpec(a.shape, lambda i: (0,) * a.ndim, pipeline_mode=pl.Buffered(1))
    consts = (p1, p2, w, g1, b1, wg, wu, wd, g2, b2)
    const_specs = [once(p1), once(p2), _slab(w, mix_idx), once(g1), once(b1),
                   _slab(wg, layer), _slab(wu, layer), _slab(wd, layer), once(g2), once(b2)]
    if out_rows is None:
        lead = None
        out_specs = row(d)
        out_shape = jax.ShapeDtypeStruct((n, d), F32)
    else:
        seq, db_rows = out_rows
        tps = sp // tm
        lead = (sp - seq) // tm
        out_specs = [
            pl.BlockSpec((None, tm, d), lambda i: (jnp.minimum(i // tps, nb - 1),
                                                   jnp.where(i < npt, jnp.maximum(i % tps - lead, 0), tps - lead - 1),
                                                   0)),
            srow(d)]
        out_shape = [jax.ShapeDtypeStruct((nb, seq, d), F32), jax.ShapeDtypeStruct((db_rows, d), F32)]
    return pl.pallas_call(
        functools.partial(_tail_kernel, even=even, alpha=alpha, chunk=chunk, nb=nb, sp=sp, fp=fp, npt=npt, lead=lead),
        grid=(n // tm,),
        in_specs=[row(d), prow(m1p.shape[1]), prow(m2p.shape[1]), srow(m1s.shape[1], **held),
                  srow(m2s.shape[1], **held)] + const_specs,
        out_specs=out_specs,
        out_shape=out_shape,
        scratch_shapes=[pltpu.VMEM((tm, dff), BF16)],
        compiler_params=_params("arbitrary" if out_rows else "parallel"),
        name=name,
    )(x, m1p, m2p, m1s, m2s, *consts)


def _conv_tile(win, w_rows):
    rows = win.shape[0]
    t = rows - CONV_BACK
    lead = CONV_BACK - (CONV_W - 1)
    acc = None
    for r in range(SUBLANES):
        rolled = win if r == 0 else pltpu.roll(win, rows - r, axis=0)
        for m in range(CONV_BACK // SUBLANES + 1):
            j = SUBLANES * m + r - lead
            if 0 <= j < CONV_W:
                term = rolled[SUBLANES * m:SUBLANES * m + t] * w_rows[j]
                acc = term if acc is None else acc + term
    return acc


def _sample_conv_kernel(st_ref, a_ref, w_ref, b_ref, o_ref, st_out_ref, ext_ref, *, ds):
    dbt, hist, c = st_ref.shape
    off = CONV_BACK - hist
    rows = CONV_BACK + ds
    ext_ref[:, 0:off, :] = jnp.zeros((dbt, off, c), F32)
    ext_ref[:, off:CONV_BACK, :] = st_ref[...]
    ext_ref[:, CONV_BACK:rows, :] = a_ref[...].reshape(dbt, ds, c)
    flat = ext_ref[...].reshape(dbt * rows, c)
    acc = None
    for r in range(SUBLANES):
        shifted = (flat if r == 0 else pltpu.roll(flat, dbt * rows - r, axis=0)).reshape(dbt, rows, c)
        for m in range(rows // SUBLANES):
            j = SUBLANES * m + r - off
            if 0 <= j < CONV_W:
                term = shifted[:, SUBLANES * m:SUBLANES * m + ds, :] * w_ref[j:j + 1, :]
                acc = term if acc is None else acc + term
    o_ref[...] = (acc + b_ref[...]).reshape(dbt * ds, c)
    st_out_ref[...] = ext_ref[:, off + ds:rows, :]


def _sample_conv(state, layer, a, w, b, *, s0, db, ds, dbt):
    c = a.shape[1]
    hist = state.shape[2]
    rows = dbt * ds
    return pl.pallas_call(
        functools.partial(_sample_conv_kernel, ds=ds),
        grid=(db // dbt,),
        in_specs=[pl.BlockSpec((dbt, None, hist, c), lambda i: (i, layer, 0, 0)),
                  pl.BlockSpec((rows, c), lambda i: (s0 // rows + i, 0)), _full(w.shape), _full(b.shape)],
        out_specs=[pl.BlockSpec((rows, c), lambda i: (i, 0)), pl.BlockSpec((dbt, hist, c), lambda i: (i, 0, 0))],
        out_shape=[jax.ShapeDtypeStruct((db * ds, c), F32), jax.ShapeDtypeStruct((db, hist, c), F32)],
        scratch_shapes=[pltpu.VMEM((dbt, CONV_BACK + ds, c), F32)],
        compiler_params=_params("parallel"),
        name="sample_conv",
    )(state, a, w, b)


def _fox_prep_kernel(q_ref, k_ref, v_ref, lf_ref, a_ref, wdw_ref, bdw_ref, *rest, fp, ns, n_prev, emit):
    prev, rest = rest[:2 * n_prev], rest[2 * n_prev:]
    qa_ref, ka_ref, vat_ref, cv_ref = rest[:4]
    carry_ref, ah_ref = rest[6:8] if emit else rest[4:6]
    s = pl.program_id(1)
    t = SEQ_TILE

    @pl.when(s < ns)
    def _():
        _fox_operands(s, q_ref, k_ref, v_ref, lf_ref, qa_ref, ka_ref, vat_ref, carry_ref, fp)

        @pl.when(s == 0)
        def _():
            ah_ref[...] = jnp.zeros_like(ah_ref)

        for cb in range(a_ref.shape[1] // LANES):
            cs = slice(cb * LANES, (cb + 1) * LANES)
            win = jnp.concatenate([ah_ref[:, cs], a_ref[:, cs]], axis=0)
            w_rows = [wdw_ref[j:j + 1, cs] for j in range(CONV_W)]
            cv_ref[:, cs] = _conv_tile(win, w_rows) + bdw_ref[:, cs]
        ah_ref[...] = a_ref[t - CONV_BACK:t, :]

    if not emit:
        return
    kt_ref, vt_ref = rest[4:6]
    kc_ref, vc_ref = rest[8:10]

    @pl.when(s == 0)
    def _():
        kc_ref[...] = jnp.zeros_like(kc_ref)
        vc_ref[...] = jnp.zeros_like(vc_ref)

    k_srcs = list(prev[0::2]) + [k_ref]
    v_srcs = list(prev[1::2]) + [v_ref]
    for srcs, held, dst in ((k_srcs, kc_ref, kt_ref), (v_srcs, vc_ref, vt_ref)):
        for li, src in enumerate(srcs):
            rows = jnp.concatenate([held[li], src[0:fp, :]], axis=0)
            for c in range(src.shape[1] // LANES):
                dst[li, c * LANES:(c + 1) * LANES, :] = rows[:, c * LANES:(c + 1) * LANES].T
            held[li] = src[fp:t, :]


def _fox_operands(s, q_ref, k_ref, v_ref, lf_ref, qa_ref, ka_ref, vat_ref, carry_ref, fp):
    @pl.when(s == 0)
    def _():
        carry_ref[...] = jnp.zeros_like(carry_ref)

    t = SEQ_TILE
    row = lax.broadcasted_iota(jnp.int32, (t, LANES), 0)
    lane = lax.broadcasted_iota(jnp.int32, (t, LANES), 1)
    c = jnp.where(s * t + row >= fp, lf_ref[...], 0.0)
    sh = 1
    while sh < t:
        c = c + jnp.where(row >= sh, pltpu.roll(c, sh, axis=0), 0.0)
        sh *= 2
    c = c + carry_ref[0:1, :]
    carry_ref[0:1, :] = c[t - 1:t, :]

    c = c * LOG2E
    c1 = c.astype(BF16).astype(F32)
    r1 = c - c1
    c2 = r1.astype(BF16).astype(F32)
    c3 = (r1 - c2).astype(BF16).astype(F32)
    g23 = jnp.where(lane < 2 * N_HEADS, pltpu.roll(c2, N_HEADS, axis=1),
                    jnp.where(lane < 3 * N_HEADS, pltpu.roll(c3, 2 * N_HEADS, axis=1), 0.0))
    g = jnp.where(lane < N_HEADS, c1, g23)
    gk = jnp.where(lane < N_HEADS, jnp.where(s * t + row >= fp, c1, -NEG), g23)
    ones_grp = (lane >= HEAD_DIM + AUX_ONES) & (lane < HEAD_DIM + 2 * AUX_ONES)
    ck = [jnp.where((lane >= HEAD_DIM) & (lane < HEAD_DIM + AUX_ONES), -pltpu.roll(gk, HEAD_DIM, axis=1),
                    jnp.where(ones_grp, 1.0, 0.0))]
    cq = [jnp.where(ones_grp, pltpu.roll(g, HEAD_DIM + AUX_ONES, axis=1), 0.0)]
    ck.append(pltpu.roll(ck[0], HEAD_DIM, axis=1))
    cq.append(pltpu.roll(cq[0], HEAD_DIM, axis=1))

    lane1 = lax.broadcasted_iota(jnp.int32, (1, LANES), 1)
    for h in range(N_HEADS):
        odd = h % 2
        pr = h // 2
        base = 0 if odd else HEAD_DIM
        a = lane1 - base
        mine = (lane1 & (N_HEADS - 1)) == h
        data = ((lane1 >= HEAD_DIM) if odd else (lane1 < HEAD_DIM)).astype(F32)
        sel1 = ((a >= 0) & (a < AUX_ONES) & mine).astype(F32)
        selc = ((a >= AUX_ONES) & (a < 2 * AUX_ONES) & mine).astype(F32)
        sl = slice(pr * LANES, (pr + 1) * LANES)
        qa_ref[h] = (q_ref[:, sl] * (data * LOG2E) + (cq[odd] * selc + sel1)).T.astype(BF16)
        ka_ref[h] = (k_ref[:, sl] * data + ck[odd]).astype(BF16)
    ones = jnp.ones((V_ROWS - HEAD_DIM, t), BF16)
    for pr in range(N_HEADS // 2):
        v_t = v_ref[:, pr * LANES:(pr + 1) * LANES].T.astype(BF16)
        vat_ref[2 * pr] = jnp.concatenate([v_t[0:HEAD_DIM], ones], axis=0)
        vat_ref[2 * pr + 1] = jnp.concatenate([v_t[HEAD_DIM:], ones], axis=0)


def _fox_prep(q, k, v, lf, a, w_dw, b_dw, prev_kv, emit, *, nb, sp, fp):
    ns = sp // SEQ_TILE
    att = q.shape[1]
    ca = a.shape[1]
    t = SEQ_TILE
    last = ns - 1
    row = lambda c: pl.BlockSpec((t, c), lambda bi, si: (bi * ns + jnp.minimum(si, last), 0))
    hm = pl.BlockSpec((None, N_HEADS, t, LANES), lambda bi, si: (bi, 0, jnp.minimum(si, last), 0))
    hm_t = lambda r: pl.BlockSpec((None, N_HEADS, r, t), lambda bi, si: (bi, 0, 0, jnp.minimum(si, last)))
    out_specs = [hm_t(LANES), hm, hm_t(V_ROWS), row(ca)]
    out_shape = [jax.ShapeDtypeStruct((nb, N_HEADS, LANES, sp), BF16),
                 jax.ShapeDtypeStruct((nb, N_HEADS, sp, LANES), BF16),
                 jax.ShapeDtypeStruct((nb, N_HEADS, V_ROWS, sp), BF16), jax.ShapeDtypeStruct((nb * sp, ca), F32)]
    scratch = [pltpu.VMEM((SUBLANES, LANES), F32), pltpu.VMEM((CONV_BACK, ca), F32)]
    n_prev = len(prev_kv) if emit else 0
    if emit:
        assert 0 < fp < t and fp % SUBLANES == 0
        n_layers = n_prev + 1
        cache = pl.BlockSpec((None, n_layers, att, t), lambda bi, si: (bi, 0, 0, jnp.maximum(si - 1, 0)))
        out_specs += [cache, cache]
        out_shape += [jax.ShapeDtypeStruct((nb, n_layers, att, sp - fp), F32)] * 2
        scratch += [pltpu.VMEM((n_layers, t - fp, att), F32)] * 2
    prev = [a for kv in prev_kv for a in kv] if emit else []
    return pl.pallas_call(
        functools.partial(_fox_prep_kernel, fp=fp, ns=ns, n_prev=n_prev, emit=emit),
        grid=(nb, ns + 1 if emit else ns),
        in_specs=[row(att), row(att), row(att), row(LANES), row(ca), _full(w_dw.shape), _full(b_dw.shape)]
        + [row(att)] * len(prev),
        out_specs=out_specs,
        out_shape=out_shape,
        scratch_shapes=scratch,
        compiler_params=_params("parallel", "arbitrary"),
        name="fox_prep",
    )(q, k, v, lf, a, w_dw, b_dw, *prev)


def _flash_kernel(q_ref, k_ref, vt_ref, o_ref, s_ref, p_ref, acc_ref, *, hg):
    i = pl.program_id(2)
    t = SEQ_TILE

    def scores_to(j, slot):
        ks = pl.multiple_of(j * t, t)
        for h in range(hg):
            s_ref[slot, h] = _dot(k_ref[h, pl.ds(ks, t), :], q_ref[h])

    def accumulate(j, als):
        ks = pl.multiple_of(j * t, t)
        pvs = [_dot(vt_ref[h, :, pl.ds(ks, t)], p_ref[h]) for h in range(hg)]
        for h in range(hg):
            acc_ref[h] = als[h] * acc_ref[h] + pvs[h]

    def softmax(slot, ms, masked):
        new_m, als = [], []
        for h in range(hg):
            s = s_ref[slot, h]
            if masked:
                visible = (lax.broadcasted_iota(jnp.int32, (t, t), 0) <= lax.broadcasted_iota(jnp.int32, (t, t), 1))
                s = jnp.where(visible, s, NEG)
            mn = jnp.maximum(ms[h], jnp.max(s, axis=0, keepdims=True))
            als.append(jnp.exp2(ms[h] - mn))
            new_m.append(mn)
            p_ref[h] = jnp.exp2(s - mn).astype(BF16)
        return tuple(new_m), tuple(als)

    def stage(j, slot, state):
        ms, als = state
        scores_to(j + 1, 1 - slot)
        accumulate(jnp.maximum(j - 1, 0), als)
        return softmax(slot, ms, False)

    def pair(jj, state):
        return stage(2 * jj + 1, 1, stage(2 * jj, 0, state))

    def finish(slot, state):
        ms, als = state
        accumulate(jnp.maximum(i - 1, 0), als)
        _, als = softmax(slot, ms, True)
        accumulate(i, als)
        for pr in range(hg // 2):
            halves = [acc_ref[h, 0:HEAD_DIM, :] * (1.0 / acc_ref[h, HEAD_DIM:HEAD_DIM + 1, :])
                      for h in (2 * pr, 2 * pr + 1)]
            o_ref[:, pr * LANES:(pr + 1) * LANES] = jnp.concatenate(halves, axis=0).T.astype(o_ref.dtype)

    p_ref[...] = jnp.zeros_like(p_ref)
    acc_ref[...] = jnp.zeros_like(acc_ref)
    scores_to(0, 0)
    state = ((jnp.full((1, t), NEG, F32),) * hg, (jnp.ones((1, t), F32),) * hg)
    state = lax.fori_loop(0, i // 2, pair, state)
    odd = i % 2 == 1
    state = lax.cond(odd, lambda st: stage(i - 1, 0, st), lambda st: st, state)
    pl.when(odd)(lambda: finish(1, state))
    pl.when(jnp.logical_not(odd))(lambda: finish(0, state))


def _flash(qat, ka, vat, *, hg):
    nb, _, sp, _ = ka.shape
    ns = sp // SEQ_TILE
    return pl.pallas_call(
        functools.partial(_flash_kernel, hg=hg),
        grid=(nb, N_HEADS // hg, ns),
        in_specs=[pl.BlockSpec((None, hg, LANES, SEQ_TILE), lambda bi, gi, qi: (bi, gi, 0, qi)),
                  pl.BlockSpec((None, hg, sp, LANES), lambda bi, gi, qi: (bi, gi, 0, 0)),
                  pl.BlockSpec((None, hg, V_ROWS, sp), lambda bi, gi, qi: (bi, gi, 0, 0))],
        out_specs=pl.BlockSpec((SEQ_TILE, hg * HEAD_DIM), lambda bi, gi, qi: (bi * ns + qi, gi)),
        out_shape=jax.ShapeDtypeStruct((nb * sp, N_HEADS * HEAD_DIM), BF16),
        scratch_shapes=[pltpu.VMEM((2, hg, SEQ_TILE, SEQ_TILE), F32), pltpu.VMEM((hg, SEQ_TILE, SEQ_TILE), BF16),
                        pltpu.VMEM((hg, V_ROWS, SEQ_TILE), F32)],
        compiler_params=_params("parallel", "parallel", "arbitrary"),
        name="fox_flash",
    )(qat, ka, vat)


def _sample_attn_kernel(pt_ref, q_ref, kn_ref, vn_ref, lfn_ref, *rest, n_pages, page, eb, ds):
    del pt_ref
    o_ref = rest[3 * eb * n_pages]
    c = q_ref.shape[1]
    rows = ds * N_HEADS
    groups = [list(range(g, min(g + 2, n_pages))) for g in range(0, n_pages, 2)]
    past = n_pages * page
    sub = lax.broadcasted_iota(jnp.int32, (N_HEADS, c), 0)
    lane = lax.broadcasted_iota(jnp.int32, (N_HEADS, c), 1)
    hm = ((lane >= sub * HEAD_DIM) & (lane < (sub + 1) * HEAD_DIM)).astype(F32)
    hm_t = jnp.concatenate([hm] * ds, axis=0)
    pad = jnp.zeros((LANES - ds, c), F32)

    def refs_of(kind, e):
        return rest[(kind * eb + e) * n_pages:(kind * eb + e + 1) * n_pages]

    def scores(e):
        sl = slice(e * ds, (e + 1) * ds)
        q = q_ref[sl, :]
        qbd = jnp.concatenate([jnp.broadcast_to(q[t:t + 1, :], (N_HEADS, c)) * hm for t in range(ds)],
                              axis=0).astype(BF16)
        k_refs, lf_refs = refs_of(0, e), refs_of(2, e)
        s_parts = []
        for grp in groups:
            kp = jnp.concatenate([k_refs[r][...] for r in grp], axis=1).astype(BF16)
            s_parts.append(_dot(qbd, kp))
        s_past = jnp.concatenate(s_parts, axis=1)

        lf = jnp.concatenate([r[...] for r in lf_refs], axis=1)
        lane_p = lax.broadcasted_iota(jnp.int32, (N_HEADS, past), 1)
        suf = lf
        sh = 1
        while sh < past:
            suf = suf + jnp.where(lane_p < past - sh, pltpu.roll(suf, past - sh, axis=1), 0.0)
            sh *= 2
        s_past = s_past + jnp.concatenate([suf - lf] * ds, axis=0)

        kn = jnp.concatenate([kn_ref[sl, :], pad], axis=0).astype(BF16)
        s_new = lax.dot_general(qbd, kn, NT_DIMS, preferred_element_type=F32)
        cn = lfn_ref[sl, :]
        sub_n = lax.broadcasted_iota(jnp.int32, cn.shape, 0)
        sh = 1
        while sh < ds:
            cn = cn + jnp.where(sub_n >= sh, pltpu.roll(cn, sh, axis=0), 0.0)
            sh *= 2
        cn_t = jnp.concatenate([cn, jnp.zeros((LANES - ds, LANES), F32)], axis=0).T[0:N_HEADS, :]
        rown = lax.broadcasted_iota(jnp.int32, (rows, LANES), 0)
        u = lax.broadcasted_iota(jnp.int32, (rows, LANES), 1)
        s_new = jnp.where(u * N_HEADS <= rown, s_new - jnp.concatenate([cn_t] * ds, axis=0), NEG)
        return s_past, s_new

    def softmax(s_past, s_new):
        m = jnp.maximum(jnp.max(s_past, axis=1, keepdims=True), jnp.max(s_new, axis=1, keepdims=True))
        p_past = jnp.exp(s_past - m)
        p_new = jnp.exp(s_new - m)
        l = jnp.sum(p_past, axis=1, keepdims=True) + jnp.sum(p_new, axis=1, keepdims=True)
        return p_past.astype(BF16), p_new.astype(BF16), l

    def output(e, p_past, p_new, l):
        sl = slice(e * ds, (e + 1) * ds)
        v_refs = refs_of(1, e)
        vn = jnp.concatenate([vn_ref[sl, :], pad], axis=0).astype(BF16)
        o = _dot(p_new, vn)
        for grp in groups:
            vp = jnp.concatenate([v_refs[r][...] for r in grp], axis=1).astype(BF16)
            lo = grp[0] * page
            o = o + lax.dot_general(p_past[:, lo:lo + len(grp) * page], vp, NT_DIMS, preferred_element_type=F32)
        o = o * (1.0 / l) * hm_t
        rr = lax.broadcasted_iota(jnp.int32, (ds, rows), 1)
        tt = lax.broadcasted_iota(jnp.int32, (ds, rows), 0)
        pick = ((rr >= tt * N_HEADS) & (rr < (tt + 1) * N_HEADS)).astype(BF16)
        o_ref[sl, :] = _dot(pick, o.astype(BF16)).astype(o_ref.dtype)

    ss = [scores(e) for e in range(eb)]
    ps = [softmax(*s) for s in ss]
    for e in range(eb):
        output(e, *ps[e])


def _sample_attn(page_table, layer, q, k, v, lf, cache_k, cache_v, cache_lft, *, s0, db, ds):
    c = q.shape[1]
    n_pages = page_table.shape[1]
    page = cache_k.shape[3]
    eb = 2 if db % 2 == 0 else 1
    rows = eb * ds
    row = lambda cc: pl.BlockSpec((rows, cc), lambda i, pt: (s0 // rows + i, 0))

    def paged(shape, e, r):
        return pl.BlockSpec((None, None) + shape, lambda i, pt: (pt[(i * eb + e) * n_pages + r], layer, 0, 0))

    pages = [(e, r) for e in range(eb) for r in range(n_pages)]
    in_specs = ([row(c), row(c), row(c), row(LANES)]
                + [paged((c, page), e, r) for e, r in pages]
                + [paged((c, page), e, r) for e, r in pages]
                + [paged((N_HEADS, page), e, r) for e, r in pages])
    return pl.pallas_call(
        functools.partial(_sample_attn_kernel, n_pages=n_pages, page=page, eb=eb, ds=ds),
        grid_spec=pltpu.PrefetchScalarGridSpec(
            num_scalar_prefetch=1, grid=(db // eb,), in_specs=in_specs,
            out_specs=pl.BlockSpec((rows, c), lambda i, pt: (i, 0))),
        out_shape=jax.ShapeDtypeStruct((db * ds, c), BF16 if rows % (2 * SUBLANES) == 0 else F32),
        compiler_params=_params("parallel"),
        name="sample_paged_attn",
    )(page_table.reshape(-1), q, k, v, lf, *([cache_k] * len(pages)), *([cache_v] * len(pages)),
      *([cache_lft] * len(pages)))


def _odd_tile(wy, wp, bg, w_rows, window, pos0):
    t = bg.shape[0]
    y0 = wy[SUBLANES:]
    y1 = pltpu.roll(wy, 1, axis=0)[SUBLANES:]
    y2 = pltpu.roll(wy, 2, axis=0)[SUBLANES:]
    out_c = bg * (w_rows[0] * y2 + w_rows[1] * y1 + w_rows[2] * y0)
    back = 2 * SUBLANES
    win = wp
    step = 1
    while step < window:
        win = win + pltpu.roll(win, step, axis=0)
        step *= 2
    if pos0 is None:
        cnt = float(window)
    else:
        pos = pos0 + lax.broadcasted_iota(jnp.int32, (t, LANES), 0)
        cnt = jnp.maximum(jnp.minimum(window, pos + 1), 1).astype(F32)
    return out_c, win[back:] / cnt - wp[back:]


def _prompt_odd_kernel(y_ref, bg_ref, pd_ref, w_ref, oc_ref, d_ref, *, n_tiles, fp):
    w_rows = [w_ref[j:j + 1, :] for j in range(SCONV_W)]
    c = y_ref.shape[1]
    t = SEQ_TILE
    n_partial = -(-(fp + POOL_HIST) // t)

    def run(window):
        wy0 = jnp.concatenate([jnp.zeros((SUBLANES, c), F32), y_ref[0:t, :]], axis=0)
        wp0 = jnp.concatenate([jnp.zeros((2 * SUBLANES, c), F32), pd_ref[0:t, :]], axis=0)
        oc, d = _odd_tile(wy0, wp0, bg_ref[0:t, :], w_rows, window, -fp)
        oc_ref[0:t, :] = oc.astype(oc_ref.dtype)
        d_ref[0:t, :] = d.astype(d_ref.dtype)

        def body(i, partial):
            s0 = pl.multiple_of(i * t, t)
            wy = y_ref[pl.ds(s0 - SUBLANES, t + SUBLANES), :]
            wp = pd_ref[pl.ds(s0 - 2 * SUBLANES, t + 2 * SUBLANES), :]
            oc, d = _odd_tile(wy, wp, bg_ref[pl.ds(s0, t), :], w_rows, window, s0 - fp if partial else None)
            oc_ref[pl.ds(s0, t), :] = oc.astype(oc_ref.dtype)
            d_ref[pl.ds(s0, t), :] = d.astype(d_ref.dtype)

        lax.fori_loop(1, n_partial, lambda i, carry: body(i, True) or carry, 0)
        lax.fori_loop(n_partial, n_tiles, lambda i, carry: body(i, False) or carry, 0)

    for gi, window in enumerate(POOL_WINDOWS):
        pl.when(pl.program_id(1) == gi)(functools.partial(run, window))


def _prompt_odd(y, bg, pd, w, *, nb, sp, fp):
    c = y.shape[1]
    assert c // LANES == len(POOL_WINDOWS)
    blk = pl.BlockSpec((sp, LANES), lambda bi, ci: (bi, ci))
    return pl.pallas_call(
        functools.partial(_prompt_odd_kernel, n_tiles=sp // SEQ_TILE, fp=fp),
        grid=(nb, c // LANES),
        in_specs=[blk, blk, blk, pl.BlockSpec((SCONV_W, LANES), lambda bi, ci: (0, ci))],
        out_specs=[blk, blk],
        out_shape=[jax.ShapeDtypeStruct((nb * sp, c), BF16)] * 2,
        compiler_params=_params("parallel", "parallel"),
        name="prompt_sconv_pool",
    )(y, bg, pd, w)


def _sample_odd_kernel(ss_ref, sp_ref, y_ref, bg_ref, pd_ref, w_ref,
                       oc_ref, d_ref, ss_out_ref, sp_out_ref, yext_ref, pext_ref, *, ds):
    dbt, _, c = ss_ref.shape
    hs = SCONV_W - 1
    y = y_ref[...].reshape(dbt, ds, c)
    pd = pd_ref[...].reshape(dbt, ds, c)
    bg = bg_ref[...].reshape(dbt, ds, c)
    ny = SUBLANES + ds
    yext_ref[:, 0:SUBLANES - hs, :] = jnp.zeros((dbt, SUBLANES - hs, c), F32)
    yext_ref[:, SUBLANES - hs:SUBLANES, :] = ss_ref[...]
    yext_ref[:, SUBLANES:ny, :] = y
    yflat = yext_ref[...].reshape(dbt * ny, c)
    conv = y * w_ref[SCONV_W - 1:SCONV_W, :]
    for back in range(1, SCONV_W):
        shifted = pltpu.roll(yflat, back, axis=0).reshape(dbt, ny, c)[:, SUBLANES:ny, :]
        conv = conv + shifted * w_ref[SCONV_W - 1 - back:SCONV_W - back, :]
    oc_ref[...] = (bg * conv).reshape(dbt * ds, c).astype(oc_ref.dtype)
    ss_out_ref[...] = yext_ref[:, ny - hs:ny, :]

    base = 2 * SUBLANES
    npd = base + ds
    pext_ref[:, 0:base - POOL_HIST, :] = jnp.zeros((dbt, base - POOL_HIST, c), F32)
    pext_ref[:, base - POOL_HIST:base, :] = sp_ref[...]
    pext_ref[:, base:npd, :] = pd
    gc = c // len(POOL_WINDOWS)
    means = []
    for gi, window in enumerate(POOL_WINDOWS):
        win = pext_ref[:, :, gi * gc:(gi + 1) * gc].reshape(dbt * npd, gc)
        step = 1
        while step < window:
            win = win + pltpu.roll(win, step, axis=0)
            step *= 2
        means.append(win.reshape(dbt, npd, gc)[:, base:npd, :] / float(window))
    d_ref[...] = (jnp.concatenate(means, axis=-1) - pd).reshape(dbt * ds, c).astype(d_ref.dtype)
    sp_out_ref[...] = pext_ref[:, npd - POOL_HIST:npd, :]


def _sample_odd(state_s, state_p, layer, y, bg, pd, w, *, s0, db, ds, dbt):
    c = y.shape[1]
    rows = dbt * ds
    row_in = pl.BlockSpec((rows, c), lambda i: (s0 // rows + i, 0))
    row_out = pl.BlockSpec((rows, c), lambda i: (i, 0))
    hs = state_s.shape[2]
    hp = state_p.shape[2]
    return pl.pallas_call(
        functools.partial(_sample_odd_kernel, ds=ds),
        grid=(db // dbt,),
        in_specs=[pl.BlockSpec((dbt, None, hs, c), lambda i: (i, layer, 0, 0)),
                  pl.BlockSpec((dbt, None, hp, c), lambda i: (i, layer, 0, 0)),
                  row_in, row_in, row_in, _full(w.shape)],
        out_specs=[row_out, row_out, pl.BlockSpec((dbt, hs, c), lambda i: (i, 0, 0)),
                   pl.BlockSpec((dbt, hp, c), lambda i: (i, 0, 0))],
        out_shape=[jax.ShapeDtypeStruct((db * ds, c), BF16 if rows % (2 * SUBLANES) == 0 else F32)] * 2 + [
                   jax.ShapeDtypeStruct((db, hs, c), F32), jax.ShapeDtypeStruct((db, hp, c), F32)],
        scratch_shapes=[pltpu.VMEM((dbt, SUBLANES + ds, c), F32), pltpu.VMEM((dbt, 2 * SUBLANES + ds, c), F32)],
        compiler_params=_params("parallel"),
        name="sample_sconv_pool",
    )(state_s, state_p, y, bg, pd, w)


def kernel(x_prompt, x_sample, cache_k, cache_v, cache_logf, page_table, state_conv_a, state_sconv, state_pool, meta_tokens, w_in_even, b_forget, w_dw_a, b_dw_a, ln_a_g, ln_a_b, w_out_even, w_in_odd, w_sconv, w_pool_mix, pool_scale, w_out_odd, ln_mix_g, ln_mix_b, w_ffn_gate, w_ffn_up, w_ffn_down, ln_ffn_g, ln_ffn_b):
    nb, seq, d = x_prompt.shape
    db, ds, _ = x_sample.shape
    depth = w_ffn_gate.shape[0]
    alpha = float((2 * depth) ** 0.25)
    ca = state_conv_a.shape[-1]
    att = N_HEADS * HEAD_DIM
    assert ds == SUBLANES and cache_k.shape[3] == N_HEADS and cache_k.shape[4] == HEAD_DIM
    assert state_conv_a.shape[2] == CONV_W - 1 and state_pool.shape[2] == POOL_HIST
    assert meta_tokens.shape[0] == N_META

    s_real = N_META + seq
    fp = (-s_real) % SEQ_TILE
    if fp < CONV_BACK:
        fp += SEQ_TILE
    sp = fp + s_real
    s0 = nb * sp
    n = s0 + db * ds
    tm = next(t for t in (1024, 512, 256, 128, 64, 32, 16, 8) if s0 % t == 0 and (db * ds) % t == 0)
    tm_in = next(t for t in (1024, 512, 256, 128, 64, 32, 16, 8) if n % t == 0)
    dbt = next(t for t in (32, 16, 8, 4, 2, 1) if db % t == 0)
    direct_out = seq % SEQ_TILE == 0 and (db * ds) % SEQ_TILE == 0

    head = jnp.concatenate([jnp.zeros((fp, d), F32), meta_tokens.astype(F32)], axis=0)
    fused_first = direct_out and tm_in % SEQ_TILE == 0
    x = None if fused_first else jnp.concatenate(
        [piece for bi in range(nb) for piece in (head, x_prompt[bi])] + [x_sample.reshape(db * ds, d)], axis=0)

    n_phys = cache_k.shape[0]
    n_even = cache_k.shape[1]
    page = cache_k.shape[2]
    ck = jnp.transpose(cache_k, (0, 1, 3, 4, 2)).reshape(n_phys, n_even, att, page)
    cv = jnp.transpose(cache_v, (0, 1, 3, 4, 2)).reshape(n_phys, n_even, att, page)
    clft = jnp.swapaxes(cache_logf, 2, 3)

    row2 = lambda v: v.reshape(1, -1).astype(F32)

    def prompt_rows(arr, lo, hi):
        return jnp.stack([arr[bi * sp + lo:bi * sp + hi] for bi in range(nb)], axis=0)

    def sample_rows(arr):
        return arr[s0:].reshape(db, ds, -1)

    wg_all, wu_all, wd_all = (w.astype(BF16) for w in (w_ffn_gate, w_ffn_up, w_ffn_down))
    wo_even, wo_odd = w_out_even.astype(BF16), w_out_odd.astype(BF16)
    wi_even = jnp.pad(w_in_even.astype(BF16), ((0, 0), (0, 0), (0, LANES - N_HEADS)))
    wi_odd = w_in_odd.astype(BF16)

    def tail(name, xin, m1p, m2p, m1s, m2s, p1, p2, w_out_all, layer):
        last = direct_out and layer == depth - 1
        return _layer_tail(name, xin, m1p, m2p, m1s, m2s, p1, p2, w_out_all,
                           row2(ln_mix_g[layer]), row2(ln_mix_b[layer]), wg_all, wu_all, wd_all,
                           row2(ln_ffn_g[layer]), row2(ln_ffn_b[layer]), even=layer % 2 == 0,
                           mix_idx=layer // 2, layer=layer, tm=SEQ_TILE if last else tm, alpha=alpha,
                           nb=nb, sp=sp, fp=fp, out_rows=(seq, db * ds) if last else None)

    prev_kv = []
    lfp, cap, scp, plp = [], [], [], []
    ks_, vs_, lfs, cas, scs, pls = [], [], [], [], [], []
    for layer in range(depth):
        i = layer // 2
        if layer % 2 == 0:
            bf = jnp.pad(b_forget[i].astype(F32), (0, LANES - N_HEADS)).reshape(1, LANES)
            if x is None:
                x, a, q, k, v, lf = _first_in(x_prompt.astype(F32), head, x_sample.reshape(db * ds, d).astype(F32),
                                              wi_even, i, bf, tm=tm_in, sp=sp, ca=ca, att=att)
            else:
                a, q, k, v, lf = _even_in(x, wi_even, i, bf, tm=tm_in, ca=ca, att=att)
            w_dw, b_dw = w_dw_a[i].astype(F32), row2(b_dw_a[i])
            conv_s, st_a = _sample_conv(state_conv_a.astype(F32), i, a, w_dw, b_dw, s0=s0, db=db, ds=ds, dbt=dbt)
            qa, ka, vat, conv_p, *caches = _fox_prep(q, k, v, lf, a, w_dw, b_dw, prev_kv, i == n_even - 1,
                                                     nb=nb, sp=sp, fp=fp)
            prev_kv.append((k, v))
            at_p = _flash(qa, ka, vat, hg=FLASH_HEADS)
            at_s = _sample_attn(page_table, i, q, k, v, lf, ck, cv, clft, s0=s0, db=db, ds=ds)
            x = tail("even_tail", x, conv_p, at_p, conv_s, at_s, row2(ln_a_g[i]), row2(ln_a_b[i]), wo_even, layer)
            lfp.append(prompt_rows(lf, fp, sp)[..., :N_HEADS])
            cap.append(prompt_rows(a, sp - (CONV_W - 1), sp))
            ks_.append(sample_rows(k).reshape(db, ds, N_HEADS, HEAD_DIM))
            vs_.append(sample_rows(v).reshape(db, ds, N_HEADS, HEAD_DIM))
            lfs.append(sample_rows(lf)[..., :N_HEADS])
            cas.append(st_a)
        else:
            c = state_sconv.shape[-1]
            y, bg, pd = _odd_in(x, wi_odd, i, tm=tm_in, c=c)
            oc_p, dd_p = _prompt_odd(y, bg, pd, w_sconv[i].astype(F32), nb=nb, sp=sp, fp=fp)
            oc_s, dd_s, st_s, st_p = _sample_odd(state_sconv.astype(F32), state_pool.astype(F32), i, y, bg, pd,
                                                 w_sconv[i].astype(F32), s0=s0, db=db, ds=ds, dbt=dbt)
            wm = jax.scipy.linalg.block_diag(*[w_pool_mix[i, g] for g in range(w_pool_mix.shape[1])]).astype(BF16)
            x = tail("odd_tail", x, oc_p, dd_p, oc_s, dd_s, wm, row2(pool_scale[i]), wo_odd, layer)
            scp.append(prompt_rows(y, sp - (SCONV_W - 1), sp))
            plp.append(prompt_rows(pd, sp - POOL_HIST, sp))
            scs.append(st_s)
            pls.append(st_p)

    if direct_out:
        y_prompt, y_sample = x[0], x[1].reshape(db, ds, d)
    else:
        y_prompt, y_sample = prompt_rows(x, fp + N_META, sp), sample_rows(x)
    st = lambda xs: jnp.stack(xs, axis=1)
    kp, vp = (jnp.transpose(c.reshape(nb, n_even, N_HEADS, HEAD_DIM, s_real), (0, 1, 4, 2, 3)) for c in caches)
    return (y_prompt, y_sample, kp, vp, st(lfp), st(cap), st(scp), st(plp),
            st(ks_), st(vs_), st(lfs), st(cas), st(scs), st(pls))
```
